```python
import jax
import jax.numpy as jnp
from jax import lax
import numpy as np


D_MODEL = 2048
BATCH = 8
SEQ = 2048
DEPTH = 2

MEM_LEN = 256
D_FF = 5632
EPS = 1e-6
N_BRANCH = 3

SGU_GROUPS = 4
SGU_GROUP_DIM = D_MODEL // 8
SGU_WIDTH = SGU_GROUPS * SGU_GROUP_DIM
SGU_CHUNK = 128

FOX_HEADS = 8
FOX_HEAD_DIM = D_MODEL // 16
FOX_WIDTH = FOX_HEADS * FOX_HEAD_DIM
FOX_BLOCK = 128

GLA_HEADS = 4
GLA_DK = D_MODEL // 16
GLA_DV = D_MODEL // 8
GLA_KW = GLA_HEADS * GLA_DK
GLA_VW = GLA_HEADS * GLA_DV
GLA_GATE_RANK = 16
GLA_GATE_TAU = 16.0
GLA_CHUNK = 64

XA_HEADS = 4
XA_HEAD_DIM = D_MODEL // 16
XA_WIDTH = XA_HEADS * XA_HEAD_DIM

N_IN = (2 * SGU_WIDTH + 3 * FOX_WIDTH + FOX_HEADS + 2 * GLA_KW + GLA_VW
        + GLA_GATE_RANK + GLA_VW + N_BRANCH * D_MODEL)

kernel_name = 'hybrid_sgu_fox_gla_macaron_block'


def _in_widths():
    return (SGU_WIDTH, SGU_WIDTH,
            FOX_WIDTH, FOX_WIDTH, FOX_WIDTH, FOX_HEADS,
            GLA_KW, GLA_KW, GLA_VW, GLA_GATE_RANK,
            GLA_VW,
            N_BRANCH * D_MODEL)


def _split_cols(z, widths):
    idx = [int(i) for i in np.cumsum(widths)[:-1]]
    return jnp.split(z, idx, axis=-1)


def rms_norm(x, g):
    xf = x.astype(jnp.float32)
    y = xf * lax.rsqrt(jnp.mean(xf * xf, axis=-1, keepdims=True) + EPS)
    return (y * g.astype(jnp.float32)).astype(x.dtype)


def layer_norm(x, g, b):
    xf = x.astype(jnp.float32)
    mu = jnp.mean(xf, axis=-1, keepdims=True)
    xc = xf - mu
    var = jnp.mean(xc * xc, axis=-1, keepdims=True)
    y = xc * lax.rsqrt(var + EPS) * g.astype(jnp.float32) + b.astype(jnp.float32)
    return y.astype(x.dtype)


def swiglu_ffn(x, w_in, w_out):
    gate, up = jnp.split(x @ w_in, 2, axis=-1)
    return (jax.nn.silu(gate) * up) @ w_out


def sgu_branch(u, v, ln_g, ln_b, w_s, b_s):
    B, S, _ = u.shape
    n_chunks = S // SGU_CHUNK
    u = jax.nn.gelu(u)
    v = jax.nn.gelu(v)
    v = layer_norm(v.reshape(B, S, SGU_GROUPS, SGU_GROUP_DIM), ln_g, ln_b)
    v = v.reshape(B, n_chunks, SGU_CHUNK, SGU_GROUPS, SGU_GROUP_DIM)
    causal = jnp.tril(jnp.ones((SGU_CHUNK, SGU_CHUNK), dtype=bool))
    w = jnp.where(causal[None], w_s, jnp.zeros_like(w_s))
    mixed = jnp.einsum('gts,bnsgc->bntgc', w, v) + jnp.swapaxes(b_s, 0, 1)[:, :, None]
    return u * mixed.reshape(B, S, SGU_WIDTH)


def fox_branch(q, k, v, f_logit, b_f):
    B, S, _ = q.shape
    H, Dh = FOX_HEADS, FOX_HEAD_DIM
    q = q.reshape(B, S, H, Dh).transpose(0, 2, 1, 3) * (Dh ** -0.5)
    k = k.reshape(B, S, H, Dh).transpose(0, 2, 1, 3)
    v = v.reshape(B, S, H, Dh).transpose(0, 2, 1, 3)
    log_f = jax.nn.log_sigmoid((f_logit + b_f).astype(jnp.float32))
    cum = jnp.cumsum(log_f, axis=1).transpose(0, 2, 1)
    outs = []
    for i in range(S // FOX_BLOCK):
        lo = i * FOX_BLOCK
        hi = lo + FOX_BLOCK
        logits = jnp.einsum('bhtd,bhsd->bhts', q[:, :, lo:hi], k[:, :, :hi]).astype(jnp.float32)
        logits = logits + cum[:, :, lo:hi, None] - cum[:, :, None, :hi]
        causal = (lo + jnp.arange(FOX_BLOCK))[:, None] >= jnp.arange(hi)[None, :]
        logits = jnp.where(causal, logits, -jnp.inf)
        p = jax.nn.softmax(logits, axis=-1).astype(v.dtype)
        outs.append(jnp.einsum('bhts,bhsd->bhtd', p, v[:, :, :hi]))
    o = jnp.concatenate(outs, axis=2)
    return o.transpose(0, 2, 1, 3).reshape(B, S, FOX_WIDTH)


def gla_branch(q, k, v, a_low, r, w_gate, b_gate, o_norm):
    B, S, _ = q.shape
    H, dk, dv, C = GLA_HEADS, GLA_DK, GLA_DV, GLA_CHUNK
    n = S // C
    dt = v.dtype
    f32 = jnp.float32
    g = jax.nn.log_sigmoid((a_low @ w_gate + b_gate).astype(f32)) / GLA_GATE_TAU

    def chunks(t, d):
        return t.astype(f32).reshape(B, n, C, H, d).transpose(0, 1, 3, 2, 4)

    qc = chunks(q, dk) * (dk ** -0.5)
    kc = chunks(k, dk)
    vc = chunks(v, dv)
    bc = jnp.cumsum(chunks(g, dk), axis=3)
    b_last = bc[:, :, :, -1:, :]
    b_ref = bc[:, :, :, C // 2:C // 2 + 1, :]
    causal = jnp.tril(jnp.ones((C, C), dtype=bool))
    att = jnp.einsum('bnhtd,bnhsd->bnhts', qc * jnp.exp(bc - b_ref), kc * jnp.exp(b_ref - bc))
    att = jnp.where(causal, att, 0.0)
    o_intra = jnp.einsum('bnhts,bnhsv->bnhtv', att, vc)
    upd = jnp.einsum('bnhsd,bnhsv->bnhdv', kc * jnp.exp(b_last - bc), vc)
    decay = jnp.swapaxes(jnp.exp(b_last), -1, -2)

    def step(state, inp):
        dec, u_n = inp
        return dec * state + u_n, state

    init = jnp.zeros((B, H, dk, dv), f32)
    _, s_prev = lax.scan(step, init, (jnp.moveaxis(decay, 1, 0), jnp.moveaxis(upd, 1, 0)))
    s_prev = jnp.moveaxis(s_prev, 0, 1)
    o_inter = jnp.einsum('bnhtd,bnhdv->bnhtv', qc * jnp.exp(bc), s_prev)
    o = (o_intra + o_inter).transpose(0, 1, 3, 2, 4).reshape(B, S, H, dv)
    o = rms_norm(o, o_norm).reshape(B, S, GLA_VW).astype(dt)
    return o * jax.nn.silu(r)


def cross_attention(n, m, w_q, w_kv, w_o):
    B, S, _ = n.shape
    M = m.shape[1]
    q = (n @ w_q).reshape(B, S, XA_HEADS, XA_HEAD_DIM) * (XA_HEAD_DIM ** -0.5)
    k, v = jnp.split(m @ w_kv, 2, axis=-1)
    k = k.reshape(B, M, XA_HEADS, XA_HEAD_DIM)
    v = v.reshape(B, M, XA_HEADS, XA_HEAD_DIM)
    logits = jnp.einsum('bthd,bmhd->bhtm', q, k).astype(jnp.float32)
    p = jax.nn.softmax(logits, axis=-1).astype(v.dtype)
    o = jnp.einsum('bhtm,bmhd->bthd', p, v).reshape(B, S, XA_WIDTH)
    return o @ w_o


def setup_inputs(seed: int = 0) -> dict:
    key = jax.random.key(seed)
    ks = iter(jax.random.split(key, 48))
    L, D = DEPTH, D_MODEL

    def nrm(shape, scale):
        return scale * jax.random.normal(next(ks), shape, jnp.float32)

    def gain(shape):
        return 1.0 + 0.02 * jax.random.normal(next(ks), shape, jnp.float32)

    return {
        'x': nrm((BATCH, SEQ, D), 1.0),
        'mem': nrm((BATCH, MEM_LEN, D), 1.0),
        'ffn1_norm': gain((L, D)),
        'ffn1_w_in': nrm((L, D, 2 * D_FF), D ** -0.5),
        'ffn1_w_out': nrm((L, D_FF, D), D_FF ** -0.5),
        'mix_norm': gain((L, D)),
        'w_in': nrm((L, D, N_IN), D ** -0.5),
        'sgu_ln_g': gain((L, SGU_GROUPS, SGU_GROUP_DIM)),
        'sgu_ln_b': nrm((L, SGU_GROUPS, SGU_GROUP_DIM), 0.02),
        'sgu_w_s': nrm((L, SGU_GROUPS, SGU_CHUNK, SGU_CHUNK), SGU_CHUNK ** -0.5),
        'sgu_b_s': gain((L, SGU_GROUPS, SGU_CHUNK)),
        'fox_b_f': 2.0 + nrm((L, FOX_HEADS), 0.5),
        'gla_w_gate': nrm((L, GLA_GATE_RANK, GLA_KW), GLA_GATE_RANK ** -0.5),
        'gla_b_gate': nrm((L, GLA_KW), 0.01),
        'gla_o_norm': gain((L, GLA_HEADS, GLA_DV)),
        'w_branch_a': nrm((L, SGU_WIDTH, D), SGU_WIDTH ** -0.5),
        'w_branch_b': nrm((L, FOX_WIDTH, D), FOX_WIDTH ** -0.5),
        'w_branch_c': nrm((L, GLA_VW, D), GLA_VW ** -0.5),
        'w_out': nrm((L, D, D), D ** -0.5),
        'xa_norm': gain((L, D)),
        'mem_norm': gain((L, D)),
        'xa_w_q': nrm((L, D, XA_WIDTH), D ** -0.5),
        'xa_w_kv': nrm((L, D, 2 * XA_WIDTH), D ** -0.5),
        'xa_w_o': nrm((L, XA_WIDTH, D), XA_WIDTH ** -0.5),
        'ffn2_norm': gain((L, D)),
        'ffn2_w_in': nrm((L, D, 2 * D_FF), D ** -0.5),
        'ffn2_w_out': nrm((L, D_FF, D), D_FF ** -0.5),
        'final_norm': gain((D,)),
    }


def reference(x, mem, ffn1_norm, ffn1_w_in, ffn1_w_out, mix_norm, w_in,
              sgu_ln_g, sgu_ln_b, sgu_w_s, sgu_b_s, fox_b_f,
              gla_w_gate, gla_b_gate, gla_o_norm,
              w_branch_a, w_branch_b, w_branch_c, w_out,
              xa_norm, mem_norm, xa_w_q, xa_w_kv, xa_w_o,
              ffn2_norm, ffn2_w_in, ffn2_w_out, final_norm):
    h = x
    for l in range(DEPTH):
        h = h + 0.5 * swiglu_ffn(rms_norm(h, ffn1_norm[l]), ffn1_w_in[l], ffn1_w_out[l])

        n = rms_norm(h, mix_norm[l])
        (su, sv, fq, fk, fv, ff, gq, gk, gv, ga, gr, gates) = _split_cols(n @ w_in[l], _in_widths())
        y_a = sgu_branch(su, sv, sgu_ln_g[l], sgu_ln_b[l], sgu_w_s[l], sgu_b_s[l]) @ w_branch_a[l]
        y_b = fox_branch(fq, fk, fv, ff, fox_b_f[l]) @ w_branch_b[l]
        y_c = gla_branch(gq, gk, gv, ga, gr, gla_w_gate[l], gla_b_gate[l], gla_o_norm[l]) @ w_branch_c[l]
        g_a, g_b, g_c = jnp.split(jax.nn.sigmoid(gates), N_BRANCH, axis=-1)
        h = h + (g_a * y_a + g_b * y_b + g_c * y_c) @ w_out[l]

        h = h + cross_attention(rms_norm(h, xa_norm[l]), rms_norm(mem, mem_norm[l]),
                                xa_w_q[l], xa_w_kv[l], xa_w_o[l])

        h = h + 0.5 * swiglu_ffn(rms_norm(h, ffn2_norm[l]), ffn2_w_in[l], ffn2_w_out[l])
    return rms_norm(h, final_norm)
```

```python
import functools

import jax
import jax.numpy as jnp
from jax import lax
from jax.experimental import pallas as pl
from jax.experimental.pallas import tpu as pltpu

F32 = jnp.float32
BF16 = jnp.bfloat16

D_MODEL = 2048
DEPTH = 2
D_FF = 5632
EPS = 1e-6

SGU_GROUPS = 4
SGU_GROUP_DIM = 256
SGU_WIDTH = 1024
SGU_CHUNK = 128

FOX_HEADS = 8
FOX_HEAD_DIM = 128
FOX_WIDTH = 1024

GLA_HEADS = 4
GLA_DK = 128
GLA_DV = 256
GLA_KW = 512
GLA_VW = 1024
GLA_GATE_RANK = 16
GLA_GATE_TAU = 16.0
GLA_CHUNK = 64

XA_HEADS = 4
XA_HEAD_DIM = 128
XA_WIDTH = 512

LANE = 128

SEG_SU, SEG_SV = 0, 8
SEG_FQ, SEG_FK, SEG_FV = 16, 24, 32
SEG_GQ, SEG_GK = 40, 44
SEG_GV, SEG_GR = 48, 56
SEG_GATES = 64
N_MAIN = 112 * LANE
SMALL_FF = 0
SMALL_GA = 8

NEG_BIG = -1e30
VMEM_LIMIT = 56 * 1024 * 1024


def _params(sem):
    return pltpu.CompilerParams(dimension_semantics=sem, vmem_limit_bytes=VMEM_LIMIT)


def _rms(x, g):
    return x * lax.rsqrt(jnp.mean(x * x, axis=-1, keepdims=True) + EPS) * g


def _log_sigmoid(x):
    return jnp.minimum(x, 0.0) - jnp.log1p(jnp.exp(-jnp.abs(x)))


def _gelu_tanh(x):
    c = 0.7978845608028654
    return x * (0.5 * (1.0 + jnp.tanh(c * (x + 0.044715 * (x * x * x)))))


def _dot(a, b):
    return jnp.dot(a, b, preferred_element_type=F32)


def _dot_nt(a, b):
    return lax.dot_general(a, b, (((1,), (1,)), ((), ())), preferred_element_type=F32)


def _dot_tn(a, b):
    return lax.dot_general(a, b, (((0,), (0,)), ((), ())), preferred_element_type=F32)


def _ffn_kernel(*refs, final):
    if final:
        x_ref, g_ref, wg_ref, wu_ref, wo_ref, fg_ref, o_ref, xn_ref = refs
    else:
        x_ref, g_ref, wg_ref, wu_ref, wo_ref, o_ref, xn_ref = refs
    j = pl.program_id(1)

    @pl.when(j == 0)
    def _():
        x = x_ref[...]
        xn_ref[...] = _rms(x, g_ref[...]).astype(BF16)
        o_ref[...] = x

    xn = xn_ref[...]
    gate = _dot(xn, wg_ref[...])
    up = _dot(xn, wu_ref[...])
    act = (gate * jax.nn.sigmoid(gate) * up * 0.5).astype(BF16)
    o_ref[...] += _dot(act, wo_ref[...])

    if final:
        @pl.when(j == pl.num_programs(1) - 1)
        def _():
            o_ref[...] = _rms(o_ref[...], fg_ref[...])


def _ffn(h, norm_g, w_in, w_out, final_g=None, *, tm=512, tf=512):
    m, d = h.shape
    f = w_out.shape[0]
    nj = f // tf
    final = final_g is not None
    in_specs = [
        pl.BlockSpec((tm, d), lambda i, j: (i, 0)),
        pl.BlockSpec((1, d), lambda i, j: (0, 0)),
        pl.BlockSpec((d, tf), lambda i, j: (0, j)),
        pl.BlockSpec((d, tf), lambda i, j: (0, j + nj)),
        pl.BlockSpec((tf, d), lambda i, j: (j, 0)),
    ]
    args = [h, norm_g.reshape(1, d), w_in, w_in, w_out]
    if final:
        in_specs.append(pl.BlockSpec((1, d), lambda i, j: (0, 0)))
        args.append(final_g.reshape(1, d))
    return pl.pallas_call(
        functools.partial(_ffn_kernel, final=final),
        grid=(m // tm, nj),
        in_specs=in_specs,
        out_specs=pl.BlockSpec((tm, d), lambda i, j: (i, 0)),
        out_shape=jax.ShapeDtypeStruct((m, d), F32),
        scratch_shapes=[pltpu.VMEM((tm, d), BF16)],
        compiler_params=_params(("parallel", "arbitrary")),
        name="ffn",
    )(*args)


def _proj_kernel(*refs, small):
    if small:
        x_ref, g_ref, w_ref, ws_ref, o_ref, os_ref, xn_ref = refs
    else:
        x_ref, g_ref, w_ref, o_ref, xn_ref = refs

    @pl.when(pl.program_id(1) == 0)
    def _():
        xn = _rms(x_ref[...], g_ref[...]).astype(BF16)
        xn_ref[...] = xn
        if small:
            os_ref[...] = _dot(xn, ws_ref[...])

    o_ref[...] = _dot(xn_ref[...], w_ref[...])


def _proj(x, norm_g, w, w_small=None, *, tm=1024, tn=1024):
    m, d = x.shape
    n = w.shape[1]
    small = w_small is not None
    in_specs = [
        pl.BlockSpec((tm, d), lambda i, j: (i, 0)),
        pl.BlockSpec((1, d), lambda i, j: (0, 0)),
        pl.BlockSpec((d, tn), lambda i, j: (0, j)),
    ]
    args = [x, norm_g.reshape(1, d), w]
    out_specs = [pl.BlockSpec((tm, tn), lambda i, j: (i, j))]
    out_shape = [jax.ShapeDtypeStruct((m, n), F32)]
    if small:
        in_specs.append(pl.BlockSpec((d, LANE), lambda i, j: (0, 0)))
        args.append(w_small)
        out_specs.append(pl.BlockSpec((tm, LANE), lambda i, j: (i, 0)))
        out_shape.append(jax.ShapeDtypeStruct((m, LANE), F32))
    out = pl.pallas_call(
        functools.partial(_proj_kernel, small=small),
        grid=(m // tm, n // tn),
        in_specs=in_specs,
        out_specs=out_specs,
        out_shape=out_shape,
        scratch_shapes=[pltpu.VMEM((tm, d), BF16)],
        compiler_params=_params(("parallel", "arbitrary")),
        name="proj",
    )(*args)
    return out if small else out[0]


def _sgu_kernel(u_ref, v_ref, lng_ref, lnb_ref, ws_ref, bs_ref, o_ref, *, chunks):
    t = SGU_CHUNK
    gd = SGU_GROUP_DIM
    row = lax.broadcasted_iota(jnp.int32, (t, t), 0)
    col = lax.broadcasted_iota(jnp.int32, (t, t), 1)
    causal = row >= col
    for g in range(SGU_GROUPS):
        w = jnp.where(causal, ws_ref[g], 0.0).astype(BF16)
        bias = bs_ref[:, g:g + 1]
        ln_g = lng_ref[:, g * gd:(g + 1) * gd]
        ln_b = lnb_ref[:, g * gd:(g + 1) * gd]
        for c in range(chunks):
            rows = slice(c * t, (c + 1) * t)
            cols = slice(g * gd, (g + 1) * gd)
            v = _gelu_tanh(v_ref[rows, cols])
            mu = jnp.mean(v, axis=-1, keepdims=True)
            vc = v - mu
            var = jnp.mean(vc * vc, axis=-1, keepdims=True)
            vn = vc * lax.rsqrt(var + EPS) * ln_g + ln_b
            mixed = _dot(w, vn.astype(BF16)) + bias
            o_ref[rows, cols] = (_gelu_tanh(u_ref[rows, cols]) * mixed).astype(o_ref.dtype)


def _sgu(z, ln_g, ln_b, w_s, b_s, *, chunks=4):
    m = z.shape[0]
    rows = chunks * SGU_CHUNK
    w = SGU_WIDTH
    return pl.pallas_call(
        functools.partial(_sgu_kernel, chunks=chunks),
        grid=(m // rows,),
        in_specs=[
            pl.BlockSpec((rows, w), lambda i: (i, SEG_SU * LANE // w)),
            pl.BlockSpec((rows, w), lambda i: (i, SEG_SV * LANE // w)),
            pl.BlockSpec((1, w), lambda i: (0, 0)),
            pl.BlockSpec((1, w), lambda i: (0, 0)),
            pl.BlockSpec((SGU_GROUPS, SGU_CHUNK, SGU_CHUNK), lambda i: (0, 0, 0)),
            pl.BlockSpec((SGU_CHUNK, SGU_GROUPS), lambda i: (0, 0)),
        ],
        out_specs=pl.BlockSpec((rows, w), lambda i: (i, 0)),
        out_shape=jax.ShapeDtypeStruct((m, w), BF16),
        compiler_params=_params(("parallel",)),
        name="sgu",
    )(z, z, ln_g.reshape(1, w), ln_b.reshape(1, w), w_s, jnp.swapaxes(b_s, 0, 1))


def _fox_gate_kernel(f_ref, bf_ref, cum_ref, cumt_ref):
    t = LANE
    s = f_ref.shape[0]
    row = lax.broadcasted_iota(jnp.int32, (t, t), 0)
    col = lax.broadcasted_iota(jnp.int32, (t, t), 1)
    tril = jnp.where(row >= col, 1.0, 0.0).astype(F32)
    carry = jnp.zeros((1, t), F32)
    for b in range(s // t):
        rows = slice(b * t, (b + 1) * t)
        lf = _log_sigmoid(f_ref[rows, :] + bf_ref[...])
        cum = jnp.dot(tril, lf, preferred_element_type=F32,
                      precision=lax.Precision.HIGHEST) + carry
        carry = cum[t - 1:t, :]
        cum_ref[rows, :] = cum
        cumt_ref[:, rows] = cum.T[:FOX_HEADS, :]


def _fox_gate(small, b_f, batch, seq):
    bf = jnp.zeros((1, LANE), F32).at[0, SMALL_FF:SMALL_FF + FOX_HEADS].set(b_f)
    return pl.pallas_call(
        _fox_gate_kernel,
        grid=(batch,),
        in_specs=[
            pl.BlockSpec((seq, LANE), lambda b: (b, 0)),
            pl.BlockSpec((1, LANE), lambda b: (0, 0)),
        ],
        out_specs=[
            pl.BlockSpec((seq, LANE), lambda b: (b, 0)),
            pl.BlockSpec((None, FOX_HEADS, seq), lambda b: (b, 0, 0)),
        ],
        out_shape=[
            jax.ShapeDtypeStruct((batch * seq, LANE), F32),
            jax.ShapeDtypeStruct((batch, FOX_HEADS, seq), F32),
        ],
        compiler_params=_params(("parallel",)),
        name="fox_gate",
    )(small, bf)


def _fox_kernel(q_ref, k_ref, v_ref, cq_ref, ck_ref, o_ref, *, tq, tk):
    h = pl.program_id(1)
    i = pl.program_id(2)
    dh = FOX_HEAD_DIM
    q = (q_ref[...] * (dh ** -0.5)).astype(BF16)
    lane = lax.broadcasted_iota(jnp.int32, (tq, LANE), 1)
    cq = jnp.sum(jnp.where(lane == h, cq_ref[...], 0.0), axis=1, keepdims=True)
    row = i * tq + lax.broadcasted_iota(jnp.int32, (tq, tk), 0)
    col0 = lax.broadcasted_iota(jnp.int32, (tq, tk), 1)

    def body(j, carry):
        m, l, acc = carry
        start = pl.multiple_of(j * tk, tk)
        kb = k_ref[pl.ds(start, tk), :].astype(BF16)
        vb = v_ref[pl.ds(start, tk), :].astype(BF16)
        s = _dot_nt(q, kb)
        s = s + cq - ck_ref[:, pl.ds(start, tk)]
        s = jnp.where(row >= col0 + start, s, NEG_BIG)
        m_new = jnp.maximum(m, jnp.max(s, axis=1, keepdims=True))
        p = jnp.exp(s - m_new)
        alpha = jnp.exp(m - m_new)
        l = alpha * l + jnp.sum(p, axis=1, keepdims=True)
        acc = alpha * acc + _dot(p.astype(BF16), vb)
        return m_new, l, acc

    n_blocks = ((i + 1) * tq + tk - 1) // tk
    init = (jnp.full((tq, 1), NEG_BIG, F32), jnp.zeros((tq, 1), F32), jnp.zeros((tq, dh), F32))
    _, l, acc = lax.fori_loop(0, n_blocks, body, init)
    o_ref[...] = (acc / l).astype(o_ref.dtype)


def _fox(z, cum, cum_t, batch, seq, *, tq=256, tk=256):
    m = z.shape[0]
    dh = FOX_HEAD_DIM
    nq = seq // tq
    cum_t = cum_t.reshape(batch, FOX_HEADS, 1, seq)
    return pl.pallas_call(
        functools.partial(_fox_kernel, tq=tq, tk=tk),
        grid=(batch, FOX_HEADS, nq),
        in_specs=[
            pl.BlockSpec((tq, dh), lambda b, h, i: (b * nq + i, SEG_FQ + h)),
            pl.BlockSpec((seq, dh), lambda b, h, i: (b, SEG_FK + h)),
            pl.BlockSpec((seq, dh), lambda b, h, i: (b, SEG_FV + h)),
            pl.BlockSpec((tq, LANE), lambda b, h, i: (b * nq + i, 0)),
            pl.BlockSpec((None, None, 1, seq), lambda b, h, i: (b, h, 0, 0)),
        ],
        out_specs=pl.BlockSpec((tq, dh), lambda b, h, i: (b * nq + i, h)),
        out_shape=jax.ShapeDtypeStruct((m, FOX_WIDTH), BF16),
        compiler_params=_params(("parallel", "parallel", "arbitrary")),
        name="fox",
    )(z, z, z, cum, cum_t)


def _gla_kernel(q_ref, k_ref, v_ref, r_ref, a_ref, wg_ref, bg_ref, on_ref, o_ref,
                g_ref, st_ref):
    c = GLA_CHUNK
    dk, dv = GLA_DK, GLA_DV
    seq = q_ref.shape[0]
    gl = _dot(a_ref[...].astype(BF16), wg_ref[...]) + bg_ref[...]
    g_ref[...] = _log_sigmoid(gl) / GLA_GATE_TAU
    st_ref[...] = jnp.zeros_like(st_ref)

    row = lax.broadcasted_iota(jnp.int32, (c, c), 0)
    col = lax.broadcasted_iota(jnp.int32, (c, c), 1)
    causal = row >= col
    tril = jnp.where(causal, 1.0, 0.0).astype(F32)
    o_gain = on_ref[...]

    def chunk(n, carry):
        r0 = pl.multiple_of(n * c, c)
        rows = pl.ds(r0, c)
        bc = jnp.dot(tril, g_ref[rows, :], preferred_element_type=F32,
                     precision=lax.Precision.HIGHEST)
        b_last = bc[c - 1:c, :]
        b_mid = bc[c // 2:c // 2 + 1, :]
        qc = q_ref[rows, :] * (dk ** -0.5)
        kc = k_ref[rows, :]
        vc = v_ref[rows, :].astype(BF16)
        att = _dot_nt((qc * jnp.exp(bc - b_mid)).astype(BF16),
                      (kc * jnp.exp(b_mid - bc)).astype(BF16))
        att = jnp.where(causal, att, 0.0)
        o = _dot(att.astype(BF16), vc)
        st = st_ref[...]
        o = o + _dot_nt((qc * jnp.exp(bc)).astype(BF16), st.astype(BF16))
        upd_t = _dot_tn(vc, (kc * jnp.exp(b_last - bc)).astype(BF16))
        st_ref[...] = st * jnp.exp(b_last) + upd_t
        o = _rms(o, o_gain)
        rg = r_ref[rows, :]
        o_ref[rows, :] = (o * (rg * jax.nn.sigmoid(rg))).astype(o_ref.dtype)
        return carry

    lax.fori_loop(0, seq // c, chunk, 0)


def _gla(z, small, w_gate_emb, b_gate, o_norm, batch, seq):
    m = z.shape[0]
    dk, dv = GLA_DK, GLA_DV
    return pl.pallas_call(
        _gla_kernel,
        grid=(batch, GLA_HEADS),
        in_specs=[
            pl.BlockSpec((seq, dk), lambda b, h: (b, SEG_GQ + h)),
            pl.BlockSpec((seq, dk), lambda b, h: (b, SEG_GK + h)),
            pl.BlockSpec((seq, dv), lambda b, h: (b, SEG_GV * LANE // dv + h)),
            pl.BlockSpec((seq, dv), lambda b, h: (b, SEG_GR * LANE // dv + h)),
            pl.BlockSpec((seq, LANE), lambda b, h: (b, 0)),
            pl.BlockSpec((LANE, dk), lambda b, h: (0, h)),
            pl.BlockSpec((1, dk), lambda b, h: (0, h)),
            pl.BlockSpec((1, dv), lambda b, h: (0, h)),
        ],
        out_specs=pl.BlockSpec((seq, dv), lambda b, h: (b, h)),
        out_shape=jax.ShapeDtypeStruct((m, GLA_VW), BF16),
        scratch_shapes=[pltpu.VMEM((seq, dk), F32), pltpu.VMEM((dv, dk), F32)],
        compiler_params=_params(("parallel", "arbitrary")),
        name="gla",
    )(z, z, z, z, small, w_gate_emb, b_gate.reshape(1, GLA_KW), o_norm.reshape(1, GLA_VW))


def _merge_kernel(a_ref, b_ref, c_ref, ga_ref, gb_ref, gc_ref, wa_ref, wb_ref, wc_ref,
                  wo_ref, h_ref, o_ref):
    @pl.when(pl.program_id(1) == 0)
    def _():
        o_ref[...] = h_ref[...]

    mix = (jax.nn.sigmoid(ga_ref[...]) * _dot(a_ref[...], wa_ref[...])
           + jax.nn.sigmoid(gb_ref[...]) * _dot(b_ref[...], wb_ref[...])
           + jax.nn.sigmoid(gc_ref[...]) * _dot(c_ref[...], wc_ref[...]))
    o_ref[...] += _dot(mix.astype(BF16), wo_ref[...])


def _merge(ya, yb, yc, z, wa, wb, wc, wo, h, *, tm=512, tn=512):
    m, d = h.shape
    kw = ya.shape[1]
    nj = d // tn
    g0 = SEG_GATES * LANE // tn
    branch = pl.BlockSpec((tm, kw), lambda i, j: (i, 0))
    wspec = pl.BlockSpec((kw, tn), lambda i, j: (0, j))
    return pl.pallas_call(
        _merge_kernel,
        grid=(m // tm, nj),
        in_specs=[
            branch, branch, branch,
            pl.BlockSpec((tm, tn), lambda i, j: (i, g0 + j)),
            pl.BlockSpec((tm, tn), lambda i, j: (i, g0 + nj + j)),
            pl.BlockSpec((tm, tn), lambda i, j: (i, g0 + 2 * nj + j)),
            wspec, wspec, wspec,
            pl.BlockSpec((tn, d), lambda i, j: (j, 0)),
            pl.BlockSpec((tm, d), lambda i, j: (i, 0)),
        ],
        out_specs=pl.BlockSpec((tm, d), lambda i, j: (i, 0)),
        out_shape=jax.ShapeDtypeStruct((m, d), F32),
        compiler_params=_params(("parallel", "arbitrary")),
        name="merge",
    )(ya, yb, yc, z, z, z, wa, wb, wc, wo, h)


def _xattn_kernel(h_ref, g_ref, wq_ref, kv_ref, wo_ref, o_ref):
    dh = XA_HEAD_DIM
    x = h_ref[...]
    n = _rms(x, g_ref[...]).astype(BF16)
    q = _dot(n, wq_ref[...]) * (dh ** -0.5)
    outs = []
    for hd in range(XA_HEADS):
        qh = q[:, hd * dh:(hd + 1) * dh].astype(BF16)
        kh = kv_ref[:, hd * dh:(hd + 1) * dh].astype(BF16)
        vh = kv_ref[:, XA_WIDTH + hd * dh:XA_WIDTH + (hd + 1) * dh].astype(BF16)
        s = _dot_nt(qh, kh)
        e = jnp.exp(s - jnp.max(s, axis=-1, keepdims=True))
        p = e / jnp.sum(e, axis=-1, keepdims=True)
        outs.append(_dot(p.astype(BF16), vh))
    o = jnp.concatenate(outs, axis=1).astype(BF16)
    o_ref[...] = x + _dot(o, wo_ref[...])


def _xattn(h, norm_g, wq, kv, wo, seq, mem_len, *, tm=512):
    m, d = h.shape
    per_batch = seq // tm
    return pl.pallas_call(
        _xattn_kernel,
        grid=(m // tm,),
        in_specs=[
            pl.BlockSpec((tm, d), lambda i: (i, 0)),
            pl.BlockSpec((1, d), lambda i: (0, 0)),
            pl.BlockSpec((d, XA_WIDTH), lambda i: (0, 0)),
            pl.BlockSpec((mem_len, 2 * XA_WIDTH), lambda i: (i // per_batch, 0)),
            pl.BlockSpec((XA_WIDTH, d), lambda i: (0, 0)),
        ],
        out_specs=pl.BlockSpec((tm, d), lambda i: (i, 0)),
        out_shape=jax.ShapeDtypeStruct((m, d), F32),
        compiler_params=_params(("parallel",)),
        name="xattn",
    )(h, norm_g.reshape(1, d), wq, kv, wo)


def _repack_w_in(w):
    o_ff = 2 * SGU_WIDTH + 3 * FOX_WIDTH
    o_gq = o_ff + FOX_HEADS
    o_ga = o_gq + 2 * GLA_KW + GLA_VW
    o_gr = o_ga + GLA_GATE_RANK
    main = jnp.concatenate([w[:, :o_ff], w[:, o_gq:o_ga], w[:, o_gr:]], axis=1).astype(BF16)
    small = jnp.zeros((w.shape[0], LANE), BF16)
    small = small.at[:, SMALL_FF:SMALL_FF + FOX_HEADS].set(w[:, o_ff:o_gq].astype(BF16))
    small = small.at[:, SMALL_GA:SMALL_GA + GLA_GATE_RANK].set(w[:, o_ga:o_gr].astype(BF16))
    return main, small


def _embed_w_gate(w_gate):
    emb = jnp.zeros((LANE, GLA_KW), BF16)
    return emb.at[SMALL_GA:SMALL_GA + GLA_GATE_RANK, :].set(w_gate.astype(BF16))


def kernel(x, mem, ffn1_norm, ffn1_w_in, ffn1_w_out, mix_norm, w_in, sgu_ln_g, sgu_ln_b, sgu_w_s, sgu_b_s, fox_b_f, gla_w_gate, gla_b_gate, gla_o_norm, w_branch_a, w_branch_b, w_branch_c, w_out, xa_norm, mem_norm, xa_w_q, xa_w_kv, xa_w_o, ffn2_norm, ffn2_w_in, ffn2_w_out, final_norm):
    batch, seq, d = x.shape
    mem_len = mem.shape[1]
    h = x.reshape(batch * seq, d)
    mem2 = mem.reshape(batch * mem_len, d)
    for l in range(DEPTH):
        h = _ffn(h, ffn1_norm[l], ffn1_w_in[l].astype(BF16), ffn1_w_out[l].astype(BF16))

        w_main, w_small = _repack_w_in(w_in[l])
        z, small = _proj(h, mix_norm[l], w_main, w_small)
        ya = _sgu(z, sgu_ln_g[l], sgu_ln_b[l], sgu_w_s[l], sgu_b_s[l])
        cum, cum_t = _fox_gate(small, fox_b_f[l], batch, seq)
        yb = _fox(z, cum, cum_t, batch, seq)
        yc = _gla(z, small, _embed_w_gate(gla_w_gate[l]), gla_b_gate[l], gla_o_norm[l], batch, seq)
        h = _merge(ya, yb, yc, z, w_branch_a[l].astype(BF16), w_branch_b[l].astype(BF16),
                   w_branch_c[l].astype(BF16), w_out[l].astype(BF16), h)

        kv = _proj(mem2, mem_norm[l], xa_w_kv[l].astype(BF16))
        h = _xattn(h, xa_norm[l], xa_w_q[l].astype(BF16), kv, xa_w_o[l].astype(BF16), seq, mem_len)

        final_g = final_norm if l == DEPTH - 1 else None
        h = _ffn(h, ffn2_norm[l], ffn2_w_in[l].astype(BF16), ffn2_w_out[l].astype(BF16), final_g)
    return h.reshape(batch, seq, d)
```

```python
import functools

import jax
import jax.numpy as jnp
from jax import lax
from jax.experimental import pallas as pl
from jax.experimental.pallas import tpu as pltpu

F32 = jnp.float32
BF16 = jnp.bfloat16

D_MODEL = 2048
DEPTH = 2
D_FF = 5632
EPS = 1e-6

SGU_GROUPS = 4
SGU_GROUP_DIM = 256
SGU_WIDTH = 1024
SGU_CHUNK = 128

FOX_HEADS = 8
FOX_HEAD_DIM = 128
FOX_WIDTH = 1024

GLA_HEADS = 4
GLA_DK = 128
GLA_DV = 256
GLA_KW = 512
GLA_VW = 1024
GLA_GATE_RANK = 16
GLA_GATE_TAU = 16.0
GLA_CHUNK = 64

XA_HEADS = 4
XA_HEAD_DIM = 128
XA_WIDTH = 512

LANE = 128

ZF_SU, ZF_SV, ZF_GQ, ZF_GK, ZF_GR, ZF_GATES = 0, 8, 16, 20, 24, 32
ZF_WIDTH = 80 * LANE
ZB_FQ, ZB_FK, ZB_FV, ZB_GV = 0, 8, 16, 24
ZB_WIDTH = 32 * LANE
SMALL_FF = 0
SMALL_GA = 8

NEG_BIG = -1e30
VMEM_LIMIT = 56 * 1024 * 1024


def _params(sem):
    return pltpu.CompilerParams(dimension_semantics=sem, vmem_limit_bytes=VMEM_LIMIT)


def _rms(x, g):
    return x * lax.rsqrt(jnp.mean(x * x, axis=-1, keepdims=True) + EPS) * g


def _log_sigmoid(x):
    return jnp.minimum(x, 0.0) - jnp.log1p(jnp.exp(-jnp.abs(x)))


def _gelu_tanh(x):
    c = 0.7978845608028654
    return x * (0.5 * (1.0 + jnp.tanh(c * (x + 0.044715 * (x * x * x)))))


def _silu(x):
    return x * jax.nn.sigmoid(x)


def _dot(a, b):
    return jnp.dot(a, b, preferred_element_type=F32)


def _dot_nt(a, b):
    return lax.dot_general(a, b, (((1,), (1,)), ((), ())), preferred_element_type=F32)


def _dot_tn(a, b):
    return lax.dot_general(a, b, (((0,), (0,)), ((), ())), preferred_element_type=F32)


def _split3(x):
    hi = x.astype(BF16)
    r = x - hi.astype(F32)
    lo = r.astype(BF16)
    lo2 = (r - lo.astype(F32)).astype(BF16)
    return hi, lo, lo2


def _dot_01(mat3, x):
    return _dot(mat3, jnp.concatenate(_split3(x), axis=0))


def _ffn_kernel(*refs, mode):
    if mode == "plain":
        x_ref, g_ref, wg_ref, wu_ref, wo_ref, o_ref, xn_ref = refs
    elif mode == "final":
        x_ref, g_ref, wg_ref, wu_ref, wo_ref, ng_ref, o_ref, xn_ref = refs
    else:
        x_ref, g_ref, wg_ref, wu_ref, wo_ref, ng_ref, o_ref, on_ref, xn_ref = refs
    j = pl.program_id(1)

    @pl.when(j == 0)
    def _():
        x = x_ref[...]
        xn_ref[...] = _rms(x, g_ref[...]).astype(BF16)
        o_ref[...] = x

    xn = xn_ref[...]
    gate = _dot(xn, wg_ref[...])
    up = _dot(xn, wu_ref[...])
    act = (_silu(gate) * up * 0.5).astype(BF16)
    o_ref[...] += _dot(act, wo_ref[...])

    if mode != "plain":
        @pl.when(j == pl.num_programs(1) - 1)
        def _():
            y = _rms(o_ref[...], ng_ref[...])
            if mode == "final":
                o_ref[...] = y
            else:
                on_ref[...] = y.astype(BF16)


def _ffn(h, norm_g, w_in, w_out, next_g=None, *, mode="plain", tm=512, tf=512):
    m, d = h.shape
    f = w_out.shape[0]
    nj = f // tf
    vec = pl.BlockSpec((1, d), lambda i, j: (0, 0))
    rows = pl.BlockSpec((tm, d), lambda i, j: (i, 0))
    in_specs = [
        rows, vec,
        pl.BlockSpec((d, tf), lambda i, j: (0, j)),
        pl.BlockSpec((d, tf), lambda i, j: (0, j + nj)),
        pl.BlockSpec((tf, d), lambda i, j: (j, 0)),
    ]
    args = [h, norm_g.reshape(1, d), w_in, w_in, w_out]
    out_specs = [rows]
    out_shape = [jax.ShapeDtypeStruct((m, d), F32)]
    if mode != "plain":
        in_specs.append(vec)
        args.append(next_g.reshape(1, d))
    if mode == "emit_norm":
        out_specs.append(rows)
        out_shape.append(jax.ShapeDtypeStruct((m, d), BF16))
    out = pl.pallas_call(
        functools.partial(_ffn_kernel, mode=mode),
        grid=(m // tm, nj),
        in_specs=in_specs,
        out_specs=out_specs,
        out_shape=out_shape,
        scratch_shapes=[pltpu.VMEM((tm, d), BF16)],
        compiler_params=_params(("parallel", "arbitrary")),
        name="ffn",
    )(*args)
    return out if mode == "emit_norm" else out[0]


def _proj_kernel(*refs, small):
    if small:
        x_ref, w_ref, cs_ref, ws_ref, o_ref, os_ref = refs

        @pl.when(pl.program_id(1) == 0)
        def _():
            os_ref[...] = _dot(x_ref[...], ws_ref[...])
    else:
        x_ref, w_ref, cs_ref, o_ref = refs
    o_ref[...] = (_dot(x_ref[...], w_ref[...]) * cs_ref[...]).astype(o_ref.dtype)


def _proj(xn, w, col_scale, out_dtype, w_small=None, *, tm=1024, tn=2048):
    m, d = xn.shape
    n = w.shape[1]
    small = w_small is not None
    in_specs = [
        pl.BlockSpec((tm, d), lambda i, j: (i, 0)),
        pl.BlockSpec((d, tn), lambda i, j: (0, j)),
        pl.BlockSpec((1, tn), lambda i, j: (0, j)),
    ]
    args = [xn, w, col_scale]
    out_specs = [pl.BlockSpec((tm, tn), lambda i, j: (i, j))]
    out_shape = [jax.ShapeDtypeStruct((m, n), out_dtype)]
    if small:
        in_specs.append(pl.BlockSpec((d, LANE), lambda i, j: (0, 0)))
        args.append(w_small)
        out_specs.append(pl.BlockSpec((tm, LANE), lambda i, j: (i, 0)))
        out_shape.append(jax.ShapeDtypeStruct((m, LANE), F32))
    out = pl.pallas_call(
        functools.partial(_proj_kernel, small=small),
        grid=(m // tm, n // tn),
        in_specs=in_specs,
        out_specs=out_specs,
        out_shape=out_shape,
        compiler_params=_params(("parallel", "arbitrary")),
        name="proj",
    )(*args)
    return out if small else out[0]


def _norm_proj_kernel(x_ref, g_ref, w_ref, o_ref):
    o_ref[...] = _dot(_rms(x_ref[...], g_ref[...]).astype(BF16), w_ref[...])


def _norm_proj(x, norm_g, w, *, tm=1024):
    m, d = x.shape
    n = w.shape[1]
    return pl.pallas_call(
        _norm_proj_kernel,
        grid=(m // tm,),
        in_specs=[
            pl.BlockSpec((tm, d), lambda i: (i, 0)),
            pl.BlockSpec((1, d), lambda i: (0, 0)),
            pl.BlockSpec((d, n), lambda i: (0, 0)),
        ],
        out_specs=pl.BlockSpec((tm, n), lambda i: (i, 0)),
        out_shape=jax.ShapeDtypeStruct((m, n), F32),
        compiler_params=_params(("parallel",)),
        name="mem_kv",
    )(x, norm_g.reshape(1, d), w)


def _sgu_kernel(u_ref, v_ref, lng_ref, lnb_ref, ws_ref, bs_ref, o_ref, *, chunks):
    t = SGU_CHUNK
    gd = SGU_GROUP_DIM
    row = lax.broadcasted_iota(jnp.int32, (t, t), 0)
    col = lax.broadcasted_iota(jnp.int32, (t, t), 1)
    causal = row >= col
    for g in range(SGU_GROUPS):
        w = jnp.where(causal, ws_ref[g], 0.0).astype(BF16)
        bias = bs_ref[:, g:g + 1]
        ln_g = lng_ref[:, g * gd:(g + 1) * gd]
        ln_b = lnb_ref[:, g * gd:(g + 1) * gd]
        for c in range(chunks):
            rows = slice(c * t, (c + 1) * t)
            cols = slice(g * gd, (g + 1) * gd)
            v = _gelu_tanh(v_ref[rows, cols])
            mu = jnp.mean(v, axis=-1, keepdims=True)
            vc = v - mu
            var = jnp.mean(vc * vc, axis=-1, keepdims=True)
            vn = vc * lax.rsqrt(var + EPS) * ln_g + ln_b
            mixed = _dot(w, vn.astype(BF16)) + bias
            o_ref[rows, cols] = (_gelu_tanh(u_ref[rows, cols]) * mixed).astype(o_ref.dtype)


def _sgu(zf, ln_g, ln_b, w_s, b_s, *, chunks=4):
    m = zf.shape[0]
    rows = chunks * SGU_CHUNK
    w = SGU_WIDTH
    return pl.pallas_call(
        functools.partial(_sgu_kernel, chunks=chunks),
        grid=(m // rows,),
        in_specs=[
            pl.BlockSpec((rows, w), lambda i: (i, ZF_SU * LANE // w)),
            pl.BlockSpec((rows, w), lambda i: (i, ZF_SV * LANE // w)),
            pl.BlockSpec((1, w), lambda i: (0, 0)),
            pl.BlockSpec((1, w), lambda i: (0, 0)),
            pl.BlockSpec((SGU_GROUPS, SGU_CHUNK, SGU_CHUNK), lambda i: (0, 0, 0)),
            pl.BlockSpec((SGU_CHUNK, SGU_GROUPS), lambda i: (0, 0)),
        ],
        out_specs=pl.BlockSpec((rows, w), lambda i: (i, 0)),
        out_shape=jax.ShapeDtypeStruct((m, w), BF16),
        compiler_params=_params(("parallel",)),
        name="sgu",
    )(zf, zf, ln_g.reshape(1, w), ln_b.reshape(1, w), w_s, jnp.swapaxes(b_s, 0, 1))


def _fox_gate_kernel(f_ref, bf_ref, tril_ref, qa_ref, ka_ref):
    t = LANE
    seq = f_ref.shape[0]
    lane = lax.broadcasted_iota(jnp.int32, (t, LANE), 1)
    carry = jnp.zeros((1, t), F32)
    for b in range(seq // t):
        rows = slice(b * t, (b + 1) * t)
        lf = _log_sigmoid(f_ref[rows, :] + bf_ref[...])
        cum = _dot_01(tril_ref[...], lf) + carry
        carry = cum[t - 1:t, :]
        hi, lo, lo2 = (p.astype(F32) for p in _split3(cum))
        for h in range(FOX_HEADS):
            a, b2, c = hi[:, h:h + 1], lo[:, h:h + 1], lo2[:, h:h + 1]
            qa = jnp.where(lane == 0, a, jnp.where(lane == 1, b2, jnp.where(
                lane == 2, c, jnp.where(lane < 6, 1.0, 0.0))))
            ka = jnp.where(lane < 3, 1.0, jnp.where(lane == 3, -a, jnp.where(
                lane == 4, -b2, jnp.where(lane == 5, -c, 0.0))))
            cols = slice(h * LANE, (h + 1) * LANE)
            qa_ref[rows, cols] = qa.astype(BF16)
            ka_ref[rows, cols] = ka.astype(BF16)


def _tril3(n):
    tril = jnp.tril(jnp.ones((n, n), BF16))
    return jnp.concatenate([tril, tril, tril], axis=1)


def _fox_gate(small, b_f, batch, seq):
    bf = jnp.zeros((1, LANE), F32).at[0, SMALL_FF:SMALL_FF + FOX_HEADS].set(b_f)
    w = FOX_HEADS * LANE
    aug = jax.ShapeDtypeStruct((batch * seq, w), BF16)
    return pl.pallas_call(
        _fox_gate_kernel,
        grid=(batch,),
        in_specs=[
            pl.BlockSpec((seq, LANE), lambda b: (b, 0)),
            pl.BlockSpec((1, LANE), lambda b: (0, 0)),
            pl.BlockSpec((LANE, 3 * LANE), lambda b: (0, 0)),
        ],
        out_specs=[pl.BlockSpec((seq, w), lambda b: (b, 0))] * 2,
        out_shape=[aug, aug],
        compiler_params=_params(("parallel",)),
        name="fox_gate",
    )(small, bf, _tril3(LANE))


def _fox_kernel(q_ref, k_ref, v_ref, qa_ref, ka_ref, o_ref, *, t):
    seq, dh = q_ref.shape
    lane = lax.broadcasted_iota(jnp.int32, (t, LANE), 1)
    ones_col = jnp.where(lane == 0, 1.0, 0.0).astype(BF16)
    row = lax.broadcasted_iota(jnp.int32, (t, t), 0)
    col = lax.broadcasted_iota(jnp.int32, (t, t), 1)
    causal = row >= col

    def block(qc, j, carry):
        m, acc = carry
        rows = slice(j * t, (j + 1) * t)
        kc = jnp.concatenate([k_ref[rows, :], ka_ref[rows, :]], axis=1)
        vc = jnp.concatenate([v_ref[rows, :], ones_col], axis=1)
        s = _dot_nt(qc, kc)
        m_new = jnp.maximum(m, jnp.max(s, axis=1, keepdims=True))
        p = jnp.exp(s - m_new).astype(BF16)
        return m_new, jnp.exp(m - m_new) * acc + _dot(p, vc)

    for i in range(seq // t):
        rows = slice(i * t, (i + 1) * t)
        qc = jnp.concatenate([q_ref[rows, :], qa_ref[rows, :]], axis=1)
        kc = jnp.concatenate([k_ref[rows, :], ka_ref[rows, :]], axis=1)
        vc = jnp.concatenate([v_ref[rows, :], ones_col], axis=1)
        s = jnp.where(causal, _dot_nt(qc, kc), NEG_BIG)
        m = jnp.max(s, axis=1, keepdims=True)
        carry = (m, _dot(jnp.exp(s - m).astype(BF16), vc))
        for j in range(i):
            carry = block(qc, j, carry)
        acc = carry[1]
        o_ref[rows, :] = (acc[:, :dh] / acc[:, dh:dh + 1]).astype(o_ref.dtype)


def _fox(zb, qa, ka, batch, seq, *, t=512):
    m = zb.shape[0]
    dh = FOX_HEAD_DIM

    def head(seg):
        return pl.BlockSpec((seq, dh), lambda b, h: (b, seg + h))

    return pl.pallas_call(
        functools.partial(_fox_kernel, t=t),
        grid=(batch, FOX_HEADS),
        in_specs=[head(ZB_FQ), head(ZB_FK), head(ZB_FV), head(0), head(0)],
        out_specs=head(0),
        out_shape=jax.ShapeDtypeStruct((m, FOX_WIDTH), BF16),
        compiler_params=_params(("parallel", "parallel")),
        name="fox",
    )(zb, zb, zb, qa, ka)


GLA_BLOCK = 4 * GLA_CHUNK


def _gla_decay_kernel(a_ref, wg_ref, bg_ref, cm_ref, o_ref, *, blocks):
    t = GLA_BLOCK
    for b in range(blocks):
        rows = slice(b * t, (b + 1) * t)
        gl = _dot(a_ref[rows, :].astype(BF16), wg_ref[...]) + bg_ref[...]
        g = _log_sigmoid(gl) / GLA_GATE_TAU
        o_ref[rows, :] = _dot_01(cm_ref[...], g)


def _gla_decay(small, w_gate_emb, b_gate, *, blocks=4):
    m = small.shape[0]
    t = GLA_BLOCK
    rows = blocks * t
    idx = jnp.arange(t)
    same = (idx[:, None] // GLA_CHUNK) == (idx[None, :] // GLA_CHUNK)
    cm = jnp.where(same & (idx[:, None] >= idx[None, :]), 1.0, 0.0).astype(BF16)
    cm3 = jnp.concatenate([cm, cm, cm], axis=1)
    return pl.pallas_call(
        functools.partial(_gla_decay_kernel, blocks=blocks),
        grid=(m // rows,),
        in_specs=[
            pl.BlockSpec((rows, LANE), lambda i: (i, 0)),
            pl.BlockSpec((LANE, GLA_KW), lambda i: (0, 0)),
            pl.BlockSpec((1, GLA_KW), lambda i: (0, 0)),
            pl.BlockSpec((t, 3 * t), lambda i: (0, 0)),
        ],
        out_specs=pl.BlockSpec((rows, GLA_KW), lambda i: (i, 0)),
        out_shape=jax.ShapeDtypeStruct((m, GLA_KW), F32),
        compiler_params=_params(("parallel",)),
        name="gla_decay",
    )(small, w_gate_emb, b_gate.reshape(1, GLA_KW), cm3)


def _gla_kernel(q_ref, k_ref, v_ref, r_ref, bc_ref, on_ref, o_ref):
    c = GLA_CHUNK
    t = GLA_BLOCK
    dk = GLA_DK
    seq = q_ref.shape[0]
    st = jnp.zeros((GLA_DV, dk), F32)
    row = lax.broadcasted_iota(jnp.int32, (t, t), 0)
    col = lax.broadcasted_iota(jnp.int32, (t, t), 1)
    intra = (row >= col) & ((row // c) == (col // c))
    o_gain = on_ref[...]

    for n in range(seq // t):
        rows = slice(n * t, (n + 1) * t)
        bc = bc_ref[rows, :]
        qb = q_ref[rows, :] * (dk ** -0.5)
        kb = k_ref[rows, :]
        vb = v_ref[rows, :]
        chunks = [slice(i * c, (i + 1) * c) for i in range(t // c)]
        b_mid = jnp.concatenate(
            [jnp.broadcast_to(bc[s.start + c // 2:s.start + c // 2 + 1, :], (c, dk)) for s in chunks], axis=0)
        b_last = jnp.concatenate(
            [jnp.broadcast_to(bc[s.stop - 1:s.stop, :], (c, dk)) for s in chunks], axis=0)
        qd = (qb * jnp.exp(bc - b_mid)).astype(BF16)
        kd = (kb * jnp.exp(b_mid - bc)).astype(BF16)
        ku = (kb * jnp.exp(b_last - bc)).astype(BF16)
        qi = (qb * jnp.exp(bc)).astype(BF16)
        att = jnp.where(intra, _dot_nt(qd, kd), 0.0)
        o = _dot(att.astype(BF16), vb)
        inter = []
        for s in chunks:
            inter.append(_dot_nt(qi[s, :], st.astype(BF16)))
            st = st * jnp.exp(bc[s.stop - 1:s.stop, :]) + _dot_tn(vb[s, :], ku[s, :])
        o = _rms(o + jnp.concatenate(inter, axis=0), o_gain)
        o_ref[rows, :] = (o * _silu(r_ref[rows, :])).astype(o_ref.dtype)


def _gla(zf, zb, bc, o_norm, batch, seq):
    m = zf.shape[0]
    dk, dv = GLA_DK, GLA_DV
    return pl.pallas_call(
        _gla_kernel,
        grid=(batch, GLA_HEADS),
        in_specs=[
            pl.BlockSpec((seq, dk), lambda b, h: (b, ZF_GQ + h)),
            pl.BlockSpec((seq, dk), lambda b, h: (b, ZF_GK + h)),
            pl.BlockSpec((seq, dv), lambda b, h: (b, ZB_GV * LANE // dv + h)),
            pl.BlockSpec((seq, dv), lambda b, h: (b, ZF_GR * LANE // dv + h)),
            pl.BlockSpec((seq, dk), lambda b, h: (b, h)),
            pl.BlockSpec((1, dv), lambda b, h: (0, h)),
        ],
        out_specs=pl.BlockSpec((seq, dv), lambda b, h: (b, h)),
        out_shape=jax.ShapeDtypeStruct((m, GLA_VW), BF16),
        compiler_params=_params(("parallel", "parallel")),
        name="gla",
    )(zf, zf, zb, zf, bc, o_norm.reshape(1, GLA_VW))


def _merge_kernel(a_ref, b_ref, c_ref, ga_ref, gb_ref, gc_ref, wa_ref, wb_ref, wc_ref,
                  wo_ref, h_ref, o_ref):
    mix = (jax.nn.sigmoid(ga_ref[...]) * _dot(a_ref[...], wa_ref[...])
           + jax.nn.sigmoid(gb_ref[...]) * _dot(b_ref[...], wb_ref[...])
           + jax.nn.sigmoid(gc_ref[...]) * _dot(c_ref[...], wc_ref[...]))
    o_ref[...] = h_ref[...] + _dot(mix.astype(BF16), wo_ref[...])


def _merge(ya, yb, yc, zf, wa, wb, wc, wo, h, *, tm=256):
    m, d = h.shape
    kw = ya.shape[1]
    g0 = ZF_GATES * LANE // d
    branch = pl.BlockSpec((tm, kw), lambda i: (i, 0))
    resident = dict(pipeline_mode=pl.Buffered(1))
    wspec = pl.BlockSpec((kw, d), lambda i: (0, 0), **resident)
    return pl.pallas_call(
        _merge_kernel,
        grid=(m // tm,),
        in_specs=[
            branch, branch, branch,
            pl.BlockSpec((tm, d), lambda i: (i, g0)),
            pl.BlockSpec((tm, d), lambda i: (i, g0 + 1)),
            pl.BlockSpec((tm, d), lambda i: (i, g0 + 2)),
            wspec, wspec, wspec,
            pl.BlockSpec((d, d), lambda i: (0, 0), **resident),
            pl.BlockSpec((tm, d), lambda i: (i, 0)),
        ],
        out_specs=pl.BlockSpec((tm, d), lambda i: (i, 0)),
        out_shape=jax.ShapeDtypeStruct((m, d), F32),
        compiler_params=_params(("parallel",)),
        name="merge",
    )(ya, yb, yc, zf, zf, zf, wa, wb, wc, wo, h)


def _xattn_kernel(h_ref, g_ref, wq_ref, kv_ref, wo_ref, o_ref):
    dh = XA_HEAD_DIM
    x = h_ref[...]
    n = _rms(x, g_ref[...]).astype(BF16)
    q = _dot(n, wq_ref[...]) * (dh ** -0.5)
    outs = []
    for hd in range(XA_HEADS):
        qh = q[:, hd * dh:(hd + 1) * dh].astype(BF16)
        kh = kv_ref[:, hd * dh:(hd + 1) * dh].astype(BF16)
        vh = kv_ref[:, XA_WIDTH + hd * dh:XA_WIDTH + (hd + 1) * dh].astype(BF16)
        s = _dot_nt(qh, kh)
        e = jnp.exp(s - jnp.max(s, axis=-1, keepdims=True))
        p = e / jnp.sum(e, axis=-1, keepdims=True)
        outs.append(_dot(p.astype(BF16), vh))
    o = jnp.concatenate(outs, axis=1).astype(BF16)
    o_ref[...] = x + _dot(o, wo_ref[...])


def _xattn(h, norm_g, wq, kv, wo, seq, mem_len, *, tm=512):
    m, d = h.shape
    per_batch = seq // tm
    return pl.pallas_call(
        _xattn_kernel,
        grid=(m // tm,),
        in_specs=[
            pl.BlockSpec((tm, d), lambda i: (i, 0)),
            pl.BlockSpec((1, d), lambda i: (0, 0)),
            pl.BlockSpec((d, XA_WIDTH), lambda i: (0, 0)),
            pl.BlockSpec((mem_len, 2 * XA_WIDTH), lambda i: (i // per_batch, 0)),
            pl.BlockSpec((XA_WIDTH, d), lambda i: (0, 0)),
        ],
        out_specs=pl.BlockSpec((tm, d), lambda i: (i, 0)),
        out_shape=jax.ShapeDtypeStruct((m, d), F32),
        compiler_params=_params(("parallel",)),
        name="xattn",
    )(h, norm_g.reshape(1, d), wq, kv, wo)


def _repack_w_in(w):
    o_fq = 2 * SGU_WIDTH
    o_ff = o_fq + 3 * FOX_WIDTH
    o_gq = o_ff + FOX_HEADS
    o_gv = o_gq + 2 * GLA_KW
    o_ga = o_gv + GLA_VW
    o_gr = o_ga + GLA_GATE_RANK
    w_f = jnp.concatenate([w[:, :o_fq], w[:, o_gq:o_gv], w[:, o_gr:]], axis=1).astype(BF16)
    w_b = jnp.concatenate([w[:, o_fq:o_ff], w[:, o_gv:o_ga]], axis=1).astype(BF16)
    small = jnp.zeros((w.shape[0], LANE), BF16)
    small = small.at[:, SMALL_FF:SMALL_FF + FOX_HEADS].set(w[:, o_ff:o_gq].astype(BF16))
    small = small.at[:, SMALL_GA:SMALL_GA + GLA_GATE_RANK].set(w[:, o_ga:o_gr].astype(BF16))
    return w_f, w_b, small


def _embed_w_gate(w_gate):
    emb = jnp.zeros((LANE, GLA_KW), BF16)
    return emb.at[SMALL_GA:SMALL_GA + GLA_GATE_RANK, :].set(w_gate.astype(BF16))


def kernel(x, mem, ffn1_norm, ffn1_w_in, ffn1_w_out, mix_norm, w_in, sgu_ln_g, sgu_ln_b, sgu_w_s, sgu_b_s, fox_b_f, gla_w_gate, gla_b_gate, gla_o_norm, w_branch_a, w_branch_b, w_branch_c, w_out, xa_norm, mem_norm, xa_w_q, xa_w_kv, xa_w_o, ffn2_norm, ffn2_w_in, ffn2_w_out, final_norm):
    batch, seq, d = x.shape
    mem_len = mem.shape[1]
    h = x.reshape(batch * seq, d)
    mem2 = mem.reshape(batch * mem_len, d)
    ones_f = jnp.ones((1, ZF_WIDTH), F32)
    scale_b = jnp.ones((1, ZB_WIDTH), F32).at[:, ZB_FQ * LANE:ZB_FQ * LANE + FOX_WIDTH].set(
        FOX_HEAD_DIM ** -0.5)
    for l in range(DEPTH):
        h, xn = _ffn(h, ffn1_norm[l], ffn1_w_in[l].astype(BF16), ffn1_w_out[l].astype(BF16),
                     mix_norm[l], mode="emit_norm")

        w_f, w_b, w_small = _repack_w_in(w_in[l])
        zf, small = _proj(xn, w_f, ones_f, F32, w_small)
        zb = _proj(xn, w_b, scale_b, BF16)
        ya = _sgu(zf, sgu_ln_g[l], sgu_ln_b[l], sgu_w_s[l], sgu_b_s[l])
        qa, ka = _fox_gate(small, fox_b_f[l], batch, seq)
        yb = _fox(zb, qa, ka, batch, seq)
        bc = _gla_decay(small, _embed_w_gate(gla_w_gate[l]), gla_b_gate[l])
        yc = _gla(zf, zb, bc, gla_o_norm[l], batch, seq)
        h = _merge(ya, yb, yc, zf, w_branch_a[l].astype(BF16), w_branch_b[l].astype(BF16),
                   w_branch_c[l].astype(BF16), w_out[l].astype(BF16), h)

        kv = _norm_proj(mem2, mem_norm[l], xa_w_kv[l].astype(BF16))
        h = _xattn(h, xa_norm[l], xa_w_q[l].astype(BF16), kv, xa_w_o[l].astype(BF16), seq, mem_len)

        if l == DEPTH - 1:
            h = _ffn(h, ffn2_norm[l], ffn2_w_in[l].astype(BF16), ffn2_w_out[l].astype(BF16),
                     final_norm, mode="final")
        else:
            h = _ffn(h, ffn2_norm[l], ffn2_w_in[l].astype(BF16), ffn2_w_out[l].astype(BF16))
    return h.reshape(batch, seq, d)
```

```python
import functools

import jax
import jax.numpy as jnp
from jax import lax
from jax.experimental import pallas as pl
from jax.experimental.pallas import tpu as pltpu

F32 = jnp.float32
BF16 = jnp.bfloat16

D_MODEL = 2048
DEPTH = 2
D_FF = 5632
EPS = 1e-6

SGU_GROUPS = 4
SGU_GROUP_DIM = 256
SGU_WIDTH = 1024
SGU_CHUNK = 128

FOX_HEADS = 8
FOX_HEAD_DIM = 128
FOX_WIDTH = 1024

GLA_HEADS = 4
GLA_DK = 128
GLA_DV = 256
GLA_KW = 512
GLA_VW = 1024
GLA_GATE_RANK = 16
GLA_GATE_TAU = 16.0
GLA_CHUNK = 64

XA_HEADS = 4
XA_HEAD_DIM = 128
XA_WIDTH = 512

LANE = 128

ZF_SU, ZF_SV, ZF_GQ, ZF_GK, ZF_GR, ZF_GATES = 0, 8, 16, 20, 24, 32
ZF_WIDTH = 80 * LANE
ZB_FQ, ZB_FK, ZB_FV, ZB_GV = 0, 8, 16, 24
ZB_WIDTH = 32 * LANE
SMALL_FF = 0
SMALL_GA = 8

NEG_BIG = -1e30
VMEM_LIMIT = 56 * 1024 * 1024


def _params(sem):
    return pltpu.CompilerParams(dimension_semantics=sem, vmem_limit_bytes=VMEM_LIMIT)


def _rms(x, g):
    return x * lax.rsqrt(jnp.mean(x * x, axis=-1, keepdims=True) + EPS) * g


def _log_sigmoid(x):
    return jnp.minimum(x, 0.0) - jnp.log1p(jnp.exp(-jnp.abs(x)))


def _gelu_tanh(x):
    c = 0.7978845608028654
    return x * (0.5 * (1.0 + jnp.tanh(c * (x + 0.044715 * (x * x * x)))))


def _silu(x):
    return x * jax.nn.sigmoid(x)


def _dot(a, b):
    return jnp.dot(a, b, preferred_element_type=F32)


def _dot_nt(a, b):
    return lax.dot_general(a, b, (((1,), (1,)), ((), ())), preferred_element_type=F32)


def _dot_tn(a, b):
    return lax.dot_general(a, b, (((0,), (0,)), ((), ())), preferred_element_type=F32)


def _split3(x):
    hi = x.astype(BF16)
    r = x - hi.astype(F32)
    lo = r.astype(BF16)
    lo2 = (r - lo.astype(F32)).astype(BF16)
    return hi, lo, lo2


def _dot_01(mat3, x):
    return _dot(mat3, jnp.concatenate(_split3(x), axis=0))


def _ffn_kernel(*refs, mode):
    if mode == "plain":
        x_ref, g_ref, wg_ref, wu_ref, wo_ref, o_ref, xn_ref = refs
    elif mode == "final":
        x_ref, g_ref, wg_ref, wu_ref, wo_ref, ng_ref, o_ref, xn_ref = refs
    else:
        x_ref, g_ref, wg_ref, wu_ref, wo_ref, ng_ref, o_ref, on_ref, xn_ref = refs
    j = pl.program_id(1)

    @pl.when(j == 0)
    def _():
        x = x_ref[...]
        xn_ref[...] = _rms(x, g_ref[...]).astype(BF16)
        o_ref[...] = x

    xn = xn_ref[...]
    gate = _dot(xn, wg_ref[...])
    up = _dot(xn, wu_ref[...])
    act = (_silu(gate) * up * 0.5).astype(BF16)
    o_ref[...] += _dot(act, wo_ref[...])

    if mode != "plain":
        @pl.when(j == pl.num_programs(1) - 1)
        def _():
            y = _rms(o_ref[...], ng_ref[...])
            if mode == "final":
                o_ref[...] = y
            else:
                on_ref[...] = y.astype(BF16)


def _ffn(h, norm_g, w_in, w_out, layer, next_g=None, *, mode="plain", tm=1024, tf=512):
    m, d = h.shape
    f = w_out.shape[1]
    nj = f // tf
    vec = pl.BlockSpec((1, d), lambda i, j: (0, 0))
    rows = pl.BlockSpec((tm, d), lambda i, j: (i, 0))
    in_specs = [
        pl.BlockSpec((tm, d), lambda i, j: (i, 0), pipeline_mode=pl.Buffered(1)),
        vec,
        pl.BlockSpec((None, d, tf), lambda i, j: (layer, 0, j)),
        pl.BlockSpec((None, d, tf), lambda i, j: (layer, 0, j + nj)),
        pl.BlockSpec((None, tf, d), lambda i, j: (layer, j, 0)),
    ]
    args = [h, norm_g.reshape(1, d), w_in, w_in, w_out]
    out_specs = [rows]
    out_shape = [jax.ShapeDtypeStruct((m, d), F32)]
    if mode != "plain":
        in_specs.append(vec)
        args.append(next_g.reshape(1, d))
    if mode == "emit_norm":
        out_specs.append(rows)
        out_shape.append(jax.ShapeDtypeStruct((m, d), BF16))
    out = pl.pallas_call(
        functools.partial(_ffn_kernel, mode=mode),
        grid=(m // tm, nj),
        in_specs=in_specs,
        out_specs=out_specs,
        out_shape=out_shape,
        scratch_shapes=[pltpu.VMEM((tm, d), BF16)],
        compiler_params=_params(("parallel", "arbitrary")),
        name="ffn",
    )(*args)
    return out if mode == "emit_norm" else out[0]


def _proj_kernel(*refs, small):
    if small:
        x_ref, w_ref, cs_ref, ws_ref, o_ref, os_ref = refs

        @pl.when(pl.program_id(1) == 0)
        def _():
            os_ref[...] = _dot(x_ref[...], ws_ref[...])
    else:
        x_ref, w_ref, cs_ref, o_ref = refs
    o_ref[...] = (_dot(x_ref[...], w_ref[...]) * cs_ref[...]).astype(o_ref.dtype)


def _proj(xn, w, layer, col_scale, out_dtype, w_small=None, *, tm=1024, tn=2048):
    m, d = xn.shape
    n = w.shape[2]
    small = w_small is not None
    in_specs = [
        pl.BlockSpec((tm, d), lambda i, j: (i, 0)),
        pl.BlockSpec((None, d, tn), lambda i, j: (layer, 0, j)),
        pl.BlockSpec((1, tn), lambda i, j: (0, j)),
    ]
    args = [xn, w, col_scale]
    out_specs = [pl.BlockSpec((tm, tn), lambda i, j: (i, j))]
    out_shape = [jax.ShapeDtypeStruct((m, n), out_dtype)]
    if small:
        in_specs.append(pl.BlockSpec((None, d, LANE), lambda i, j: (layer, 0, 0)))
        args.append(w_small)
        out_specs.append(pl.BlockSpec((tm, LANE), lambda i, j: (i, 0)))
        out_shape.append(jax.ShapeDtypeStruct((m, LANE), F32))
    out = pl.pallas_call(
        functools.partial(_proj_kernel, small=small),
        grid=(m // tm, n // tn),
        in_specs=in_specs,
        out_specs=out_specs,
        out_shape=out_shape,
        compiler_params=_params(("parallel", "arbitrary")),
        name="proj",
    )(*args)
    return out if small else out[0]


def _norm_proj_kernel(x_ref, g_ref, w_ref, o_ref):
    o_ref[...] = _dot(_rms(x_ref[...], g_ref[...]).astype(BF16), w_ref[...])


def _norm_proj(x, norm_g, w, layer, *, tm=1024):
    m, d = x.shape
    n = w.shape[2]
    return pl.pallas_call(
        _norm_proj_kernel,
        grid=(m // tm,),
        in_specs=[
            pl.BlockSpec((tm, d), lambda i: (i, 0)),
            pl.BlockSpec((1, d), lambda i: (0, 0)),
            pl.BlockSpec((None, d, n), lambda i: (layer, 0, 0)),
        ],
        out_specs=pl.BlockSpec((tm, n), lambda i: (i, 0)),
        out_shape=jax.ShapeDtypeStruct((m, n), F32),
        compiler_params=_params(("parallel",)),
        name="mem_kv",
    )(x, norm_g.reshape(1, d), w)


def _sgu_kernel(u_ref, v_ref, lng_ref, lnb_ref, ws_ref, bs_ref, o_ref, *, chunks):
    t = SGU_CHUNK
    gd = SGU_GROUP_DIM
    row = lax.broadcasted_iota(jnp.int32, (t, t), 0)
    col = lax.broadcasted_iota(jnp.int32, (t, t), 1)
    causal = row >= col
    for g in range(SGU_GROUPS):
        w = jnp.where(causal, ws_ref[g], 0.0).astype(BF16)
        bias = bs_ref[:, g:g + 1]
        ln_g = lng_ref[:, g * gd:(g + 1) * gd]
        ln_b = lnb_ref[:, g * gd:(g + 1) * gd]
        for c in range(chunks):
            rows = slice(c * t, (c + 1) * t)
            cols = slice(g * gd, (g + 1) * gd)
            v = _gelu_tanh(v_ref[rows, cols])
            mu = jnp.mean(v, axis=-1, keepdims=True)
            vc = v - mu
            var = jnp.mean(vc * vc, axis=-1, keepdims=True)
            vn = vc * lax.rsqrt(var + EPS) * ln_g + ln_b
            mixed = _dot(w, vn.astype(BF16)) + bias
            o_ref[rows, cols] = (_gelu_tanh(u_ref[rows, cols]) * mixed).astype(o_ref.dtype)


def _sgu(zf, ln_g, ln_b, w_s, b_s, *, chunks=4):
    m = zf.shape[0]
    rows = chunks * SGU_CHUNK
    w = SGU_WIDTH
    return pl.pallas_call(
        functools.partial(_sgu_kernel, chunks=chunks),
        grid=(m // rows,),
        in_specs=[
            pl.BlockSpec((rows, w), lambda i: (i, ZF_SU * LANE // w)),
            pl.BlockSpec((rows, w), lambda i: (i, ZF_SV * LANE // w)),
            pl.BlockSpec((1, w), lambda i: (0, 0)),
            pl.BlockSpec((1, w), lambda i: (0, 0)),
            pl.BlockSpec((SGU_GROUPS, SGU_CHUNK, SGU_CHUNK), lambda i: (0, 0, 0)),
            pl.BlockSpec((SGU_CHUNK, SGU_GROUPS), lambda i: (0, 0)),
        ],
        out_specs=pl.BlockSpec((rows, w), lambda i: (i, 0)),
        out_shape=jax.ShapeDtypeStruct((m, w), BF16),
        compiler_params=_params(("parallel",)),
        name="sgu",
    )(zf, zf, ln_g.reshape(1, w), ln_b.reshape(1, w), w_s, jnp.swapaxes(b_s, 0, 1))


def _fox_gate_kernel(f_ref, bf_ref, tril_ref, pq_ref, pk_ref, cq_ref, ck_ref, qa_ref, ka_ref):
    t = LANE
    seq = f_ref.shape[0]
    carry = jnp.zeros((1, t), F32)
    for b in range(seq // t):
        rows = slice(b * t, (b + 1) * t)
        lf = _log_sigmoid(f_ref[rows, :] + bf_ref[...])
        cum = _dot_01(tril_ref[...], lf) + carry
        carry = cum[t - 1:t, :]
        parts = jnp.concatenate(_split3(cum), axis=1)
        qa_ref[rows, :] = (_dot(parts, pq_ref[...]) + cq_ref[...]).astype(BF16)
        ka_ref[rows, :] = (_dot(parts, pk_ref[...]) + ck_ref[...]).astype(BF16)


def _tril3(n):
    tril = jnp.tril(jnp.ones((n, n), BF16))
    return jnp.concatenate([tril, tril, tril], axis=1)


def _fox_placement():
    w = FOX_HEADS * LANE
    src = jnp.arange(3 * LANE)[:, None]
    dst = jnp.arange(w)[None, :]
    p, h = src // LANE, src % LANE
    valid = h < FOX_HEADS
    pq = jnp.where(valid & (dst == h * LANE + p), 1.0, 0.0).astype(BF16)
    pk = jnp.where(valid & (dst == h * LANE + 3 + p), -1.0, 0.0).astype(BF16)
    lane = dst % LANE
    cq = jnp.where((lane >= 3) & (lane < 6), 1.0, 0.0).astype(F32)
    ck = jnp.where(lane < 3, 1.0, 0.0).astype(F32)
    return pq, pk, cq, ck


def _fox_gate(small, b_f, batch, seq):
    bf = jnp.zeros((1, LANE), F32).at[0, SMALL_FF:SMALL_FF + FOX_HEADS].set(b_f)
    w = FOX_HEADS * LANE
    aug = jax.ShapeDtypeStruct((batch * seq, w), BF16)

    def const(shape):
        return pl.BlockSpec(shape, lambda b: (0, 0))

    return pl.pallas_call(
        _fox_gate_kernel,
        grid=(batch,),
        in_specs=[
            pl.BlockSpec((seq, LANE), lambda b: (b, 0)),
            const((1, LANE)), const((LANE, 3 * LANE)),
            const((3 * LANE, w)), const((3 * LANE, w)), const((1, w)), const((1, w)),
        ],
        out_specs=[pl.BlockSpec((seq, w), lambda b: (b, 0))] * 2,
        out_shape=[aug, aug],
        compiler_params=_params(("parallel",)),
        name="fox_gate",
    )(small, bf, _tril3(LANE), *_fox_placement())


def _fox_kernel(q_ref, k_ref, v_ref, qa_ref, ka_ref, o_ref, *, t):
    seq, dh = q_ref.shape
    lane = lax.broadcasted_iota(jnp.int32, (t, LANE), 1)
    ones_col = jnp.where(lane == 0, 1.0, 0.0).astype(BF16)
    row = lax.broadcasted_iota(jnp.int32, (t, t), 0)
    col = lax.broadcasted_iota(jnp.int32, (t, t), 1)
    causal = row >= col

    def block(qc, j, carry):
        m, acc = carry
        rows = slice(j * t, (j + 1) * t)
        kc = jnp.concatenate([k_ref[rows, :], ka_ref[rows, :]], axis=1)
        vc = jnp.concatenate([v_ref[rows, :], ones_col], axis=1)
        s = _dot_nt(qc, kc)
        m_new = jnp.maximum(m, jnp.max(s, axis=1, keepdims=True))
        p = jnp.exp(s - m_new).astype(BF16)
        return m_new, jnp.exp(m - m_new) * acc + _dot(p, vc)

    for i in range(seq // t):
        rows = slice(i * t, (i + 1) * t)
        qc = jnp.concatenate([q_ref[rows, :], qa_ref[rows, :]], axis=1)
        kc = jnp.concatenate([k_ref[rows, :], ka_ref[rows, :]], axis=1)
        vc = jnp.concatenate([v_ref[rows, :], ones_col], axis=1)
        s = jnp.where(causal, _dot_nt(qc, kc), NEG_BIG)
        m = jnp.max(s, axis=1, keepdims=True)
        carry = (m, _dot(jnp.exp(s - m).astype(BF16), vc))
        for j in range(i):
            carry = block(qc, j, carry)
        acc = carry[1]
        o_ref[rows, :] = (acc[:, :dh] / acc[:, dh:dh + 1]).astype(o_ref.dtype)


def _fox(zb, qa, ka, batch, seq, *, t=512):
    m = zb.shape[0]
    dh = FOX_HEAD_DIM

    def head(seg):
        return pl.BlockSpec((seq, dh), lambda b, h: (b, seg + h))

    return pl.pallas_call(
        functools.partial(_fox_kernel, t=t),
        grid=(batch, FOX_HEADS),
        in_specs=[head(ZB_FQ), head(ZB_FK), head(ZB_FV), head(0), head(0)],
        out_specs=head(0),
        out_shape=jax.ShapeDtypeStruct((m, FOX_WIDTH), BF16),
        compiler_params=_params(("parallel", "parallel")),
        name="fox",
    )(zb, zb, zb, qa, ka)


GLA_BLOCK = 4 * GLA_CHUNK


def _gla_decay_kernel(a_ref, wg_ref, bg_ref, cm_ref, o_ref, *, blocks):
    t = GLA_BLOCK
    for b in range(blocks):
        rows = slice(b * t, (b + 1) * t)
        gl = _dot(a_ref[rows, :].astype(BF16), wg_ref[...]) + bg_ref[...]
        g = _log_sigmoid(gl) / GLA_GATE_TAU
        o_ref[rows, :] = _dot_01(cm_ref[...], g)


def _gla_decay(small, w_gate_emb, b_gate, *, blocks=4):
    m = small.shape[0]
    t = GLA_BLOCK
    rows = blocks * t
    idx = jnp.arange(t)
    same = (idx[:, None] // GLA_CHUNK) == (idx[None, :] // GLA_CHUNK)
    cm = jnp.where(same & (idx[:, None] >= idx[None, :]), 1.0, 0.0).astype(BF16)
    cm3 = jnp.concatenate([cm, cm, cm], axis=1)
    return pl.pallas_call(
        functools.partial(_gla_decay_kernel, blocks=blocks),
        grid=(m // rows,),
        in_specs=[
            pl.BlockSpec((rows, LANE), lambda i: (i, 0)),
            pl.BlockSpec((LANE, GLA_KW), lambda i: (0, 0)),
            pl.BlockSpec((1, GLA_KW), lambda i: (0, 0)),
            pl.BlockSpec((t, 3 * t), lambda i: (0, 0)),
        ],
        out_specs=pl.BlockSpec((rows, GLA_KW), lambda i: (i, 0)),
        out_shape=jax.ShapeDtypeStruct((m, GLA_KW), F32),
        compiler_params=_params(("parallel",)),
        name="gla_decay",
    )(small, w_gate_emb, b_gate.reshape(1, GLA_KW), cm3)


def _gla_kernel(q_ref, k_ref, v_ref, r_ref, bc_ref, on_ref, o_ref):
    c = GLA_CHUNK
    t = GLA_BLOCK
    dk = GLA_DK
    seq = q_ref.shape[0]
    st = jnp.zeros((GLA_DV, dk), F32)
    row = lax.broadcasted_iota(jnp.int32, (t, t), 0)
    col = lax.broadcasted_iota(jnp.int32, (t, t), 1)
    intra = (row >= col) & ((row // c) == (col // c))
    o_gain = on_ref[...]

    for n in range(seq // t):
        rows = slice(n * t, (n + 1) * t)
        bc = bc_ref[rows, :]
        qb = q_ref[rows, :] * (dk ** -0.5)
        kb = k_ref[rows, :]
        vb = v_ref[rows, :]
        chunks = [slice(i * c, (i + 1) * c) for i in range(t // c)]
        b_mid = jnp.concatenate(
            [jnp.broadcast_to(bc[s.start + c // 2:s.start + c // 2 + 1, :], (c, dk)) for s in chunks], axis=0)
        b_last = jnp.concatenate(
            [jnp.broadcast_to(bc[s.stop - 1:s.stop, :], (c, dk)) for s in chunks], axis=0)
        qd = (qb * jnp.exp(bc - b_mid)).astype(BF16)
        kd = (kb * jnp.exp(b_mid - bc)).astype(BF16)
        ku = (kb * jnp.exp(b_last - bc)).astype(BF16)
        qi = (qb * jnp.exp(bc)).astype(BF16)
        att = jnp.where(intra, _dot_nt(qd, kd), 0.0)
        o = _dot(att.astype(BF16), vb)
        inter = []
        for s in chunks:
            inter.append(_dot_nt(qi[s, :], st.astype(BF16)))
            st = st * jnp.exp(bc[s.stop - 1:s.stop, :]) + _dot_tn(vb[s, :], ku[s, :])
        o = _rms(o + jnp.concatenate(inter, axis=0), o_gain)
        o_ref[rows, :] = (o * _silu(r_ref[rows, :])).astype(o_ref.dtype)


def _gla(zf, zb, bc, o_norm, batch, seq):
    m = zf.shape[0]
    dk, dv = GLA_DK, GLA_DV
    return pl.pallas_call(
        _gla_kernel,
        grid=(batch, GLA_HEADS),
        in_specs=[
            pl.BlockSpec((seq, dk), lambda b, h: (b, ZF_GQ + h)),
            pl.BlockSpec((seq, dk), lambda b, h: (b, ZF_GK + h)),
            pl.BlockSpec((seq, dv), lambda b, h: (b, ZB_GV * LANE // dv + h)),
            pl.BlockSpec((seq, dv), lambda b, h: (b, ZF_GR * LANE // dv + h)),
            pl.BlockSpec((seq, dk), lambda b, h: (b, h)),
            pl.BlockSpec((1, dv), lambda b, h: (0, h)),
        ],
        out_specs=pl.BlockSpec((seq, dv), lambda b, h: (b, h)),
        out_shape=jax.ShapeDtypeStruct((m, GLA_VW), BF16),
        compiler_params=_params(("parallel", "parallel")),
        name="gla",
    )(zf, zf, zb, zf, bc, o_norm.reshape(1, GLA_VW))


def _merge_kernel(a_ref, b_ref, c_ref, ga_ref, gb_ref, gc_ref, wa_ref, wb_ref, wc_ref,
                  wo_ref, h_ref, o_ref):
    mix = (jax.nn.sigmoid(ga_ref[...]) * _dot(a_ref[...], wa_ref[...])
           + jax.nn.sigmoid(gb_ref[...]) * _dot(b_ref[...], wb_ref[...])
           + jax.nn.sigmoid(gc_ref[...]) * _dot(c_ref[...], wc_ref[...]))
    o_ref[...] = h_ref[...] + _dot(mix.astype(BF16), wo_ref[...])


def _merge(ya, yb, yc, zf, wa, wb, wc, wo, h, layer, *, tm=256):
    m, d = h.shape
    kw = ya.shape[1]
    g0 = ZF_GATES * LANE // d
    branch = pl.BlockSpec((tm, kw), lambda i: (i, 0))
    resident = dict(pipeline_mode=pl.Buffered(1))
    wspec = pl.BlockSpec((None, kw, d), lambda i: (layer, 0, 0), **resident)
    return pl.pallas_call(
        _merge_kernel,
        grid=(m // tm,),
        in_specs=[
            branch, branch, branch,
            pl.BlockSpec((tm, d), lambda i: (i, g0)),
            pl.BlockSpec((tm, d), lambda i: (i, g0 + 1)),
            pl.BlockSpec((tm, d), lambda i: (i, g0 + 2)),
            wspec, wspec, wspec,
            pl.BlockSpec((None, d, d), lambda i: (layer, 0, 0), **resident),
            pl.BlockSpec((tm, d), lambda i: (i, 0)),
        ],
        out_specs=pl.BlockSpec((tm, d), lambda i: (i, 0)),
        out_shape=jax.ShapeDtypeStruct((m, d), F32),
        compiler_params=_params(("parallel",)),
        name="merge",
    )(ya, yb, yc, zf, zf, zf, wa, wb, wc, wo, h)


def _xattn_kernel(h_ref, g_ref, wq_ref, kv_ref, wo_ref, o_ref):
    dh = XA_HEAD_DIM
    x = h_ref[...]
    n = _rms(x, g_ref[...]).astype(BF16)
    q = _dot(n, wq_ref[...]) * (dh ** -0.5)
    outs = []
    for hd in range(XA_HEADS):
        qh = q[:, hd * dh:(hd + 1) * dh].astype(BF16)
        kh = kv_ref[:, hd * dh:(hd + 1) * dh].astype(BF16)
        vh = kv_ref[:, XA_WIDTH + hd * dh:XA_WIDTH + (hd + 1) * dh].astype(BF16)
        s = _dot_nt(qh, kh)
        e = jnp.exp(s - jnp.max(s, axis=-1, keepdims=True))
        p = e / jnp.sum(e, axis=-1, keepdims=True)
        outs.append(_dot(p.astype(BF16), vh))
    o = jnp.concatenate(outs, axis=1).astype(BF16)
    o_ref[...] = x + _dot(o, wo_ref[...])


def _xattn(h, norm_g, wq, kv, wo, layer, seq, mem_len, *, tm=512):
    m, d = h.shape
    per_batch = seq // tm
    return pl.pallas_call(
        _xattn_kernel,
        grid=(m // tm,),
        in_specs=[
            pl.BlockSpec((tm, d), lambda i: (i, 0)),
            pl.BlockSpec((1, d), lambda i: (0, 0)),
            pl.BlockSpec((None, d, XA_WIDTH), lambda i: (layer, 0, 0)),
            pl.BlockSpec((mem_len, 2 * XA_WIDTH), lambda i: (i // per_batch, 0)),
            pl.BlockSpec((None, XA_WIDTH, d), lambda i: (layer, 0, 0)),
        ],
        out_specs=pl.BlockSpec((tm, d), lambda i: (i, 0)),
        out_shape=jax.ShapeDtypeStruct((m, d), F32),
        compiler_params=_params(("parallel",)),
        name="xattn",
    )(h, norm_g.reshape(1, d), wq, kv, wo)


CAST_BLOCK_BYTES = 6 * 1024 * 1024


def _cast_kernel(x_ref, o_ref):
    o_ref[...] = x_ref[...].astype(o_ref.dtype)


def _to_bf16(w):
    lead, c = w.shape[:-1], w.shape[-1]
    w2 = w.reshape(-1, c)
    r = w2.shape[0]
    rows = 16
    while rows * 2 * c * 4 <= CAST_BLOCK_BYTES and r % (rows * 2) == 0:
        rows *= 2
    spec = pl.BlockSpec((rows, c), lambda i: (i, 0))
    out = pl.pallas_call(
        _cast_kernel,
        grid=(r // rows,),
        in_specs=[spec],
        out_specs=spec,
        out_shape=jax.ShapeDtypeStruct((r, c), BF16),
        compiler_params=_params(("parallel",)),
        name="cast",
    )(w2)
    return out.reshape(*lead, c)

W_FQ = 2 * SGU_WIDTH
W_FF = W_FQ + 3 * FOX_WIDTH
W_GQ = W_FF + FOX_HEADS
W_GV = W_GQ + 2 * GLA_KW
W_GA = W_GV + GLA_VW
W_GR = W_GA + GLA_GATE_RANK
W_END = W_GR + GLA_VW + 3 * D_MODEL


def _repack_kernel(w_ref, wf_ref, wb_ref, ws_ref):
    def cols(lo, hi):
        return w_ref[:, lo:hi].astype(BF16)

    wf_ref[:, 0:W_FQ] = cols(0, W_FQ)
    wf_ref[:, W_FQ:W_FQ + 2 * GLA_KW] = cols(W_GQ, W_GV)
    wf_ref[:, W_FQ + 2 * GLA_KW:ZF_WIDTH] = cols(W_GR, W_END)
    wb_ref[:, 0:3 * FOX_WIDTH] = cols(W_FQ, W_FF)
    wb_ref[:, 3 * FOX_WIDTH:ZB_WIDTH] = cols(W_GV, W_GA)
    pad = jnp.zeros((w_ref.shape[0], LANE - FOX_HEADS - GLA_GATE_RANK), BF16)
    ws_ref[...] = jnp.concatenate([cols(W_FF, W_GQ), cols(W_GA, W_GR), pad], axis=1)


def _repack_w_in(w, *, rows=64):
    nl, d, n = w.shape

    def spec(width):
        return pl.BlockSpec((None, rows, width), lambda l, i: (l, i, 0))

    return pl.pallas_call(
        _repack_kernel,
        grid=(nl, d // rows),
        in_specs=[spec(n)],
        out_specs=[spec(ZF_WIDTH), spec(ZB_WIDTH), spec(LANE)],
        out_shape=[jax.ShapeDtypeStruct((nl, d, width), BF16) for width in (ZF_WIDTH, ZB_WIDTH, LANE)],
        compiler_params=_params(("parallel", "parallel")),
        name="repack",
    )(w)


def _embed_w_gate(w_gate):
    emb = jnp.zeros((LANE, GLA_KW), BF16)
    return emb.at[SMALL_GA:SMALL_GA + GLA_GATE_RANK, :].set(w_gate.astype(BF16))


def kernel(x, mem, ffn1_norm, ffn1_w_in, ffn1_w_out, mix_norm, w_in, sgu_ln_g, sgu_ln_b, sgu_w_s, sgu_b_s, fox_b_f, gla_w_gate, gla_b_gate, gla_o_norm, w_branch_a, w_branch_b, w_branch_c, w_out, xa_norm, mem_norm, xa_w_q, xa_w_kv, xa_w_o, ffn2_norm, ffn2_w_in, ffn2_w_out, final_norm):
    batch, seq, d = x.shape
    mem_len = mem.shape[1]
    h = x.reshape(batch * seq, d)
    mem2 = mem.reshape(batch * mem_len, d)
    ones_f = jnp.ones((1, ZF_WIDTH), F32)
    scale_b = jnp.ones((1, ZB_WIDTH), F32).at[:, ZB_FQ * LANE:ZB_FQ * LANE + FOX_WIDTH].set(
        FOX_HEAD_DIM ** -0.5)
    f1_in, f1_out = _to_bf16(ffn1_w_in), _to_bf16(ffn1_w_out)
    f2_in, f2_out = _to_bf16(ffn2_w_in), _to_bf16(ffn2_w_out)
    wa, wb, wc, wo = (_to_bf16(w) for w in (w_branch_a, w_branch_b, w_branch_c, w_out))
    xq, xkv, xo = _to_bf16(xa_w_q), _to_bf16(xa_w_kv), _to_bf16(xa_w_o)
    w_f, w_b, w_small = _repack_w_in(w_in)
    for l in range(DEPTH):
        h, xn = _ffn(h, ffn1_norm[l], f1_in, f1_out, l, mix_norm[l], mode="emit_norm")

        zf, small = _proj(xn, w_f, l, ones_f, F32, w_small)
        zb = _proj(xn, w_b, l, scale_b, BF16)
        ya = _sgu(zf, sgu_ln_g[l], sgu_ln_b[l], sgu_w_s[l], sgu_b_s[l])
        qa, ka = _fox_gate(small, fox_b_f[l], batch, seq)
        yb = _fox(zb, qa, ka, batch, seq)
        bc = _gla_decay(small, _embed_w_gate(gla_w_gate[l]), gla_b_gate[l])
        yc = _gla(zf, zb, bc, gla_o_norm[l], batch, seq)
        h = _merge(ya, yb, yc, zf, wa, wb, wc, wo, h, l)

        kv = _norm_proj(mem2, mem_norm[l], xkv, l)
        h = _xattn(h, xa_norm[l], xq, kv, xo, l, seq, mem_len)

        if l == DEPTH - 1:
            h = _ffn(h, ffn2_norm[l], f2_in, f2_out, l, final_norm, mode="final")
        else:
            h = _ffn(h, ffn2_norm[l], f2_in, f2_out, l)
    return h.reshape(batch, seq, d)
```

```python
import functools

import jax
import jax.numpy as jnp
from jax import lax
from jax.experimental import pallas as pl
from jax.experimental.pallas import tpu as pltpu

F32 = jnp.float32
BF16 = jnp.bfloat16

D_MODEL = 2048
DEPTH = 2
D_FF = 5632
EPS = 1e-6

SGU_GROUPS = 4
SGU_GROUP_DIM = 256
SGU_WIDTH = 1024
SGU_CHUNK = 128

FOX_HEADS = 8
FOX_HEAD_DIM = 128
FOX_WIDTH = 1024

GLA_HEADS = 4
GLA_DK = 128
GLA_DV = 256
GLA_KW = 512
GLA_VW = 1024
GLA_GATE_RANK = 16
GLA_GATE_TAU = 16.0
GLA_CHUNK = 64

XA_HEADS = 4
XA_HEAD_DIM = 128
XA_WIDTH = 512

LANE = 128

ZF_SU, ZF_SV, ZF_GQ, ZF_GK, ZF_GR, ZF_GATES = 0, 8, 16, 20, 24, 32
ZF_WIDTH = 80 * LANE
ZB_FQ, ZB_FK, ZB_FV, ZB_GV = 0, 8, 16, 24
ZB_WIDTH = 32 * LANE
SMALL_FF = 0
SMALL_GA = 8

NEG_BIG = -1e30
VMEM_LIMIT = 56 * 1024 * 1024


def _params(sem):
    return pltpu.CompilerParams(dimension_semantics=sem, vmem_limit_bytes=VMEM_LIMIT)


def _rms(x, g):
    return x * lax.rsqrt(jnp.mean(x * x, axis=-1, keepdims=True) + EPS) * g


def _log_sigmoid(x):
    return jnp.minimum(x, 0.0) - jnp.log1p(jnp.exp(-jnp.abs(x)))


def _gelu_tanh(x):
    c = 0.7978845608028654
    return x * (0.5 * (1.0 + jnp.tanh(c * (x + 0.044715 * (x * x * x)))))


def _silu(x):
    return x * jax.nn.sigmoid(x)


def _dot(a, b):
    return jnp.dot(a, b, preferred_element_type=F32)


def _dot_nt(a, b):
    return lax.dot_general(a, b, (((1,), (1,)), ((), ())), preferred_element_type=F32)


def _dot_tn(a, b):
    return lax.dot_general(a, b, (((0,), (0,)), ((), ())), preferred_element_type=F32)


def _split3(x):
    hi = x.astype(BF16)
    r = x - hi.astype(F32)
    lo = r.astype(BF16)
    lo2 = (r - lo.astype(F32)).astype(BF16)
    return hi, lo, lo2


def _dot_01(mat3, x):
    return _dot(mat3, jnp.concatenate(_split3(x), axis=0))


def _ffn_kernel(*refs, mode):
    if mode == "plain":
        x_ref, g_ref, wg_ref, wu_ref, wo_ref, o_ref, xn_ref = refs
    elif mode == "final":
        x_ref, g_ref, wg_ref, wu_ref, wo_ref, ng_ref, o_ref, xn_ref = refs
    else:
        x_ref, g_ref, wg_ref, wu_ref, wo_ref, ng_ref, o_ref, on_ref, xn_ref = refs
    j = pl.program_id(1)

    @pl.when(j == 0)
    def _():
        x = x_ref[...]
        xn_ref[...] = _rms(x, g_ref[...]).astype(BF16)
        o_ref[...] = x

    xn = xn_ref[...]
    gate = _dot(xn, wg_ref[...])
    up = _dot(xn, wu_ref[...])
    act = (_silu(gate) * up * 0.5).astype(BF16)
    o_ref[...] += _dot(act, wo_ref[...])

    if mode != "plain":
        @pl.when(j == pl.num_programs(1) - 1)
        def _():
            y = _rms(o_ref[...], ng_ref[...])
            if mode == "final":
                o_ref[...] = y
            else:
                on_ref[...] = y.astype(BF16)


def _ffn(h, norm_g, w_in, w_out, layer, next_g=None, *, mode="plain", tm=512, tf=512):
    m, d = h.shape
    f = w_out.shape[1]
    nj = f // tf
    vec = pl.BlockSpec((1, d), lambda i, j: (0, 0))
    rows = pl.BlockSpec((tm, d), lambda i, j: (i, 0))
    in_specs = [
        rows, vec,
        pl.BlockSpec((None, d, tf), lambda i, j: (layer, 0, j)),
        pl.BlockSpec((None, d, tf), lambda i, j: (layer, 0, j + nj)),
        pl.BlockSpec((None, tf, d), lambda i, j: (layer, j, 0)),
    ]
    args = [h, norm_g.reshape(1, d), w_in, w_in, w_out]
    out_specs = [rows]
    out_shape = [jax.ShapeDtypeStruct((m, d), F32)]
    if mode != "plain":
        in_specs.append(vec)
        args.append(next_g.reshape(1, d))
    if mode == "emit_norm":
        out_specs.append(rows)
        out_shape.append(jax.ShapeDtypeStruct((m, d), BF16))
    out = pl.pallas_call(
        functools.partial(_ffn_kernel, mode=mode),
        grid=(m // tm, nj),
        in_specs=in_specs,
        out_specs=out_specs,
        out_shape=out_shape,
        scratch_shapes=[pltpu.VMEM((tm, d), BF16)],
        compiler_params=_params(("parallel", "arbitrary")),
        name="ffn",
    )(*args)
    return out if mode == "emit_norm" else out[0]


def _proj_kernel(*refs, small):
    if small:
        x_ref, w_ref, cs_ref, ws_ref, o_ref, os_ref = refs

        @pl.when(pl.program_id(1) == 0)
        def _():
            os_ref[...] = _dot(x_ref[...], ws_ref[...])
    else:
        x_ref, w_ref, cs_ref, o_ref = refs
    o_ref[...] = (_dot(x_ref[...], w_ref[...]) * cs_ref[...]).astype(o_ref.dtype)


def _proj(xn, w, layer, col_scale, out_dtype, w_small=None, *, tm=1024, tn=2048):
    m, d = xn.shape
    n = w.shape[2]
    small = w_small is not None
    in_specs = [
        pl.BlockSpec((tm, d), lambda i, j: (i, 0)),
        pl.BlockSpec((None, d, tn), lambda i, j: (layer, 0, j)),
        pl.BlockSpec((1, tn), lambda i, j: (0, j)),
    ]
    args = [xn, w, col_scale]
    out_specs = [pl.BlockSpec((tm, tn), lambda i, j: (i, j))]
    out_shape = [jax.ShapeDtypeStruct((m, n), out_dtype)]
    if small:
        in_specs.append(pl.BlockSpec((None, d, LANE), lambda i, j: (layer, 0, 0)))
        args.append(w_small)
        out_specs.append(pl.BlockSpec((tm, LANE), lambda i, j: (i, 0)))
        out_shape.append(jax.ShapeDtypeStruct((m, LANE), F32))
    out = pl.pallas_call(
        functools.partial(_proj_kernel, small=small),
        grid=(m // tm, n // tn),
        in_specs=in_specs,
        out_specs=out_specs,
        out_shape=out_shape,
        compiler_params=_params(("parallel", "arbitrary")),
        name="proj",
    )(*args)
    return out if small else out[0]


def _norm_proj_kernel(x_ref, g_ref, w_ref, o_ref):
    o_ref[...] = _dot(_rms(x_ref[...], g_ref[...]).astype(BF16), w_ref[...])


def _norm_proj(x, norm_g, w, layer, *, tm=1024):
    m, d = x.shape
    n = w.shape[2]
    return pl.pallas_call(
        _norm_proj_kernel,
        grid=(m // tm,),
        in_specs=[
            pl.BlockSpec((tm, d), lambda i: (i, 0)),
            pl.BlockSpec((1, d), lambda i: (0, 0)),
            pl.BlockSpec((None, d, n), lambda i: (layer, 0, 0)),
        ],
        out_specs=pl.BlockSpec((tm, n), lambda i: (i, 0)),
        out_shape=jax.ShapeDtypeStruct((m, n), F32),
        compiler_params=_params(("parallel",)),
        name="mem_kv",
    )(x, norm_g.reshape(1, d), w)


def _sgu_kernel(u_ref, v_ref, lng_ref, lnb_ref, ws_ref, bs_ref, o_ref, *, chunks):
    t = SGU_CHUNK
    gd = SGU_GROUP_DIM
    row = lax.broadcasted_iota(jnp.int32, (t, t), 0)
    col = lax.broadcasted_iota(jnp.int32, (t, t), 1)
    causal = row >= col
    for g in range(SGU_GROUPS):
        w = jnp.where(causal, ws_ref[g], 0.0).astype(BF16)
        bias = bs_ref[:, g:g + 1]
        ln_g = lng_ref[:, g * gd:(g + 1) * gd]
        ln_b = lnb_ref[:, g * gd:(g + 1) * gd]
        for c in range(chunks):
            rows = slice(c * t, (c + 1) * t)
            cols = slice(g * gd, (g + 1) * gd)
            v = _gelu_tanh(v_ref[rows, cols])
            mu = jnp.mean(v, axis=-1, keepdims=True)
            vc = v - mu
            var = jnp.mean(vc * vc, axis=-1, keepdims=True)
            vn = vc * lax.rsqrt(var + EPS) * ln_g + ln_b
            mixed = _dot(w, vn.astype(BF16)) + bias
            o_ref[rows, cols] = (_gelu_tanh(u_ref[rows, cols]) * mixed).astype(o_ref.dtype)


def _sgu(zf, ln_g, ln_b, w_s, b_s, *, chunks=4):
    m = zf.shape[0]
    rows = chunks * SGU_CHUNK
    w = SGU_WIDTH
    return pl.pallas_call(
        functools.partial(_sgu_kernel, chunks=chunks),
        grid=(m // rows,),
        in_specs=[
            pl.BlockSpec((rows, w), lambda i: (i, ZF_SU * LANE // w)),
            pl.BlockSpec((rows, w), lambda i: (i, ZF_SV * LANE // w)),
            pl.BlockSpec((1, w), lambda i: (0, 0)),
            pl.BlockSpec((1, w), lambda i: (0, 0)),
            pl.BlockSpec((SGU_GROUPS, SGU_CHUNK, SGU_CHUNK), lambda i: (0, 0, 0)),
            pl.BlockSpec((SGU_CHUNK, SGU_GROUPS), lambda i: (0, 0)),
        ],
        out_specs=pl.BlockSpec((rows, w), lambda i: (i, 0)),
        out_shape=jax.ShapeDtypeStruct((m, w), BF16),
        compiler_params=_params(("parallel",)),
        name="sgu",
    )(zf, zf, ln_g.reshape(1, w), ln_b.reshape(1, w), w_s, jnp.swapaxes(b_s, 0, 1))


def _fox_gate_kernel(f_ref, bf_ref, tril_ref, pq_ref, pk_ref, cq_ref, ck_ref, qa_ref, ka_ref):
    t = LANE
    seq = f_ref.shape[0]
    carry = jnp.zeros((1, t), F32)
    for b in range(seq // t):
        rows = slice(b * t, (b + 1) * t)
        lf = _log_sigmoid(f_ref[rows, :] + bf_ref[...])
        cum = _dot_01(tril_ref[...], lf) + carry
        carry = cum[t - 1:t, :]
        parts = jnp.concatenate(_split3(cum), axis=1)
        qa_ref[rows, :] = (_dot(parts, pq_ref[...]) + cq_ref[...]).astype(BF16)
        ka_ref[rows, :] = (_dot(parts, pk_ref[...]) + ck_ref[...]).astype(BF16)


def _tril3(n):
    tril = jnp.tril(jnp.ones((n, n), BF16))
    return jnp.concatenate([tril, tril, tril], axis=1)


def _fox_placement():
    w = FOX_HEADS * LANE
    src = jnp.arange(3 * LANE)[:, None]
    dst = jnp.arange(w)[None, :]
    p, h = src // LANE, src % LANE
    valid = h < FOX_HEADS
    pq = jnp.where(valid & (dst == h * LANE + p), 1.0, 0.0).astype(BF16)
    pk = jnp.where(valid & (dst == h * LANE + 3 + p), -1.0, 0.0).astype(BF16)
    lane = dst % LANE
    cq = jnp.where((lane >= 3) & (lane < 6), 1.0, 0.0).astype(F32)
    ck = jnp.where(lane < 3, 1.0, 0.0).astype(F32)
    return pq, pk, cq, ck


def _fox_gate(small, b_f, batch, seq):
    bf = jnp.zeros((1, LANE), F32).at[0, SMALL_FF:SMALL_FF + FOX_HEADS].set(b_f)
    w = FOX_HEADS * LANE
    aug = jax.ShapeDtypeStruct((batch * seq, w), BF16)

    def const(shape):
        return pl.BlockSpec(shape, lambda b: (0, 0))

    return pl.pallas_call(
        _fox_gate_kernel,
        grid=(batch,),
        in_specs=[
            pl.BlockSpec((seq, LANE), lambda b: (b, 0)),
            const((1, LANE)), const((LANE, 3 * LANE)),
            const((3 * LANE, w)), const((3 * LANE, w)), const((1, w)), const((1, w)),
        ],
        out_specs=[pl.BlockSpec((seq, w), lambda b: (b, 0))] * 2,
        out_shape=[aug, aug],
        compiler_params=_params(("parallel",)),
        name="fox_gate",
    )(small, bf, _tril3(LANE), *_fox_placement())


def _fox_kernel(q_ref, k_ref, v_ref, qa_ref, ka_ref, o_ref, *, t):
    seq, dh = q_ref.shape
    lane = lax.broadcasted_iota(jnp.int32, (t, LANE), 1)
    ones_col = jnp.where(lane == 0, 1.0, 0.0).astype(BF16)
    row = lax.broadcasted_iota(jnp.int32, (t, t), 0)
    col = lax.broadcasted_iota(jnp.int32, (t, t), 1)
    causal = row >= col

    def block(qc, j, carry):
        m, acc = carry
        rows = slice(j * t, (j + 1) * t)
        kc = jnp.concatenate([k_ref[rows, :], ka_ref[rows, :]], axis=1)
        vc = jnp.concatenate([v_ref[rows, :], ones_col], axis=1)
        s = _dot_nt(qc, kc)
        m_new = jnp.maximum(m, jnp.max(s, axis=1, keepdims=True))
        p = jnp.exp(s - m_new).astype(BF16)
        return m_new, jnp.exp(m - m_new) * acc + _dot(p, vc)

    for i in range(seq // t):
        rows = slice(i * t, (i + 1) * t)
        qc = jnp.concatenate([q_ref[rows, :], qa_ref[rows, :]], axis=1)
        kc = jnp.concatenate([k_ref[rows, :], ka_ref[rows, :]], axis=1)
        vc = jnp.concatenate([v_ref[rows, :], ones_col], axis=1)
        s = jnp.where(causal, _dot_nt(qc, kc), NEG_BIG)
        m = jnp.max(s, axis=1, keepdims=True)
        carry = (m, _dot(jnp.exp(s - m).astype(BF16), vc))
        for j in range(i):
            carry = block(qc, j, carry)
        acc = carry[1]
        o_ref[rows, :] = (acc[:, :dh] / acc[:, dh:dh + 1]).astype(o_ref.dtype)


def _fox(zb, qa, ka, batch, seq, *, t=512):
    m = zb.shape[0]
    dh = FOX_HEAD_DIM

    def head(seg):
        return pl.BlockSpec((seq, dh), lambda b, h: (b, seg + h))

    return pl.pallas_call(
        functools.partial(_fox_kernel, t=t),
        grid=(batch, FOX_HEADS),
        in_specs=[head(ZB_FQ), head(ZB_FK), head(ZB_FV), head(0), head(0)],
        out_specs=head(0),
        out_shape=jax.ShapeDtypeStruct((m, FOX_WIDTH), BF16),
        compiler_params=_params(("parallel", "parallel")),
        name="fox",
    )(zb, zb, zb, qa, ka)


GLA_BLOCK = 4 * GLA_CHUNK


def _gla_decay_kernel(a_ref, wg_ref, bg_ref, cm_ref, o_ref, *, blocks):
    t = GLA_BLOCK
    for b in range(blocks):
        rows = slice(b * t, (b + 1) * t)
        gl = _dot(a_ref[rows, :].astype(BF16), wg_ref[...]) + bg_ref[...]
        g = _log_sigmoid(gl) / GLA_GATE_TAU
        o_ref[rows, :] = _dot_01(cm_ref[...], g)


def _gla_decay(small, w_gate_emb, b_gate, *, blocks=4):
    m = small.shape[0]
    t = GLA_BLOCK
    rows = blocks * t
    idx = jnp.arange(t)
    same = (idx[:, None] // GLA_CHUNK) == (idx[None, :] // GLA_CHUNK)
    cm = jnp.where(same & (idx[:, None] >= idx[None, :]), 1.0, 0.0).astype(BF16)
    cm3 = jnp.concatenate([cm, cm, cm], axis=1)
    return pl.pallas_call(
        functools.partial(_gla_decay_kernel, blocks=blocks),
        grid=(m // rows,),
        in_specs=[
            pl.BlockSpec((rows, LANE), lambda i: (i, 0)),
            pl.BlockSpec((LANE, GLA_KW), lambda i: (0, 0)),
            pl.BlockSpec((1, GLA_KW), lambda i: (0, 0)),
            pl.BlockSpec((t, 3 * t), lambda i: (0, 0)),
        ],
        out_specs=pl.BlockSpec((rows, GLA_KW), lambda i: (i, 0)),
        out_shape=jax.ShapeDtypeStruct((m, GLA_KW), F32),
        compiler_params=_params(("parallel",)),
        name="gla_decay",
    )(small, w_gate_emb, b_gate.reshape(1, GLA_KW), cm3)


def _gla_kernel(q_ref, k_ref, v_ref, r_ref, bc_ref, on_ref, o_ref):
    c = GLA_CHUNK
    t = GLA_BLOCK
    dk = GLA_DK
    seq = q_ref.shape[0]
    st = jnp.zeros((GLA_DV, dk), F32)
    row = lax.broadcasted_iota(jnp.int32, (t, t), 0)
    col = lax.broadcasted_iota(jnp.int32, (t, t), 1)
    intra = (row >= col) & ((row // c) == (col // c))
    o_gain = on_ref[...]

    for n in range(seq // t):
        rows = slice(n * t, (n + 1) * t)
        bc = bc_ref[rows, :]
        qb = q_ref[rows, :] * (dk ** -0.5)
        kb = k_ref[rows, :]
        vb = v_ref[rows, :]
        chunks = [slice(i * c, (i + 1) * c) for i in range(t // c)]
        b_mid = jnp.concatenate(
            [jnp.broadcast_to(bc[s.start + c // 2:s.start + c // 2 + 1, :], (c, dk)) for s in chunks], axis=0)
        b_last = jnp.concatenate(
            [jnp.broadcast_to(bc[s.stop - 1:s.stop, :], (c, dk)) for s in chunks], axis=0)
        qd = (qb * jnp.exp(bc - b_mid)).astype(BF16)
        kd = (kb * jnp.exp(b_mid - bc)).astype(BF16)
        ku = (kb * jnp.exp(b_last - bc)).astype(BF16)
        qi = (qb * jnp.exp(bc)).astype(BF16)
        att = jnp.where(intra, _dot_nt(qd, kd), 0.0)
        o = _dot(att.astype(BF16), vb)
        inter = []
        for s in chunks:
            inter.append(_dot_nt(qi[s, :], st.astype(BF16)))
            st = st * jnp.exp(bc[s.stop - 1:s.stop, :]) + _dot_tn(vb[s, :], ku[s, :])
        o = _rms(o + jnp.concatenate(inter, axis=0), o_gain)
        o_ref[rows, :] = (o * _silu(r_ref[rows, :])).astype(o_ref.dtype)


def _gla(zf, zb, bc, o_norm, batch, seq):
    m = zf.shape[0]
    dk, dv = GLA_DK, GLA_DV
    return pl.pallas_call(
        _gla_kernel,
        grid=(batch, GLA_HEADS),
        in_specs=[
            pl.BlockSpec((seq, dk), lambda b, h: (b, ZF_GQ + h)),
            pl.BlockSpec((seq, dk), lambda b, h: (b, ZF_GK + h)),
            pl.BlockSpec((seq, dv), lambda b, h: (b, ZB_GV * LANE // dv + h)),
            pl.BlockSpec((seq, dv), lambda b, h: (b, ZF_GR * LANE // dv + h)),
            pl.BlockSpec((seq, dk), lambda b, h: (b, h)),
            pl.BlockSpec((1, dv), lambda b, h: (0, h)),
        ],
        out_specs=pl.BlockSpec((seq, dv), lambda b, h: (b, h)),
        out_shape=jax.ShapeDtypeStruct((m, GLA_VW), BF16),
        compiler_params=_params(("parallel", "parallel")),
        name="gla",
    )(zf, zf, zb, zf, bc, o_norm.reshape(1, GLA_VW))


def _merge_kernel(a_ref, b_ref, c_ref, ga_ref, gb_ref, gc_ref, wa_ref, wb_ref, wc_ref,
                  wo_ref, h_ref, o_ref):
    mix = (jax.nn.sigmoid(ga_ref[...]) * _dot(a_ref[...], wa_ref[...])
           + jax.nn.sigmoid(gb_ref[...]) * _dot(b_ref[...], wb_ref[...])
           + jax.nn.sigmoid(gc_ref[...]) * _dot(c_ref[...], wc_ref[...]))
    o_ref[...] = h_ref[...] + _dot(mix.astype(BF16), wo_ref[...])


def _merge(ya, yb, yc, zf, wa, wb, wc, wo, h, layer, *, tm=256):
    m, d = h.shape
    kw = ya.shape[1]
    g0 = ZF_GATES * LANE // d
    branch = pl.BlockSpec((tm, kw), lambda i: (i, 0))
    resident = dict(pipeline_mode=pl.Buffered(1))
    wspec = pl.BlockSpec((None, kw, d), lambda i: (layer, 0, 0), **resident)
    return pl.pallas_call(
        _merge_kernel,
        grid=(m // tm,),
        in_specs=[
            branch, branch, branch,
            pl.BlockSpec((tm, d), lambda i: (i, g0)),
            pl.BlockSpec((tm, d), lambda i: (i, g0 + 1)),
            pl.BlockSpec((tm, d), lambda i: (i, g0 + 2)),
            wspec, wspec, wspec,
            pl.BlockSpec((None, d, d), lambda i: (layer, 0, 0), **resident),
            pl.BlockSpec((tm, d), lambda i: (i, 0)),
        ],
        out_specs=pl.BlockSpec((tm, d), lambda i: (i, 0)),
        out_shape=jax.ShapeDtypeStruct((m, d), F32),
        compiler_params=_params(("parallel",)),
        name="merge",
    )(ya, yb, yc, zf, zf, zf, wa, wb, wc, wo, h)


def _xattn_kernel(h_ref, g_ref, wq_ref, kv_ref, wo_ref, o_ref):
    dh = XA_HEAD_DIM
    x = h_ref[...]
    n = _rms(x, g_ref[...]).astype(BF16)
    q = _dot(n, wq_ref[...]) * (dh ** -0.5)
    outs = []
    for hd in range(XA_HEADS):
        qh = q[:, hd * dh:(hd + 1) * dh].astype(BF16)
        kh = kv_ref[:, hd * dh:(hd + 1) * dh].astype(BF16)
        vh = kv_ref[:, XA_WIDTH + hd * dh:XA_WIDTH + (hd + 1) * dh].astype(BF16)
        s = _dot_nt(qh, kh)
        e = jnp.exp(s - jnp.max(s, axis=-1, keepdims=True))
        p = e / jnp.sum(e, axis=-1, keepdims=True)
        outs.append(_dot(p.astype(BF16), vh))
    o = jnp.concatenate(outs, axis=1).astype(BF16)
    o_ref[...] = x + _dot(o, wo_ref[...])


def _xattn(h, norm_g, wq, kv, wo, layer, seq, mem_len, *, tm=512):
    m, d = h.shape
    per_batch = seq // tm
    return pl.pallas_call(
        _xattn_kernel,
        grid=(m // tm,),
        in_specs=[
            pl.BlockSpec((tm, d), lambda i: (i, 0)),
            pl.BlockSpec((1, d), lambda i: (0, 0)),
            pl.BlockSpec((None, d, XA_WIDTH), lambda i: (layer, 0, 0)),
            pl.BlockSpec((mem_len, 2 * XA_WIDTH), lambda i: (i // per_batch, 0)),
            pl.BlockSpec((None, XA_WIDTH, d), lambda i: (layer, 0, 0)),
        ],
        out_specs=pl.BlockSpec((tm, d), lambda i: (i, 0)),
        out_shape=jax.ShapeDtypeStruct((m, d), F32),
        compiler_params=_params(("parallel",)),
        name="xattn",
    )(h, norm_g.reshape(1, d), wq, kv, wo)


CAST_BLOCK_BYTES = 6 * 1024 * 1024


def _cast_kernel(x_ref, o_ref):
    o_ref[...] = x_ref[...].astype(o_ref.dtype)


def _to_bf16(w):
    lead, c = w.shape[:-1], w.shape[-1]
    w2 = w.reshape(-1, c)
    r = w2.shape[0]
    rows = 16
    while rows * 2 * c * 4 <= CAST_BLOCK_BYTES and r % (rows * 2) == 0:
        rows *= 2
    spec = pl.BlockSpec((rows, c), lambda i: (i, 0))
    out = pl.pallas_call(
        _cast_kernel,
        grid=(r // rows,),
        in_specs=[spec],
        out_specs=spec,
        out_shape=jax.ShapeDtypeStruct((r, c), BF16),
        compiler_params=_params(("parallel",)),
        name="cast",
    )(w2)
    return out.reshape(*lead, c)

W_FQ = 2 * SGU_WIDTH
W_FF = W_FQ + 3 * FOX_WIDTH
W_GQ = W_FF + FOX_HEADS
W_GV = W_GQ + 2 * GLA_KW
W_GA = W_GV + GLA_VW
W_GR = W_GA + GLA_GATE_RANK
W_END = W_GR + GLA_VW + 3 * D_MODEL


REPACK_CHUNK = 1024


def _repack_kernel(wt_ref, wf_ref, wb_ref, ws_ref):
    def move(dst_ref, dst, src, width):
        for off in range(0, width, REPACK_CHUNK):
            n = min(REPACK_CHUNK, width - off)
            dst_ref[:, dst + off:dst + off + n] = wt_ref[src + off:src + off + n, :].T.astype(BF16)

    move(wf_ref, 0, 0, W_FQ)
    move(wf_ref, W_FQ, W_GQ, 2 * GLA_KW)
    move(wf_ref, W_FQ + 2 * GLA_KW, W_GR, W_END - W_GR)
    move(wb_ref, 0, W_FQ, 3 * FOX_WIDTH)
    move(wb_ref, 3 * FOX_WIDTH, W_GV, GLA_VW)
    lane = lax.broadcasted_iota(jnp.int32, (wt_ref.shape[1], LANE), 1)
    ff = wt_ref[W_FF - SMALL_FF:W_FF - SMALL_FF + LANE, :].T
    ga = wt_ref[W_GA - SMALL_GA:W_GA - SMALL_GA + LANE, :].T
    side = jnp.where(lane < SMALL_FF + FOX_HEADS, ff,
                     jnp.where(lane < SMALL_GA + GLA_GATE_RANK, ga, 0.0))
    ws_ref[...] = side.astype(BF16)


def _repack_w_in(w, *, kb=256):
    nl, d, n = w.shape
    wt = jnp.swapaxes(w, 1, 2)

    def spec(width):
        return pl.BlockSpec((None, kb, width), lambda l, i: (l, i, 0))

    return pl.pallas_call(
        _repack_kernel,
        grid=(nl, d // kb),
        in_specs=[pl.BlockSpec((None, n, kb), lambda l, i: (l, 0, i))],
        out_specs=[spec(ZF_WIDTH), spec(ZB_WIDTH), spec(LANE)],
        out_shape=[jax.ShapeDtypeStruct((nl, d, width), BF16) for width in (ZF_WIDTH, ZB_WIDTH, LANE)],
        compiler_params=_params(("parallel", "parallel")),
        name="repack",
    )(wt)


def _embed_w_gate(w_gate):
    emb = jnp.zeros((LANE, GLA_KW), BF16)
    return emb.at[SMALL_GA:SMALL_GA + GLA_GATE_RANK, :].set(w_gate.astype(BF16))


def kernel(x, mem, ffn1_norm, ffn1_w_in, ffn1_w_out, mix_norm, w_in, sgu_ln_g, sgu_ln_b, sgu_w_s, sgu_b_s, fox_b_f, gla_w_gate, gla_b_gate, gla_o_norm, w_branch_a, w_branch_b, w_branch_c, w_out, xa_norm, mem_norm, xa_w_q, xa_w_kv, xa_w_o, ffn2_norm, ffn2_w_in, ffn2_w_out, final_norm):
    batch, seq, d = x.shape
    mem_len = mem.shape[1]
    h = x.reshape(batch * seq, d)
    mem2 = mem.reshape(batch * mem_len, d)
    ones_f = jnp.ones((1, ZF_WIDTH), F32)
    scale_b = jnp.ones((1, ZB_WIDTH), F32).at[:, ZB_FQ * LANE:ZB_FQ * LANE + FOX_WIDTH].set(
        FOX_HEAD_DIM ** -0.5)
    f1_in, f1_out = _to_bf16(ffn1_w_in), _to_bf16(ffn1_w_out)
    f2_in, f2_out = _to_bf16(ffn2_w_in), _to_bf16(ffn2_w_out)
    wa, wb, wc, wo = (_to_bf16(w) for w in (w_branch_a, w_branch_b, w_branch_c, w_out))
    xq, xkv, xo = _to_bf16(xa_w_q), _to_bf16(xa_w_kv), _to_bf16(xa_w_o)
    w_f, w_b, w_small = _repack_w_in(w_in)
    for l in range(DEPTH):
        h, xn = _ffn(h, ffn1_norm[l], f1_in, f1_out, l, mix_norm[l], mode="emit_norm")

        zf, small = _proj(xn, w_f, l, ones_f, F32, w_small)
        zb = _proj(xn, w_b, l, scale_b, BF16)
        ya = _sgu(zf, sgu_ln_g[l], sgu_ln_b[l], sgu_w_s[l], sgu_b_s[l])
        qa, ka = _fox_gate(small, fox_b_f[l], batch, seq)
        yb = _fox(zb, qa, ka, batch, seq)
        bc = _gla_decay(small, _embed_w_gate(gla_w_gate[l]), gla_b_gate[l])
        yc = _gla(zf, zb, bc, gla_o_norm[l], batch, seq)
        h = _merge(ya, yb, yc, zf, wa, wb, wc, wo, h, l)

        kv = _norm_proj(mem2, mem_norm[l], xkv, l)
        h = _xattn(h, xa_norm[l], xq, kv, xo, l, seq, mem_len)

        if l == DEPTH - 1:
            h = _ffn(h, ffn2_norm[l], f2_in, f2_out, l, final_norm, mode="final")
        else:
            h = _ffn(h, ffn2_norm[l], f2_in, f2_out, l)
    return h.reshape(batch, seq, d)
```

```python
import functools

import jax
import jax.numpy as jnp
from jax import lax
from jax.experimental import pallas as pl
from jax.experimental.pallas import tpu as pltpu

F32 = jnp.float32
BF16 = jnp.bfloat16

D_MODEL = 2048
DEPTH = 2
D_FF = 5632
EPS = 1e-6

SGU_GROUPS = 4
SGU_GROUP_DIM = 256
SGU_WIDTH = 1024
SGU_CHUNK = 128

FOX_HEADS = 8
FOX_HEAD_DIM = 128
FOX_WIDTH = 1024

GLA_HEADS = 4
GLA_DK = 128
GLA_DV = 256
GLA_KW = 512
GLA_VW = 1024
GLA_GATE_RANK = 16
GLA_GATE_TAU = 16.0
GLA_CHUNK = 64

XA_HEADS = 4
XA_HEAD_DIM = 128
XA_WIDTH = 512

LANE = 128

ZF_SU, ZF_SV, ZF_GQ, ZF_GK, ZF_GR, ZF_GATES = 0, 8, 16, 20, 24, 32
ZF_WIDTH = 80 * LANE
ZB_FQ, ZB_FK, ZB_FV, ZB_GV = 0, 8, 16, 24
ZB_WIDTH = 32 * LANE
SMALL_FF = 0
SMALL_GA = 8

NEG_BIG = -1e30
VMEM_LIMIT = 56 * 1024 * 1024


def _params(sem):
    return pltpu.CompilerParams(dimension_semantics=sem, vmem_limit_bytes=VMEM_LIMIT)


def _rms(x, g):
    return x * lax.rsqrt(jnp.mean(x * x, axis=-1, keepdims=True) + EPS) * g


def _log_sigmoid(x):
    return jnp.minimum(x, 0.0) - jnp.log1p(jnp.exp(-jnp.abs(x)))


def _gelu_tanh(x):
    c = 0.7978845608028654
    return x * (0.5 * (1.0 + jnp.tanh(c * (x + 0.044715 * (x * x * x)))))


def _silu(x):
    return x * jax.nn.sigmoid(x)


def _dot(a, b):
    return jnp.dot(a, b, preferred_element_type=F32)


def _dot_nt(a, b):
    return lax.dot_general(a, b, (((1,), (1,)), ((), ())), preferred_element_type=F32)


def _dot_tn(a, b):
    return lax.dot_general(a, b, (((0,), (0,)), ((), ())), preferred_element_type=F32)


def _split3(x):
    hi = x.astype(BF16)
    r = x - hi.astype(F32)
    lo = r.astype(BF16)
    lo2 = (r - lo.astype(F32)).astype(BF16)
    return hi, lo, lo2


def _dot_01(mat3, x):
    return _dot(mat3, jnp.concatenate(_split3(x), axis=0))


def _ffn_kernel(*refs, mode):
    if mode == "plain":
        x_ref, g_ref, wg_ref, wu_ref, wo_ref, o_ref, xn_ref = refs
    elif mode == "final":
        x_ref, g_ref, wg_ref, wu_ref, wo_ref, ng_ref, o_ref, xn_ref = refs
    else:
        x_ref, g_ref, wg_ref, wu_ref, wo_ref, ng_ref, o_ref, on_ref, xn_ref = refs
    j = pl.program_id(1)

    @pl.when(j == 0)
    def _():
        x = x_ref[...]
        xn_ref[...] = _rms(x, g_ref[...]).astype(BF16)
        o_ref[...] = x

    xn = xn_ref[...]
    gate = _dot(xn, wg_ref[...])
    up = _dot(xn, wu_ref[...])
    act = (_silu(gate) * up * 0.5).astype(BF16)
    o_ref[...] += _dot(act, wo_ref[...])

    if mode != "plain":
        @pl.when(j == pl.num_programs(1) - 1)
        def _():
            y = _rms(o_ref[...], ng_ref[...])
            if mode == "final":
                o_ref[...] = y
            else:
                on_ref[...] = y.astype(BF16)


def _ffn(h, norm_g, w_in, w_out, layer, next_g=None, *, mode="plain", tm=512, tf=512):
    m, d = h.shape
    f = w_out.shape[1]
    nj = f // tf
    vec = pl.BlockSpec((1, d), lambda i, j: (0, 0))
    rows = pl.BlockSpec((tm, d), lambda i, j: (i, 0))
    in_specs = [
        rows, vec,
        pl.BlockSpec((None, d, tf), lambda i, j: (layer, 0, j)),
        pl.BlockSpec((None, d, tf), lambda i, j: (layer, 0, j + nj)),
        pl.BlockSpec((None, tf, d), lambda i, j: (layer, j, 0)),
    ]
    args = [h, norm_g.reshape(1, d), w_in, w_in, w_out]
    out_specs = [rows]
    out_shape = [jax.ShapeDtypeStruct((m, d), F32)]
    if mode != "plain":
        in_specs.append(vec)
        args.append(next_g.reshape(1, d))
    if mode == "emit_norm":
        out_specs.append(rows)
        out_shape.append(jax.ShapeDtypeStruct((m, d), BF16))
    out = pl.pallas_call(
        functools.partial(_ffn_kernel, mode=mode),
        grid=(m // tm, nj),
        in_specs=in_specs,
        out_specs=out_specs,
        out_shape=out_shape,
        scratch_shapes=[pltpu.VMEM((tm, d), BF16)],
        compiler_params=_params(("parallel", "arbitrary")),
        name="ffn",
    )(*args)
    return out if mode == "emit_norm" else out[0]


def _proj_kernel(*refs, small):
    if small:
        x_ref, w_ref, cs_ref, ws_ref, o_ref, os_ref = refs

        @pl.when(pl.program_id(1) == 0)
        def _():
            os_ref[...] = _dot(x_ref[...], ws_ref[...])
    else:
        x_ref, w_ref, cs_ref, o_ref = refs
    o_ref[...] = (_dot(x_ref[...], w_ref[...]) * cs_ref[...]).astype(o_ref.dtype)


def _proj(xn, w, layer, col_scale, out_dtype, w_small=None, *, tm=1024, tn=2048):
    m, d = xn.shape
    n = w.shape[2]
    small = w_small is not None
    in_specs = [
        pl.BlockSpec((tm, d), lambda i, j: (i, 0)),
        pl.BlockSpec((None, d, tn), lambda i, j: (layer, 0, j)),
        pl.BlockSpec((1, tn), lambda i, j: (0, j)),
    ]
    args = [xn, w, col_scale]
    out_specs = [pl.BlockSpec((tm, tn), lambda i, j: (i, j))]
    out_shape = [jax.ShapeDtypeStruct((m, n), out_dtype)]
    if small:
        in_specs.append(pl.BlockSpec((None, d, LANE), lambda i, j: (layer, 0, 0)))
        args.append(w_small)
        out_specs.append(pl.BlockSpec((tm, LANE), lambda i, j: (i, 0)))
        out_shape.append(jax.ShapeDtypeStruct((m, LANE), F32))
    out = pl.pallas_call(
        functools.partial(_proj_kernel, small=small),
        grid=(m // tm, n // tn),
        in_specs=in_specs,
        out_specs=out_specs,
        out_shape=out_shape,
        compiler_params=_params(("parallel", "arbitrary")),
        name="proj",
    )(*args)
    return out if small else out[0]


def _norm_proj_kernel(x_ref, g_ref, w_ref, o_ref):
    o_ref[...] = _dot(_rms(x_ref[...], g_ref[...]).astype(BF16), w_ref[...])


def _norm_proj(x, norm_g, w, layer, *, tm=1024):
    m, d = x.shape
    n = w.shape[2]
    return pl.pallas_call(
        _norm_proj_kernel,
        grid=(m // tm,),
        in_specs=[
            pl.BlockSpec((tm, d), lambda i: (i, 0)),
            pl.BlockSpec((1, d), lambda i: (0, 0)),
            pl.BlockSpec((None, d, n), lambda i: (layer, 0, 0)),
        ],
        out_specs=pl.BlockSpec((tm, n), lambda i: (i, 0)),
        out_shape=jax.ShapeDtypeStruct((m, n), F32),
        compiler_params=_params(("parallel",)),
        name="mem_kv",
    )(x, norm_g.reshape(1, d), w)


def _sgu_kernel(u_ref, v_ref, lng_ref, lnb_ref, ws_ref, bs_ref, o_ref, *, chunks):
    t = SGU_CHUNK
    gd = SGU_GROUP_DIM
    row = lax.broadcasted_iota(jnp.int32, (t, t), 0)
    col = lax.broadcasted_iota(jnp.int32, (t, t), 1)
    causal = row >= col
    for g in range(SGU_GROUPS):
        w = jnp.where(causal, ws_ref[g], 0.0).astype(BF16)
        bias = bs_ref[:, g:g + 1]
        ln_g = lng_ref[:, g * gd:(g + 1) * gd]
        ln_b = lnb_ref[:, g * gd:(g + 1) * gd]
        for c in range(chunks):
            rows = slice(c * t, (c + 1) * t)
            cols = slice(g * gd, (g + 1) * gd)
            v = _gelu_tanh(v_ref[rows, cols])
            mu = jnp.mean(v, axis=-1, keepdims=True)
            vc = v - mu
            var = jnp.mean(vc * vc, axis=-1, keepdims=True)
            vn = vc * lax.rsqrt(var + EPS) * ln_g + ln_b
            mixed = _dot(w, vn.astype(BF16)) + bias
            o_ref[rows, cols] = (_gelu_tanh(u_ref[rows, cols]) * mixed).astype(o_ref.dtype)


def _sgu(zf, ln_g, ln_b, w_s, b_s, *, chunks=4):
    m = zf.shape[0]
    rows = chunks * SGU_CHUNK
    w = SGU_WIDTH
    return pl.pallas_call(
        functools.partial(_sgu_kernel, chunks=chunks),
        grid=(m // rows,),
        in_specs=[
            pl.BlockSpec((rows, w), lambda i: (i, ZF_SU * LANE // w)),
            pl.BlockSpec((rows, w), lambda i: (i, ZF_SV * LANE // w)),
            pl.BlockSpec((1, w), lambda i: (0, 0)),
            pl.BlockSpec((1, w), lambda i: (0, 0)),
            pl.BlockSpec((SGU_GROUPS, SGU_CHUNK, SGU_CHUNK), lambda i: (0, 0, 0)),
            pl.BlockSpec((SGU_CHUNK, SGU_GROUPS), lambda i: (0, 0)),
        ],
        out_specs=pl.BlockSpec((rows, w), lambda i: (i, 0)),
        out_shape=jax.ShapeDtypeStruct((m, w), BF16),
        compiler_params=_params(("parallel",)),
        name="sgu",
    )(zf, zf, ln_g.reshape(1, w), ln_b.reshape(1, w), w_s, jnp.swapaxes(b_s, 0, 1))


FOX_AUG = 6


def _fox_gate_kernel(f_ref, bf_ref, tril_ref, pq_ref, pk_ref, cq_ref, ck_ref, qa_ref, ka_ref):
    t = LANE
    seq = f_ref.shape[0]
    carry = jnp.zeros((1, t), F32)
    for b in range(seq // t):
        rows = slice(b * t, (b + 1) * t)
        lf = _log_sigmoid(f_ref[rows, :] + bf_ref[...])
        cum = _dot_01(tril_ref[...], lf) + carry
        carry = cum[t - 1:t, :]
        parts = jnp.concatenate(_split3(cum), axis=1)
        qa_ref[rows, :] = (_dot(parts, pq_ref[...]) + cq_ref[...]).astype(BF16)
        ka_ref[rows, :] = (_dot(parts, pk_ref[...]) + ck_ref[...]).astype(BF16)


def _tril3(n):
    tril = jnp.tril(jnp.ones((n, n), BF16))
    return jnp.concatenate([tril, tril, tril], axis=1)


def _fox_placement():
    src = jnp.arange(3 * LANE)[:, None]
    dst = jnp.arange(LANE)[None, :]
    p, h = src // LANE, src % LANE
    valid = h < FOX_HEADS
    pq = jnp.where(valid & (dst == h * FOX_AUG + p), 1.0, 0.0).astype(BF16)
    pk = jnp.where(valid & (dst == h * FOX_AUG + 3 + p), -1.0, 0.0).astype(BF16)
    used = dst < FOX_HEADS * FOX_AUG
    cq = jnp.where(used & (dst % FOX_AUG >= 3), 1.0, 0.0).astype(F32)
    ck = jnp.where(used & (dst % FOX_AUG < 3), 1.0, 0.0).astype(F32)
    return pq, pk, cq, ck


def _fox_gate(small, b_f, batch, seq):
    bf = jnp.zeros((1, LANE), F32).at[0, SMALL_FF:SMALL_FF + FOX_HEADS].set(b_f)
    w = LANE
    aug = jax.ShapeDtypeStruct((batch * seq, w), BF16)

    def const(shape):
        return pl.BlockSpec(shape, lambda b: (0, 0))

    return pl.pallas_call(
        _fox_gate_kernel,
        grid=(batch,),
        in_specs=[
            pl.BlockSpec((seq, LANE), lambda b: (b, 0)),
            const((1, LANE)), const((LANE, 3 * LANE)),
            const((3 * LANE, w)), const((3 * LANE, w)), const((1, w)), const((1, w)),
        ],
        out_specs=[pl.BlockSpec((seq, w), lambda b: (b, 0))] * 2,
        out_shape=[aug, aug],
        compiler_params=_params(("parallel",)),
        name="fox_gate",
    )(small, bf, _tril3(LANE), *_fox_placement())


def _fox_kernel(q_ref, k_ref, v_ref, qa_ref, ka_ref, o_ref, *, t, heads):
    seq = q_ref.shape[0]
    dh = FOX_HEAD_DIM
    lane = lax.broadcasted_iota(jnp.int32, (t, LANE), 1)
    ones_col = jnp.where(lane == 0, 1.0, 0.0).astype(BF16)
    row = lax.broadcasted_iota(jnp.int32, (t, t), 0)
    col = lax.broadcasted_iota(jnp.int32, (t, t), 1)
    causal = row >= col

    for g in range(heads):
        head = pl.program_id(1) * heads + g
        own = (lane >= head * FOX_AUG) & (lane < (head + 1) * FOX_AUG)
        cols = slice(g * dh, (g + 1) * dh)

        def k_block(j):
            rows = slice(j * t, (j + 1) * t)
            ka = jnp.where(own, ka_ref[rows, :], jnp.zeros((), BF16))
            return (jnp.concatenate([k_ref[rows, cols], ka], axis=1),
                    jnp.concatenate([v_ref[rows, cols], ones_col], axis=1))

        for i in range(seq // t):
            rows = slice(i * t, (i + 1) * t)
            qc = jnp.concatenate([q_ref[rows, cols], qa_ref[rows, :]], axis=1)
            kc, vc = k_block(i)
            s = jnp.where(causal, _dot_nt(qc, kc), NEG_BIG)
            m = jnp.max(s, axis=1, keepdims=True)
            acc = _dot(jnp.exp(s - m).astype(BF16), vc)
            for j in range(i):
                kc, vc = k_block(j)
                s = _dot_nt(qc, kc)
                m_new = jnp.maximum(m, jnp.max(s, axis=1, keepdims=True))
                acc = jnp.exp(m - m_new) * acc + _dot(jnp.exp(s - m_new).astype(BF16), vc)
                m = m_new
            o_ref[rows, cols] = (acc[:, :dh] / acc[:, dh:dh + 1]).astype(o_ref.dtype)


def _fox(zb, qa, ka, batch, seq, *, t=512, heads=2):
    m = zb.shape[0]
    w = heads * FOX_HEAD_DIM

    def group(seg):
        return pl.BlockSpec((seq, w), lambda b, h: (b, seg * LANE // w + h))

    aug = pl.BlockSpec((seq, LANE), lambda b, h: (b, 0))
    return pl.pallas_call(
        functools.partial(_fox_kernel, t=t, heads=heads),
        grid=(batch, FOX_HEADS // heads),
        in_specs=[group(ZB_FQ), group(ZB_FK), group(ZB_FV), aug, aug],
        out_specs=group(0),
        out_shape=jax.ShapeDtypeStruct((m, FOX_WIDTH), BF16),
        compiler_params=_params(("parallel", "parallel")),
        name="fox",
    )(zb, zb, zb, qa, ka)


GLA_BLOCK = 4 * GLA_CHUNK


def _gla_decay_kernel(a_ref, wg_ref, bg_ref, cm_ref, o_ref, *, blocks):
    t = GLA_BLOCK
    for b in range(blocks):
        rows = slice(b * t, (b + 1) * t)
        gl = _dot(a_ref[rows, :].astype(BF16), wg_ref[...]) + bg_ref[...]
        g = _log_sigmoid(gl) / GLA_GATE_TAU
        o_ref[rows, :] = _dot_01(cm_ref[...], g)


def _gla_decay(small, w_gate_emb, b_gate, *, blocks=4):
    m = small.shape[0]
    t = GLA_BLOCK
    rows = blocks * t
    idx = jnp.arange(t)
    same = (idx[:, None] // GLA_CHUNK) == (idx[None, :] // GLA_CHUNK)
    cm = jnp.where(same & (idx[:, None] >= idx[None, :]), 1.0, 0.0).astype(BF16)
    cm3 = jnp.concatenate([cm, cm, cm], axis=1)
    return pl.pallas_call(
        functools.partial(_gla_decay_kernel, blocks=blocks),
        grid=(m // rows,),
        in_specs=[
            pl.BlockSpec((rows, LANE), lambda i: (i, 0)),
            pl.BlockSpec((LANE, GLA_KW), lambda i: (0, 0)),
            pl.BlockSpec((1, GLA_KW), lambda i: (0, 0)),
            pl.BlockSpec((t, 3 * t), lambda i: (0, 0)),
        ],
        out_specs=pl.BlockSpec((rows, GLA_KW), lambda i: (i, 0)),
        out_shape=jax.ShapeDtypeStruct((m, GLA_KW), F32),
        compiler_params=_params(("parallel",)),
        name="gla_decay",
    )(small, w_gate_emb, b_gate.reshape(1, GLA_KW), cm3)


def _gla_kernel(q_ref, k_ref, v_ref, r_ref, bc_ref, on_ref, o_ref):
    c = GLA_CHUNK
    t = GLA_BLOCK
    dk = GLA_DK
    seq = q_ref.shape[0]
    st = jnp.zeros((GLA_DV, dk), F32)
    row = lax.broadcasted_iota(jnp.int32, (t, t), 0)
    col = lax.broadcasted_iota(jnp.int32, (t, t), 1)
    intra = (row >= col) & ((row // c) == (col // c))
    o_gain = on_ref[...]

    for n in range(seq // t):
        rows = slice(n * t, (n + 1) * t)
        bc = bc_ref[rows, :]
        qb = q_ref[rows, :] * (dk ** -0.5)
        kb = k_ref[rows, :]
        vb = v_ref[rows, :]
        chunks = [slice(i * c, (i + 1) * c) for i in range(t // c)]
        b_mid = jnp.concatenate(
            [jnp.broadcast_to(bc[s.start + c // 2:s.start + c // 2 + 1, :], (c, dk)) for s in chunks], axis=0)
        b_last = jnp.concatenate(
            [jnp.broadcast_to(bc[s.stop - 1:s.stop, :], (c, dk)) for s in chunks], axis=0)
        qd = (qb * jnp.exp(bc - b_mid)).astype(BF16)
        kd = (kb * jnp.exp(b_mid - bc)).astype(BF16)
        ku = (kb * jnp.exp(b_last - bc)).astype(BF16)
        qi = (qb * jnp.exp(bc)).astype(BF16)
        att = jnp.where(intra, _dot_nt(qd, kd), 0.0)
        o = _dot(att.astype(BF16), vb)
        inter = []
        for s in chunks:
            inter.append(_dot_nt(qi[s, :], st.astype(BF16)))
            st = st * jnp.exp(bc[s.stop - 1:s.stop, :]) + _dot_tn(vb[s, :], ku[s, :])
        o = _rms(o + jnp.concatenate(inter, axis=0), o_gain)
        o_ref[rows, :] = (o * _silu(r_ref[rows, :])).astype(o_ref.dtype)


def _gla(zf, zb, bc, o_norm, batch, seq):
    m = zf.shape[0]
    dk, dv = GLA_DK, GLA_DV
    return pl.pallas_call(
        _gla_kernel,
        grid=(batch, GLA_HEADS),
        in_specs=[
            pl.BlockSpec((seq, dk), lambda b, h: (b, ZF_GQ + h)),
            pl.BlockSpec((seq, dk), lambda b, h: (b, ZF_GK + h)),
            pl.BlockSpec((seq, dv), lambda b, h: (b, ZB_GV * LANE // dv + h)),
            pl.BlockSpec((seq, dv), lambda b, h: (b, ZF_GR * LANE // dv + h)),
            pl.BlockSpec((seq, dk), lambda b, h: (b, h)),
            pl.BlockSpec((1, dv), lambda b, h: (0, h)),
        ],
        out_specs=pl.BlockSpec((seq, dv), lambda b, h: (b, h)),
        out_shape=jax.ShapeDtypeStruct((m, GLA_VW), BF16),
        compiler_params=_params(("parallel", "parallel")),
        name="gla",
    )(zf, zf, zb, zf, bc, o_norm.reshape(1, GLA_VW))


def _merge_kernel(a_ref, b_ref, c_ref, ga_ref, gb_ref, gc_ref, wa_ref, wb_ref, wc_ref,
                  wo_ref, h_ref, o_ref):
    mix = (jax.nn.sigmoid(ga_ref[...]) * _dot(a_ref[...], wa_ref[...])
           + jax.nn.sigmoid(gb_ref[...]) * _dot(b_ref[...], wb_ref[...])
           + jax.nn.sigmoid(gc_ref[...]) * _dot(c_ref[...], wc_ref[...]))
    o_ref[...] = h_ref[...] + _dot(mix.astype(BF16), wo_ref[...])


def _merge(ya, yb, yc, zf, wa, wb, wc, wo, h, layer, *, tm=256):
    m, d = h.shape
    kw = ya.shape[1]
    g0 = ZF_GATES * LANE // d
    branch = pl.BlockSpec((tm, kw), lambda i: (i, 0))
    resident = dict(pipeline_mode=pl.Buffered(1))
    wspec = pl.BlockSpec((None, kw, d), lambda i: (layer, 0, 0), **resident)
    return pl.pallas_call(
        _merge_kernel,
        grid=(m // tm,),
        in_specs=[
            branch, branch, branch,
            pl.BlockSpec((tm, d), lambda i: (i, g0)),
            pl.BlockSpec((tm, d), lambda i: (i, g0 + 1)),
            pl.BlockSpec((tm, d), lambda i: (i, g0 + 2)),
            wspec, wspec, wspec,
            pl.BlockSpec((None, d, d), lambda i: (layer, 0, 0), **resident),
            pl.BlockSpec((tm, d), lambda i: (i, 0)),
        ],
        out_specs=pl.BlockSpec((tm, d), lambda i: (i, 0)),
        out_shape=jax.ShapeDtypeStruct((m, d), F32),
        compiler_params=_params(("parallel",)),
        name="merge",
    )(ya, yb, yc, zf, zf, zf, wa, wb, wc, wo, h)


def _xattn_kernel(h_ref, g_ref, wq_ref, kv_ref, wo_ref, o_ref):
    dh = XA_HEAD_DIM
    x = h_ref[...]
    n = _rms(x, g_ref[...]).astype(BF16)
    q = _dot(n, wq_ref[...]) * (dh ** -0.5)
    outs = []
    for hd in range(XA_HEADS):
        qh = q[:, hd * dh:(hd + 1) * dh].astype(BF16)
        kh = kv_ref[:, hd * dh:(hd + 1) * dh].astype(BF16)
        vh = kv_ref[:, XA_WIDTH + hd * dh:XA_WIDTH + (hd + 1) * dh].astype(BF16)
        s = _dot_nt(qh, kh)
        e = jnp.exp(s - jnp.max(s, axis=-1, keepdims=True))
        p = e / jnp.sum(e, axis=-1, keepdims=True)
        outs.append(_dot(p.astype(BF16), vh))
    o = jnp.concatenate(outs, axis=1).astype(BF16)
    o_ref[...] = x + _dot(o, wo_ref[...])


def _xattn(h, norm_g, wq, kv, wo, layer, seq, mem_len, *, tm=512):
    m, d = h.shape
    per_batch = seq // tm
    return pl.pallas_call(
        _xattn_kernel,
        grid=(m // tm,),
        in_specs=[
            pl.BlockSpec((tm, d), lambda i: (i, 0)),
            pl.BlockSpec((1, d), lambda i: (0, 0)),
            pl.BlockSpec((None, d, XA_WIDTH), lambda i: (layer, 0, 0)),
            pl.BlockSpec((mem_len, 2 * XA_WIDTH), lambda i: (i // per_batch, 0)),
            pl.BlockSpec((None, XA_WIDTH, d), lambda i: (layer, 0, 0)),
        ],
        out_specs=pl.BlockSpec((tm, d), lambda i: (i, 0)),
        out_shape=jax.ShapeDtypeStruct((m, d), F32),
        compiler_params=_params(("parallel",)),
        name="xattn",
    )(h, norm_g.reshape(1, d), wq, kv, wo)


CAST_BLOCK_BYTES = 6 * 1024 * 1024


def _cast_kernel(x_ref, o_ref):
    o_ref[...] = x_ref[...].astype(o_ref.dtype)


def _to_bf16(w):
    lead, c = w.shape[:-1], w.shape[-1]
    w2 = w.reshape(-1, c)
    r = w2.shape[0]
    rows = 16
    while rows * 2 * c * 4 <= CAST_BLOCK_BYTES and r % (rows * 2) == 0:
        rows *= 2
    spec = pl.BlockSpec((rows, c), lambda i: (i, 0))
    out = pl.pallas_call(
        _cast_kernel,
        grid=(r // rows,),
        in_specs=[spec],
        out_specs=spec,
        out_shape=jax.ShapeDtypeStruct((r, c), BF16),
        compiler_params=_params(("parallel",)),
        name="cast",
    )(w2)
    return out.reshape(*lead, c)

W_FQ = 2 * SGU_WIDTH
W_FF = W_FQ + 3 * FOX_WIDTH
W_GQ = W_FF + FOX_HEADS
W_GV = W_GQ + 2 * GLA_KW
W_GA = W_GV + GLA_VW
W_GR = W_GA + GLA_GATE_RANK
W_END = W_GR + GLA_VW + 3 * D_MODEL


REPACK_CHUNK = 1024


def _repack_kernel(wt_ref, wf_ref, wb_ref, ws_ref):
    def move(dst_ref, dst, src, width):
        for off in range(0, width, REPACK_CHUNK):
            n = min(REPACK_CHUNK, width - off)
            dst_ref[:, dst + off:dst + off + n] = wt_ref[src + off:src + off + n, :].T.astype(BF16)

    move(wf_ref, 0, 0, W_FQ)
    move(wf_ref, W_FQ, W_GQ, 2 * GLA_KW)
    move(wf_ref, W_FQ + 2 * GLA_KW, W_GR, W_END - W_GR)
    move(wb_ref, 0, W_FQ, 3 * FOX_WIDTH)
    move(wb_ref, 3 * FOX_WIDTH, W_GV, GLA_VW)
    lane = lax.broadcasted_iota(jnp.int32, (wt_ref.shape[1], LANE), 1)
    ff = wt_ref[W_FF - SMALL_FF:W_FF - SMALL_FF + LANE, :].T
    ga = wt_ref[W_GA - SMALL_GA:W_GA - SMALL_GA + LANE, :].T
    side = jnp.where(lane < SMALL_FF + FOX_HEADS, ff,
                     jnp.where(lane < SMALL_GA + GLA_GATE_RANK, ga, 0.0))
    ws_ref[...] = side.astype(BF16)


def _repack_w_in(w, *, kb=256):
    nl, d, n = w.shape
    wt = jnp.swapaxes(w, 1, 2)

    def spec(width):
        return pl.BlockSpec((None, kb, width), lambda l, i: (l, i, 0))

    return pl.pallas_call(
        _repack_kernel,
        grid=(nl, d // kb),
        in_specs=[pl.BlockSpec((None, n, kb), lambda l, i: (l, 0, i))],
        out_specs=[spec(ZF_WIDTH), spec(ZB_WIDTH), spec(LANE)],
        out_shape=[jax.ShapeDtypeStruct((nl, d, width), BF16) for width in (ZF_WIDTH, ZB_WIDTH, LANE)],
        compiler_params=_params(("parallel", "parallel")),
        name="repack",
    )(wt)


def _embed_w_gate(w_gate):
    emb = jnp.zeros((LANE, GLA_KW), BF16)
    return emb.at[SMALL_GA:SMALL_GA + GLA_GATE_RANK, :].set(w_gate.astype(BF16))


def kernel(x, mem, ffn1_norm, ffn1_w_in, ffn1_w_out, mix_norm, w_in, sgu_ln_g, sgu_ln_b, sgu_w_s, sgu_b_s, fox_b_f, gla_w_gate, gla_b_gate, gla_o_norm, w_branch_a, w_branch_b, w_branch_c, w_out, xa_norm, mem_norm, xa_w_q, xa_w_kv, xa_w_o, ffn2_norm, ffn2_w_in, ffn2_w_out, final_norm):
    batch, seq, d = x.shape
    mem_len = mem.shape[1]
    h = x.reshape(batch * seq, d)
    mem2 = mem.reshape(batch * mem_len, d)
    ones_f = jnp.ones((1, ZF_WIDTH), F32)
    scale_b = jnp.ones((1, ZB_WIDTH), F32).at[:, ZB_FQ * LANE:ZB_FQ * LANE + FOX_WIDTH].set(
        FOX_HEAD_DIM ** -0.5)
    f1_in, f1_out = _to_bf16(ffn1_w_in), _to_bf16(ffn1_w_out)
    f2_in, f2_out = _to_bf16(ffn2_w_in), _to_bf16(ffn2_w_out)
    wa, wb, wc, wo = (_to_bf16(w) for w in (w_branch_a, w_branch_b, w_branch_c, w_out))
    xq, xkv, xo = _to_bf16(xa_w_q), _to_bf16(xa_w_kv), _to_bf16(xa_w_o)
    w_f, w_b, w_small = _repack_w_in(w_in)
    for l in range(DEPTH):
        h, xn = _ffn(h, ffn1_norm[l], f1_in, f1_out, l, mix_norm[l], mode="emit_norm")

        zf, small = _proj(xn, w_f, l, ones_f, F32, w_small)
        zb = _proj(xn, w_b, l, scale_b, BF16)
        ya = _sgu(zf, sgu_ln_g[l], sgu_ln_b[l], sgu_w_s[l], sgu_b_s[l])
        qa, ka = _fox_gate(small, fox_b_f[l], batch, seq)
        yb = _fox(zb, qa, ka, batch, seq)
        bc = _gla_decay(small, _embed_w_gate(gla_w_gate[l]), gla_b_gate[l])
        yc = _gla(zf, zb, bc, gla_o_norm[l], batch, seq)
        h = _merge(ya, yb, yc, zf, wa, wb, wc, wo, h, l)

        kv = _norm_proj(mem2, mem_norm[l], xkv, l)
        h = _xattn(h, xa_norm[l], xq, kv, xo, l, seq, mem_len)

        if l == DEPTH - 1:
            h = _ffn(h, ffn2_norm[l], f2_in, f2_out, l, final_norm, mode="final")
        else:
            h = _ffn(h, ffn2_norm[l], f2_in, f2_out, l)
    return h.reshape(batch, seq, d)
```

```python
import functools

import jax
import jax.numpy as jnp
from jax import lax
from jax.experimental import pallas as pl
from jax.experimental.pallas import tpu as pltpu

F32 = jnp.float32
BF16 = jnp.bfloat16

D_MODEL = 2048
DEPTH = 2
D_FF = 5632
EPS = 1e-6

SGU_GROUPS = 4
SGU_GROUP_DIM = 256
SGU_WIDTH = 1024
SGU_CHUNK = 128

FOX_HEADS = 8
FOX_HEAD_DIM = 128
FOX_WIDTH = 1024

GLA_HEADS = 4
GLA_DK = 128
GLA_DV = 256
GLA_KW = 512
GLA_VW = 1024
GLA_GATE_RANK = 16
GLA_GATE_TAU = 16.0
GLA_CHUNK = 64

XA_HEADS = 4
XA_HEAD_DIM = 128
XA_WIDTH = 512

LANE = 128

ZF_SU, ZF_SV, ZF_GQ, ZF_GK, ZF_GR, ZF_GATES = 0, 8, 16, 20, 24, 32
ZF_WIDTH = 80 * LANE
ZB_FQ, ZB_FK, ZB_FV, ZB_GV = 0, 8, 16, 24
ZB_WIDTH = 32 * LANE
SMALL_FF = 0
SMALL_GA = 8

NEG_BIG = -1e30
VMEM_LIMIT = 63 * 1024 * 1024


def _params(sem):
    return pltpu.CompilerParams(dimension_semantics=sem, vmem_limit_bytes=VMEM_LIMIT)


def _rms(x, g):
    return x * lax.rsqrt(jnp.mean(x * x, axis=-1, keepdims=True) + EPS) * g


def _log_sigmoid(x):
    return jnp.minimum(x, 0.0) - jnp.log1p(jnp.exp(-jnp.abs(x)))


def _gelu_tanh(x):
    c = 0.7978845608028654
    return x * (0.5 * (1.0 + jnp.tanh(c * (x + 0.044715 * (x * x * x)))))


def _silu(x):
    return x * jax.nn.sigmoid(x)


def _dot(a, b):
    return jnp.dot(a, b, preferred_element_type=F32)


def _dot_nt(a, b):
    return lax.dot_general(a, b, (((1,), (1,)), ((), ())), preferred_element_type=F32)


def _dot_tn(a, b):
    return lax.dot_general(a, b, (((0,), (0,)), ((), ())), preferred_element_type=F32)


def _split3(x):
    hi = x.astype(BF16)
    r = x - hi.astype(F32)
    lo = r.astype(BF16)
    lo2 = (r - lo.astype(F32)).astype(BF16)
    return hi, lo, lo2


def _dot_01(mat3, x):
    return _dot(mat3, jnp.concatenate(_split3(x), axis=0))


def _ffn_kernel(*refs, mode):
    if mode == "plain":
        x_ref, g_ref, wg_ref, wu_ref, wo_ref, o_ref, xn_ref = refs
    elif mode == "final":
        x_ref, g_ref, wg_ref, wu_ref, wo_ref, ng_ref, o_ref, xn_ref = refs
    else:
        x_ref, g_ref, wg_ref, wu_ref, wo_ref, ng_ref, o_ref, on_ref, xn_ref = refs
    j = pl.program_id(1)

    @pl.when(j == 0)
    def _():
        x = x_ref[...]
        xn_ref[...] = _rms(x, g_ref[...]).astype(BF16)
        o_ref[...] = x

    xn = xn_ref[...]
    gate = _dot(xn, wg_ref[...])
    up = _dot(xn, wu_ref[...])
    act = (_silu(gate) * up * 0.5).astype(BF16)
    o_ref[...] += _dot(act, wo_ref[...])

    if mode != "plain":
        @pl.when(j == pl.num_programs(1) - 1)
        def _():
            y = _rms(o_ref[...], ng_ref[...])
            if mode == "final":
                o_ref[...] = y
            else:
                on_ref[...] = y.astype(BF16)


def _ffn(h, norm_g, w_in, w_out, layer, next_g=None, *, mode="plain", tm=1024, tf=512):
    m, d = h.shape
    f = w_out.shape[1]
    nj = f // tf
    vec = pl.BlockSpec((1, d), lambda i, j: (0, 0))
    rows = pl.BlockSpec((tm, d), lambda i, j: (i, 0))
    in_specs = [
        rows, vec,
        pl.BlockSpec((None, d, tf), lambda i, j: (layer, 0, j)),
        pl.BlockSpec((None, d, tf), lambda i, j: (layer, 0, j + nj)),
        pl.BlockSpec((None, tf, d), lambda i, j: (layer, j, 0)),
    ]
    args = [h, norm_g.reshape(1, d), w_in, w_in, w_out]
    out_specs = [rows]
    out_shape = [jax.ShapeDtypeStruct((m, d), F32)]
    if mode != "plain":
        in_specs.append(vec)
        args.append(next_g.reshape(1, d))
    if mode == "emit_norm":
        out_specs.append(rows)
        out_shape.append(jax.ShapeDtypeStruct((m, d), BF16))
    out = pl.pallas_call(
        functools.partial(_ffn_kernel, mode=mode),
        grid=(m // tm, nj),
        in_specs=in_specs,
        out_specs=out_specs,
        out_shape=out_shape,
        scratch_shapes=[pltpu.VMEM((tm, d), BF16)],
        compiler_params=_params(("parallel", "arbitrary")),
        name="ffn",
    )(*args)
    return out if mode == "emit_norm" else out[0]


def _proj_kernel(*refs, small):
    if small:
        x_ref, w_ref, cs_ref, ws_ref, o_ref, os_ref = refs

        @pl.when(pl.program_id(1) == 0)
        def _():
            os_ref[...] = _dot(x_ref[...], ws_ref[...])
    else:
        x_ref, w_ref, cs_ref, o_ref = refs
    o_ref[...] = (_dot(x_ref[...], w_ref[...]) * cs_ref[...]).astype(o_ref.dtype)


def _proj(xn, w, layer, col_scale, out_dtype, w_small=None, *, tm=1024, tn=2048):
    m, d = xn.shape
    n = w.shape[2]
    small = w_small is not None
    in_specs = [
        pl.BlockSpec((tm, d), lambda i, j: (i, 0)),
        pl.BlockSpec((None, d, tn), lambda i, j: (layer, 0, j)),
        pl.BlockSpec((1, tn), lambda i, j: (0, j)),
    ]
    args = [xn, w, col_scale]
    out_specs = [pl.BlockSpec((tm, tn), lambda i, j: (i, j))]
    out_shape = [jax.ShapeDtypeStruct((m, n), out_dtype)]
    if small:
        in_specs.append(pl.BlockSpec((None, d, LANE), lambda i, j: (layer, 0, 0)))
        args.append(w_small)
        out_specs.append(pl.BlockSpec((tm, LANE), lambda i, j: (i, 0)))
        out_shape.append(jax.ShapeDtypeStruct((m, LANE), F32))
    out = pl.pallas_call(
        functools.partial(_proj_kernel, small=small),
        grid=(m // tm, n // tn),
        in_specs=in_specs,
        out_specs=out_specs,
        out_shape=out_shape,
        compiler_params=_params(("parallel", "arbitrary")),
        name="proj",
    )(*args)
    return out if small else out[0]


def _norm_proj_kernel(x_ref, g_ref, w_ref, o_ref):
    o_ref[...] = _dot(_rms(x_ref[...], g_ref[...]).astype(BF16), w_ref[...])


def _norm_proj(x, norm_g, w, layer, *, tm=1024):
    m, d = x.shape
    n = w.shape[2]
    return pl.pallas_call(
        _norm_proj_kernel,
        grid=(m // tm,),
        in_specs=[
            pl.BlockSpec((tm, d), lambda i: (i, 0)),
            pl.BlockSpec((1, d), lambda i: (0, 0)),
            pl.BlockSpec((None, d, n), lambda i: (layer, 0, 0)),
        ],
        out_specs=pl.BlockSpec((tm, n), lambda i: (i, 0)),
        out_shape=jax.ShapeDtypeStruct((m, n), F32),
        compiler_params=_params(("parallel",)),
        name="mem_kv",
    )(x, norm_g.reshape(1, d), w)


def _sgu_kernel(u_ref, v_ref, lng_ref, lnb_ref, ws_ref, bs_ref, o_ref, *, chunks):
    t = SGU_CHUNK
    gd = SGU_GROUP_DIM
    row = lax.broadcasted_iota(jnp.int32, (t, t), 0)
    col = lax.broadcasted_iota(jnp.int32, (t, t), 1)
    causal = row >= col
    for g in range(SGU_GROUPS):
        w = jnp.where(causal, ws_ref[g], 0.0).astype(BF16)
        bias = bs_ref[:, g:g + 1]
        ln_g = lng_ref[:, g * gd:(g + 1) * gd]
        ln_b = lnb_ref[:, g * gd:(g + 1) * gd]
        for c in range(chunks):
            rows = slice(c * t, (c + 1) * t)
            cols = slice(g * gd, (g + 1) * gd)
            v = _gelu_tanh(v_ref[rows, cols])
            mu = jnp.mean(v, axis=-1, keepdims=True)
            vc = v - mu
            var = jnp.mean(vc * vc, axis=-1, keepdims=True)
            vn = vc * lax.rsqrt(var + EPS) * ln_g + ln_b
            mixed = _dot(w, vn.astype(BF16)) + bias
            o_ref[rows, cols] = (_gelu_tanh(u_ref[rows, cols]) * mixed).astype(o_ref.dtype)


def _sgu(zf, ln_g, ln_b, w_s, b_s, *, chunks=4):
    m = zf.shape[0]
    rows = chunks * SGU_CHUNK
    w = SGU_WIDTH
    return pl.pallas_call(
        functools.partial(_sgu_kernel, chunks=chunks),
        grid=(m // rows,),
        in_specs=[
            pl.BlockSpec((rows, w), lambda i: (i, ZF_SU * LANE // w)),
            pl.BlockSpec((rows, w), lambda i: (i, ZF_SV * LANE // w)),
            pl.BlockSpec((1, w), lambda i: (0, 0)),
            pl.BlockSpec((1, w), lambda i: (0, 0)),
            pl.BlockSpec((SGU_GROUPS, SGU_CHUNK, SGU_CHUNK), lambda i: (0, 0, 0)),
            pl.BlockSpec((SGU_CHUNK, SGU_GROUPS), lambda i: (0, 0)),
        ],
        out_specs=pl.BlockSpec((rows, w), lambda i: (i, 0)),
        out_shape=jax.ShapeDtypeStruct((m, w), BF16),
        compiler_params=_params(("parallel",)),
        name="sgu",
    )(zf, zf, ln_g.reshape(1, w), ln_b.reshape(1, w), w_s, jnp.swapaxes(b_s, 0, 1))


FOX_AUG = 6


def _fox_gate_kernel(f_ref, bf_ref, tril_ref, pq_ref, pk_ref, cq_ref, ck_ref, qa_ref, ka_ref):
    t = LANE
    seq = f_ref.shape[0]
    carry = jnp.zeros((1, t), F32)
    for b in range(seq // t):
        rows = slice(b * t, (b + 1) * t)
        lf = _log_sigmoid(f_ref[rows, :] + bf_ref[...])
        cum = _dot_01(tril_ref[...], lf) + carry
        carry = cum[t - 1:t, :]
        parts = jnp.concatenate(_split3(cum), axis=1)
        qa_ref[rows, :] = (_dot(parts, pq_ref[...]) + cq_ref[...]).astype(BF16)
        ka_ref[rows, :] = (_dot(parts, pk_ref[...]) + ck_ref[...]).astype(BF16)


def _tril3(n):
    tril = jnp.tril(jnp.ones((n, n), BF16))
    return jnp.concatenate([tril, tril, tril], axis=1)


def _fox_placement():
    src = jnp.arange(3 * LANE)[:, None]
    dst = jnp.arange(LANE)[None, :]
    p, h = src // LANE, src % LANE
    valid = h < FOX_HEADS
    pq = jnp.where(valid & (dst == h * FOX_AUG + p), 1.0, 0.0).astype(BF16)
    pk = jnp.where(valid & (dst == h * FOX_AUG + 3 + p), -1.0, 0.0).astype(BF16)
    used = dst < FOX_HEADS * FOX_AUG
    cq = jnp.where(used & (dst % FOX_AUG >= 3), 1.0, 0.0).astype(F32)
    ck = jnp.where(used & (dst % FOX_AUG < 3), 1.0, 0.0).astype(F32)
    return pq, pk, cq, ck


def _fox_gate(small, b_f, batch, seq):
    bf = jnp.zeros((1, LANE), F32).at[0, SMALL_FF:SMALL_FF + FOX_HEADS].set(b_f)
    w = LANE
    aug = jax.ShapeDtypeStruct((batch * seq, w), BF16)

    def const(shape):
        return pl.BlockSpec(shape, lambda b: (0, 0))

    return pl.pallas_call(
        _fox_gate_kernel,
        grid=(batch,),
        in_specs=[
            pl.BlockSpec((seq, LANE), lambda b: (b, 0)),
            const((1, LANE)), const((LANE, 3 * LANE)),
            const((3 * LANE, w)), const((3 * LANE, w)), const((1, w)), const((1, w)),
        ],
        out_specs=[pl.BlockSpec((seq, w), lambda b: (b, 0))] * 2,
        out_shape=[aug, aug],
        compiler_params=_params(("parallel",)),
        name="fox_gate",
    )(small, bf, _tril3(LANE), *_fox_placement())


def _fox_kernel(q_ref, k_ref, v_ref, qa_ref, ka_ref, o_ref, *, t, heads):
    seq = q_ref.shape[0]
    dh = FOX_HEAD_DIM
    lane = lax.broadcasted_iota(jnp.int32, (t, LANE), 1)
    ones_col = jnp.where(lane == 0, 1.0, 0.0).astype(BF16)
    row = lax.broadcasted_iota(jnp.int32, (t, t), 0)
    col = lax.broadcasted_iota(jnp.int32, (t, t), 1)
    causal = row >= col

    for g in range(heads):
        head = pl.program_id(1) * heads + g
        own = (lane >= head * FOX_AUG) & (lane < (head + 1) * FOX_AUG)
        cols = slice(g * dh, (g + 1) * dh)

        def k_block(j):
            rows = slice(j * t, (j + 1) * t)
            ka = jnp.where(own, ka_ref[rows, :], jnp.zeros((), BF16))
            return (jnp.concatenate([k_ref[rows, cols], ka], axis=1),
                    jnp.concatenate([v_ref[rows, cols], ones_col], axis=1))

        for i in range(seq // t):
            rows = slice(i * t, (i + 1) * t)
            qc = jnp.concatenate([q_ref[rows, cols], qa_ref[rows, :]], axis=1)
            kc, vc = k_block(i)
            s = jnp.where(causal, _dot_nt(qc, kc), NEG_BIG)
            m = jnp.max(s, axis=1, keepdims=True)
            acc = _dot(jnp.exp(s - m).astype(BF16), vc)
            for j in range(i):
                kc, vc = k_block(j)
                s = _dot_nt(qc, kc)
                m_new = jnp.maximum(m, jnp.max(s, axis=1, keepdims=True))
                acc = jnp.exp(m - m_new) * acc + _dot(jnp.exp(s - m_new).astype(BF16), vc)
                m = m_new
            o_ref[rows, cols] = (acc[:, :dh] / acc[:, dh:dh + 1]).astype(o_ref.dtype)


def _fox(zb, qa, ka, batch, seq, *, t=512, heads=2):
    m = zb.shape[0]
    w = heads * FOX_HEAD_DIM

    def group(seg):
        return pl.BlockSpec((seq, w), lambda b, h: (b, seg * LANE // w + h))

    aug = pl.BlockSpec((seq, LANE), lambda b, h: (b, 0))
    return pl.pallas_call(
        functools.partial(_fox_kernel, t=t, heads=heads),
        grid=(batch, FOX_HEADS // heads),
        in_specs=[group(ZB_FQ), group(ZB_FK), group(ZB_FV), aug, aug],
        out_specs=group(0),
        out_shape=jax.ShapeDtypeStruct((m, FOX_WIDTH), BF16),
        compiler_params=_params(("parallel", "parallel")),
        name="fox",
    )(zb, zb, zb, qa, ka)


GLA_BLOCK = 4 * GLA_CHUNK


def _gla_decay_kernel(a_ref, wg_ref, bg_ref, cm_ref, o_ref, *, blocks):
    t = GLA_BLOCK
    for b in range(blocks):
        rows = slice(b * t, (b + 1) * t)
        gl = _dot(a_ref[rows, :].astype(BF16), wg_ref[...]) + bg_ref[...]
        g = _log_sigmoid(gl) / GLA_GATE_TAU
        o_ref[rows, :] = _dot_01(cm_ref[...], g)


def _gla_decay(small, w_gate_emb, b_gate, *, blocks=4):
    m = small.shape[0]
    t = GLA_BLOCK
    rows = blocks * t
    idx = jnp.arange(t)
    same = (idx[:, None] // GLA_CHUNK) == (idx[None, :] // GLA_CHUNK)
    cm = jnp.where(same & (idx[:, None] >= idx[None, :]), 1.0, 0.0).astype(BF16)
    cm3 = jnp.concatenate([cm, cm, cm], axis=1)
    return pl.pallas_call(
        functools.partial(_gla_decay_kernel, blocks=blocks),
        grid=(m // rows,),
        in_specs=[
            pl.BlockSpec((rows, LANE), lambda i: (i, 0)),
            pl.BlockSpec((LANE, GLA_KW), lambda i: (0, 0)),
            pl.BlockSpec((1, GLA_KW), lambda i: (0, 0)),
            pl.BlockSpec((t, 3 * t), lambda i: (0, 0)),
        ],
        out_specs=pl.BlockSpec((rows, GLA_KW), lambda i: (i, 0)),
        out_shape=jax.ShapeDtypeStruct((m, GLA_KW), F32),
        compiler_params=_params(("parallel",)),
        name="gla_decay",
    )(small, w_gate_emb, b_gate.reshape(1, GLA_KW), cm3)


def _gla_kernel(q_ref, k_ref, v_ref, r_ref, bc_ref, on_ref, o_ref):
    c = GLA_CHUNK
    t = GLA_BLOCK
    dk = GLA_DK
    seq = q_ref.shape[0]
    st = jnp.zeros((GLA_DV, dk), F32)
    row = lax.broadcasted_iota(jnp.int32, (t, t), 0)
    col = lax.broadcasted_iota(jnp.int32, (t, t), 1)
    intra = (row >= col) & ((row // c) == (col // c))
    o_gain = on_ref[...]

    for n in range(seq // t):
        rows = slice(n * t, (n + 1) * t)
        bc = bc_ref[rows, :]
        qb = q_ref[rows, :] * (dk ** -0.5)
        kb = k_ref[rows, :]
        vb = v_ref[rows, :]
        chunks = [slice(i * c, (i + 1) * c) for i in range(t // c)]
        b_mid = jnp.concatenate(
            [jnp.broadcast_to(bc[s.start + c // 2:s.start + c // 2 + 1, :], (c, dk)) for s in chunks], axis=0)
        b_last = jnp.concatenate(
            [jnp.broadcast_to(bc[s.stop - 1:s.stop, :], (c, dk)) for s in chunks], axis=0)
        qd = (qb * jnp.exp(bc - b_mid)).astype(BF16)
        kd = (kb * jnp.exp(b_mid - bc)).astype(BF16)
        ku = (kb * jnp.exp(b_last - bc)).astype(BF16)
        qi = (qb * jnp.exp(bc)).astype(BF16)
        att = jnp.where(intra, _dot_nt(qd, kd), 0.0)
        o = _dot(att.astype(BF16), vb)
        inter = []
        for s in chunks:
            inter.append(_dot_nt(qi[s, :], st.astype(BF16)))
            st = st * jnp.exp(bc[s.stop - 1:s.stop, :]) + _dot_tn(vb[s, :], ku[s, :])
        o = _rms(o + jnp.concatenate(inter, axis=0), o_gain)
        o_ref[rows, :] = (o * _silu(r_ref[rows, :])).astype(o_ref.dtype)


def _gla(zf, zb, bc, o_norm, batch, seq):
    m = zf.shape[0]
    dk, dv = GLA_DK, GLA_DV
    return pl.pallas_call(
        _gla_kernel,
        grid=(batch, GLA_HEADS),
        in_specs=[
            pl.BlockSpec((seq, dk), lambda b, h: (b, ZF_GQ + h)),
            pl.BlockSpec((seq, dk), lambda b, h: (b, ZF_GK + h)),
            pl.BlockSpec((seq, dv), lambda b, h: (b, ZB_GV * LANE // dv + h)),
            pl.BlockSpec((seq, dv), lambda b, h: (b, ZF_GR * LANE // dv + h)),
            pl.BlockSpec((seq, dk), lambda b, h: (b, h)),
            pl.BlockSpec((1, dv), lambda b, h: (0, h)),
        ],
        out_specs=pl.BlockSpec((seq, dv), lambda b, h: (b, h)),
        out_shape=jax.ShapeDtypeStruct((m, GLA_VW), BF16),
        compiler_params=_params(("parallel", "parallel")),
        name="gla",
    )(zf, zf, zb, zf, bc, o_norm.reshape(1, GLA_VW))


def _merge_kernel(a_ref, b_ref, c_ref, ga_ref, gb_ref, gc_ref, wa_ref, wb_ref, wc_ref,
                  wo_ref, h_ref, o_ref):
    mix = (jax.nn.sigmoid(ga_ref[...]) * _dot(a_ref[...], wa_ref[...])
           + jax.nn.sigmoid(gb_ref[...]) * _dot(b_ref[...], wb_ref[...])
           + jax.nn.sigmoid(gc_ref[...]) * _dot(c_ref[...], wc_ref[...]))
    o_ref[...] = h_ref[...] + _dot(mix.astype(BF16), wo_ref[...])


def _merge(ya, yb, yc, zf, wa, wb, wc, wo, h, layer, *, tm=256):
    m, d = h.shape
    kw = ya.shape[1]
    g0 = ZF_GATES * LANE // d
    branch = pl.BlockSpec((tm, kw), lambda i: (i, 0))
    resident = dict(pipeline_mode=pl.Buffered(1))
    wspec = pl.BlockSpec((None, kw, d), lambda i: (layer, 0, 0), **resident)
    return pl.pallas_call(
        _merge_kernel,
        grid=(m // tm,),
        in_specs=[
            branch, branch, branch,
            pl.BlockSpec((tm, d), lambda i: (i, g0)),
            pl.BlockSpec((tm, d), lambda i: (i, g0 + 1)),
            pl.BlockSpec((tm, d), lambda i: (i, g0 + 2)),
            wspec, wspec, wspec,
            pl.BlockSpec((None, d, d), lambda i: (layer, 0, 0), **resident),
            pl.BlockSpec((tm, d), lambda i: (i, 0)),
        ],
        out_specs=pl.BlockSpec((tm, d), lambda i: (i, 0)),
        out_shape=jax.ShapeDtypeStruct((m, d), F32),
        compiler_params=_params(("parallel",)),
        name="merge",
    )(ya, yb, yc, zf, zf, zf, wa, wb, wc, wo, h)


def _xattn_kernel(h_ref, g_ref, wq_ref, kv_ref, wo_ref, o_ref):
    dh = XA_HEAD_DIM
    x = h_ref[...]
    n = _rms(x, g_ref[...]).astype(BF16)
    q = _dot(n, wq_ref[...]) * (dh ** -0.5)
    outs = []
    for hd in range(XA_HEADS):
        qh = q[:, hd * dh:(hd + 1) * dh].astype(BF16)
        kh = kv_ref[:, hd * dh:(hd + 1) * dh].astype(BF16)
        vh = kv_ref[:, XA_WIDTH + hd * dh:XA_WIDTH + (hd + 1) * dh].astype(BF16)
        s = _dot_nt(qh, kh)
        e = jnp.exp(s - jnp.max(s, axis=-1, keepdims=True))
        p = e / jnp.sum(e, axis=-1, keepdims=True)
        outs.append(_dot(p.astype(BF16), vh))
    o = jnp.concatenate(outs, axis=1).astype(BF16)
    o_ref[...] = x + _dot(o, wo_ref[...])


def _xattn(h, norm_g, wq, kv, wo, layer, seq, mem_len, *, tm=512):
    m, d = h.shape
    per_batch = seq // tm
    return pl.pallas_call(
        _xattn_kernel,
        grid=(m // tm,),
        in_specs=[
            pl.BlockSpec((tm, d), lambda i: (i, 0)),
            pl.BlockSpec((1, d), lambda i: (0, 0)),
            pl.BlockSpec((None, d, XA_WIDTH), lambda i: (layer, 0, 0)),
            pl.BlockSpec((mem_len, 2 * XA_WIDTH), lambda i: (i // per_batch, 0)),
            pl.BlockSpec((None, XA_WIDTH, d), lambda i: (layer, 0, 0)),
        ],
        out_specs=pl.BlockSpec((tm, d), lambda i: (i, 0)),
        out_shape=jax.ShapeDtypeStruct((m, d), F32),
        compiler_params=_params(("parallel",)),
        name="xattn",
    )(h, norm_g.reshape(1, d), wq, kv, wo)


CAST_BLOCK_BYTES = 6 * 1024 * 1024


def _cast_kernel(x_ref, o_ref):
    o_ref[...] = x_ref[...].astype(o_ref.dtype)


def _to_bf16(w):
    lead, c = w.shape[:-1], w.shape[-1]
    w2 = w.reshape(-1, c)
    r = w2.shape[0]
    rows = 16
    while rows * 2 * c * 4 <= CAST_BLOCK_BYTES and r % (rows * 2) == 0:
        rows *= 2
    spec = pl.BlockSpec((rows, c), lambda i: (i, 0))
    out = pl.pallas_call(
        _cast_kernel,
        grid=(r // rows,),
        in_specs=[spec],
        out_specs=spec,
        out_shape=jax.ShapeDtypeStruct((r, c), BF16),
        compiler_params=_params(("parallel",)),
        name="cast",
    )(w2)
    return out.reshape(*lead, c)

W_FQ = 2 * SGU_WIDTH
W_FF = W_FQ + 3 * FOX_WIDTH
W_GQ = W_FF + FOX_HEADS
W_GV = W_GQ + 2 * GLA_KW
W_GA = W_GV + GLA_VW
W_GR = W_GA + GLA_GATE_RANK
W_END = W_GR + GLA_VW + 3 * D_MODEL


REPACK_CHUNK = 1024


def _repack_kernel(wt_ref, wf_ref, wb_ref, ws_ref):
    def move(dst_ref, dst, src, width):
        for off in range(0, width, REPACK_CHUNK):
            n = min(REPACK_CHUNK, width - off)
            dst_ref[:, dst + off:dst + off + n] = wt_ref[src + off:src + off + n, :].T.astype(BF16)

    move(wf_ref, 0, 0, W_FQ)
    move(wf_ref, W_FQ, W_GQ, 2 * GLA_KW)
    move(wf_ref, W_FQ + 2 * GLA_KW, W_GR, W_END - W_GR)
    move(wb_ref, 0, W_FQ, 3 * FOX_WIDTH)
    move(wb_ref, 3 * FOX_WIDTH, W_GV, GLA_VW)
    lane = lax.broadcasted_iota(jnp.int32, (wt_ref.shape[1], LANE), 1)
    ff = wt_ref[W_FF - SMALL_FF:W_FF - SMALL_FF + LANE, :].T
    ga = wt_ref[W_GA - SMALL_GA:W_GA - SMALL_GA + LANE, :].T
    side = jnp.where(lane < SMALL_FF + FOX_HEADS, ff,
                     jnp.where(lane < SMALL_GA + GLA_GATE_RANK, ga, 0.0))
    ws_ref[...] = side.astype(BF16)


def _repack_w_in(w, *, kb=256):
    nl, d, n = w.shape
    wt = jnp.swapaxes(w, 1, 2)

    def spec(width):
        return pl.BlockSpec((None, kb, width), lambda l, i: (l, i, 0))

    return pl.pallas_call(
        _repack_kernel,
        grid=(nl, d // kb),
        in_specs=[pl.BlockSpec((None, n, kb), lambda l, i: (l, 0, i))],
        out_specs=[spec(ZF_WIDTH), spec(ZB_WIDTH), spec(LANE)],
        out_shape=[jax.ShapeDtypeStruct((nl, d, width), BF16) for width in (ZF_WIDTH, ZB_WIDTH, LANE)],
        compiler_params=_params(("parallel", "parallel")),
        name="repack",
    )(wt)


def _embed_w_gate(w_gate):
    emb = jnp.zeros((LANE, GLA_KW), BF16)
    return emb.at[SMALL_GA:SMALL_GA + GLA_GATE_RANK, :].set(w_gate.astype(BF16))


def kernel(x, mem, ffn1_norm, ffn1_w_in, ffn1_w_out, mix_norm, w_in, sgu_ln_g, sgu_ln_b, sgu_w_s, sgu_b_s, fox_b_f, gla_w_gate, gla_b_gate, gla_o_norm, w_branch_a, w_branch_b, w_branch_c, w_out, xa_norm, mem_norm, xa_w_q, xa_w_kv, xa_w_o, ffn2_norm, ffn2_w_in, ffn2_w_out, final_norm):
    batch, seq, d = x.shape
    mem_len = mem.shape[1]
    h = x.reshape(batch * seq, d)
    mem2 = mem.reshape(batch * mem_len, d)
    ones_f = jnp.ones((1, ZF_WIDTH), F32)
    scale_b = jnp.ones((1, ZB_WIDTH), F32).at[:, ZB_FQ * LANE:ZB_FQ * LANE + FOX_WIDTH].set(
        FOX_HEAD_DIM ** -0.5)
    f1_in, f1_out = _to_bf16(ffn1_w_in), _to_bf16(ffn1_w_out)
    f2_in, f2_out = _to_bf16(ffn2_w_in), _to_bf16(ffn2_w_out)
    wa, wb, wc, wo = (_to_bf16(w) for w in (w_branch_a, w_branch_b, w_branch_c, w_out))
    xq, xkv, xo = _to_bf16(xa_w_q), _to_bf16(xa_w_kv), _to_bf16(xa_w_o)
    w_f, w_b, w_small = _repack_w_in(w_in)
    for l in range(DEPTH):
        h, xn = _ffn(h, ffn1_norm[l], f1_in, f1_out, l, mix_norm[l], mode="emit_norm")

        zf, small = _proj(xn, w_f, l, ones_f, F32, w_small)
        zb = _proj(xn, w_b, l, scale_b, BF16)
        ya = _sgu(zf, sgu_ln_g[l], sgu_ln_b[l], sgu_w_s[l], sgu_b_s[l])
        qa, ka = _fox_gate(small, fox_b_f[l], batch, seq)
        yb = _fox(zb, qa, ka, batch, seq)
        bc = _gla_decay(small, _embed_w_gate(gla_w_gate[l]), gla_b_gate[l])
        yc = _gla(zf, zb, bc, gla_o_norm[l], batch, seq)
        h = _merge(ya, yb, yc, zf, wa, wb, wc, wo, h, l)

        kv = _norm_proj(mem2, mem_norm[l], xkv, l)
        h = _xattn(h, xa_norm[l], xq, kv, xo, l, seq, mem_len)

        if l == DEPTH - 1:
            h = _ffn(h, ffn2_norm[l], f2_in, f2_out, l, final_norm, mode="final")
        else:
            h = _ffn(h, ffn2_norm[l], f2_in, f2_out, l)
    return h.reshape(batch, seq, d)
```

```python
import functools

import jax
import jax.numpy as jnp
from jax import lax
from jax.experimental import pallas as pl
from jax.experimental.pallas import tpu as pltpu

F32 = jnp.float32
BF16 = jnp.bfloat16

D_MODEL = 2048
DEPTH = 2
D_FF = 5632
EPS = 1e-6

SGU_GROUPS = 4
SGU_GROUP_DIM = 256
SGU_WIDTH = 1024
SGU_CHUNK = 128

FOX_HEADS = 8
FOX_HEAD_DIM = 128
FOX_WIDTH = 1024

GLA_HEADS = 4
GLA_DK = 128
GLA_DV = 256
GLA_KW = 512
GLA_VW = 1024
GLA_GATE_RANK = 16
GLA_GATE_TAU = 16.0
GLA_CHUNK = 64

XA_HEADS = 4
XA_HEAD_DIM = 128
XA_WIDTH = 512

LANE = 128

ZF_SU, ZF_SV, ZF_GQ, ZF_GK, ZF_GR, ZF_GATES = 0, 8, 16, 20, 24, 32
ZF_WIDTH = 80 * LANE
ZB_FQ, ZB_FK, ZB_FV, ZB_GV = 0, 8, 16, 24
ZB_WIDTH = 32 * LANE
SMALL_FF = 0
SMALL_GA = 8

NEG_BIG = -1e30
VMEM_LIMIT = 63 * 1024 * 1024


def _params(sem):
    return pltpu.CompilerParams(dimension_semantics=sem, vmem_limit_bytes=VMEM_LIMIT)


def _rms(x, g):
    return x * lax.rsqrt(jnp.mean(x * x, axis=-1, keepdims=True) + EPS) * g


def _log_sigmoid(x):
    return jnp.minimum(x, 0.0) - jnp.log1p(jnp.exp(-jnp.abs(x)))


def _gelu_tanh(x):
    c = 0.7978845608028654
    return x * (0.5 * (1.0 + jnp.tanh(c * (x + 0.044715 * (x * x * x)))))


def _silu(x):
    return x * jax.nn.sigmoid(x)


def _dot(a, b):
    return jnp.dot(a, b, preferred_element_type=F32)


def _dot_nt(a, b):
    return lax.dot_general(a, b, (((1,), (1,)), ((), ())), preferred_element_type=F32)


def _dot_tn(a, b):
    return lax.dot_general(a, b, (((0,), (0,)), ((), ())), preferred_element_type=F32)


def _split3(x):
    hi = x.astype(BF16)
    r = x - hi.astype(F32)
    lo = r.astype(BF16)
    lo2 = (r - lo.astype(F32)).astype(BF16)
    return hi, lo, lo2


def _dot_01(mat3, x):
    return _dot(mat3, jnp.concatenate(_split3(x), axis=0))


def _ffn_kernel(*refs, final):
    if final:
        x_ref, g_ref, wg_ref, wu_ref, wo_ref, fg_ref, o_ref, xn_ref = refs
    else:
        x_ref, g_ref, wg_ref, wu_ref, wo_ref, o_ref, xn_ref = refs
    j = pl.program_id(1)

    def tile(xn):
        gate = _dot(xn, wg_ref[...])
        up = _dot(xn, wu_ref[...])
        act = (_silu(gate) * up * 0.5).astype(BF16)
        return _dot(act, wo_ref[...])

    @pl.when(j == 0)
    def _():
        x = x_ref[...]
        xn = _rms(x, g_ref[...]).astype(BF16)
        xn_ref[...] = xn
        o_ref[...] = x + tile(xn)

    @pl.when(j > 0)
    def _():
        o_ref[...] += tile(xn_ref[...])

    if final:
        @pl.when(j == pl.num_programs(1) - 1)
        def _():
            o_ref[...] = _rms(o_ref[...], fg_ref[...])


def _ffn(h, norm_g, w_in, w_out, layer, final_g=None, *, tm=1024, tf=512):
    m, d = h.shape
    f = w_out.shape[1]
    nt = f // tf
    final = final_g is not None
    vec = pl.BlockSpec((1, d), lambda i, j: (0, 0))
    rows = pl.BlockSpec((tm, d), lambda i, j: (i, 0))
    in_specs = [
        rows, vec,
        pl.BlockSpec((None, d, tf), lambda i, j: (layer, 0, j)),
        pl.BlockSpec((None, d, tf), lambda i, j: (layer, 0, j + nt)),
        pl.BlockSpec((None, tf, d), lambda i, j: (layer, j, 0)),
    ]
    args = [h, norm_g.reshape(1, d), w_in, w_in, w_out]
    if final:
        in_specs.append(vec)
        args.append(final_g.reshape(1, d))
    return pl.pallas_call(
        functools.partial(_ffn_kernel, final=final),
        grid=(m // tm, nt),
        in_specs=in_specs,
        out_specs=rows,
        out_shape=jax.ShapeDtypeStruct((m, d), F32),
        scratch_shapes=[pltpu.VMEM((tm, d), BF16)],
        compiler_params=_params(("parallel", "arbitrary")),
        name="ffn",
    )(*args)


def _proj_kernel(*refs, small):
    if small:
        x_ref, g_ref, w_ref, cs_ref, ws_ref, o_ref, os_ref, xn_ref = refs
    else:
        x_ref, g_ref, w_ref, cs_ref, o_ref, xn_ref = refs
    j = pl.program_id(1)

    def project(xn):
        o_ref[...] = (_dot(xn, w_ref[...]) * cs_ref[...]).astype(o_ref.dtype)

    @pl.when(j == 0)
    def _():
        xn = _rms(x_ref[...], g_ref[...]).astype(BF16)
        xn_ref[...] = xn
        if small:
            os_ref[...] = _dot(xn, ws_ref[...])
        project(xn)

    @pl.when(j > 0)
    def _():
        project(xn_ref[...])


def _proj(x, norm_g, w, layer, col_scale, out_dtype, w_small=None, *, tm=1024, tn=2048):
    m, d = x.shape
    n = w.shape[2]
    small = w_small is not None
    in_specs = [
        pl.BlockSpec((tm, d), lambda i, j: (i, 0)),
        pl.BlockSpec((1, d), lambda i, j: (0, 0)),
        pl.BlockSpec((None, d, tn), lambda i, j: (layer, 0, j)),
        pl.BlockSpec((1, tn), lambda i, j: (0, j)),
    ]
    args = [x, norm_g.reshape(1, d), w, col_scale]
    out_specs = [pl.BlockSpec((tm, tn), lambda i, j: (i, j))]
    out_shape = [jax.ShapeDtypeStruct((m, n), out_dtype)]
    if small:
        in_specs.append(pl.BlockSpec((None, d, LANE), lambda i, j: (layer, 0, 0)))
        args.append(w_small)
        out_specs.append(pl.BlockSpec((tm, LANE), lambda i, j: (i, 0)))
        out_shape.append(jax.ShapeDtypeStruct((m, LANE), F32))
    out = pl.pallas_call(
        functools.partial(_proj_kernel, small=small),
        grid=(m // tm, n // tn),
        in_specs=in_specs,
        out_specs=out_specs,
        out_shape=out_shape,
        scratch_shapes=[pltpu.VMEM((tm, d), BF16)],
        compiler_params=_params(("parallel", "arbitrary")),
        name="proj",
    )(*args)
    return out if small else out[0]


def _norm_proj_kernel(x_ref, g_ref, w_ref, o_ref):
    o_ref[...] = _dot(_rms(x_ref[...], g_ref[...]).astype(BF16), w_ref[...])


def _norm_proj(x, norm_g, w, layer, *, tm=1024):
    m, d = x.shape
    n = w.shape[2]
    return pl.pallas_call(
        _norm_proj_kernel,
        grid=(m // tm,),
        in_specs=[
            pl.BlockSpec((tm, d), lambda i: (i, 0)),
            pl.BlockSpec((1, d), lambda i: (0, 0)),
            pl.BlockSpec((None, d, n), lambda i: (layer, 0, 0)),
        ],
        out_specs=pl.BlockSpec((tm, n), lambda i: (i, 0)),
        out_shape=jax.ShapeDtypeStruct((m, n), F32),
        compiler_params=_params(("parallel",)),
        name="mem_kv",
    )(x, norm_g.reshape(1, d), w)


def _sgu_kernel(u_ref, v_ref, lng_ref, lnb_ref, ws_ref, bs_ref, o_ref, *, chunks):
    t = SGU_CHUNK
    gd = SGU_GROUP_DIM
    row = lax.broadcasted_iota(jnp.int32, (t, t), 0)
    col = lax.broadcasted_iota(jnp.int32, (t, t), 1)
    causal = row >= col
    for g in range(SGU_GROUPS):
        w = jnp.where(causal, ws_ref[g], 0.0).astype(BF16)
        bias = bs_ref[:, g:g + 1]
        ln_g = lng_ref[:, g * gd:(g + 1) * gd]
        ln_b = lnb_ref[:, g * gd:(g + 1) * gd]
        for c in range(chunks):
            rows = slice(c * t, (c + 1) * t)
            cols = slice(g * gd, (g + 1) * gd)
            v = _gelu_tanh(v_ref[rows, cols])
            mu = jnp.mean(v, axis=-1, keepdims=True)
            vc = v - mu
            var = jnp.mean(vc * vc, axis=-1, keepdims=True)
            vn = vc * lax.rsqrt(var + EPS) * ln_g + ln_b
            mixed = _dot(w, vn.astype(BF16)) + bias
            o_ref[rows, cols] = (_gelu_tanh(u_ref[rows, cols]) * mixed).astype(o_ref.dtype)


def _sgu(zf, ln_g, ln_b, w_s, b_s, *, chunks=4):
    m = zf.shape[0]
    rows = chunks * SGU_CHUNK
    w = SGU_WIDTH
    return pl.pallas_call(
        functools.partial(_sgu_kernel, chunks=chunks),
        grid=(m // rows,),
        in_specs=[
            pl.BlockSpec((rows, w), lambda i: (i, ZF_SU * LANE // w)),
            pl.BlockSpec((rows, w), lambda i: (i, ZF_SV * LANE // w)),
            pl.BlockSpec((1, w), lambda i: (0, 0)),
            pl.BlockSpec((1, w), lambda i: (0, 0)),
            pl.BlockSpec((SGU_GROUPS, SGU_CHUNK, SGU_CHUNK), lambda i: (0, 0, 0)),
            pl.BlockSpec((SGU_CHUNK, SGU_GROUPS), lambda i: (0, 0)),
        ],
        out_specs=pl.BlockSpec((rows, w), lambda i: (i, 0)),
        out_shape=jax.ShapeDtypeStruct((m, w), BF16),
        compiler_params=_params(("parallel",)),
        name="sgu",
    )(zf, zf, ln_g.reshape(1, w), ln_b.reshape(1, w), w_s, jnp.swapaxes(b_s, 0, 1))


FOX_AUG = 6


def _fox_gate_kernel(f_ref, bf_ref, tril_ref, pq_ref, pk_ref, cq_ref, ck_ref, qa_ref, ka_ref):
    t = LANE
    seq = f_ref.shape[0]
    carry = jnp.zeros((1, t), F32)
    for b in range(seq // t):
        rows = slice(b * t, (b + 1) * t)
        lf = _log_sigmoid(f_ref[rows, :] + bf_ref[...])
        cum = _dot_01(tril_ref[...], lf) + carry
        carry = cum[t - 1:t, :]
        parts = jnp.concatenate(_split3(cum), axis=1)
        qa_ref[rows, :] = (_dot(parts, pq_ref[...]) + cq_ref[...]).astype(BF16)
        ka_ref[rows, :] = (_dot(parts, pk_ref[...]) + ck_ref[...]).astype(BF16)


def _tril3(n):
    tril = jnp.tril(jnp.ones((n, n), BF16))
    return jnp.concatenate([tril, tril, tril], axis=1)


def _fox_placement():
    src = jnp.arange(3 * LANE)[:, None]
    dst = jnp.arange(LANE)[None, :]
    p, h = src // LANE, src % LANE
    valid = h < FOX_HEADS
    pq = jnp.where(valid & (dst == h * FOX_AUG + p), 1.0, 0.0).astype(BF16)
    pk = jnp.where(valid & (dst == h * FOX_AUG + 3 + p), -1.0, 0.0).astype(BF16)
    used = dst < FOX_HEADS * FOX_AUG
    cq = jnp.where(used & (dst % FOX_AUG >= 3), 1.0, 0.0).astype(F32)
    ck = jnp.where(used & (dst % FOX_AUG < 3), 1.0, 0.0).astype(F32)
    return pq, pk, cq, ck


def _fox_gate(small, b_f, batch, seq):
    bf = jnp.zeros((1, LANE), F32).at[0, SMALL_FF:SMALL_FF + FOX_HEADS].set(b_f)
    w = LANE
    aug = jax.ShapeDtypeStruct((batch * seq, w), BF16)

    def const(shape):
        return pl.BlockSpec(shape, lambda b: (0, 0))

    return pl.pallas_call(
        _fox_gate_kernel,
        grid=(batch,),
        in_specs=[
            pl.BlockSpec((seq, LANE), lambda b: (b, 0)),
            const((1, LANE)), const((LANE, 3 * LANE)),
            const((3 * LANE, w)), const((3 * LANE, w)), const((1, w)), const((1, w)),
        ],
        out_specs=[pl.BlockSpec((seq, w), lambda b: (b, 0))] * 2,
        out_shape=[aug, aug],
        compiler_params=_params(("parallel",)),
        name="fox_gate",
    )(small, bf, _tril3(LANE), *_fox_placement())


def _fox_kernel(q_ref, k_ref, v_ref, qa_ref, ka_ref, o_ref, *, t, heads):
    seq = q_ref.shape[0]
    dh = FOX_HEAD_DIM
    lane = lax.broadcasted_iota(jnp.int32, (t, LANE), 1)
    ones_col = jnp.where(lane == 0, 1.0, 0.0).astype(BF16)
    row = lax.broadcasted_iota(jnp.int32, (t, t), 0)
    col = lax.broadcasted_iota(jnp.int32, (t, t), 1)
    causal = row >= col

    for g in range(heads):
        head = pl.program_id(1) * heads + g
        own = (lane >= head * FOX_AUG) & (lane < (head + 1) * FOX_AUG)
        cols = slice(g * dh, (g + 1) * dh)

        def k_block(j):
            rows = slice(j * t, (j + 1) * t)
            ka = jnp.where(own, ka_ref[rows, :], jnp.zeros((), BF16))
            return (jnp.concatenate([k_ref[rows, cols], ka], axis=1),
                    jnp.concatenate([v_ref[rows, cols], ones_col], axis=1))

        for i in range(seq // t):
            rows = slice(i * t, (i + 1) * t)
            qc = jnp.concatenate([q_ref[rows, cols], qa_ref[rows, :]], axis=1)
            kc, vc = k_block(i)
            s = jnp.where(causal, _dot_nt(qc, kc), NEG_BIG)
            m = jnp.max(s, axis=1, keepdims=True)
            acc = _dot(jnp.exp(s - m).astype(BF16), vc)
            for j in range(i):
                kc, vc = k_block(j)
                s = _dot_nt(qc, kc)
                m_new = jnp.maximum(m, jnp.max(s, axis=1, keepdims=True))
                acc = jnp.exp(m - m_new) * acc + _dot(jnp.exp(s - m_new).astype(BF16), vc)
                m = m_new
            o_ref[rows, cols] = (acc[:, :dh] / acc[:, dh:dh + 1]).astype(o_ref.dtype)


def _fox(zb, qa, ka, batch, seq, *, t=512, heads=2):
    m = zb.shape[0]
    w = heads * FOX_HEAD_DIM

    def group(seg):
        return pl.BlockSpec((seq, w), lambda b, h: (b, seg * LANE // w + h))

    aug = pl.BlockSpec((seq, LANE), lambda b, h: (b, 0))
    return pl.pallas_call(
        functools.partial(_fox_kernel, t=t, heads=heads),
        grid=(batch, FOX_HEADS // heads),
        in_specs=[group(ZB_FQ), group(ZB_FK), group(ZB_FV), aug, aug],
        out_specs=group(0),
        out_shape=jax.ShapeDtypeStruct((m, FOX_WIDTH), BF16),
        compiler_params=_params(("parallel", "parallel")),
        name="fox",
    )(zb, zb, zb, qa, ka)


GLA_BLOCK = 4 * GLA_CHUNK


def _gla_decay_kernel(a_ref, wg_ref, bg_ref, cm_ref, o_ref, *, blocks):
    t = GLA_BLOCK
    for b in range(blocks):
        rows = slice(b * t, (b + 1) * t)
        gl = _dot(a_ref[rows, :].astype(BF16), wg_ref[...]) + bg_ref[...]
        g = _log_sigmoid(gl) / GLA_GATE_TAU
        o_ref[rows, :] = _dot_01(cm_ref[...], g)


def _gla_decay(small, w_gate_emb, b_gate, *, blocks=4):
    m = small.shape[0]
    t = GLA_BLOCK
    rows = blocks * t
    idx = jnp.arange(t)
    same = (idx[:, None] // GLA_CHUNK) == (idx[None, :] // GLA_CHUNK)
    cm = jnp.where(same & (idx[:, None] >= idx[None, :]), 1.0, 0.0).astype(BF16)
    cm3 = jnp.concatenate([cm, cm, cm], axis=1)
    return pl.pallas_call(
        functools.partial(_gla_decay_kernel, blocks=blocks),
        grid=(m // rows,),
        in_specs=[
            pl.BlockSpec((rows, LANE), lambda i: (i, 0)),
            pl.BlockSpec((LANE, GLA_KW), lambda i: (0, 0)),
            pl.BlockSpec((1, GLA_KW), lambda i: (0, 0)),
            pl.BlockSpec((t, 3 * t), lambda i: (0, 0)),
        ],
        out_specs=pl.BlockSpec((rows, GLA_KW), lambda i: (i, 0)),
        out_shape=jax.ShapeDtypeStruct((m, GLA_KW), F32),
        compiler_params=_params(("parallel",)),
        name="gla_decay",
    )(small, w_gate_emb, b_gate.reshape(1, GLA_KW), cm3)


def _gla_kernel(q_ref, k_ref, v_ref, r_ref, bc_ref, on_ref, o_ref):
    c = GLA_CHUNK
    t = GLA_BLOCK
    dk = GLA_DK
    seq = q_ref.shape[0]
    st = jnp.zeros((GLA_DV, dk), F32)
    row = lax.broadcasted_iota(jnp.int32, (t, t), 0)
    col = lax.broadcasted_iota(jnp.int32, (t, t), 1)
    intra = (row >= col) & ((row // c) == (col // c))
    o_gain = on_ref[...]

    for n in range(seq // t):
        rows = slice(n * t, (n + 1) * t)
        bc = bc_ref[rows, :]
        qb = q_ref[rows, :] * (dk ** -0.5)
        kb = k_ref[rows, :]
        vb = v_ref[rows, :]
        chunks = [slice(i * c, (i + 1) * c) for i in range(t // c)]
        b_mid = jnp.concatenate(
            [jnp.broadcast_to(bc[s.start + c // 2:s.start + c // 2 + 1, :], (c, dk)) for s in chunks], axis=0)
        b_last = jnp.concatenate(
            [jnp.broadcast_to(bc[s.stop - 1:s.stop, :], (c, dk)) for s in chunks], axis=0)
        qd = (qb * jnp.exp(bc - b_mid)).astype(BF16)
        kd = (kb * jnp.exp(b_mid - bc)).astype(BF16)
        ku = (kb * jnp.exp(b_last - bc)).astype(BF16)
        qi = (qb * jnp.exp(bc)).astype(BF16)
        att = jnp.where(intra, _dot_nt(qd, kd), 0.0)
        o = _dot(att.astype(BF16), vb)
        inter = []
        for s in chunks:
            inter.append(_dot_nt(qi[s, :], st.astype(BF16)))
            st = st * jnp.exp(bc[s.stop - 1:s.stop, :]) + _dot_tn(vb[s, :], ku[s, :])
        o = _rms(o + jnp.concatenate(inter, axis=0), o_gain)
        o_ref[rows, :] = (o * _silu(r_ref[rows, :])).astype(o_ref.dtype)


def _gla(zf, zb, bc, o_norm, batch, seq):
    m = zf.shape[0]
    dk, dv = GLA_DK, GLA_DV
    return pl.pallas_call(
        _gla_kernel,
        grid=(batch, GLA_HEADS),
        in_specs=[
            pl.BlockSpec((seq, dk), lambda b, h: (b, ZF_GQ + h)),
            pl.BlockSpec((seq, dk), lambda b, h: (b, ZF_GK + h)),
            pl.BlockSpec((seq, dv), lambda b, h: (b, ZB_GV * LANE // dv + h)),
            pl.BlockSpec((seq, dv), lambda b, h: (b, ZF_GR * LANE // dv + h)),
            pl.BlockSpec((seq, dk), lambda b, h: (b, h)),
            pl.BlockSpec((1, dv), lambda b, h: (0, h)),
        ],
        out_specs=pl.BlockSpec((seq, dv), lambda b, h: (b, h)),
        out_shape=jax.ShapeDtypeStruct((m, GLA_VW), BF16),
        compiler_params=_params(("parallel", "parallel")),
        name="gla",
    )(zf, zf, zb, zf, bc, o_norm.reshape(1, GLA_VW))


def _merge_kernel(a_ref, b_ref, c_ref, ga_ref, gb_ref, gc_ref, wa_ref, wb_ref, wc_ref,
                  wo_ref, h_ref, o_ref):
    mix = (jax.nn.sigmoid(ga_ref[...]) * _dot(a_ref[...], wa_ref[...])
           + jax.nn.sigmoid(gb_ref[...]) * _dot(b_ref[...], wb_ref[...])
           + jax.nn.sigmoid(gc_ref[...]) * _dot(c_ref[...], wc_ref[...]))
    o_ref[...] = h_ref[...] + _dot(mix.astype(BF16), wo_ref[...])


def _merge(ya, yb, yc, zf, wa, wb, wc, wo, h, layer, *, tm=256):
    m, d = h.shape
    kw = ya.shape[1]
    g0 = ZF_GATES * LANE // d
    branch = pl.BlockSpec((tm, kw), lambda i: (i, 0))
    resident = dict(pipeline_mode=pl.Buffered(1))
    wspec = pl.BlockSpec((None, kw, d), lambda i: (layer, 0, 0), **resident)
    return pl.pallas_call(
        _merge_kernel,
        grid=(m // tm,),
        in_specs=[
            branch, branch, branch,
            pl.BlockSpec((tm, d), lambda i: (i, g0)),
            pl.BlockSpec((tm, d), lambda i: (i, g0 + 1)),
            pl.BlockSpec((tm, d), lambda i: (i, g0 + 2)),
            wspec, wspec, wspec,
            pl.BlockSpec((None, d, d), lambda i: (layer, 0, 0), **resident),
            pl.BlockSpec((tm, d), lambda i: (i, 0)),
        ],
        out_specs=pl.BlockSpec((tm, d), lambda i: (i, 0)),
        out_shape=jax.ShapeDtypeStruct((m, d), F32),
        compiler_params=_params(("parallel",)),
        name="merge",
    )(ya, yb, yc, zf, zf, zf, wa, wb, wc, wo, h)


def _xattn_kernel(h_ref, g_ref, wq_ref, kv_ref, wo_ref, o_ref):
    dh = XA_HEAD_DIM
    x = h_ref[...]
    n = _rms(x, g_ref[...]).astype(BF16)
    q = _dot(n, wq_ref[...]) * (dh ** -0.5)
    outs = []
    for hd in range(XA_HEADS):
        qh = q[:, hd * dh:(hd + 1) * dh].astype(BF16)
        kh = kv_ref[:, hd * dh:(hd + 1) * dh].astype(BF16)
        vh = kv_ref[:, XA_WIDTH + hd * dh:XA_WIDTH + (hd + 1) * dh].astype(BF16)
        s = _dot_nt(qh, kh)
        e = jnp.exp(s - jnp.max(s, axis=-1, keepdims=True))
        p = e / jnp.sum(e, axis=-1, keepdims=True)
        outs.append(_dot(p.astype(BF16), vh))
    o = jnp.concatenate(outs, axis=1).astype(BF16)
    o_ref[...] = x + _dot(o, wo_ref[...])


def _xattn(h, norm_g, wq, kv, wo, layer, seq, mem_len, *, tm=1024):
    m, d = h.shape
    per_batch = seq // tm
    return pl.pallas_call(
        _xattn_kernel,
        grid=(m // tm,),
        in_specs=[
            pl.BlockSpec((tm, d), lambda i: (i, 0)),
            pl.BlockSpec((1, d), lambda i: (0, 0)),
            pl.BlockSpec((None, d, XA_WIDTH), lambda i: (layer, 0, 0)),
            pl.BlockSpec((mem_len, 2 * XA_WIDTH), lambda i: (i // per_batch, 0)),
            pl.BlockSpec((None, XA_WIDTH, d), lambda i: (layer, 0, 0)),
        ],
        out_specs=pl.BlockSpec((tm, d), lambda i: (i, 0)),
        out_shape=jax.ShapeDtypeStruct((m, d), F32),
        compiler_params=_params(("parallel",)),
        name="xattn",
    )(h, norm_g.reshape(1, d), wq, kv, wo)


CAST_BLOCK_BYTES = 6 * 1024 * 1024


def _cast_kernel(x_ref, o_ref):
    o_ref[...] = x_ref[...].astype(o_ref.dtype)


def _to_bf16(w):
    lead, c = w.shape[:-1], w.shape[-1]
    w2 = w.reshape(-1, c)
    r = w2.shape[0]
    rows = 16
    while rows * 2 * c * 4 <= CAST_BLOCK_BYTES and r % (rows * 2) == 0:
        rows *= 2
    spec = pl.BlockSpec((rows, c), lambda i: (i, 0))
    out = pl.pallas_call(
        _cast_kernel,
        grid=(r // rows,),
        in_specs=[spec],
        out_specs=spec,
        out_shape=jax.ShapeDtypeStruct((r, c), BF16),
        compiler_params=_params(("parallel",)),
        name="cast",
    )(w2)
    return out.reshape(*lead, c)

W_FQ = 2 * SGU_WIDTH
W_FF = W_FQ + 3 * FOX_WIDTH
W_GQ = W_FF + FOX_HEADS
W_GV = W_GQ + 2 * GLA_KW
W_GA = W_GV + GLA_VW
W_GR = W_GA + GLA_GATE_RANK
W_END = W_GR + GLA_VW + 3 * D_MODEL


REPACK_CHUNK = 1024


def _repack_kernel(wt_ref, wf_ref, wb_ref, ws_ref):
    def move(dst_ref, dst, src, width):
        for off in range(0, width, REPACK_CHUNK):
            n = min(REPACK_CHUNK, width - off)
            dst_ref[:, dst + off:dst + off + n] = wt_ref[src + off:src + off + n, :].T.astype(BF16)

    move(wf_ref, 0, 0, W_FQ)
    move(wf_ref, W_FQ, W_GQ, 2 * GLA_KW)
    move(wf_ref, W_FQ + 2 * GLA_KW, W_GR, W_END - W_GR)
    move(wb_ref, 0, W_FQ, 3 * FOX_WIDTH)
    move(wb_ref, 3 * FOX_WIDTH, W_GV, GLA_VW)
    lane = lax.broadcasted_iota(jnp.int32, (wt_ref.shape[1], LANE), 1)
    ff = wt_ref[W_FF - SMALL_FF:W_FF - SMALL_FF + LANE, :].T
    ga = wt_ref[W_GA - SMALL_GA:W_GA - SMALL_GA + LANE, :].T
    side = jnp.where(lane < SMALL_FF + FOX_HEADS, ff,
                     jnp.where(lane < SMALL_GA + GLA_GATE_RANK, ga, 0.0))
    ws_ref[...] = side.astype(BF16)


def _repack_w_in(w, *, kb=256):
    nl, d, n = w.shape
    wt = jnp.swapaxes(w, 1, 2)

    def spec(width):
        return pl.BlockSpec((None, kb, width), lambda l, i: (l, i, 0))

    return pl.pallas_call(
        _repack_kernel,
        grid=(nl, d // kb),
        in_specs=[pl.BlockSpec((None, n, kb), lambda l, i: (l, 0, i))],
        out_specs=[spec(ZF_WIDTH), spec(ZB_WIDTH), spec(LANE)],
        out_shape=[jax.ShapeDtypeStruct((nl, d, width), BF16) for width in (ZF_WIDTH, ZB_WIDTH, LANE)],
        compiler_params=_params(("parallel", "parallel")),
        name="repack",
    )(wt)


def _embed_w_gate(w_gate):
    emb = jnp.zeros((LANE, GLA_KW), BF16)
    return emb.at[SMALL_GA:SMALL_GA + GLA_GATE_RANK, :].set(w_gate.astype(BF16))


def kernel(x, mem, ffn1_norm, ffn1_w_in, ffn1_w_out, mix_norm, w_in, sgu_ln_g, sgu_ln_b, sgu_w_s, sgu_b_s, fox_b_f, gla_w_gate, gla_b_gate, gla_o_norm, w_branch_a, w_branch_b, w_branch_c, w_out, xa_norm, mem_norm, xa_w_q, xa_w_kv, xa_w_o, ffn2_norm, ffn2_w_in, ffn2_w_out, final_norm):
    batch, seq, d = x.shape
    mem_len = mem.shape[1]
    h = x.reshape(batch * seq, d)
    mem2 = mem.reshape(batch * mem_len, d)
    ones_f = jnp.ones((1, ZF_WIDTH), F32)
    scale_b = jnp.ones((1, ZB_WIDTH), F32).at[:, ZB_FQ * LANE:ZB_FQ * LANE + FOX_WIDTH].set(
        FOX_HEAD_DIM ** -0.5)
    f1_in, f1_out = _to_bf16(ffn1_w_in), _to_bf16(ffn1_w_out)
    f2_in, f2_out = _to_bf16(ffn2_w_in), _to_bf16(ffn2_w_out)
    wa, wb, wc, wo = (_to_bf16(w) for w in (w_branch_a, w_branch_b, w_branch_c, w_out))
    xq, xkv, xo = _to_bf16(xa_w_q), _to_bf16(xa_w_kv), _to_bf16(xa_w_o)
    w_f, w_b, w_small = _repack_w_in(w_in)
    for l in range(DEPTH):
        h = _ffn(h, ffn1_norm[l], f1_in, f1_out, l)

        zf, small = _proj(h, mix_norm[l], w_f, l, ones_f, F32, w_small)
        zb = _proj(h, mix_norm[l], w_b, l, scale_b, BF16)
        ya = _sgu(zf, sgu_ln_g[l], sgu_ln_b[l], sgu_w_s[l], sgu_b_s[l])
        qa, ka = _fox_gate(small, fox_b_f[l], batch, seq)
        yb = _fox(zb, qa, ka, batch, seq)
        bc = _gla_decay(small, _embed_w_gate(gla_w_gate[l]), gla_b_gate[l])
        yc = _gla(zf, zb, bc, gla_o_norm[l], batch, seq)
        h = _merge(ya, yb, yc, zf, wa, wb, wc, wo, h, l)

        kv = _norm_proj(mem2, mem_norm[l], xkv, l)
        h = _xattn(h, xa_norm[l], xq, kv, xo, l, seq, mem_len)

        h = _ffn(h, ffn2_norm[l], f2_in, f2_out, l, final_norm if l == DEPTH - 1 else None)
    return h.reshape(batch, seq, d)
```

```python
import functools

import jax
import jax.numpy as jnp
from jax import lax
from jax.experimental import pallas as pl
from jax.experimental.pallas import tpu as pltpu

F32 = jnp.float32
BF16 = jnp.bfloat16

D_MODEL = 2048
DEPTH = 2
D_FF = 5632
EPS = 1e-6

SGU_GROUPS = 4
SGU_GROUP_DIM = 256
SGU_WIDTH = 1024
SGU_CHUNK = 128

FOX_HEADS = 8
FOX_HEAD_DIM = 128
FOX_WIDTH = 1024

GLA_HEADS = 4
GLA_DK = 128
GLA_DV = 256
GLA_KW = 512
GLA_VW = 1024
GLA_GATE_RANK = 16
GLA_GATE_TAU = 16.0
GLA_CHUNK = 64

XA_HEADS = 4
XA_HEAD_DIM = 128
XA_WIDTH = 512

LANE = 128

ZF_SU, ZF_SV, ZF_GQ, ZF_GK, ZF_GR, ZF_GATES = 0, 8, 16, 20, 24, 32
ZF_WIDTH = 80 * LANE
ZB_FQ, ZB_FK, ZB_FV, ZB_GV = 0, 8, 16, 24
ZB_WIDTH = 32 * LANE
SMALL_FF = 0
SMALL_GA = 8

NEG_BIG = -1e30
VMEM_LIMIT = 63 * 1024 * 1024


def _params(sem):
    return pltpu.CompilerParams(dimension_semantics=sem, vmem_limit_bytes=VMEM_LIMIT)


def _rms(x, g):
    return x * lax.rsqrt(jnp.mean(x * x, axis=-1, keepdims=True) + EPS) * g


def _log_sigmoid(x):
    return jnp.minimum(x, 0.0) - jnp.log1p(jnp.exp(-jnp.abs(x)))


def _gelu_tanh(x):
    c = 0.7978845608028654
    return x * (0.5 * (1.0 + jnp.tanh(c * (x + 0.044715 * (x * x * x)))))


def _silu(x):
    return x * jax.nn.sigmoid(x)


def _dot(a, b):
    return jnp.dot(a, b, preferred_element_type=F32)


def _dot_nt(a, b):
    return lax.dot_general(a, b, (((1,), (1,)), ((), ())), preferred_element_type=F32)


def _dot_tn(a, b):
    return lax.dot_general(a, b, (((0,), (0,)), ((), ())), preferred_element_type=F32)


def _split3(x):
    hi = x.astype(BF16)
    r = x - hi.astype(F32)
    lo = r.astype(BF16)
    lo2 = (r - lo.astype(F32)).astype(BF16)
    return hi, lo, lo2


def _dot_01(mat3, x):
    return _dot(mat3, jnp.concatenate(_split3(x), axis=0))


def _ffn_kernel(*refs, final):
    if final:
        x_ref, g_ref, wg_ref, wu_ref, wo_ref, fg_ref, o_ref, xn_ref = refs
    else:
        x_ref, g_ref, wg_ref, wu_ref, wo_ref, o_ref, xn_ref = refs
    j = pl.program_id(1)

    def tile(xn):
        gate = _dot(xn, wg_ref[...])
        up = _dot(xn, wu_ref[...].astype(BF16))
        act = (_silu(gate) * up * 0.5).astype(BF16)
        return _dot(act, wo_ref[...].astype(BF16))

    @pl.when(j == 0)
    def _():
        x = x_ref[...]
        xn = _rms(x, g_ref[...]).astype(BF16)
        xn_ref[...] = xn
        o_ref[...] = x + tile(xn)

    @pl.when(j > 0)
    def _():
        o_ref[...] += tile(xn_ref[...])

    if final:
        @pl.when(j == pl.num_programs(1) - 1)
        def _():
            o_ref[...] = _rms(o_ref[...], fg_ref[...])


def _ffn(h, norm_g, w_gate, w_in, w_out, layer, final_g=None, *, tm=1024, tf=512):
    m, d = h.shape
    f = w_out.shape[1]
    nt = f // tf
    final = final_g is not None
    vec = pl.BlockSpec((1, d), lambda i, j: (0, 0))
    rows = pl.BlockSpec((tm, d), lambda i, j: (i, 0))
    in_specs = [
        rows, vec,
        pl.BlockSpec((None, d, tf), lambda i, j: (layer, 0, j)),
        pl.BlockSpec((None, d, tf), lambda i, j: (layer, 0, j + nt)),
        pl.BlockSpec((None, tf, d), lambda i, j: (layer, j, 0)),
    ]
    args = [h, norm_g.reshape(1, d), w_gate, w_in, w_out]
    if final:
        in_specs.append(vec)
        args.append(final_g.reshape(1, d))
    return pl.pallas_call(
        functools.partial(_ffn_kernel, final=final),
        grid=(m // tm, nt),
        in_specs=in_specs,
        out_specs=rows,
        out_shape=jax.ShapeDtypeStruct((m, d), F32),
        scratch_shapes=[pltpu.VMEM((tm, d), BF16)],
        compiler_params=_params(("parallel", "arbitrary")),
        name="ffn",
    )(*args)


def _proj_kernel(*refs, small):
    if small:
        x_ref, g_ref, w_ref, cs_ref, ws_ref, o_ref, os_ref, xn_ref = refs
    else:
        x_ref, g_ref, w_ref, cs_ref, o_ref, xn_ref = refs
    j = pl.program_id(1)

    def project(xn):
        o_ref[...] = (_dot(xn, w_ref[...]) * cs_ref[...]).astype(o_ref.dtype)

    @pl.when(j == 0)
    def _():
        xn = _rms(x_ref[...], g_ref[...]).astype(BF16)
        xn_ref[...] = xn
        if small:
            os_ref[...] = _dot(xn, ws_ref[...])
        project(xn)

    @pl.when(j > 0)
    def _():
        project(xn_ref[...])


def _proj(x, norm_g, w, layer, col_scale, out_dtype, w_small=None, *, tm=1024, tn=2048):
    m, d = x.shape
    n = w.shape[2]
    small = w_small is not None
    in_specs = [
        pl.BlockSpec((tm, d), lambda i, j: (i, 0)),
        pl.BlockSpec((1, d), lambda i, j: (0, 0)),
        pl.BlockSpec((None, d, tn), lambda i, j: (layer, 0, j)),
        pl.BlockSpec((1, tn), lambda i, j: (0, j)),
    ]
    args = [x, norm_g.reshape(1, d), w, col_scale]
    out_specs = [pl.BlockSpec((tm, tn), lambda i, j: (i, j))]
    out_shape = [jax.ShapeDtypeStruct((m, n), out_dtype)]
    if small:
        in_specs.append(pl.BlockSpec((None, d, LANE), lambda i, j: (layer, 0, 0)))
        args.append(w_small)
        out_specs.append(pl.BlockSpec((tm, LANE), lambda i, j: (i, 0)))
        out_shape.append(jax.ShapeDtypeStruct((m, LANE), F32))
    out = pl.pallas_call(
        functools.partial(_proj_kernel, small=small),
        grid=(m // tm, n // tn),
        in_specs=in_specs,
        out_specs=out_specs,
        out_shape=out_shape,
        scratch_shapes=[pltpu.VMEM((tm, d), BF16)],
        compiler_params=_params(("parallel", "arbitrary")),
        name="proj",
    )(*args)
    return out if small else out[0]


def _norm_proj_kernel(x_ref, g_ref, w_ref, o_ref):
    o_ref[...] = _dot(_rms(x_ref[...], g_ref[...]).astype(BF16), w_ref[...])


def _norm_proj(x, norm_g, w, layer, *, tm=1024):
    m, d = x.shape
    n = w.shape[2]
    return pl.pallas_call(
        _norm_proj_kernel,
        grid=(m // tm,),
        in_specs=[
            pl.BlockSpec((tm, d), lambda i: (i, 0)),
            pl.BlockSpec((1, d), lambda i: (0, 0)),
            pl.BlockSpec((None, d, n), lambda i: (layer, 0, 0)),
        ],
        out_specs=pl.BlockSpec((tm, n), lambda i: (i, 0)),
        out_shape=jax.ShapeDtypeStruct((m, n), F32),
        compiler_params=_params(("parallel",)),
        name="mem_kv",
    )(x, norm_g.reshape(1, d), w)


def _sgu_kernel(u_ref, v_ref, lng_ref, lnb_ref, ws_ref, bs_ref, o_ref, *, chunks):
    t = SGU_CHUNK
    gd = SGU_GROUP_DIM
    row = lax.broadcasted_iota(jnp.int32, (t, t), 0)
    col = lax.broadcasted_iota(jnp.int32, (t, t), 1)
    causal = row >= col
    for g in range(SGU_GROUPS):
        w = jnp.where(causal, ws_ref[g], 0.0).astype(BF16)
        bias = bs_ref[:, g:g + 1]
        ln_g = lng_ref[:, g * gd:(g + 1) * gd]
        ln_b = lnb_ref[:, g * gd:(g + 1) * gd]
        for c in range(chunks):
            rows = slice(c * t, (c + 1) * t)
            cols = slice(g * gd, (g + 1) * gd)
            v = _gelu_tanh(v_ref[rows, cols])
            mu = jnp.mean(v, axis=-1, keepdims=True)
            vc = v - mu
            var = jnp.mean(vc * vc, axis=-1, keepdims=True)
            vn = vc * lax.rsqrt(var + EPS) * ln_g + ln_b
            mixed = _dot(w, vn.astype(BF16)) + bias
            o_ref[rows, cols] = (_gelu_tanh(u_ref[rows, cols]) * mixed).astype(o_ref.dtype)


def _sgu(zf, ln_g, ln_b, w_s, b_s, *, chunks=4):
    m = zf.shape[0]
    rows = chunks * SGU_CHUNK
    w = SGU_WIDTH
    return pl.pallas_call(
        functools.partial(_sgu_kernel, chunks=chunks),
        grid=(m // rows,),
        in_specs=[
            pl.BlockSpec((rows, w), lambda i: (i, ZF_SU * LANE // w)),
            pl.BlockSpec((rows, w), lambda i: (i, ZF_SV * LANE // w)),
            pl.BlockSpec((1, w), lambda i: (0, 0)),
            pl.BlockSpec((1, w), lambda i: (0, 0)),
            pl.BlockSpec((SGU_GROUPS, SGU_CHUNK, SGU_CHUNK), lambda i: (0, 0, 0)),
            pl.BlockSpec((SGU_CHUNK, SGU_GROUPS), lambda i: (0, 0)),
        ],
        out_specs=pl.BlockSpec((rows, w), lambda i: (i, 0)),
        out_shape=jax.ShapeDtypeStruct((m, w), BF16),
        compiler_params=_params(("parallel",)),
        name="sgu",
    )(zf, zf, ln_g.reshape(1, w), ln_b.reshape(1, w), w_s, jnp.swapaxes(b_s, 0, 1))


FOX_AUG = 6


def _fox_gate_kernel(f_ref, bf_ref, tril_ref, pq_ref, pk_ref, cq_ref, ck_ref, qa_ref, ka_ref):
    t = LANE
    seq = f_ref.shape[0]
    carry = jnp.zeros((1, t), F32)
    for b in range(seq // t):
        rows = slice(b * t, (b + 1) * t)
        lf = _log_sigmoid(f_ref[rows, :] + bf_ref[...])
        cum = _dot_01(tril_ref[...], lf) + carry
        carry = cum[t - 1:t, :]
        parts = jnp.concatenate(_split3(cum), axis=1)
        qa_ref[rows, :] = (_dot(parts, pq_ref[...]) + cq_ref[...]).astype(BF16)
        ka_ref[rows, :] = (_dot(parts, pk_ref[...]) + ck_ref[...]).astype(BF16)


def _tril3(n):
    tril = jnp.tril(jnp.ones((n, n), BF16))
    return jnp.concatenate([tril, tril, tril], axis=1)


def _fox_placement():
    src = jnp.arange(3 * LANE)[:, None]
    dst = jnp.arange(LANE)[None, :]
    p, h = src // LANE, src % LANE
    valid = h < FOX_HEADS
    pq = jnp.where(valid & (dst == h * FOX_AUG + p), 1.0, 0.0).astype(BF16)
    pk = jnp.where(valid & (dst == h * FOX_AUG + 3 + p), -1.0, 0.0).astype(BF16)
    used = dst < FOX_HEADS * FOX_AUG
    cq = jnp.where(used & (dst % FOX_AUG >= 3), 1.0, 0.0).astype(F32)
    ck = jnp.where(used & (dst % FOX_AUG < 3), 1.0, 0.0).astype(F32)
    return pq, pk, cq, ck


def _fox_gate(small, b_f, batch, seq):
    bf = jnp.zeros((1, LANE), F32).at[0, SMALL_FF:SMALL_FF + FOX_HEADS].set(b_f)
    w = LANE
    aug = jax.ShapeDtypeStruct((batch * seq, w), BF16)

    def const(shape):
        return pl.BlockSpec(shape, lambda b: (0, 0))

    return pl.pallas_call(
        _fox_gate_kernel,
        grid=(batch,),
        in_specs=[
            pl.BlockSpec((seq, LANE), lambda b: (b, 0)),
            const((1, LANE)), const((LANE, 3 * LANE)),
            const((3 * LANE, w)), const((3 * LANE, w)), const((1, w)), const((1, w)),
        ],
        out_specs=[pl.BlockSpec((seq, w), lambda b: (b, 0))] * 2,
        out_shape=[aug, aug],
        compiler_params=_params(("parallel",)),
        name="fox_gate",
    )(small, bf, _tril3(LANE), *_fox_placement())


def _fox_kernel(q_ref, k_ref, v_ref, qa_ref, ka_ref, o_ref, *, t, heads):
    seq = q_ref.shape[0]
    dh = FOX_HEAD_DIM
    lane = lax.broadcasted_iota(jnp.int32, (t, LANE), 1)
    ones_col = jnp.where(lane == 0, 1.0, 0.0).astype(BF16)
    row = lax.broadcasted_iota(jnp.int32, (t, t), 0)
    col = lax.broadcasted_iota(jnp.int32, (t, t), 1)
    causal = row >= col

    for g in range(heads):
        head = pl.program_id(1) * heads + g
        own = (lane >= head * FOX_AUG) & (lane < (head + 1) * FOX_AUG)
        cols = slice(g * dh, (g + 1) * dh)

        def k_block(j):
            rows = slice(j * t, (j + 1) * t)
            ka = jnp.where(own, ka_ref[rows, :], jnp.zeros((), BF16))
            return (jnp.concatenate([k_ref[rows, cols], ka], axis=1),
                    jnp.concatenate([v_ref[rows, cols], ones_col], axis=1))

        for i in range(seq // t):
            rows = slice(i * t, (i + 1) * t)
            qc = jnp.concatenate([q_ref[rows, cols], qa_ref[rows, :]], axis=1)
            kc, vc = k_block(i)
            s = jnp.where(causal, _dot_nt(qc, kc), NEG_BIG)
            m = jnp.max(s, axis=1, keepdims=True)
            acc = _dot(jnp.exp(s - m).astype(BF16), vc)
            for j in range(i):
                kc, vc = k_block(j)
                s = _dot_nt(qc, kc)
                m_new = jnp.maximum(m, jnp.max(s, axis=1, keepdims=True))
                acc = jnp.exp(m - m_new) * acc + _dot(jnp.exp(s - m_new).astype(BF16), vc)
                m = m_new
            o_ref[rows, cols] = (acc[:, :dh] / acc[:, dh:dh + 1]).astype(o_ref.dtype)


def _fox(zb, qa, ka, batch, seq, *, t=512, heads=2):
    m = zb.shape[0]
    w = heads * FOX_HEAD_DIM

    def group(seg):
        return pl.BlockSpec((seq, w), lambda b, h: (b, seg * LANE // w + h))

    aug = pl.BlockSpec((seq, LANE), lambda b, h: (b, 0))
    return pl.pallas_call(
        functools.partial(_fox_kernel, t=t, heads=heads),
        grid=(batch, FOX_HEADS // heads),
        in_specs=[group(ZB_FQ), group(ZB_FK), group(ZB_FV), aug, aug],
        out_specs=group(0),
        out_shape=jax.ShapeDtypeStruct((m, FOX_WIDTH), BF16),
        compiler_params=_params(("parallel", "parallel")),
        name="fox",
    )(zb, zb, zb, qa, ka)


GLA_BLOCK = 4 * GLA_CHUNK


def _gla_decay_kernel(a_ref, wg_ref, bg_ref, cm_ref, o_ref, *, blocks):
    t = GLA_BLOCK
    for b in range(blocks):
        rows = slice(b * t, (b + 1) * t)
        gl = _dot(a_ref[rows, :].astype(BF16), wg_ref[...]) + bg_ref[...]
        g = _log_sigmoid(gl) / GLA_GATE_TAU
        o_ref[rows, :] = _dot_01(cm_ref[...], g)


def _gla_decay(small, w_gate_emb, b_gate, *, blocks=4):
    m = small.shape[0]
    t = GLA_BLOCK
    rows = blocks * t
    idx = jnp.arange(t)
    same = (idx[:, None] // GLA_CHUNK) == (idx[None, :] // GLA_CHUNK)
    cm = jnp.where(same & (idx[:, None] >= idx[None, :]), 1.0, 0.0).astype(BF16)
    cm3 = jnp.concatenate([cm, cm, cm], axis=1)
    return pl.pallas_call(
        functools.partial(_gla_decay_kernel, blocks=blocks),
        grid=(m // rows,),
        in_specs=[
            pl.BlockSpec((rows, LANE), lambda i: (i, 0)),
            pl.BlockSpec((LANE, GLA_KW), lambda i: (0, 0)),
            pl.BlockSpec((1, GLA_KW), lambda i: (0, 0)),
            pl.BlockSpec((t, 3 * t), lambda i: (0, 0)),
        ],
        out_specs=pl.BlockSpec((rows, GLA_KW), lambda i: (i, 0)),
        out_shape=jax.ShapeDtypeStruct((m, GLA_KW), F32),
        compiler_params=_params(("parallel",)),
        name="gla_decay",
    )(small, w_gate_emb, b_gate.reshape(1, GLA_KW), cm3)


def _gla_kernel(q_ref, k_ref, v_ref, r_ref, bc_ref, on_ref, o_ref):
    c = GLA_CHUNK
    t = GLA_BLOCK
    dk = GLA_DK
    seq = q_ref.shape[0]
    st = jnp.zeros((GLA_DV, dk), F32)
    row = lax.broadcasted_iota(jnp.int32, (t, t), 0)
    col = lax.broadcasted_iota(jnp.int32, (t, t), 1)
    intra = (row >= col) & ((row // c) == (col // c))
    o_gain = on_ref[...]

    for n in range(seq // t):
        rows = slice(n * t, (n + 1) * t)
        bc = bc_ref[rows, :]
        qb = q_ref[rows, :] * (dk ** -0.5)
        kb = k_ref[rows, :]
        vb = v_ref[rows, :]
        chunks = [slice(i * c, (i + 1) * c) for i in range(t // c)]
        b_mid = jnp.concatenate(
            [jnp.broadcast_to(bc[s.start + c // 2:s.start + c // 2 + 1, :], (c, dk)) for s in chunks], axis=0)
        b_last = jnp.concatenate(
            [jnp.broadcast_to(bc[s.stop - 1:s.stop, :], (c, dk)) for s in chunks], axis=0)
        qd = (qb * jnp.exp(bc - b_mid)).astype(BF16)
        kd = (kb * jnp.exp(b_mid - bc)).astype(BF16)
        ku = (kb * jnp.exp(b_last - bc)).astype(BF16)
        qi = (qb * jnp.exp(bc)).astype(BF16)
        att = jnp.where(intra, _dot_nt(qd, kd), 0.0)
        o = _dot(att.astype(BF16), vb)
        inter = []
        for s in chunks:
            inter.append(_dot_nt(qi[s, :], st.astype(BF16)))
            st = st * jnp.exp(bc[s.stop - 1:s.stop, :]) + _dot_tn(vb[s, :], ku[s, :])
        o = _rms(o + jnp.concatenate(inter, axis=0), o_gain)
        o_ref[rows, :] = (o * _silu(r_ref[rows, :])).astype(o_ref.dtype)


def _gla(zf, zb, bc, o_norm, batch, seq):
    m = zf.shape[0]
    dk, dv = GLA_DK, GLA_DV
    return pl.pallas_call(
        _gla_kernel,
        grid=(batch, GLA_HEADS),
        in_specs=[
            pl.BlockSpec((seq, dk), lambda b, h: (b, ZF_GQ + h)),
            pl.BlockSpec((seq, dk), lambda b, h: (b, ZF_GK + h)),
            pl.BlockSpec((seq, dv), lambda b, h: (b, ZB_GV * LANE // dv + h)),
            pl.BlockSpec((seq, dv), lambda b, h: (b, ZF_GR * LANE // dv + h)),
            pl.BlockSpec((seq, dk), lambda b, h: (b, h)),
            pl.BlockSpec((1, dv), lambda b, h: (0, h)),
        ],
        out_specs=pl.BlockSpec((seq, dv), lambda b, h: (b, h)),
        out_shape=jax.ShapeDtypeStruct((m, GLA_VW), BF16),
        compiler_params=_params(("parallel", "parallel")),
        name="gla",
    )(zf, zf, zb, zf, bc, o_norm.reshape(1, GLA_VW))


def _merge_kernel(a_ref, b_ref, c_ref, ga_ref, gb_ref, gc_ref, wa_ref, wb_ref, wc_ref,
                  wo_ref, h_ref, o_ref):
    mix = (jax.nn.sigmoid(ga_ref[...]) * _dot(a_ref[...], wa_ref[...])
           + jax.nn.sigmoid(gb_ref[...]) * _dot(b_ref[...], wb_ref[...])
           + jax.nn.sigmoid(gc_ref[...]) * _dot(c_ref[...], wc_ref[...]))
    o_ref[...] = h_ref[...] + _dot(mix.astype(BF16), wo_ref[...])


def _merge(ya, yb, yc, zf, wa, wb, wc, wo, h, layer, *, tm=256):
    m, d = h.shape
    kw = ya.shape[1]
    g0 = ZF_GATES * LANE // d
    branch = pl.BlockSpec((tm, kw), lambda i: (i, 0))
    resident = dict(pipeline_mode=pl.Buffered(1))
    wspec = pl.BlockSpec((None, kw, d), lambda i: (layer, 0, 0), **resident)
    return pl.pallas_call(
        _merge_kernel,
        grid=(m // tm,),
        in_specs=[
            branch, branch, branch,
            pl.BlockSpec((tm, d), lambda i: (i, g0)),
            pl.BlockSpec((tm, d), lambda i: (i, g0 + 1)),
            pl.BlockSpec((tm, d), lambda i: (i, g0 + 2)),
            wspec, wspec, wspec,
            pl.BlockSpec((None, d, d), lambda i: (layer, 0, 0), **resident),
            pl.BlockSpec((tm, d), lambda i: (i, 0)),
        ],
        out_specs=pl.BlockSpec((tm, d), lambda i: (i, 0)),
        out_shape=jax.ShapeDtypeStruct((m, d), F32),
        compiler_params=_params(("parallel",)),
        name="merge",
    )(ya, yb, yc, zf, zf, zf, wa, wb, wc, wo, h)


def _xattn_kernel(h_ref, g_ref, wq_ref, kv_ref, wo_ref, o_ref):
    dh = XA_HEAD_DIM
    x = h_ref[...]
    n = _rms(x, g_ref[...]).astype(BF16)
    q = _dot(n, wq_ref[...]) * (dh ** -0.5)
    outs = []
    for hd in range(XA_HEADS):
        qh = q[:, hd * dh:(hd + 1) * dh].astype(BF16)
        kh = kv_ref[:, hd * dh:(hd + 1) * dh].astype(BF16)
        vh = kv_ref[:, XA_WIDTH + hd * dh:XA_WIDTH + (hd + 1) * dh].astype(BF16)
        s = _dot_nt(qh, kh)
        e = jnp.exp(s - jnp.max(s, axis=-1, keepdims=True))
        p = e / jnp.sum(e, axis=-1, keepdims=True)
        outs.append(_dot(p.astype(BF16), vh))
    o = jnp.concatenate(outs, axis=1).astype(BF16)
    o_ref[...] = x + _dot(o, wo_ref[...])


def _xattn(h, norm_g, wq, kv, wo, layer, seq, mem_len, *, tm=1024):
    m, d = h.shape
    per_batch = seq // tm
    return pl.pallas_call(
        _xattn_kernel,
        grid=(m // tm,),
        in_specs=[
            pl.BlockSpec((tm, d), lambda i: (i, 0)),
            pl.BlockSpec((1, d), lambda i: (0, 0)),
            pl.BlockSpec((None, d, XA_WIDTH), lambda i: (layer, 0, 0)),
            pl.BlockSpec((mem_len, 2 * XA_WIDTH), lambda i: (i // per_batch, 0)),
            pl.BlockSpec((None, XA_WIDTH, d), lambda i: (layer, 0, 0)),
        ],
        out_specs=pl.BlockSpec((tm, d), lambda i: (i, 0)),
        out_shape=jax.ShapeDtypeStruct((m, d), F32),
        compiler_params=_params(("parallel",)),
        name="xattn",
    )(h, norm_g.reshape(1, d), wq, kv, wo)


CAST_BLOCK_BYTES = 6 * 1024 * 1024


def _cast_kernel(x_ref, o_ref):
    o_ref[...] = x_ref[...].astype(o_ref.dtype)


def _to_bf16(w, cols=None):
    lead = w.shape[:-1]
    c = w.shape[-1] if cols is None else cols
    w2 = w.reshape(-1, w.shape[-1])
    r = w2.shape[0]
    rows = 16
    while rows * 2 * c * 4 <= CAST_BLOCK_BYTES and r % (rows * 2) == 0:
        rows *= 2
    spec = pl.BlockSpec((rows, c), lambda i: (i, 0))
    out = pl.pallas_call(
        _cast_kernel,
        grid=(r // rows,),
        in_specs=[spec],
        out_specs=spec,
        out_shape=jax.ShapeDtypeStruct((r, c), BF16),
        compiler_params=_params(("parallel",)),
        name="cast",
    )(w2)
    return out.reshape(*lead, c)

W_FQ = 2 * SGU_WIDTH
W_FF = W_FQ + 3 * FOX_WIDTH
W_GQ = W_FF + FOX_HEADS
W_GV = W_GQ + 2 * GLA_KW
W_GA = W_GV + GLA_VW
W_GR = W_GA + GLA_GATE_RANK
W_END = W_GR + GLA_VW + 3 * D_MODEL


REPACK_CHUNK = 1024


def _repack_kernel(wt_ref, wf_ref, wb_ref, ws_ref):
    def move(dst_ref, dst, src, width):
        for off in range(0, width, REPACK_CHUNK):
            n = min(REPACK_CHUNK, width - off)
            dst_ref[:, dst + off:dst + off + n] = wt_ref[src + off:src + off + n, :].T.astype(BF16)

    move(wf_ref, 0, 0, W_FQ)
    move(wf_ref, W_FQ, W_GQ, 2 * GLA_KW)
    move(wf_ref, W_FQ + 2 * GLA_KW, W_GR, W_END - W_GR)
    move(wb_ref, 0, W_FQ, 3 * FOX_WIDTH)
    move(wb_ref, 3 * FOX_WIDTH, W_GV, GLA_VW)
    lane = lax.broadcasted_iota(jnp.int32, (wt_ref.shape[1], LANE), 1)
    ff = wt_ref[W_FF - SMALL_FF:W_FF - SMALL_FF + LANE, :].T
    ga = wt_ref[W_GA - SMALL_GA:W_GA - SMALL_GA + LANE, :].T
    side = jnp.where(lane < SMALL_FF + FOX_HEADS, ff,
                     jnp.where(lane < SMALL_GA + GLA_GATE_RANK, ga, 0.0))
    ws_ref[...] = side.astype(BF16)


def _repack_w_in(w, *, kb=256):
    nl, d, n = w.shape
    wt = jnp.swapaxes(w, 1, 2)

    def spec(width):
        return pl.BlockSpec((None, kb, width), lambda l, i: (l, i, 0))

    return pl.pallas_call(
        _repack_kernel,
        grid=(nl, d // kb),
        in_specs=[pl.BlockSpec((None, n, kb), lambda l, i: (l, 0, i))],
        out_specs=[spec(ZF_WIDTH), spec(ZB_WIDTH), spec(LANE)],
        out_shape=[jax.ShapeDtypeStruct((nl, d, width), BF16) for width in (ZF_WIDTH, ZB_WIDTH, LANE)],
        compiler_params=_params(("parallel", "parallel")),
        name="repack",
    )(wt)


def _embed_w_gate(w_gate):
    emb = jnp.zeros((LANE, GLA_KW), BF16)
    return emb.at[SMALL_GA:SMALL_GA + GLA_GATE_RANK, :].set(w_gate.astype(BF16))


def kernel(x, mem, ffn1_norm, ffn1_w_in, ffn1_w_out, mix_norm, w_in, sgu_ln_g, sgu_ln_b, sgu_w_s, sgu_b_s, fox_b_f, gla_w_gate, gla_b_gate, gla_o_norm, w_branch_a, w_branch_b, w_branch_c, w_out, xa_norm, mem_norm, xa_w_q, xa_w_kv, xa_w_o, ffn2_norm, ffn2_w_in, ffn2_w_out, final_norm):
    batch, seq, d = x.shape
    mem_len = mem.shape[1]
    h = x.reshape(batch * seq, d)
    mem2 = mem.reshape(batch * mem_len, d)
    ones_f = jnp.ones((1, ZF_WIDTH), F32)
    scale_b = jnp.ones((1, ZB_WIDTH), F32).at[:, ZB_FQ * LANE:ZB_FQ * LANE + FOX_WIDTH].set(
        FOX_HEAD_DIM ** -0.5)
    f1_gate, f2_gate = _to_bf16(ffn1_w_in, D_FF), _to_bf16(ffn2_w_in, D_FF)
    wa, wb, wc, wo = (_to_bf16(w) for w in (w_branch_a, w_branch_b, w_branch_c, w_out))
    xq, xkv, xo = _to_bf16(xa_w_q), _to_bf16(xa_w_kv), _to_bf16(xa_w_o)
    w_f, w_b, w_small = _repack_w_in(w_in)
    for l in range(DEPTH):
        h = _ffn(h, ffn1_norm[l], f1_gate, ffn1_w_in, ffn1_w_out, l)

        zf, small = _proj(h, mix_norm[l], w_f, l, ones_f, F32, w_small)
        zb = _proj(h, mix_norm[l], w_b, l, scale_b, BF16)
        ya = _sgu(zf, sgu_ln_g[l], sgu_ln_b[l], sgu_w_s[l], sgu_b_s[l])
        qa, ka = _fox_gate(small, fox_b_f[l], batch, seq)
        yb = _fox(zb, qa, ka, batch, seq)
        bc = _gla_decay(small, _embed_w_gate(gla_w_gate[l]), gla_b_gate[l])
        yc = _gla(zf, zb, bc, gla_o_norm[l], batch, seq)
        h = _merge(ya, yb, yc, zf, wa, wb, wc, wo, h, l)

        kv = _norm_proj(mem2, mem_norm[l], xkv, l)
        h = _xattn(h, xa_norm[l], xq, kv, xo, l, seq, mem_len)

        h = _ffn(h, ffn2_norm[l], f2_gate, ffn2_w_in, ffn2_w_out, l,
                 final_norm if l == DEPTH - 1 else None)
    return h.reshape(batch, seq, d)
```

```python
import functools

import jax
import jax.numpy as jnp
from jax import lax
from jax.experimental import pallas as pl
from jax.experimental.pallas import tpu as pltpu

F32 = jnp.float32
BF16 = jnp.bfloat16

D_MODEL = 2048
DEPTH = 2
D_FF = 5632
EPS = 1e-6

SGU_GROUPS = 4
SGU_GROUP_DIM = 256
SGU_WIDTH = 1024
SGU_CHUNK = 128

FOX_HEADS = 8
FOX_HEAD_DIM = 128
FOX_WIDTH = 1024

GLA_HEADS = 4
GLA_DK = 128
GLA_DV = 256
GLA_KW = 512
GLA_VW = 1024
GLA_GATE_RANK = 16
GLA_GATE_TAU = 16.0
GLA_CHUNK = 64

XA_HEADS = 4
XA_HEAD_DIM = 128
XA_WIDTH = 512

LANE = 128

ZF_SU, ZF_SV, ZF_GQ, ZF_GK, ZF_GR, ZF_GATES = 0, 8, 16, 20, 24, 32
ZF_WIDTH = 80 * LANE
ZB_FQ, ZB_FK, ZB_FV, ZB_GV = 0, 8, 16, 24
ZB_WIDTH = 32 * LANE
SMALL_FF = 0
SMALL_GA = 8

NEG_BIG = -1e30
VMEM_LIMIT = 63 * 1024 * 1024


def _params(sem):
    return pltpu.CompilerParams(dimension_semantics=sem, vmem_limit_bytes=VMEM_LIMIT)


def _rms(x, g):
    return x * lax.rsqrt(jnp.mean(x * x, axis=-1, keepdims=True) + EPS) * g


def _log_sigmoid(x):
    return jnp.minimum(x, 0.0) - jnp.log1p(jnp.exp(-jnp.abs(x)))


def _gelu_tanh(x):
    c = 0.7978845608028654
    return x * (0.5 * (1.0 + jnp.tanh(c * (x + 0.044715 * (x * x * x)))))


def _silu(x):
    return x * jax.nn.sigmoid(x)


def _dot(a, b):
    return jnp.dot(a, b, preferred_element_type=F32)


def _dot_nt(a, b):
    return lax.dot_general(a, b, (((1,), (1,)), ((), ())), preferred_element_type=F32)


def _dot_tn(a, b):
    return lax.dot_general(a, b, (((0,), (0,)), ((), ())), preferred_element_type=F32)


def _split3(x):
    hi = x.astype(BF16)
    r = x - hi.astype(F32)
    lo = r.astype(BF16)
    lo2 = (r - lo.astype(F32)).astype(BF16)
    return hi, lo, lo2


def _dot_01(mat3, x):
    return _dot(mat3, jnp.concatenate(_split3(x), axis=0))


def _ffn_kernel(*refs, final):
    if final:
        x_ref, g_ref, wg_ref, wu_ref, wo_ref, fg_ref, o_ref, xn_ref = refs
    else:
        x_ref, g_ref, wg_ref, wu_ref, wo_ref, o_ref, xn_ref = refs
    j = pl.program_id(1)

    def tile(xn):
        gate = _dot(xn, wg_ref[...])
        up = _dot(xn, wu_ref[...].astype(BF16))
        act = (_silu(gate) * up * 0.5).astype(BF16)
        return _dot(act, wo_ref[...].astype(BF16))

    @pl.when(j == 0)
    def _():
        x = x_ref[...]
        xn = _rms(x, g_ref[...]).astype(BF16)
        xn_ref[...] = xn
        o_ref[...] = x + tile(xn)

    @pl.when(j > 0)
    def _():
        o_ref[...] += tile(xn_ref[...])

    if final:
        @pl.when(j == pl.num_programs(1) - 1)
        def _():
            o_ref[...] = _rms(o_ref[...], fg_ref[...])


def _ffn(h, norm_g, w_gate, w_in, w_out, layer, final_g=None, *, tm=1024, tf=512):
    m, d = h.shape
    f = w_out.shape[1]
    nt = f // tf
    final = final_g is not None
    vec = pl.BlockSpec((1, d), lambda i, j: (0, 0))
    rows = pl.BlockSpec((tm, d), lambda i, j: (i, 0))
    in_specs = [
        rows, vec,
        pl.BlockSpec((None, d, tf), lambda i, j: (layer, 0, j)),
        pl.BlockSpec((None, d, tf), lambda i, j: (layer, 0, j + nt)),
        pl.BlockSpec((None, tf, d), lambda i, j: (layer, j, 0)),
    ]
    args = [h, norm_g.reshape(1, d), w_gate, w_in, w_out]
    if final:
        in_specs.append(vec)
        args.append(final_g.reshape(1, d))
    return pl.pallas_call(
        functools.partial(_ffn_kernel, final=final),
        grid=(m // tm, nt),
        in_specs=in_specs,
        out_specs=rows,
        out_shape=jax.ShapeDtypeStruct((m, d), F32),
        scratch_shapes=[pltpu.VMEM((tm, d), BF16)],
        compiler_params=_params(("parallel", "arbitrary")),
        name="ffn",
    )(*args)


def _proj_kernel(*refs, small):
    if small:
        x_ref, g_ref, w_ref, cs_ref, ws_ref, o_ref, os_ref, xn_ref = refs
    else:
        x_ref, g_ref, w_ref, cs_ref, o_ref, xn_ref = refs
    j = pl.program_id(1)

    def project(xn):
        o_ref[...] = (_dot(xn, w_ref[...]) * cs_ref[...]).astype(o_ref.dtype)

    @pl.when(j == 0)
    def _():
        xn = _rms(x_ref[...], g_ref[...]).astype(BF16)
        xn_ref[...] = xn
        if small:
            os_ref[...] = _dot(xn, ws_ref[...])
        project(xn)

    @pl.when(j > 0)
    def _():
        project(xn_ref[...])


def _proj(x, norm_g, w, layer, col_scale, out_dtype, w_small=None, *, tm=1024, tn=2048):
    m, d = x.shape
    n = w.shape[2]
    small = w_small is not None
    in_specs = [
        pl.BlockSpec((tm, d), lambda i, j: (i, 0)),
        pl.BlockSpec((1, d), lambda i, j: (0, 0)),
        pl.BlockSpec((None, d, tn), lambda i, j: (layer, 0, j)),
        pl.BlockSpec((1, tn), lambda i, j: (0, j)),
    ]
    args = [x, norm_g.reshape(1, d), w, col_scale]
    out_specs = [pl.BlockSpec((tm, tn), lambda i, j: (i, j))]
    out_shape = [jax.ShapeDtypeStruct((m, n), out_dtype)]
    if small:
        in_specs.append(pl.BlockSpec((None, d, LANE), lambda i, j: (layer, 0, 0)))
        args.append(w_small)
        out_specs.append(pl.BlockSpec((tm, LANE), lambda i, j: (i, 0)))
        out_shape.append(jax.ShapeDtypeStruct((m, LANE), F32))
    out = pl.pallas_call(
        functools.partial(_proj_kernel, small=small),
        grid=(m // tm, n // tn),
        in_specs=in_specs,
        out_specs=out_specs,
        out_shape=out_shape,
        scratch_shapes=[pltpu.VMEM((tm, d), BF16)],
        compiler_params=_params(("parallel", "arbitrary")),
        name="proj",
    )(*args)
    return out if small else out[0]


def _norm_proj_kernel(x_ref, g_ref, w_ref, o_ref):
    o_ref[...] = _dot(_rms(x_ref[...], g_ref[...]).astype(BF16), w_ref[...].astype(BF16))


def _norm_proj(x, norm_g, w, layer, *, tm=1024):
    m, d = x.shape
    n = w.shape[2]
    return pl.pallas_call(
        _norm_proj_kernel,
        grid=(m // tm,),
        in_specs=[
            pl.BlockSpec((tm, d), lambda i: (i, 0)),
            pl.BlockSpec((1, d), lambda i: (0, 0)),
            pl.BlockSpec((None, d, n), lambda i: (layer, 0, 0)),
        ],
        out_specs=pl.BlockSpec((tm, n), lambda i: (i, 0)),
        out_shape=jax.ShapeDtypeStruct((m, n), F32),
        compiler_params=_params(("parallel",)),
        name="mem_kv",
    )(x, norm_g.reshape(1, d), w)


def _sgu_kernel(u_ref, v_ref, lng_ref, lnb_ref, ws_ref, bs_ref, o_ref, *, chunks):
    t = SGU_CHUNK
    gd = SGU_GROUP_DIM
    row = lax.broadcasted_iota(jnp.int32, (t, t), 0)
    col = lax.broadcasted_iota(jnp.int32, (t, t), 1)
    causal = row >= col
    for g in range(SGU_GROUPS):
        w = jnp.where(causal, ws_ref[g], 0.0).astype(BF16)
        bias = bs_ref[:, g:g + 1]
        ln_g = lng_ref[:, g * gd:(g + 1) * gd]
        ln_b = lnb_ref[:, g * gd:(g + 1) * gd]
        for c in range(chunks):
            rows = slice(c * t, (c + 1) * t)
            cols = slice(g * gd, (g + 1) * gd)
            v = _gelu_tanh(v_ref[rows, cols])
            mu = jnp.mean(v, axis=-1, keepdims=True)
            vc = v - mu
            var = jnp.mean(vc * vc, axis=-1, keepdims=True)
            vn = vc * lax.rsqrt(var + EPS) * ln_g + ln_b
            mixed = _dot(w, vn.astype(BF16)) + bias
            o_ref[rows, cols] = (_gelu_tanh(u_ref[rows, cols]) * mixed).astype(o_ref.dtype)


def _sgu(zf, ln_g, ln_b, w_s, b_s, *, chunks=8):
    m = zf.shape[0]
    rows = chunks * SGU_CHUNK
    w = SGU_WIDTH
    return pl.pallas_call(
        functools.partial(_sgu_kernel, chunks=chunks),
        grid=(m // rows,),
        in_specs=[
            pl.BlockSpec((rows, w), lambda i: (i, ZF_SU * LANE // w)),
            pl.BlockSpec((rows, w), lambda i: (i, ZF_SV * LANE // w)),
            pl.BlockSpec((1, w), lambda i: (0, 0)),
            pl.BlockSpec((1, w), lambda i: (0, 0)),
            pl.BlockSpec((SGU_GROUPS, SGU_CHUNK, SGU_CHUNK), lambda i: (0, 0, 0)),
            pl.BlockSpec((SGU_CHUNK, SGU_GROUPS), lambda i: (0, 0)),
        ],
        out_specs=pl.BlockSpec((rows, w), lambda i: (i, 0)),
        out_shape=jax.ShapeDtypeStruct((m, w), BF16),
        compiler_params=_params(("parallel",)),
        name="sgu",
    )(zf, zf, ln_g.reshape(1, w), ln_b.reshape(1, w), w_s, jnp.swapaxes(b_s, 0, 1))


FOX_AUG = 6


def _fox_gate_kernel(f_ref, bf_ref, tril_ref, pq_ref, pk_ref, cq_ref, ck_ref, qa_ref, ka_ref):
    t = LANE
    seq = f_ref.shape[0]
    carry = jnp.zeros((1, t), F32)
    for b in range(seq // t):
        rows = slice(b * t, (b + 1) * t)
        lf = _log_sigmoid(f_ref[rows, :] + bf_ref[...])
        cum = _dot_01(tril_ref[...], lf) + carry
        carry = cum[t - 1:t, :]
        parts = jnp.concatenate(_split3(cum), axis=1)
        qa_ref[rows, :] = (_dot(parts, pq_ref[...]) + cq_ref[...]).astype(BF16)
        ka_ref[rows, :] = (_dot(parts, pk_ref[...]) + ck_ref[...]).astype(BF16)


def _tril3(n):
    tril = jnp.tril(jnp.ones((n, n), BF16))
    return jnp.concatenate([tril, tril, tril], axis=1)


def _fox_placement():
    src = jnp.arange(3 * LANE)[:, None]
    dst = jnp.arange(LANE)[None, :]
    p, h = src // LANE, src % LANE
    valid = h < FOX_HEADS
    pq = jnp.where(valid & (dst == h * FOX_AUG + p), 1.0, 0.0).astype(BF16)
    pk = jnp.where(valid & (dst == h * FOX_AUG + 3 + p), -1.0, 0.0).astype(BF16)
    used = dst < FOX_HEADS * FOX_AUG
    cq = jnp.where(used & (dst % FOX_AUG >= 3), 1.0, 0.0).astype(F32)
    ck = jnp.where(used & (dst % FOX_AUG < 3), 1.0, 0.0).astype(F32)
    return pq, pk, cq, ck


def _fox_gate(small, b_f, batch, seq):
    bf = jnp.zeros((1, LANE), F32).at[0, SMALL_FF:SMALL_FF + FOX_HEADS].set(b_f)
    w = LANE
    aug = jax.ShapeDtypeStruct((batch * seq, w), BF16)

    def const(shape):
        return pl.BlockSpec(shape, lambda b: (0, 0))

    return pl.pallas_call(
        _fox_gate_kernel,
        grid=(batch,),
        in_specs=[
            pl.BlockSpec((seq, LANE), lambda b: (b, 0)),
            const((1, LANE)), const((LANE, 3 * LANE)),
            const((3 * LANE, w)), const((3 * LANE, w)), const((1, w)), const((1, w)),
        ],
        out_specs=[pl.BlockSpec((seq, w), lambda b: (b, 0))] * 2,
        out_shape=[aug, aug],
        compiler_params=_params(("parallel",)),
        name="fox_gate",
    )(small, bf, _tril3(LANE), *_fox_placement())


def _fox_kernel(q_ref, k_ref, v_ref, qa_ref, ka_ref, o_ref, *, t, heads):
    seq = q_ref.shape[0]
    dh = FOX_HEAD_DIM
    lane = lax.broadcasted_iota(jnp.int32, (t, LANE), 1)
    ones_col = jnp.where(lane == 0, 1.0, 0.0).astype(BF16)
    row = lax.broadcasted_iota(jnp.int32, (t, t), 0)
    col = lax.broadcasted_iota(jnp.int32, (t, t), 1)
    causal = row >= col

    for g in range(heads):
        head = pl.program_id(1) * heads + g
        own = (lane >= head * FOX_AUG) & (lane < (head + 1) * FOX_AUG)
        cols = slice(g * dh, (g + 1) * dh)

        def k_block(j):
            rows = slice(j * t, (j + 1) * t)
            ka = jnp.where(own, ka_ref[rows, :], jnp.zeros((), BF16))
            return (jnp.concatenate([k_ref[rows, cols], ka], axis=1),
                    jnp.concatenate([v_ref[rows, cols], ones_col], axis=1))

        for i in range(seq // t):
            rows = slice(i * t, (i + 1) * t)
            qc = jnp.concatenate([q_ref[rows, cols], qa_ref[rows, :]], axis=1)
            kc, vc = k_block(i)
            s = jnp.where(causal, _dot_nt(qc, kc), NEG_BIG)
            m = jnp.max(s, axis=1, keepdims=True)
            acc = _dot(jnp.exp(s - m).astype(BF16), vc)
            for j in range(i):
                kc, vc = k_block(j)
                s = _dot_nt(qc, kc)
                m_new = jnp.maximum(m, jnp.max(s, axis=1, keepdims=True))
                acc = jnp.exp(m - m_new) * acc + _dot(jnp.exp(s - m_new).astype(BF16), vc)
                m = m_new
            o_ref[rows, cols] = (acc[:, :dh] / acc[:, dh:dh + 1]).astype(o_ref.dtype)


def _fox(zb, qa, ka, batch, seq, *, t=512, heads=2):
    m = zb.shape[0]
    w = heads * FOX_HEAD_DIM

    def group(seg):
        return pl.BlockSpec((seq, w), lambda b, h: (b, seg * LANE // w + h))

    aug = pl.BlockSpec((seq, LANE), lambda b, h: (b, 0))
    return pl.pallas_call(
        functools.partial(_fox_kernel, t=t, heads=heads),
        grid=(batch, FOX_HEADS // heads),
        in_specs=[group(ZB_FQ), group(ZB_FK), group(ZB_FV), aug, aug],
        out_specs=group(0),
        out_shape=jax.ShapeDtypeStruct((m, FOX_WIDTH), BF16),
        compiler_params=_params(("parallel", "parallel")),
        name="fox",
    )(zb, zb, zb, qa, ka)


GLA_BLOCK = 4 * GLA_CHUNK


def _gla_decay_kernel(a_ref, wg_ref, bg_ref, cm_ref, o_ref, *, blocks):
    t = GLA_BLOCK
    for b in range(blocks):
        rows = slice(b * t, (b + 1) * t)
        gl = _dot(a_ref[rows, :].astype(BF16), wg_ref[...]) + bg_ref[...]
        g = _log_sigmoid(gl) / GLA_GATE_TAU
        o_ref[rows, :] = _dot_01(cm_ref[...], g)


def _gla_decay(small, w_gate_emb, b_gate, *, blocks=4):
    m = small.shape[0]
    t = GLA_BLOCK
    rows = blocks * t
    idx = jnp.arange(t)
    same = (idx[:, None] // GLA_CHUNK) == (idx[None, :] // GLA_CHUNK)
    cm = jnp.where(same & (idx[:, None] >= idx[None, :]), 1.0, 0.0).astype(BF16)
    cm3 = jnp.concatenate([cm, cm, cm], axis=1)
    return pl.pallas_call(
        functools.partial(_gla_decay_kernel, blocks=blocks),
        grid=(m // rows,),
        in_specs=[
            pl.BlockSpec((rows, LANE), lambda i: (i, 0)),
            pl.BlockSpec((LANE, GLA_KW), lambda i: (0, 0)),
            pl.BlockSpec((1, GLA_KW), lambda i: (0, 0)),
            pl.BlockSpec((t, 3 * t), lambda i: (0, 0)),
        ],
        out_specs=pl.BlockSpec((rows, GLA_KW), lambda i: (i, 0)),
        out_shape=jax.ShapeDtypeStruct((m, GLA_KW), F32),
        compiler_params=_params(("parallel",)),
        name="gla_decay",
    )(small, w_gate_emb, b_gate.reshape(1, GLA_KW), cm3)


def _gla_kernel(q_ref, k_ref, v_ref, r_ref, bc_ref, on_ref, o_ref, *, heads):
    for g in range(heads):
        _gla_head(q_ref, k_ref, v_ref, r_ref, bc_ref, on_ref, o_ref,
                  slice(g * GLA_DK, (g + 1) * GLA_DK), slice(g * GLA_DV, (g + 1) * GLA_DV))


def _gla_head(q_ref, k_ref, v_ref, r_ref, bc_ref, on_ref, o_ref, kcols, vcols):
    c = GLA_CHUNK
    t = GLA_BLOCK
    dk = GLA_DK
    seq = q_ref.shape[0]
    st = jnp.zeros((GLA_DV, dk), F32)
    row = lax.broadcasted_iota(jnp.int32, (t, t), 0)
    col = lax.broadcasted_iota(jnp.int32, (t, t), 1)
    intra = (row >= col) & ((row // c) == (col // c))
    o_gain = on_ref[:, vcols]

    for n in range(seq // t):
        rows = slice(n * t, (n + 1) * t)
        bc = bc_ref[rows, kcols]
        qb = q_ref[rows, kcols] * (dk ** -0.5)
        kb = k_ref[rows, kcols]
        vb = v_ref[rows, vcols]
        chunks = [slice(i * c, (i + 1) * c) for i in range(t // c)]
        b_mid = jnp.concatenate(
            [jnp.broadcast_to(bc[s.start + c // 2:s.start + c // 2 + 1, :], (c, dk)) for s in chunks], axis=0)
        b_last = jnp.concatenate(
            [jnp.broadcast_to(bc[s.stop - 1:s.stop, :], (c, dk)) for s in chunks], axis=0)
        qd = (qb * jnp.exp(bc - b_mid)).astype(BF16)
        kd = (kb * jnp.exp(b_mid - bc)).astype(BF16)
        ku = (kb * jnp.exp(b_last - bc)).astype(BF16)
        qi = (qb * jnp.exp(bc)).astype(BF16)
        att = jnp.where(intra, _dot_nt(qd, kd), 0.0)
        o = _dot(att.astype(BF16), vb)
        inter = []
        for s in chunks:
            inter.append(_dot_nt(qi[s, :], st.astype(BF16)))
            st = st * jnp.exp(bc[s.stop - 1:s.stop, :]) + _dot_tn(vb[s, :], ku[s, :])
        o = _rms(o + jnp.concatenate(inter, axis=0), o_gain)
        o_ref[rows, vcols] = (o * _silu(r_ref[rows, vcols])).astype(o_ref.dtype)


def _gla(zf, zb, bc, o_norm, batch, seq, *, heads=2):
    m = zf.shape[0]
    dk, dv = heads * GLA_DK, heads * GLA_DV
    return pl.pallas_call(
        functools.partial(_gla_kernel, heads=heads),
        grid=(batch, GLA_HEADS // heads),
        in_specs=[
            pl.BlockSpec((seq, dk), lambda b, h: (b, ZF_GQ * LANE // dk + h)),
            pl.BlockSpec((seq, dk), lambda b, h: (b, ZF_GK * LANE // dk + h)),
            pl.BlockSpec((seq, dv), lambda b, h: (b, ZB_GV * LANE // dv + h)),
            pl.BlockSpec((seq, dv), lambda b, h: (b, ZF_GR * LANE // dv + h)),
            pl.BlockSpec((seq, dk), lambda b, h: (b, h)),
            pl.BlockSpec((1, dv), lambda b, h: (0, h)),
        ],
        out_specs=pl.BlockSpec((seq, dv), lambda b, h: (b, h)),
        out_shape=jax.ShapeDtypeStruct((m, GLA_VW), BF16),
        compiler_params=_params(("parallel", "parallel")),
        name="gla",
    )(zf, zf, zb, zf, bc, o_norm.reshape(1, GLA_VW))


def _merge_kernel(a_ref, b_ref, c_ref, ga_ref, gb_ref, gc_ref, wa_ref, wb_ref, wc_ref,
                  wo_ref, h_ref, o_ref):
    mix = (jax.nn.sigmoid(ga_ref[...]) * _dot(a_ref[...], wa_ref[...])
           + jax.nn.sigmoid(gb_ref[...]) * _dot(b_ref[...], wb_ref[...])
           + jax.nn.sigmoid(gc_ref[...]) * _dot(c_ref[...], wc_ref[...]))
    o_ref[...] = h_ref[...] + _dot(mix.astype(BF16), wo_ref[...])


def _merge(ya, yb, yc, zf, wa, wb, wc, wo, h, layer, *, tm=256):
    m, d = h.shape
    kw = ya.shape[1]
    g0 = ZF_GATES * LANE // d
    branch = pl.BlockSpec((tm, kw), lambda i: (i, 0))
    resident = dict(pipeline_mode=pl.Buffered(1))
    wspec = pl.BlockSpec((None, kw, d), lambda i: (layer, 0, 0), **resident)
    return pl.pallas_call(
        _merge_kernel,
        grid=(m // tm,),
        in_specs=[
            branch, branch, branch,
            pl.BlockSpec((tm, d), lambda i: (i, g0)),
            pl.BlockSpec((tm, d), lambda i: (i, g0 + 1)),
            pl.BlockSpec((tm, d), lambda i: (i, g0 + 2)),
            wspec, wspec, wspec,
            pl.BlockSpec((None, d, d), lambda i: (layer, 0, 0), **resident),
            pl.BlockSpec((tm, d), lambda i: (i, 0)),
        ],
        out_specs=pl.BlockSpec((tm, d), lambda i: (i, 0)),
        out_shape=jax.ShapeDtypeStruct((m, d), F32),
        compiler_params=_params(("parallel",)),
        name="merge",
    )(ya, yb, yc, zf, zf, zf, wa, wb, wc, wo, h)


def _xattn_kernel(h_ref, g_ref, wq_ref, kv_ref, wo_ref, o_ref):
    dh = XA_HEAD_DIM
    x = h_ref[...]
    n = _rms(x, g_ref[...]).astype(BF16)
    q = _dot(n, wq_ref[...].astype(BF16)) * (dh ** -0.5)
    outs = []
    for hd in range(XA_HEADS):
        qh = q[:, hd * dh:(hd + 1) * dh].astype(BF16)
        kh = kv_ref[:, hd * dh:(hd + 1) * dh].astype(BF16)
        vh = kv_ref[:, XA_WIDTH + hd * dh:XA_WIDTH + (hd + 1) * dh].astype(BF16)
        s = _dot_nt(qh, kh)
        e = jnp.exp(s - jnp.max(s, axis=-1, keepdims=True))
        p = e / jnp.sum(e, axis=-1, keepdims=True)
        outs.append(_dot(p.astype(BF16), vh))
    o = jnp.concatenate(outs, axis=1).astype(BF16)
    o_ref[...] = x + _dot(o, wo_ref[...].astype(BF16))


def _xattn(h, norm_g, wq, kv, wo, layer, seq, mem_len, *, tm=1024):
    m, d = h.shape
    per_batch = seq // tm
    return pl.pallas_call(
        _xattn_kernel,
        grid=(m // tm,),
        in_specs=[
            pl.BlockSpec((tm, d), lambda i: (i, 0)),
            pl.BlockSpec((1, d), lambda i: (0, 0)),
            pl.BlockSpec((None, d, XA_WIDTH), lambda i: (layer, 0, 0)),
            pl.BlockSpec((mem_len, 2 * XA_WIDTH), lambda i: (i // per_batch, 0)),
            pl.BlockSpec((None, XA_WIDTH, d), lambda i: (layer, 0, 0)),
        ],
        out_specs=pl.BlockSpec((tm, d), lambda i: (i, 0)),
        out_shape=jax.ShapeDtypeStruct((m, d), F32),
        compiler_params=_params(("parallel",)),
        name="xattn",
    )(h, norm_g.reshape(1, d), wq, kv, wo)


CAST_BLOCK_BYTES = 6 * 1024 * 1024


def _cast_kernel(x_ref, o_ref):
    o_ref[...] = x_ref[...].astype(o_ref.dtype)


def _to_bf16(w, cols=None):
    lead = w.shape[:-1]
    c = w.shape[-1] if cols is None else cols
    w2 = w.reshape(-1, w.shape[-1])
    r = w2.shape[0]
    rows = 16
    while rows * 2 * c * 4 <= CAST_BLOCK_BYTES and r % (rows * 2) == 0:
        rows *= 2
    spec = pl.BlockSpec((rows, c), lambda i: (i, 0))
    out = pl.pallas_call(
        _cast_kernel,
        grid=(r // rows,),
        in_specs=[spec],
        out_specs=spec,
        out_shape=jax.ShapeDtypeStruct((r, c), BF16),
        compiler_params=_params(("parallel",)),
        name="cast",
    )(w2)
    return out.reshape(*lead, c)

W_FQ = 2 * SGU_WIDTH
W_FF = W_FQ + 3 * FOX_WIDTH
W_GQ = W_FF + FOX_HEADS
W_GV = W_GQ + 2 * GLA_KW
W_GA = W_GV + GLA_VW
W_GR = W_GA + GLA_GATE_RANK
W_END = W_GR + GLA_VW + 3 * D_MODEL


REPACK_CHUNK = 1024


def _repack_kernel(wt_ref, wf_ref, wb_ref, ws_ref):
    def move(dst_ref, dst, src, width):
        for off in range(0, width, REPACK_CHUNK):
            n = min(REPACK_CHUNK, width - off)
            dst_ref[:, dst + off:dst + off + n] = wt_ref[src + off:src + off + n, :].T.astype(BF16)

    move(wf_ref, 0, 0, W_FQ)
    move(wf_ref, W_FQ, W_GQ, 2 * GLA_KW)
    move(wf_ref, W_FQ + 2 * GLA_KW, W_GR, W_END - W_GR)
    move(wb_ref, 0, W_FQ, 3 * FOX_WIDTH)
    move(wb_ref, 3 * FOX_WIDTH, W_GV, GLA_VW)
    lane = lax.broadcasted_iota(jnp.int32, (wt_ref.shape[1], LANE), 1)
    ff = wt_ref[W_FF - SMALL_FF:W_FF - SMALL_FF + LANE, :].T
    ga = wt_ref[W_GA - SMALL_GA:W_GA - SMALL_GA + LANE, :].T
    side = jnp.where(lane < SMALL_FF + FOX_HEADS, ff,
                     jnp.where(lane < SMALL_GA + GLA_GATE_RANK, ga, 0.0))
    ws_ref[...] = side.astype(BF16)


def _repack_w_in(w, *, kb=256):
    nl, d, n = w.shape
    wt = jnp.swapaxes(w, 1, 2)

    def spec(width):
        return pl.BlockSpec((None, kb, width), lambda l, i: (l, i, 0))

    return pl.pallas_call(
        _repack_kernel,
        grid=(nl, d // kb),
        in_specs=[pl.BlockSpec((None, n, kb), lambda l, i: (l, 0, i))],
        out_specs=[spec(ZF_WIDTH), spec(ZB_WIDTH), spec(LANE)],
        out_shape=[jax.ShapeDtypeStruct((nl, d, width), BF16) for width in (ZF_WIDTH, ZB_WIDTH, LANE)],
        compiler_params=_params(("parallel", "parallel")),
        name="repack",
    )(wt)


def _embed_w_gate(w_gate):
    emb = jnp.zeros((LANE, GLA_KW), BF16)
    return emb.at[SMALL_GA:SMALL_GA + GLA_GATE_RANK, :].set(w_gate.astype(BF16))


def kernel(x, mem, ffn1_norm, ffn1_w_in, ffn1_w_out, mix_norm, w_in, sgu_ln_g, sgu_ln_b, sgu_w_s, sgu_b_s, fox_b_f, gla_w_gate, gla_b_gate, gla_o_norm, w_branch_a, w_branch_b, w_branch_c, w_out, xa_norm, mem_norm, xa_w_q, xa_w_kv, xa_w_o, ffn2_norm, ffn2_w_in, ffn2_w_out, final_norm):
    batch, seq, d = x.shape
    mem_len = mem.shape[1]
    h = x.reshape(batch * seq, d)
    mem2 = mem.reshape(batch * mem_len, d)
    ones_f = jnp.ones((1, ZF_WIDTH), F32)
    scale_b = jnp.ones((1, ZB_WIDTH), F32).at[:, ZB_FQ * LANE:ZB_FQ * LANE + FOX_WIDTH].set(
        FOX_HEAD_DIM ** -0.5)
    f1_gate, f2_gate = _to_bf16(ffn1_w_in, D_FF), _to_bf16(ffn2_w_in, D_FF)
    wa, wb, wc, wo = (_to_bf16(w) for w in (w_branch_a, w_branch_b, w_branch_c, w_out))
    w_f, w_b, w_small = _repack_w_in(w_in)
    for l in range(DEPTH):
        h = _ffn(h, ffn1_norm[l], f1_gate, ffn1_w_in, ffn1_w_out, l)

        zf, small = _proj(h, mix_norm[l], w_f, l, ones_f, F32, w_small)
        zb = _proj(h, mix_norm[l], w_b, l, scale_b, BF16)
        ya = _sgu(zf, sgu_ln_g[l], sgu_ln_b[l], sgu_w_s[l], sgu_b_s[l])
        qa, ka = _fox_gate(small, fox_b_f[l], batch, seq)
        yb = _fox(zb, qa, ka, batch, seq)
        bc = _gla_decay(small, _embed_w_gate(gla_w_gate[l]), gla_b_gate[l])
        yc = _gla(zf, zb, bc, gla_o_norm[l], batch, seq)
        h = _merge(ya, yb, yc, zf, wa, wb, wc, wo, h, l)

        kv = _norm_proj(mem2, mem_norm[l], xa_w_kv, l)
        h = _xattn(h, xa_norm[l], xa_w_q, kv, xa_w_o, l, seq, mem_len)

        h = _ffn(h, ffn2_norm[l], f2_gate, ffn2_w_in, ffn2_w_out, l,
                 final_norm if l == DEPTH - 1 else None)
    return h.reshape(batch, seq, d)
```

```python
import functools

import jax
import jax.numpy as jnp
from jax import lax
from jax.experimental import pallas as pl
from jax.experimental.pallas import tpu as pltpu

F32 = jnp.float32
BF16 = jnp.bfloat16

D_MODEL = 2048
DEPTH = 2
D_FF = 5632
EPS = 1e-6

SGU_GROUPS = 4
SGU_GROUP_DIM = 256
SGU_WIDTH = 1024
SGU_CHUNK = 128

FOX_HEADS = 8
FOX_HEAD_DIM = 128
FOX_WIDTH = 1024

GLA_HEADS = 4
GLA_DK = 128
GLA_DV = 256
GLA_KW = 512
GLA_VW = 1024
GLA_GATE_RANK = 16
GLA_GATE_TAU = 16.0
GLA_CHUNK = 64

XA_HEADS = 4
XA_HEAD_DIM = 128
XA_WIDTH = 512

LANE = 128

ZF_SU, ZF_SV, ZF_GQ, ZF_GK, ZF_GR, ZF_GATES = 0, 8, 16, 20, 24, 32
ZF_WIDTH = 80 * LANE
ZB_FQ, ZB_FK, ZB_FV, ZB_GV = 0, 8, 16, 24
ZB_WIDTH = 32 * LANE
SMALL_FF = 0
SMALL_GA = 8

NEG_BIG = -1e30
VMEM_LIMIT = 63 * 1024 * 1024

FFN_ROWS, FFN_HIDDEN_TILE = 1024, 512
PROJ_ROWS, PROJ_COLS = 1024, 2048
MEM_KV_ROWS = 1024
SGU_CHUNKS_PER_STEP = 8
FOX_TILE, FOX_HEADS_PER_STEP = 512, 2
GLA_DECAY_BLOCKS_PER_STEP = 4
GLA_HEADS_PER_STEP = 2
MERGE_ROWS = 256
XATTN_ROWS = 1024
REPACK_K_ROWS = 256


def _params(sem):
    return pltpu.CompilerParams(dimension_semantics=sem, vmem_limit_bytes=VMEM_LIMIT)


def _rms(x, g):
    return x * lax.rsqrt(jnp.mean(x * x, axis=-1, keepdims=True) + EPS) * g


def _log_sigmoid(x):
    return jnp.minimum(x, 0.0) - jnp.log1p(jnp.exp(-jnp.abs(x)))


def _gelu_tanh(x):
    c = 0.7978845608028654
    return x * (0.5 * (1.0 + jnp.tanh(c * (x + 0.044715 * (x * x * x)))))


def _silu(x):
    return x * jax.nn.sigmoid(x)


def _dot(a, b):
    return jnp.dot(a, b, preferred_element_type=F32)


def _dot_nt(a, b):
    return lax.dot_general(a, b, (((1,), (1,)), ((), ())), preferred_element_type=F32)


def _dot_tn(a, b):
    return lax.dot_general(a, b, (((0,), (0,)), ((), ())), preferred_element_type=F32)


def _split3(x):
    hi = x.astype(BF16)
    r = x - hi.astype(F32)
    lo = r.astype(BF16)
    lo2 = (r - lo.astype(F32)).astype(BF16)
    return hi, lo, lo2


def _dot_01(mat3, x):
    return _dot(mat3, jnp.concatenate(_split3(x), axis=0))


def _ffn_kernel(*refs, final):
    if final:
        x_ref, g_ref, wg_ref, wu_ref, wo_ref, fg_ref, o_ref, xn_ref = refs
    else:
        x_ref, g_ref, wg_ref, wu_ref, wo_ref, o_ref, xn_ref = refs
    j = pl.program_id(1)

    def tile(xn):
        gate = _dot(xn, wg_ref[...])
        up = _dot(xn, wu_ref[...].astype(BF16))
        act = (_silu(gate) * up * 0.5).astype(BF16)
        return _dot(act, wo_ref[...].astype(BF16))

    @pl.when(j == 0)
    def _():
        x = x_ref[...]
        xn = _rms(x, g_ref[...]).astype(BF16)
        xn_ref[...] = xn
        o_ref[...] = x + tile(xn)

    @pl.when(j > 0)
    def _():
        o_ref[...] += tile(xn_ref[...])

    if final:
        @pl.when(j == pl.num_programs(1) - 1)
        def _():
            o_ref[...] = _rms(o_ref[...], fg_ref[...])


def _ffn(h, norm_g, w_gate, w_in, w_out, layer, final_g=None, *, tm=FFN_ROWS, tf=FFN_HIDDEN_TILE):
    m, d = h.shape
    f = w_out.shape[1]
    nt = f // tf
    final = final_g is not None
    vec = pl.BlockSpec((1, d), lambda i, j: (0, 0))
    rows = pl.BlockSpec((tm, d), lambda i, j: (i, 0))
    in_specs = [
        rows, vec,
        pl.BlockSpec((None, d, tf), lambda i, j: (layer, 0, j)),
        pl.BlockSpec((None, d, tf), lambda i, j: (layer, 0, j + nt)),
        pl.BlockSpec((None, tf, d), lambda i, j: (layer, j, 0)),
    ]
    args = [h, norm_g.reshape(1, d), w_gate, w_in, w_out]
    if final:
        in_specs.append(vec)
        args.append(final_g.reshape(1, d))
    return pl.pallas_call(
        functools.partial(_ffn_kernel, final=final),
        grid=(m // tm, nt),
        in_specs=in_specs,
        out_specs=rows,
        out_shape=jax.ShapeDtypeStruct((m, d), F32),
        scratch_shapes=[pltpu.VMEM((tm, d), BF16)],
        compiler_params=_params(("parallel", "arbitrary")),
        name="ffn",
    )(*args)


def _proj_kernel(*refs, small):
    if small:
        x_ref, g_ref, w_ref, cs_ref, ws_ref, o_ref, os_ref, xn_ref = refs
    else:
        x_ref, g_ref, w_ref, cs_ref, o_ref, xn_ref = refs
    j = pl.program_id(1)

    def project(xn):
        o_ref[...] = (_dot(xn, w_ref[...]) * cs_ref[...]).astype(o_ref.dtype)

    @pl.when(j == 0)
    def _():
        xn = _rms(x_ref[...], g_ref[...]).astype(BF16)
        xn_ref[...] = xn
        if small:
            os_ref[...] = _dot(xn, ws_ref[...])
        project(xn)

    @pl.when(j > 0)
    def _():
        project(xn_ref[...])


def _proj(x, norm_g, w, layer, col_scale, out_dtype, w_small=None, *, tm=PROJ_ROWS, tn=PROJ_COLS):
    m, d = x.shape
    n = w.shape[2]
    small = w_small is not None
    in_specs = [
        pl.BlockSpec((tm, d), lambda i, j: (i, 0)),
        pl.BlockSpec((1, d), lambda i, j: (0, 0)),
        pl.BlockSpec((None, d, tn), lambda i, j: (layer, 0, j)),
        pl.BlockSpec((1, tn), lambda i, j: (0, j)),
    ]
    args = [x, norm_g.reshape(1, d), w, col_scale]
    out_specs = [pl.BlockSpec((tm, tn), lambda i, j: (i, j))]
    out_shape = [jax.ShapeDtypeStruct((m, n), out_dtype)]
    if small:
        in_specs.append(pl.BlockSpec((None, d, LANE), lambda i, j: (layer, 0, 0)))
        args.append(w_small)
        out_specs.append(pl.BlockSpec((tm, LANE), lambda i, j: (i, 0)))
        out_shape.append(jax.ShapeDtypeStruct((m, LANE), F32))
    out = pl.pallas_call(
        functools.partial(_proj_kernel, small=small),
        grid=(m // tm, n // tn),
        in_specs=in_specs,
        out_specs=out_specs,
        out_shape=out_shape,
        scratch_shapes=[pltpu.VMEM((tm, d), BF16)],
        compiler_params=_params(("parallel", "arbitrary")),
        name="proj",
    )(*args)
    return out if small else out[0]


def _norm_proj_kernel(x_ref, g_ref, w_ref, o_ref):
    o_ref[...] = _dot(_rms(x_ref[...], g_ref[...]).astype(BF16), w_ref[...].astype(BF16))


def _norm_proj(x, norm_g, w, layer, *, tm=MEM_KV_ROWS):
    m, d = x.shape
    n = w.shape[2]
    return pl.pallas_call(
        _norm_proj_kernel,
        grid=(m // tm,),
        in_specs=[
            pl.BlockSpec((tm, d), lambda i: (i, 0)),
            pl.BlockSpec((1, d), lambda i: (0, 0)),
            pl.BlockSpec((None, d, n), lambda i: (layer, 0, 0)),
        ],
        out_specs=pl.BlockSpec((tm, n), lambda i: (i, 0)),
        out_shape=jax.ShapeDtypeStruct((m, n), F32),
        compiler_params=_params(("parallel",)),
        name="mem_kv",
    )(x, norm_g.reshape(1, d), w)


def _sgu_kernel(u_ref, v_ref, lng_ref, lnb_ref, ws_ref, bs_ref, o_ref, *, chunks):
    t = SGU_CHUNK
    gd = SGU_GROUP_DIM
    row = lax.broadcasted_iota(jnp.int32, (t, t), 0)
    col = lax.broadcasted_iota(jnp.int32, (t, t), 1)
    causal = row >= col
    for g in range(SGU_GROUPS):
        w = jnp.where(causal, ws_ref[g], 0.0).astype(BF16)
        bias = bs_ref[:, g:g + 1]
        ln_g = lng_ref[:, g * gd:(g + 1) * gd]
        ln_b = lnb_ref[:, g * gd:(g + 1) * gd]
        for c in range(chunks):
            rows = slice(c * t, (c + 1) * t)
            cols = slice(g * gd, (g + 1) * gd)
            v = _gelu_tanh(v_ref[rows, cols])
            mu = jnp.mean(v, axis=-1, keepdims=True)
            vc = v - mu
            var = jnp.mean(vc * vc, axis=-1, keepdims=True)
            vn = vc * lax.rsqrt(var + EPS) * ln_g + ln_b
            mixed = _dot(w, vn.astype(BF16)) + bias
            o_ref[rows, cols] = (_gelu_tanh(u_ref[rows, cols]) * mixed).astype(o_ref.dtype)


def _sgu(zf, ln_g, ln_b, w_s, b_s, *, chunks=SGU_CHUNKS_PER_STEP):
    m = zf.shape[0]
    rows = chunks * SGU_CHUNK
    w = SGU_WIDTH
    return pl.pallas_call(
        functools.partial(_sgu_kernel, chunks=chunks),
        grid=(m // rows,),
        in_specs=[
            pl.BlockSpec((rows, w), lambda i: (i, ZF_SU * LANE // w)),
            pl.BlockSpec((rows, w), lambda i: (i, ZF_SV * LANE // w)),
            pl.BlockSpec((1, w), lambda i: (0, 0)),
            pl.BlockSpec((1, w), lambda i: (0, 0)),
            pl.BlockSpec((SGU_GROUPS, SGU_CHUNK, SGU_CHUNK), lambda i: (0, 0, 0)),
            pl.BlockSpec((SGU_CHUNK, SGU_GROUPS), lambda i: (0, 0)),
        ],
        out_specs=pl.BlockSpec((rows, w), lambda i: (i, 0)),
        out_shape=jax.ShapeDtypeStruct((m, w), BF16),
        compiler_params=_params(("parallel",)),
        name="sgu",
    )(zf, zf, ln_g.reshape(1, w), ln_b.reshape(1, w), w_s, jnp.swapaxes(b_s, 0, 1))


FOX_AUG = 6


def _fox_gate_kernel(f_ref, bf_ref, tril_ref, pq_ref, pk_ref, cq_ref, ck_ref, qa_ref, ka_ref):
    t = LANE
    seq = f_ref.shape[0]
    carry = jnp.zeros((1, t), F32)
    for b in range(seq // t):
        rows = slice(b * t, (b + 1) * t)
        lf = _log_sigmoid(f_ref[rows, :] + bf_ref[...])
        cum = _dot_01(tril_ref[...], lf) + carry
        carry = cum[t - 1:t, :]
        parts = jnp.concatenate(_split3(cum), axis=1)
        qa_ref[rows, :] = (_dot(parts, pq_ref[...]) + cq_ref[...]).astype(BF16)
        ka_ref[rows, :] = (_dot(parts, pk_ref[...]) + ck_ref[...]).astype(BF16)


def _tril3(n):
    tril = jnp.tril(jnp.ones((n, n), BF16))
    return jnp.concatenate([tril, tril, tril], axis=1)


def _fox_placement():
    src = jnp.arange(3 * LANE)[:, None]
    dst = jnp.arange(LANE)[None, :]
    p, h = src // LANE, src % LANE
    valid = h < FOX_HEADS
    pq = jnp.where(valid & (dst == h * FOX_AUG + p), 1.0, 0.0).astype(BF16)
    pk = jnp.where(valid & (dst == h * FOX_AUG + 3 + p), -1.0, 0.0).astype(BF16)
    used = dst < FOX_HEADS * FOX_AUG
    cq = jnp.where(used & (dst % FOX_AUG >= 3), 1.0, 0.0).astype(F32)
    ck = jnp.where(used & (dst % FOX_AUG < 3), 1.0, 0.0).astype(F32)
    return pq, pk, cq, ck


def _fox_gate(small, b_f, batch, seq):
    bf = jnp.zeros((1, LANE), F32).at[0, SMALL_FF:SMALL_FF + FOX_HEADS].set(b_f)
    w = LANE
    aug = jax.ShapeDtypeStruct((batch * seq, w), BF16)

    def const(shape):
        return pl.BlockSpec(shape, lambda b: (0, 0))

    return pl.pallas_call(
        _fox_gate_kernel,
        grid=(batch,),
        in_specs=[
            pl.BlockSpec((seq, LANE), lambda b: (b, 0)),
            const((1, LANE)), const((LANE, 3 * LANE)),
            const((3 * LANE, w)), const((3 * LANE, w)), const((1, w)), const((1, w)),
        ],
        out_specs=[pl.BlockSpec((seq, w), lambda b: (b, 0))] * 2,
        out_shape=[aug, aug],
        compiler_params=_params(("parallel",)),
        name="fox_gate",
    )(small, bf, _tril3(LANE), *_fox_placement())


def _fox_kernel(q_ref, k_ref, v_ref, qa_ref, ka_ref, o_ref, *, t, heads):
    seq = q_ref.shape[0]
    dh = FOX_HEAD_DIM
    lane = lax.broadcasted_iota(jnp.int32, (t, LANE), 1)
    ones_col = jnp.where(lane == 0, 1.0, 0.0).astype(BF16)
    row = lax.broadcasted_iota(jnp.int32, (t, t), 0)
    col = lax.broadcasted_iota(jnp.int32, (t, t), 1)
    causal = row >= col

    for g in range(heads):
        head = pl.program_id(1) * heads + g
        own = (lane >= head * FOX_AUG) & (lane < (head + 1) * FOX_AUG)
        cols = slice(g * dh, (g + 1) * dh)

        def k_block(j):
            rows = slice(j * t, (j + 1) * t)
            ka = jnp.where(own, ka_ref[rows, :], jnp.zeros((), BF16))
            return (jnp.concatenate([k_ref[rows, cols], ka], axis=1),
                    jnp.concatenate([v_ref[rows, cols], ones_col], axis=1))

        for i in range(seq // t):
            rows = slice(i * t, (i + 1) * t)
            qc = jnp.concatenate([q_ref[rows, cols], qa_ref[rows, :]], axis=1)
            kc, vc = k_block(i)
            s = jnp.where(causal, _dot_nt(qc, kc), NEG_BIG)
            m = jnp.max(s, axis=1, keepdims=True)
            acc = _dot(jnp.exp(s - m).astype(BF16), vc)
            for j in range(i):
                kc, vc = k_block(j)
                s = _dot_nt(qc, kc)
                m_new = jnp.maximum(m, jnp.max(s, axis=1, keepdims=True))
                acc = jnp.exp(m - m_new) * acc + _dot(jnp.exp(s - m_new).astype(BF16), vc)
                m = m_new
            o_ref[rows, cols] = (acc[:, :dh] / acc[:, dh:dh + 1]).astype(o_ref.dtype)


def _fox(zb, qa, ka, batch, seq, *, t=FOX_TILE, heads=FOX_HEADS_PER_STEP):
    m = zb.shape[0]
    w = heads * FOX_HEAD_DIM

    def group(seg):
        return pl.BlockSpec((seq, w), lambda b, h: (b, seg * LANE // w + h))

    aug = pl.BlockSpec((seq, LANE), lambda b, h: (b, 0))
    return pl.pallas_call(
        functools.partial(_fox_kernel, t=t, heads=heads),
        grid=(batch, FOX_HEADS // heads),
        in_specs=[group(ZB_FQ), group(ZB_FK), group(ZB_FV), aug, aug],
        out_specs=group(0),
        out_shape=jax.ShapeDtypeStruct((m, FOX_WIDTH), BF16),
        compiler_params=_params(("parallel", "parallel")),
        name="fox",
    )(zb, zb, zb, qa, ka)


GLA_BLOCK = 4 * GLA_CHUNK


def _gla_decay_kernel(a_ref, wg_ref, bg_ref, cm_ref, o_ref, *, blocks):
    t = GLA_BLOCK
    for b in range(blocks):
        rows = slice(b * t, (b + 1) * t)
        gl = _dot(a_ref[rows, :].astype(BF16), wg_ref[...]) + bg_ref[...]
        g = _log_sigmoid(gl) / GLA_GATE_TAU
        o_ref[rows, :] = _dot_01(cm_ref[...], g)


def _gla_decay(small, w_gate_emb, b_gate, *, blocks=GLA_DECAY_BLOCKS_PER_STEP):
    m = small.shape[0]
    t = GLA_BLOCK
    rows = blocks * t
    idx = jnp.arange(t)
    same = (idx[:, None] // GLA_CHUNK) == (idx[None, :] // GLA_CHUNK)
    cm = jnp.where(same & (idx[:, None] >= idx[None, :]), 1.0, 0.0).astype(BF16)
    cm3 = jnp.concatenate([cm, cm, cm], axis=1)
    return pl.pallas_call(
        functools.partial(_gla_decay_kernel, blocks=blocks),
        grid=(m // rows,),
        in_specs=[
            pl.BlockSpec((rows, LANE), lambda i: (i, 0)),
            pl.BlockSpec((LANE, GLA_KW), lambda i: (0, 0)),
            pl.BlockSpec((1, GLA_KW), lambda i: (0, 0)),
            pl.BlockSpec((t, 3 * t), lambda i: (0, 0)),
        ],
        out_specs=pl.BlockSpec((rows, GLA_KW), lambda i: (i, 0)),
        out_shape=jax.ShapeDtypeStruct((m, GLA_KW), F32),
        compiler_params=_params(("parallel",)),
        name="gla_decay",
    )(small, w_gate_emb, b_gate.reshape(1, GLA_KW), cm3)


def _gla_kernel(q_ref, k_ref, v_ref, r_ref, bc_ref, on_ref, o_ref, *, heads):
    for g in range(heads):
        _gla_head(q_ref, k_ref, v_ref, r_ref, bc_ref, on_ref, o_ref,
                  slice(g * GLA_DK, (g + 1) * GLA_DK), slice(g * GLA_DV, (g + 1) * GLA_DV))


def _gla_head(q_ref, k_ref, v_ref, r_ref, bc_ref, on_ref, o_ref, kcols, vcols):
    c = GLA_CHUNK
    t = GLA_BLOCK
    dk = GLA_DK
    seq = q_ref.shape[0]
    st = jnp.zeros((GLA_DV, dk), F32)
    row = lax.broadcasted_iota(jnp.int32, (t, t), 0)
    col = lax.broadcasted_iota(jnp.int32, (t, t), 1)
    intra = (row >= col) & ((row // c) == (col // c))
    o_gain = on_ref[:, vcols]

    for n in range(seq // t):
        rows = slice(n * t, (n + 1) * t)
        bc = bc_ref[rows, kcols]
        qb = q_ref[rows, kcols] * (dk ** -0.5)
        kb = k_ref[rows, kcols]
        vb = v_ref[rows, vcols]
        chunks = [slice(i * c, (i + 1) * c) for i in range(t // c)]
        b_mid = jnp.concatenate(
            [jnp.broadcast_to(bc[s.start + c // 2:s.start + c // 2 + 1, :], (c, dk)) for s in chunks], axis=0)
        b_last = jnp.concatenate(
            [jnp.broadcast_to(bc[s.stop - 1:s.stop, :], (c, dk)) for s in chunks], axis=0)
        qd = (qb * jnp.exp(bc - b_mid)).astype(BF16)
        kd = (kb * jnp.exp(b_mid - bc)).astype(BF16)
        ku = (kb * jnp.exp(b_last - bc)).astype(BF16)
        qi = (qb * jnp.exp(bc)).astype(BF16)
        att = jnp.where(intra, _dot_nt(qd, kd), 0.0)
        o = _dot(att.astype(BF16), vb)
        inter = []
        for s in chunks:
            inter.append(_dot_nt(qi[s, :], st.astype(BF16)))
            st = st * jnp.exp(bc[s.stop - 1:s.stop, :]) + _dot_tn(vb[s, :], ku[s, :])
        o = _rms(o + jnp.concatenate(inter, axis=0), o_gain)
        o_ref[rows, vcols] = (o * _silu(r_ref[rows, vcols])).astype(o_ref.dtype)


def _gla(zf, zb, bc, o_norm, batch, seq, *, heads=GLA_HEADS_PER_STEP):
    m = zf.shape[0]
    dk, dv = heads * GLA_DK, heads * GLA_DV
    return pl.pallas_call(
        functools.partial(_gla_kernel, heads=heads),
        grid=(batch, GLA_HEADS // heads),
        in_specs=[
            pl.BlockSpec((seq, dk), lambda b, h: (b, ZF_GQ * LANE // dk + h)),
            pl.BlockSpec((seq, dk), lambda b, h: (b, ZF_GK * LANE // dk + h)),
            pl.BlockSpec((seq, dv), lambda b, h: (b, ZB_GV * LANE // dv + h)),
            pl.BlockSpec((seq, dv), lambda b, h: (b, ZF_GR * LANE // dv + h)),
            pl.BlockSpec((seq, dk), lambda b, h: (b, h)),
            pl.BlockSpec((1, dv), lambda b, h: (0, h)),
        ],
        out_specs=pl.BlockSpec((seq, dv), lambda b, h: (b, h)),
        out_shape=jax.ShapeDtypeStruct((m, GLA_VW), BF16),
        compiler_params=_params(("parallel", "parallel")),
        name="gla",
    )(zf, zf, zb, zf, bc, o_norm.reshape(1, GLA_VW))


def _merge_kernel(a_ref, b_ref, c_ref, ga_ref, gb_ref, gc_ref, wa_ref, wb_ref, wc_ref,
                  wo_ref, h_ref, o_ref):
    mix = (jax.nn.sigmoid(ga_ref[...]) * _dot(a_ref[...], wa_ref[...])
           + jax.nn.sigmoid(gb_ref[...]) * _dot(b_ref[...], wb_ref[...])
           + jax.nn.sigmoid(gc_ref[...]) * _dot(c_ref[...], wc_ref[...]))
    o_ref[...] = h_ref[...] + _dot(mix.astype(BF16), wo_ref[...])


def _merge(ya, yb, yc, zf, wa, wb, wc, wo, h, layer, *, tm=MERGE_ROWS):
    m, d = h.shape
    kw = ya.shape[1]
    g0 = ZF_GATES * LANE // d
    branch = pl.BlockSpec((tm, kw), lambda i: (i, 0))
    resident = dict(pipeline_mode=pl.Buffered(1))
    wspec = pl.BlockSpec((None, kw, d), lambda i: (layer, 0, 0), **resident)
    return pl.pallas_call(
        _merge_kernel,
        grid=(m // tm,),
        in_specs=[
            branch, branch, branch,
            pl.BlockSpec((tm, d), lambda i: (i, g0)),
            pl.BlockSpec((tm, d), lambda i: (i, g0 + 1)),
            pl.BlockSpec((tm, d), lambda i: (i, g0 + 2)),
            wspec, wspec, wspec,
            pl.BlockSpec((None, d, d), lambda i: (layer, 0, 0), **resident),
            pl.BlockSpec((tm, d), lambda i: (i, 0)),
        ],
        out_specs=pl.BlockSpec((tm, d), lambda i: (i, 0)),
        out_shape=jax.ShapeDtypeStruct((m, d), F32),
        compiler_params=_params(("parallel",)),
        name="merge",
    )(ya, yb, yc, zf, zf, zf, wa, wb, wc, wo, h)


def _xattn_kernel(h_ref, g_ref, wq_ref, kv_ref, wo_ref, o_ref):
    dh = XA_HEAD_DIM
    x = h_ref[...]
    n = _rms(x, g_ref[...]).astype(BF16)
    q = _dot(n, wq_ref[...].astype(BF16)) * (dh ** -0.5)
    outs = []
    for hd in range(XA_HEADS):
        qh = q[:, hd * dh:(hd + 1) * dh].astype(BF16)
        kh = kv_ref[:, hd * dh:(hd + 1) * dh].astype(BF16)
        vh = kv_ref[:, XA_WIDTH + hd * dh:XA_WIDTH + (hd + 1) * dh].astype(BF16)
        s = _dot_nt(qh, kh)
        e = jnp.exp(s - jnp.max(s, axis=-1, keepdims=True))
        p = e / jnp.sum(e, axis=-1, keepdims=True)
        outs.append(_dot(p.astype(BF16), vh))
    o = jnp.concatenate(outs, axis=1).astype(BF16)
    o_ref[...] = x + _dot(o, wo_ref[...].astype(BF16))


def _xattn(h, norm_g, wq, kv, wo, layer, seq, mem_len, *, tm=XATTN_ROWS):
    m, d = h.shape
    per_batch = seq // tm
    return pl.pallas_call(
        _xattn_kernel,
        grid=(m // tm,),
        in_specs=[
            pl.BlockSpec((tm, d), lambda i: (i, 0)),
            pl.BlockSpec((1, d), lambda i: (0, 0)),
            pl.BlockSpec((None, d, XA_WIDTH), lambda i: (layer, 0, 0)),
            pl.BlockSpec((mem_len, 2 * XA_WIDTH), lambda i: (i // per_batch, 0)),
            pl.BlockSpec((None, XA_WIDTH, d), lambda i: (layer, 0, 0)),
        ],
        out_specs=pl.BlockSpec((tm, d), lambda i: (i, 0)),
        out_shape=jax.ShapeDtypeStruct((m, d), F32),
        compiler_params=_params(("parallel",)),
        name="xattn",
    )(h, norm_g.reshape(1, d), wq, kv, wo)


CAST_BLOCK_BYTES = 6 * 1024 * 1024


def _cast_kernel(x_ref, o_ref):
    o_ref[...] = x_ref[...].astype(o_ref.dtype)


def _to_bf16(w, cols=None):
    lead = w.shape[:-1]
    c = w.shape[-1] if cols is None else cols
    w2 = w.reshape(-1, w.shape[-1])
    r = w2.shape[0]
    rows = 16
    while rows * 2 * c * 4 <= CAST_BLOCK_BYTES and r % (rows * 2) == 0:
        rows *= 2
    spec = pl.BlockSpec((rows, c), lambda i: (i, 0))
    out = pl.pallas_call(
        _cast_kernel,
        grid=(r // rows,),
        in_specs=[spec],
        out_specs=spec,
        out_shape=jax.ShapeDtypeStruct((r, c), BF16),
        compiler_params=_params(("parallel",)),
        name="cast",
    )(w2)
    return out.reshape(*lead, c)

W_FQ = 2 * SGU_WIDTH
W_FF = W_FQ + 3 * FOX_WIDTH
W_GQ = W_FF + FOX_HEADS
W_GV = W_GQ + 2 * GLA_KW
W_GA = W_GV + GLA_VW
W_GR = W_GA + GLA_GATE_RANK
W_END = W_GR + GLA_VW + 3 * D_MODEL


REPACK_CHUNK = 1024


def _repack_kernel(wt_ref, wf_ref, wb_ref, ws_ref):
    def move(dst_ref, dst, src, width):
        for off in range(0, width, REPACK_CHUNK):
            n = min(REPACK_CHUNK, width - off)
            dst_ref[:, dst + off:dst + off + n] = wt_ref[src + off:src + off + n, :].T.astype(BF16)

    move(wf_ref, 0, 0, W_FQ)
    move(wf_ref, W_FQ, W_GQ, 2 * GLA_KW)
    move(wf_ref, W_FQ + 2 * GLA_KW, W_GR, W_END - W_GR)
    move(wb_ref, 0, W_FQ, 3 * FOX_WIDTH)
    move(wb_ref, 3 * FOX_WIDTH, W_GV, GLA_VW)
    lane = lax.broadcasted_iota(jnp.int32, (wt_ref.shape[1], LANE), 1)
    ff = wt_ref[W_FF - SMALL_FF:W_FF - SMALL_FF + LANE, :].T
    ga = wt_ref[W_GA - SMALL_GA:W_GA - SMALL_GA + LANE, :].T
    side = jnp.where(lane < SMALL_FF + FOX_HEADS, ff,
                     jnp.where(lane < SMALL_GA + GLA_GATE_RANK, ga, 0.0))
    ws_ref[...] = side.astype(BF16)


def _repack_w_in(w, *, kb=REPACK_K_ROWS):
    nl, d, n = w.shape
    wt = jnp.swapaxes(w, 1, 2)

    def spec(width):
        return pl.BlockSpec((None, kb, width), lambda l, i: (l, i, 0))

    return pl.pallas_call(
        _repack_kernel,
        grid=(nl, d // kb),
        in_specs=[pl.BlockSpec((None, n, kb), lambda l, i: (l, 0, i))],
        out_specs=[spec(ZF_WIDTH), spec(ZB_WIDTH), spec(LANE)],
        out_shape=[jax.ShapeDtypeStruct((nl, d, width), BF16) for width in (ZF_WIDTH, ZB_WIDTH, LANE)],
        compiler_params=_params(("parallel", "parallel")),
        name="repack",
    )(wt)


def _embed_w_gate(w_gate):
    emb = jnp.zeros((LANE, GLA_KW), BF16)
    return emb.at[SMALL_GA:SMALL_GA + GLA_GATE_RANK, :].set(w_gate.astype(BF16))


def kernel(x, mem, ffn1_norm, ffn1_w_in, ffn1_w_out, mix_norm, w_in, sgu_ln_g, sgu_ln_b, sgu_w_s, sgu_b_s, fox_b_f, gla_w_gate, gla_b_gate, gla_o_norm, w_branch_a, w_branch_b, w_branch_c, w_out, xa_norm, mem_norm, xa_w_q, xa_w_kv, xa_w_o, ffn2_norm, ffn2_w_in, ffn2_w_out, final_norm):
    batch, seq, d = x.shape
    mem_len = mem.shape[1]
    h = x.reshape(batch * seq, d)
    mem2 = mem.reshape(batch * mem_len, d)
    ones_f = jnp.ones((1, ZF_WIDTH), F32)
    scale_b = jnp.ones((1, ZB_WIDTH), F32).at[:, ZB_FQ * LANE:ZB_FQ * LANE + FOX_WIDTH].set(
        FOX_HEAD_DIM ** -0.5)
    f1_gate, f2_gate = _to_bf16(ffn1_w_in, D_FF), _to_bf16(ffn2_w_in, D_FF)
    wa, wb, wc, wo = (_to_bf16(w) for w in (w_branch_a, w_branch_b, w_branch_c, w_out))
    w_f, w_b, w_small = _repack_w_in(w_in)
    for l in range(DEPTH):
        h = _ffn(h, ffn1_norm[l], f1_gate, ffn1_w_in, ffn1_w_out, l)

        zf, small = _proj(h, mix_norm[l], w_f, l, ones_f, F32, w_small)
        zb = _proj(h, mix_norm[l], w_b, l, scale_b, BF16)
        ya = _sgu(zf, sgu_ln_g[l], sgu_ln_b[l], sgu_w_s[l], sgu_b_s[l])
        qa, ka = _fox_gate(small, fox_b_f[l], batch, seq)
        yb = _fox(zb, qa, ka, batch, seq)
        bc = _gla_decay(small, _embed_w_gate(gla_w_gate[l]), gla_b_gate[l])
        yc = _gla(zf, zb, bc, gla_o_norm[l], batch, seq)
        h = _merge(ya, yb, yc, zf, wa, wb, wc, wo, h, l)

        kv = _norm_proj(mem2, mem_norm[l], xa_w_kv, l)
        h = _xattn(h, xa_norm[l], xa_w_q, kv, xa_w_o, l, seq, mem_len)

        h = _ffn(h, ffn2_norm[l], f2_gate, ffn2_w_in, ffn2_w_out, l,
                 final_norm if l == DEPTH - 1 else None)
    return h.reshape(batch, seq, d)
```

```python
import functools

import jax
import jax.numpy as jnp
from jax import lax
from jax.experimental import pallas as pl
from jax.experimental.pallas import tpu as pltpu

F32 = jnp.float32
BF16 = jnp.bfloat16

D_MODEL = 2048
DEPTH = 2
D_FF = 5632
EPS = 1e-6

SGU_GROUPS = 4
SGU_GROUP_DIM = 256
SGU_WIDTH = 1024
SGU_CHUNK = 128

FOX_HEADS = 8
FOX_HEAD_DIM = 128
FOX_WIDTH = 1024

GLA_HEADS = 4
GLA_DK = 128
GLA_DV = 256
GLA_KW = 512
GLA_VW = 1024
GLA_GATE_RANK = 16
GLA_GATE_TAU = 16.0
GLA_CHUNK = 64

XA_HEADS = 4
XA_HEAD_DIM = 128
XA_WIDTH = 512

LANE = 128

ZF_SU, ZF_SV, ZF_GQ, ZF_GK, ZF_GR, ZF_GATES = 0, 8, 16, 20, 24, 32
ZF_WIDTH = 80 * LANE
ZB_FQ, ZB_FK, ZB_FV, ZB_GV = 0, 8, 16, 24
ZB_WIDTH = 32 * LANE
SMALL_FF = 0
SMALL_GA = 8

NEG_BIG = -1e30
VMEM_LIMIT = 63 * 1024 * 1024

FFN_ROWS, FFN_HIDDEN_TILE = 1024, 512
PROJ_ROWS, PROJ_COLS = 1024, 2048
MEM_KV_ROWS = 1024
FOX_TILE, FOX_HEADS_PER_STEP = 512, 2
GLA_DECAY_BLOCKS_PER_STEP = 4
GLA_HEADS_PER_STEP = 2
MERGE_ROWS = 256
XATTN_ROWS = 1024
REPACK_K_ROWS = 256


def _params(sem):
    return pltpu.CompilerParams(dimension_semantics=sem, vmem_limit_bytes=VMEM_LIMIT)


def _rms(x, g):
    return x * lax.rsqrt(jnp.mean(x * x, axis=-1, keepdims=True) + EPS) * g


def _log_sigmoid(x):
    return jnp.minimum(x, 0.0) - jnp.log1p(jnp.exp(-jnp.abs(x)))


def _gelu_tanh(x):
    c = 0.7978845608028654
    return x * (0.5 * (1.0 + jnp.tanh(c * (x + 0.044715 * (x * x * x)))))


def _silu(x):
    return x * jax.nn.sigmoid(x)


def _dot(a, b):
    return jnp.dot(a, b, preferred_element_type=F32)


def _dot_nt(a, b):
    return lax.dot_general(a, b, (((1,), (1,)), ((), ())), preferred_element_type=F32)


def _dot_tn(a, b):
    return lax.dot_general(a, b, (((0,), (0,)), ((), ())), preferred_element_type=F32)


def _split3(x):
    hi = x.astype(BF16)
    r = x - hi.astype(F32)
    lo = r.astype(BF16)
    lo2 = (r - lo.astype(F32)).astype(BF16)
    return hi, lo, lo2


def _dot_01(mat3, x):
    return _dot(mat3, jnp.concatenate(_split3(x), axis=0))


def _ffn_kernel(*refs, final):
    if final:
        x_ref, g_ref, wg_ref, wu_ref, wo_ref, fg_ref, o_ref, xn_ref = refs
    else:
        x_ref, g_ref, wg_ref, wu_ref, wo_ref, o_ref, xn_ref = refs
    j = pl.program_id(1)

    def tile(xn):
        gate = _dot(xn, wg_ref[...])
        up = _dot(xn, wu_ref[...].astype(BF16))
        act = (_silu(gate) * up * 0.5).astype(BF16)
        return _dot(act, wo_ref[...].astype(BF16))

    @pl.when(j == 0)
    def _():
        x = x_ref[...]
        xn = _rms(x, g_ref[...]).astype(BF16)
        xn_ref[...] = xn
        o_ref[...] = x + tile(xn)

    @pl.when(j > 0)
    def _():
        o_ref[...] += tile(xn_ref[...])

    if final:
        @pl.when(j == pl.num_programs(1) - 1)
        def _():
            o_ref[...] = _rms(o_ref[...], fg_ref[...])


def _ffn(h, norm_g, w_gate, w_in, w_out, layer, final_g=None, *, tm=FFN_ROWS, tf=FFN_HIDDEN_TILE):
    m, d = h.shape
    f = w_out.shape[1]
    nt = f // tf
    final = final_g is not None
    vec = pl.BlockSpec((1, d), lambda i, j: (0, 0))
    rows = pl.BlockSpec((tm, d), lambda i, j: (i, 0))
    in_specs = [
        rows, vec,
        pl.BlockSpec((None, d, tf), lambda i, j: (layer, 0, j)),
        pl.BlockSpec((None, d, tf), lambda i, j: (layer, 0, j + nt)),
        pl.BlockSpec((None, tf, d), lambda i, j: (layer, j, 0)),
    ]
    args = [h, norm_g.reshape(1, d), w_gate, w_in, w_out]
    if final:
        in_specs.append(vec)
        args.append(final_g.reshape(1, d))
    return pl.pallas_call(
        functools.partial(_ffn_kernel, final=final),
        grid=(m // tm, nt),
        in_specs=in_specs,
        out_specs=rows,
        out_shape=jax.ShapeDtypeStruct((m, d), F32),
        scratch_shapes=[pltpu.VMEM((tm, d), BF16)],
        compiler_params=_params(("parallel", "arbitrary")),
        name="ffn",
    )(*args)


def _proj_kernel(*refs, small):
    if small:
        x_ref, g_ref, w_ref, cs_ref, ws_ref, o_ref, os_ref, xn_ref = refs
    else:
        x_ref, g_ref, w_ref, cs_ref, o_ref, xn_ref = refs
    j = pl.program_id(1)

    def project(xn):
        o_ref[...] = (_dot(xn, w_ref[...]) * cs_ref[...]).astype(o_ref.dtype)

    @pl.when(j == 0)
    def _():
        xn = _rms(x_ref[...], g_ref[...]).astype(BF16)
        xn_ref[...] = xn
        if small:
            os_ref[...] = _dot(xn, ws_ref[...])
        project(xn)

    @pl.when(j > 0)
    def _():
        project(xn_ref[...])


def _proj(x, norm_g, w, layer, col_scale, out_dtype, w_small=None, *, tm=PROJ_ROWS, tn=PROJ_COLS):
    m, d = x.shape
    n = w.shape[2]
    small = w_small is not None
    in_specs = [
        pl.BlockSpec((tm, d), lambda i, j: (i, 0)),
        pl.BlockSpec((1, d), lambda i, j: (0, 0)),
        pl.BlockSpec((None, d, tn), lambda i, j: (layer, 0, j)),
        pl.BlockSpec((1, tn), lambda i, j: (0, j)),
    ]
    args = [x, norm_g.reshape(1, d), w, col_scale]
    out_specs = [pl.BlockSpec((tm, tn), lambda i, j: (i, j))]
    out_shape = [jax.ShapeDtypeStruct((m, n), out_dtype)]
    if small:
        in_specs.append(pl.BlockSpec((None, d, LANE), lambda i, j: (layer, 0, 0)))
        args.append(w_small)
        out_specs.append(pl.BlockSpec((tm, LANE), lambda i, j: (i, 0)))
        out_shape.append(jax.ShapeDtypeStruct((m, LANE), F32))
    out = pl.pallas_call(
        functools.partial(_proj_kernel, small=small),
        grid=(m // tm, n // tn),
        in_specs=in_specs,
        out_specs=out_specs,
        out_shape=out_shape,
        scratch_shapes=[pltpu.VMEM((tm, d), BF16)],
        compiler_params=_params(("parallel", "arbitrary")),
        name="proj",
    )(*args)
    return out if small else out[0]


def _norm_proj_kernel(x_ref, g_ref, w_ref, o_ref):
    o_ref[...] = _dot(_rms(x_ref[...], g_ref[...]).astype(BF16), w_ref[...].astype(BF16))


def _norm_proj(x, norm_g, w, layer, *, tm=MEM_KV_ROWS):
    m, d = x.shape
    n = w.shape[2]
    return pl.pallas_call(
        _norm_proj_kernel,
        grid=(m // tm,),
        in_specs=[
            pl.BlockSpec((tm, d), lambda i: (i, 0)),
            pl.BlockSpec((1, d), lambda i: (0, 0)),
            pl.BlockSpec((None, d, n), lambda i: (layer, 0, 0)),
        ],
        out_specs=pl.BlockSpec((tm, n), lambda i: (i, 0)),
        out_shape=jax.ShapeDtypeStruct((m, n), F32),
        compiler_params=_params(("parallel",)),
        name="mem_kv",
    )(x, norm_g.reshape(1, d), w)


def _sgu_rows(u_ref, v_ref, lng_ref, lnb_ref, ws_ref, bs_ref):
    t = SGU_CHUNK
    gd = SGU_GROUP_DIM
    row = lax.broadcasted_iota(jnp.int32, (t, t), 0)
    col = lax.broadcasted_iota(jnp.int32, (t, t), 1)
    causal = row >= col
    out = [[None] * SGU_GROUPS for _ in range(u_ref.shape[0] // t)]
    for g in range(SGU_GROUPS):
        w = jnp.where(causal, ws_ref[g], 0.0).astype(BF16)
        bias = bs_ref[:, g:g + 1]
        ln_g = lng_ref[:, g * gd:(g + 1) * gd]
        ln_b = lnb_ref[:, g * gd:(g + 1) * gd]
        for c in range(len(out)):
            rows = slice(c * t, (c + 1) * t)
            cols = slice(g * gd, (g + 1) * gd)
            v = _gelu_tanh(v_ref[rows, cols])
            mu = jnp.mean(v, axis=-1, keepdims=True)
            vc = v - mu
            var = jnp.mean(vc * vc, axis=-1, keepdims=True)
            vn = vc * lax.rsqrt(var + EPS) * ln_g + ln_b
            mixed = _dot(w, vn.astype(BF16)) + bias
            out[c][g] = (_gelu_tanh(u_ref[rows, cols]) * mixed).astype(BF16)
    return jnp.concatenate([jnp.concatenate(chunk, axis=1) for chunk in out], axis=0)


FOX_AUG = 6


def _fox_gate_kernel(f_ref, bf_ref, tril_ref, pq_ref, pk_ref, cq_ref, ck_ref, qa_ref, ka_ref):
    t = LANE
    seq = f_ref.shape[0]
    carry = jnp.zeros((1, t), F32)
    for b in range(seq // t):
        rows = slice(b * t, (b + 1) * t)
        lf = _log_sigmoid(f_ref[rows, :] + bf_ref[...])
        cum = _dot_01(tril_ref[...], lf) + carry
        carry = cum[t - 1:t, :]
        parts = jnp.concatenate(_split3(cum), axis=1)
        qa_ref[rows, :] = (_dot(parts, pq_ref[...]) + cq_ref[...]).astype(BF16)
        ka_ref[rows, :] = (_dot(parts, pk_ref[...]) + ck_ref[...]).astype(BF16)


def _tril3(n):
    tril = jnp.tril(jnp.ones((n, n), BF16))
    return jnp.concatenate([tril, tril, tril], axis=1)


def _fox_placement():
    src = jnp.arange(3 * LANE)[:, None]
    dst = jnp.arange(LANE)[None, :]
    p, h = src // LANE, src % LANE
    valid = h < FOX_HEADS
    pq = jnp.where(valid & (dst == h * FOX_AUG + p), 1.0, 0.0).astype(BF16)
    pk = jnp.where(valid & (dst == h * FOX_AUG + 3 + p), -1.0, 0.0).astype(BF16)
    used = dst < FOX_HEADS * FOX_AUG
    cq = jnp.where(used & (dst % FOX_AUG >= 3), 1.0, 0.0).astype(F32)
    ck = jnp.where(used & (dst % FOX_AUG < 3), 1.0, 0.0).astype(F32)
    return pq, pk, cq, ck


def _fox_gate(small, b_f, batch, seq):
    bf = jnp.zeros((1, LANE), F32).at[0, SMALL_FF:SMALL_FF + FOX_HEADS].set(b_f)
    w = LANE
    aug = jax.ShapeDtypeStruct((batch * seq, w), BF16)

    def const(shape):
        return pl.BlockSpec(shape, lambda b: (0, 0))

    return pl.pallas_call(
        _fox_gate_kernel,
        grid=(batch,),
        in_specs=[
            pl.BlockSpec((seq, LANE), lambda b: (b, 0)),
            const((1, LANE)), const((LANE, 3 * LANE)),
            const((3 * LANE, w)), const((3 * LANE, w)), const((1, w)), const((1, w)),
        ],
        out_specs=[pl.BlockSpec((seq, w), lambda b: (b, 0))] * 2,
        out_shape=[aug, aug],
        compiler_params=_params(("parallel",)),
        name="fox_gate",
    )(small, bf, _tril3(LANE), *_fox_placement())


def _fox_kernel(q_ref, k_ref, v_ref, qa_ref, ka_ref, o_ref, *, t, heads):
    seq = q_ref.shape[0]
    dh = FOX_HEAD_DIM
    lane = lax.broadcasted_iota(jnp.int32, (t, LANE), 1)
    ones_col = jnp.where(lane == 0, 1.0, 0.0).astype(BF16)
    row = lax.broadcasted_iota(jnp.int32, (t, t), 0)
    col = lax.broadcasted_iota(jnp.int32, (t, t), 1)
    causal = row >= col

    for g in range(heads):
        head = pl.program_id(1) * heads + g
        own = (lane >= head * FOX_AUG) & (lane < (head + 1) * FOX_AUG)
        cols = slice(g * dh, (g + 1) * dh)

        def k_block(j):
            rows = slice(j * t, (j + 1) * t)
            ka = jnp.where(own, ka_ref[rows, :], jnp.zeros((), BF16))
            return (jnp.concatenate([k_ref[rows, cols], ka], axis=1),
                    jnp.concatenate([v_ref[rows, cols], ones_col], axis=1))

        for i in range(seq // t):
            rows = slice(i * t, (i + 1) * t)
            qc = jnp.concatenate([q_ref[rows, cols], qa_ref[rows, :]], axis=1)
            kc, vc = k_block(i)
            s = jnp.where(causal, _dot_nt(qc, kc), NEG_BIG)
            m = jnp.max(s, axis=1, keepdims=True)
            acc = _dot(jnp.exp(s - m).astype(BF16), vc)
            for j in range(i):
                kc, vc = k_block(j)
                s = _dot_nt(qc, kc)
                m_new = jnp.maximum(m, jnp.max(s, axis=1, keepdims=True))
                acc = jnp.exp(m - m_new) * acc + _dot(jnp.exp(s - m_new).astype(BF16), vc)
                m = m_new
            o_ref[rows, cols] = (acc[:, :dh] / acc[:, dh:dh + 1]).astype(o_ref.dtype)


def _fox(zb, qa, ka, batch, seq, *, t=FOX_TILE, heads=FOX_HEADS_PER_STEP):
    m = zb.shape[0]
    w = heads * FOX_HEAD_DIM

    def group(seg):
        return pl.BlockSpec((seq, w), lambda b, h: (b, seg * LANE // w + h))

    aug = pl.BlockSpec((seq, LANE), lambda b, h: (b, 0))
    return pl.pallas_call(
        functools.partial(_fox_kernel, t=t, heads=heads),
        grid=(batch, FOX_HEADS // heads),
        in_specs=[group(ZB_FQ), group(ZB_FK), group(ZB_FV), aug, aug],
        out_specs=group(0),
        out_shape=jax.ShapeDtypeStruct((m, FOX_WIDTH), BF16),
        compiler_params=_params(("parallel", "parallel")),
        name="fox",
    )(zb, zb, zb, qa, ka)


GLA_BLOCK = 4 * GLA_CHUNK


def _gla_decay_kernel(a_ref, wg_ref, bg_ref, cm_ref, o_ref, *, blocks):
    t = GLA_BLOCK
    for b in range(blocks):
        rows = slice(b * t, (b + 1) * t)
        gl = _dot(a_ref[rows, :].astype(BF16), wg_ref[...]) + bg_ref[...]
        g = _log_sigmoid(gl) / GLA_GATE_TAU
        o_ref[rows, :] = _dot_01(cm_ref[...], g)


def _gla_decay(small, w_gate_emb, b_gate, *, blocks=GLA_DECAY_BLOCKS_PER_STEP):
    m = small.shape[0]
    t = GLA_BLOCK
    rows = blocks * t
    idx = jnp.arange(t)
    same = (idx[:, None] // GLA_CHUNK) == (idx[None, :] // GLA_CHUNK)
    cm = jnp.where(same & (idx[:, None] >= idx[None, :]), 1.0, 0.0).astype(BF16)
    cm3 = jnp.concatenate([cm, cm, cm], axis=1)
    return pl.pallas_call(
        functools.partial(_gla_decay_kernel, blocks=blocks),
        grid=(m // rows,),
        in_specs=[
            pl.BlockSpec((rows, LANE), lambda i: (i, 0)),
            pl.BlockSpec((LANE, GLA_KW), lambda i: (0, 0)),
            pl.BlockSpec((1, GLA_KW), lambda i: (0, 0)),
            pl.BlockSpec((t, 3 * t), lambda i: (0, 0)),
        ],
        out_specs=pl.BlockSpec((rows, GLA_KW), lambda i: (i, 0)),
        out_shape=jax.ShapeDtypeStruct((m, GLA_KW), F32),
        compiler_params=_params(("parallel",)),
        name="gla_decay",
    )(small, w_gate_emb, b_gate.reshape(1, GLA_KW), cm3)


def _gla_kernel(q_ref, k_ref, v_ref, r_ref, bc_ref, on_ref, o_ref, *, heads):
    for g in range(heads):
        _gla_head(q_ref, k_ref, v_ref, r_ref, bc_ref, on_ref, o_ref,
                  slice(g * GLA_DK, (g + 1) * GLA_DK), slice(g * GLA_DV, (g + 1) * GLA_DV))


def _gla_head(q_ref, k_ref, v_ref, r_ref, bc_ref, on_ref, o_ref, kcols, vcols):
    c = GLA_CHUNK
    t = GLA_BLOCK
    dk = GLA_DK
    seq = q_ref.shape[0]
    st = jnp.zeros((GLA_DV, dk), F32)
    row = lax.broadcasted_iota(jnp.int32, (t, t), 0)
    col = lax.broadcasted_iota(jnp.int32, (t, t), 1)
    intra = (row >= col) & ((row // c) == (col // c))
    o_gain = on_ref[:, vcols]

    for n in range(seq // t):
        rows = slice(n * t, (n + 1) * t)
        bc = bc_ref[rows, kcols]
        qb = q_ref[rows, kcols] * (dk ** -0.5)
        kb = k_ref[rows, kcols]
        vb = v_ref[rows, vcols]
        chunks = [slice(i * c, (i + 1) * c) for i in range(t // c)]
        b_mid = jnp.concatenate(
            [jnp.broadcast_to(bc[s.start + c // 2:s.start + c // 2 + 1, :], (c, dk)) for s in chunks], axis=0)
        b_last = jnp.concatenate(
            [jnp.broadcast_to(bc[s.stop - 1:s.stop, :], (c, dk)) for s in chunks], axis=0)
        qd = (qb * jnp.exp(bc - b_mid)).astype(BF16)
        kd = (kb * jnp.exp(b_mid - bc)).astype(BF16)
        ku = (kb * jnp.exp(b_last - bc)).astype(BF16)
        qi = (qb * jnp.exp(bc)).astype(BF16)
        att = jnp.where(intra, _dot_nt(qd, kd), 0.0)
        o = _dot(att.astype(BF16), vb)
        inter = []
        for s in chunks:
            inter.append(_dot_nt(qi[s, :], st.astype(BF16)))
            st = st * jnp.exp(bc[s.stop - 1:s.stop, :]) + _dot_tn(vb[s, :], ku[s, :])
        o = _rms(o + jnp.concatenate(inter, axis=0), o_gain)
        o_ref[rows, vcols] = (o * _silu(r_ref[rows, vcols])).astype(o_ref.dtype)


def _gla(zf, zb, bc, o_norm, batch, seq, *, heads=GLA_HEADS_PER_STEP):
    m = zf.shape[0]
    dk, dv = heads * GLA_DK, heads * GLA_DV
    return pl.pallas_call(
        functools.partial(_gla_kernel, heads=heads),
        grid=(batch, GLA_HEADS // heads),
        in_specs=[
            pl.BlockSpec((seq, dk), lambda b, h: (b, ZF_GQ * LANE // dk + h)),
            pl.BlockSpec((seq, dk), lambda b, h: (b, ZF_GK * LANE // dk + h)),
            pl.BlockSpec((seq, dv), lambda b, h: (b, ZB_GV * LANE // dv + h)),
            pl.BlockSpec((seq, dv), lambda b, h: (b, ZF_GR * LANE // dv + h)),
            pl.BlockSpec((seq, dk), lambda b, h: (b, h)),
            pl.BlockSpec((1, dv), lambda b, h: (0, h)),
        ],
        out_specs=pl.BlockSpec((seq, dv), lambda b, h: (b, h)),
        out_shape=jax.ShapeDtypeStruct((m, GLA_VW), BF16),
        compiler_params=_params(("parallel", "parallel")),
        name="gla",
    )(zf, zf, zb, zf, bc, o_norm.reshape(1, GLA_VW))


def _merge_kernel(u_ref, v_ref, lng_ref, lnb_ref, ws_ref, bs_ref, b_ref, c_ref,
                  ga_ref, gb_ref, gc_ref, wa_ref, wb_ref, wc_ref, wo_ref, h_ref, o_ref):
    ya = _sgu_rows(u_ref, v_ref, lng_ref, lnb_ref, ws_ref, bs_ref)
    mix = (jax.nn.sigmoid(ga_ref[...]) * _dot(ya, wa_ref[...])
           + jax.nn.sigmoid(gb_ref[...]) * _dot(b_ref[...], wb_ref[...])
           + jax.nn.sigmoid(gc_ref[...]) * _dot(c_ref[...], wc_ref[...]))
    o_ref[...] = h_ref[...] + _dot(mix.astype(BF16), wo_ref[...])


def _merge(sgu_params, yb, yc, zf, wa, wb, wc, wo, h, layer, *, tm=MERGE_ROWS):
    m, d = h.shape
    kw = yb.shape[1]
    g0 = ZF_GATES * LANE // d
    ln_g, ln_b, w_s, b_s = sgu_params
    branch = pl.BlockSpec((tm, kw), lambda i: (i, 0))
    resident = dict(pipeline_mode=pl.Buffered(1))
    wspec = pl.BlockSpec((None, kw, d), lambda i: (layer, 0, 0), **resident)
    return pl.pallas_call(
        _merge_kernel,
        grid=(m // tm,),
        in_specs=[
            pl.BlockSpec((tm, kw), lambda i: (i, ZF_SU * LANE // kw)),
            pl.BlockSpec((tm, kw), lambda i: (i, ZF_SV * LANE // kw)),
            pl.BlockSpec((1, kw), lambda i: (0, 0)),
            pl.BlockSpec((1, kw), lambda i: (0, 0)),
            pl.BlockSpec((SGU_GROUPS, SGU_CHUNK, SGU_CHUNK), lambda i: (0, 0, 0)),
            pl.BlockSpec((SGU_CHUNK, SGU_GROUPS), lambda i: (0, 0)),
            branch, branch,
            pl.BlockSpec((tm, d), lambda i: (i, g0)),
            pl.BlockSpec((tm, d), lambda i: (i, g0 + 1)),
            pl.BlockSpec((tm, d), lambda i: (i, g0 + 2)),
            wspec, wspec, wspec,
            pl.BlockSpec((None, d, d), lambda i: (layer, 0, 0), **resident),
            pl.BlockSpec((tm, d), lambda i: (i, 0)),
        ],
        out_specs=pl.BlockSpec((tm, d), lambda i: (i, 0)),
        out_shape=jax.ShapeDtypeStruct((m, d), F32),
        compiler_params=_params(("parallel",)),
        name="merge",
    )(zf, zf, ln_g.reshape(1, kw), ln_b.reshape(1, kw), w_s, jnp.swapaxes(b_s, 0, 1),
      yb, yc, zf, zf, zf, wa, wb, wc, wo, h)


def _xattn_kernel(h_ref, g_ref, wq_ref, kv_ref, wo_ref, o_ref):
    dh = XA_HEAD_DIM
    x = h_ref[...]
    n = _rms(x, g_ref[...]).astype(BF16)
    q = _dot(n, wq_ref[...].astype(BF16)) * (dh ** -0.5)
    outs = []
    for hd in range(XA_HEADS):
        qh = q[:, hd * dh:(hd + 1) * dh].astype(BF16)
        kh = kv_ref[:, hd * dh:(hd + 1) * dh].astype(BF16)
        vh = kv_ref[:, XA_WIDTH + hd * dh:XA_WIDTH + (hd + 1) * dh].astype(BF16)
        s = _dot_nt(qh, kh)
        e = jnp.exp(s - jnp.max(s, axis=-1, keepdims=True))
        p = e / jnp.sum(e, axis=-1, keepdims=True)
        outs.append(_dot(p.astype(BF16), vh))
    o = jnp.concatenate(outs, axis=1).astype(BF16)
    o_ref[...] = x + _dot(o, wo_ref[...].astype(BF16))


def _xattn(h, norm_g, wq, kv, wo, layer, seq, mem_len, *, tm=XATTN_ROWS):
    m, d = h.shape
    per_batch = seq // tm
    return pl.pallas_call(
        _xattn_kernel,
        grid=(m // tm,),
        in_specs=[
            pl.BlockSpec((tm, d), lambda i: (i, 0)),
            pl.BlockSpec((1, d), lambda i: (0, 0)),
            pl.BlockSpec((None, d, XA_WIDTH), lambda i: (layer, 0, 0)),
            pl.BlockSpec((mem_len, 2 * XA_WIDTH), lambda i: (i // per_batch, 0)),
            pl.BlockSpec((None, XA_WIDTH, d), lambda i: (layer, 0, 0)),
        ],
        out_specs=pl.BlockSpec((tm, d), lambda i: (i, 0)),
        out_shape=jax.ShapeDtypeStruct((m, d), F32),
        compiler_params=_params(("parallel",)),
        name="xattn",
    )(h, norm_g.reshape(1, d), wq, kv, wo)


CAST_BLOCK_BYTES = 6 * 1024 * 1024


def _cast_kernel(x_ref, o_ref):
    o_ref[...] = x_ref[...].astype(o_ref.dtype)


def _to_bf16(w, cols=None):
    lead = w.shape[:-1]
    c = w.shape[-1] if cols is None else cols
    w2 = w.reshape(-1, w.shape[-1])
    r = w2.shape[0]
    rows = 16
    while rows * 2 * c * 4 <= CAST_BLOCK_BYTES and r % (rows * 2) == 0:
        rows *= 2
    spec = pl.BlockSpec((rows, c), lambda i: (i, 0))
    out = pl.pallas_call(
        _cast_kernel,
        grid=(r // rows,),
        in_specs=[spec],
        out_specs=spec,
        out_shape=jax.ShapeDtypeStruct((r, c), BF16),
        compiler_params=_params(("parallel",)),
        name="cast",
    )(w2)
    return out.reshape(*lead, c)

W_FQ = 2 * SGU_WIDTH
W_FF = W_FQ + 3 * FOX_WIDTH
W_GQ = W_FF + FOX_HEADS
W_GV = W_GQ + 2 * GLA_KW
W_GA = W_GV + GLA_VW
W_GR = W_GA + GLA_GATE_RANK
W_END = W_GR + GLA_VW + 3 * D_MODEL


REPACK_CHUNK = 1024


def _repack_kernel(wt_ref, wf_ref, wb_ref, ws_ref):
    def move(dst_ref, dst, src, width):
        for off in range(0, width, REPACK_CHUNK):
            n = min(REPACK_CHUNK, width - off)
            dst_ref[:, dst + off:dst + off + n] = wt_ref[src + off:src + off + n, :].T.astype(BF16)

    move(wf_ref, 0, 0, W_FQ)
    move(wf_ref, W_FQ, W_GQ, 2 * GLA_KW)
    move(wf_ref, W_FQ + 2 * GLA_KW, W_GR, W_END - W_GR)
    move(wb_ref, 0, W_FQ, 3 * FOX_WIDTH)
    move(wb_ref, 3 * FOX_WIDTH, W_GV, GLA_VW)
    lane = lax.broadcasted_iota(jnp.int32, (wt_ref.shape[1], LANE), 1)
    ff = wt_ref[W_FF - SMALL_FF:W_FF - SMALL_FF + LANE, :].T
    ga = wt_ref[W_GA - SMALL_GA:W_GA - SMALL_GA + LANE, :].T
    side = jnp.where(lane < SMALL_FF + FOX_HEADS, ff,
                     jnp.where(lane < SMALL_GA + GLA_GATE_RANK, ga, 0.0))
    ws_ref[...] = side.astype(BF16)


def _repack_w_in(w, *, kb=REPACK_K_ROWS):
    nl, d, n = w.shape
    wt = jnp.swapaxes(w, 1, 2)

    def spec(width):
        return pl.BlockSpec((None, kb, width), lambda l, i: (l, i, 0))

    return pl.pallas_call(
        _repack_kernel,
        grid=(nl, d // kb),
        in_specs=[pl.BlockSpec((None, n, kb), lambda l, i: (l, 0, i))],
        out_specs=[spec(ZF_WIDTH), spec(ZB_WIDTH), spec(LANE)],
        out_shape=[jax.ShapeDtypeStruct((nl, d, width), BF16) for width in (ZF_WIDTH, ZB_WIDTH, LANE)],
        compiler_params=_params(("parallel", "parallel")),
        name="repack",
    )(wt)


def _embed_w_gate(w_gate):
    emb = jnp.zeros((LANE, GLA_KW), BF16)
    return emb.at[SMALL_GA:SMALL_GA + GLA_GATE_RANK, :].set(w_gate.astype(BF16))


def kernel(x, mem, ffn1_norm, ffn1_w_in, ffn1_w_out, mix_norm, w_in, sgu_ln_g, sgu_ln_b, sgu_w_s, sgu_b_s, fox_b_f, gla_w_gate, gla_b_gate, gla_o_norm, w_branch_a, w_branch_b, w_branch_c, w_out, xa_norm, mem_norm, xa_w_q, xa_w_kv, xa_w_o, ffn2_norm, ffn2_w_in, ffn2_w_out, final_norm):
    batch, seq, d = x.shape
    mem_len = mem.shape[1]
    h = x.reshape(batch * seq, d)
    mem2 = mem.reshape(batch * mem_len, d)
    ones_f = jnp.ones((1, ZF_WIDTH), F32)
    scale_b = jnp.ones((1, ZB_WIDTH), F32).at[:, ZB_FQ * LANE:ZB_FQ * LANE + FOX_WIDTH].set(
        FOX_HEAD_DIM ** -0.5)
    f1_gate, f2_gate = _to_bf16(ffn1_w_in, D_FF), _to_bf16(ffn2_w_in, D_FF)
    wa, wb, wc, wo = (_to_bf16(w) for w in (w_branch_a, w_branch_b, w_branch_c, w_out))
    w_f, w_b, w_small = _repack_w_in(w_in)
    for l in range(DEPTH):
        h = _ffn(h, ffn1_norm[l], f1_gate, ffn1_w_in, ffn1_w_out, l)

        zf, small = _proj(h, mix_norm[l], w_f, l, ones_f, F32, w_small)
        zb = _proj(h, mix_norm[l], w_b, l, scale_b, BF16)
        qa, ka = _fox_gate(small, fox_b_f[l], batch, seq)
        yb = _fox(zb, qa, ka, batch, seq)
        bc = _gla_decay(small, _embed_w_gate(gla_w_gate[l]), gla_b_gate[l])
        yc = _gla(zf, zb, bc, gla_o_norm[l], batch, seq)
        sgu_params = (sgu_ln_g[l], sgu_ln_b[l], sgu_w_s[l], sgu_b_s[l])
        h = _merge(sgu_params, yb, yc, zf, wa, wb, wc, wo, h, l)

        kv = _norm_proj(mem2, mem_norm[l], xa_w_kv, l)
        h = _xattn(h, xa_norm[l], xa_w_q, kv, xa_w_o, l, seq, mem_len)

        h = _ffn(h, ffn2_norm[l], f2_gate, ffn2_w_in, ffn2_w_out, l,
                 final_norm if l == DEPTH - 1 else None)
    return h.reshape(batch, seq, d)
```

```python
import functools

import jax
import jax.numpy as jnp
from jax import lax
from jax.experimental import pallas as pl
from jax.experimental.pallas import tpu as pltpu

F32 = jnp.float32
BF16 = jnp.bfloat16

D_MODEL = 2048
DEPTH = 2
D_FF = 5632
EPS = 1e-6

SGU_GROUPS = 4
SGU_GROUP_DIM = 256
SGU_WIDTH = 1024
SGU_CHUNK = 128

FOX_HEADS = 8
FOX_HEAD_DIM = 128
FOX_WIDTH = 1024

GLA_HEADS = 4
GLA_DK = 128
GLA_DV = 256
GLA_KW = 512
GLA_VW = 1024
GLA_GATE_RANK = 16
GLA_GATE_TAU = 16.0
GLA_CHUNK = 64

XA_HEADS = 4
XA_HEAD_DIM = 128
XA_WIDTH = 512

LANE = 128

ZF_SU, ZF_SV, ZF_GQ, ZF_GK, ZF_GR, ZF_GATES = 0, 8, 16, 20, 24, 32
ZF_WIDTH = 80 * LANE
ZB_FQ, ZB_FK, ZB_FV, ZB_GV = 0, 8, 16, 24
ZB_WIDTH = 32 * LANE
SMALL_FF = 0
SMALL_GA = 8

NEG_BIG = -1e30
VMEM_LIMIT = 63 * 1024 * 1024

FFN_ROWS, FFN_HIDDEN_TILE = 1024, 512
PROJ_ROWS, PROJ_COLS = 1024, 2048
MEM_KV_ROWS = 1024
FOX_TILE, FOX_HEADS_PER_STEP = 512, 2
GLA_DECAY_BLOCKS_PER_STEP = 4
GLA_HEADS_PER_STEP = 2
MERGE_ROWS = 256
XATTN_ROWS = 1024
REPACK_K_ROWS = 256


def _params(sem):
    return pltpu.CompilerParams(dimension_semantics=sem, vmem_limit_bytes=VMEM_LIMIT)


def _rms(x, g):
    return x * lax.rsqrt(jnp.mean(x * x, axis=-1, keepdims=True) + EPS) * g


def _log_sigmoid(x):
    return jnp.minimum(x, 0.0) - jnp.log1p(jnp.exp(-jnp.abs(x)))


def _gelu_tanh(x):
    c = 0.7978845608028654
    return x * (0.5 * (1.0 + jnp.tanh(c * (x + 0.044715 * (x * x * x)))))


def _silu(x):
    return x * jax.nn.sigmoid(x)


def _dot(a, b):
    return jnp.dot(a, b, preferred_element_type=F32)


def _dot_nt(a, b):
    return lax.dot_general(a, b, (((1,), (1,)), ((), ())), preferred_element_type=F32)


def _dot_tn(a, b):
    return lax.dot_general(a, b, (((0,), (0,)), ((), ())), preferred_element_type=F32)


def _split3(x):
    hi = x.astype(BF16)
    r = x - hi.astype(F32)
    lo = r.astype(BF16)
    lo2 = (r - lo.astype(F32)).astype(BF16)
    return hi, lo, lo2


def _dot_01(mat3, x):
    return _dot(mat3, jnp.concatenate(_split3(x), axis=0))


def _ffn_kernel(*refs, final):
    if final:
        x_ref, g_ref, wg_ref, wu_ref, wo_ref, fg_ref, o_ref, xn_ref = refs
    else:
        x_ref, g_ref, wg_ref, wu_ref, wo_ref, o_ref, xn_ref = refs
    j = pl.program_id(1)

    def tile(xn):
        gate = _dot(xn, wg_ref[...])
        up = _dot(xn, wu_ref[...].astype(BF16))
        act = (_silu(gate) * up * 0.5).astype(BF16)
        return _dot(act, wo_ref[...].astype(BF16))

    @pl.when(j == 0)
    def _():
        x = x_ref[...]
        xn = _rms(x, g_ref[...]).astype(BF16)
        xn_ref[...] = xn
        o_ref[...] = x + tile(xn)

    @pl.when(j > 0)
    def _():
        o_ref[...] += tile(xn_ref[...])

    if final:
        @pl.when(j == pl.num_programs(1) - 1)
        def _():
            o_ref[...] = _rms(o_ref[...], fg_ref[...])


def _ffn(h, norm_g, w_gate, w_in, w_out, layer, final_g=None, *, tm=FFN_ROWS, tf=FFN_HIDDEN_TILE):
    m, d = h.shape
    f = w_out.shape[1]
    nt = f // tf
    final = final_g is not None
    vec = pl.BlockSpec((1, d), lambda i, j: (0, 0))
    rows = pl.BlockSpec((tm, d), lambda i, j: (i, 0))
    in_specs = [
        rows, vec,
        pl.BlockSpec((None, d, tf), lambda i, j: (layer, 0, j)),
        pl.BlockSpec((None, d, tf), lambda i, j: (layer, 0, j + nt)),
        pl.BlockSpec((None, tf, d), lambda i, j: (layer, j, 0)),
    ]
    args = [h, norm_g.reshape(1, d), w_gate, w_in, w_out]
    if final:
        in_specs.append(vec)
        args.append(final_g.reshape(1, d))
    return pl.pallas_call(
        functools.partial(_ffn_kernel, final=final),
        grid=(m // tm, nt),
        in_specs=in_specs,
        out_specs=rows,
        out_shape=jax.ShapeDtypeStruct((m, d), F32),
        scratch_shapes=[pltpu.VMEM((tm, d), BF16)],
        compiler_params=_params(("parallel", "arbitrary")),
        name="ffn",
    )(*args)


def _proj_kernel(*refs, small):
    if small:
        x_ref, g_ref, w_ref, cs_ref, ws_ref, o_ref, os_ref, xn_ref = refs
    else:
        x_ref, g_ref, w_ref, cs_ref, o_ref, xn_ref = refs
    j = pl.program_id(1)

    def project(xn):
        o_ref[...] = (_dot(xn, w_ref[...]) * cs_ref[...]).astype(o_ref.dtype)

    @pl.when(j == 0)
    def _():
        xn = _rms(x_ref[...], g_ref[...]).astype(BF16)
        xn_ref[...] = xn
        if small:
            os_ref[...] = _dot(xn, ws_ref[...])
        project(xn)

    @pl.when(j > 0)
    def _():
        project(xn_ref[...])


def _proj(x, norm_g, w, layer, col_scale, out_dtype, w_small=None, *, tm=PROJ_ROWS, tn=PROJ_COLS):
    m, d = x.shape
    n = w.shape[2]
    small = w_small is not None
    in_specs = [
        pl.BlockSpec((tm, d), lambda i, j: (i, 0)),
        pl.BlockSpec((1, d), lambda i, j: (0, 0)),
        pl.BlockSpec((None, d, tn), lambda i, j: (layer, 0, j)),
        pl.BlockSpec((1, tn), lambda i, j: (0, j)),
    ]
    args = [x, norm_g.reshape(1, d), w, col_scale]
    out_specs = [pl.BlockSpec((tm, tn), lambda i, j: (i, j))]
    out_shape = [jax.ShapeDtypeStruct((m, n), out_dtype)]
    if small:
        in_specs.append(pl.BlockSpec((None, d, LANE), lambda i, j: (layer, 0, 0)))
        args.append(w_small)
        out_specs.append(pl.BlockSpec((tm, LANE), lambda i, j: (i, 0)))
        out_shape.append(jax.ShapeDtypeStruct((m, LANE), F32))
    out = pl.pallas_call(
        functools.partial(_proj_kernel, small=small),
        grid=(m // tm, n // tn),
        in_specs=in_specs,
        out_specs=out_specs,
        out_shape=out_shape,
        scratch_shapes=[pltpu.VMEM((tm, d), BF16)],
        compiler_params=_params(("parallel", "arbitrary")),
        name="proj",
    )(*args)
    return out if small else out[0]


def _norm_proj_kernel(x_ref, g_ref, w_ref, o_ref):
    o_ref[...] = _dot(_rms(x_ref[...], g_ref[...]).astype(BF16), w_ref[...].astype(BF16))


def _norm_proj(x, norm_g, w, layer, *, tm=MEM_KV_ROWS):
    m, d = x.shape
    n = w.shape[2]
    return pl.pallas_call(
        _norm_proj_kernel,
        grid=(m // tm,),
        in_specs=[
            pl.BlockSpec((tm, d), lambda i: (i, 0)),
            pl.BlockSpec((1, d), lambda i: (0, 0)),
            pl.BlockSpec((None, d, n), lambda i: (layer, 0, 0)),
        ],
        out_specs=pl.BlockSpec((tm, n), lambda i: (i, 0)),
        out_shape=jax.ShapeDtypeStruct((m, n), F32),
        compiler_params=_params(("parallel",)),
        name="mem_kv",
    )(x, norm_g.reshape(1, d), w)


def _sgu_kernel(u_ref, v_ref, lng_ref, lnb_ref, ws_ref, bs_ref, o_ref, *, chunks):
    t = SGU_CHUNK
    gd = SGU_GROUP_DIM
    row = lax.broadcasted_iota(jnp.int32, (t, t), 0)
    col = lax.broadcasted_iota(jnp.int32, (t, t), 1)
    causal = row >= col
    for g in range(SGU_GROUPS):
        w = jnp.where(causal, ws_ref[g], 0.0).astype(BF16)
        bias = bs_ref[:, g:g + 1]
        ln_g = lng_ref[:, g * gd:(g + 1) * gd]
        ln_b = lnb_ref[:, g * gd:(g + 1) * gd]
        for c in range(chunks):
            rows = slice(c * t, (c + 1) * t)
            cols = slice(g * gd, (g + 1) * gd)
            v = _gelu_tanh(v_ref[rows, cols])
            mu = jnp.mean(v, axis=-1, keepdims=True)
            vc = v - mu
            var = jnp.mean(vc * vc, axis=-1, keepdims=True)
            vn = vc * lax.rsqrt(var + EPS) * ln_g + ln_b
            mixed = _dot(w, vn.astype(BF16)) + bias
            o_ref[rows, cols] = (_gelu_tanh(u_ref[rows, cols]) * mixed).astype(o_ref.dtype)


FOX_AUG = 6


def _fox_gate_kernel(f_ref, bf_ref, tril_ref, pq_ref, pk_ref, cq_ref, ck_ref, qa_ref, ka_ref):
    t = LANE
    seq = f_ref.shape[0]
    carry = jnp.zeros((1, t), F32)
    for b in range(seq // t):
        rows = slice(b * t, (b + 1) * t)
        lf = _log_sigmoid(f_ref[rows, :] + bf_ref[...])
        cum = _dot_01(tril_ref[...], lf) + carry
        carry = cum[t - 1:t, :]
        parts = jnp.concatenate(_split3(cum), axis=1)
        qa_ref[rows, :] = (_dot(parts, pq_ref[...]) + cq_ref[...]).astype(BF16)
        ka_ref[rows, :] = (_dot(parts, pk_ref[...]) + ck_ref[...]).astype(BF16)


def _tril3(n):
    tril = jnp.tril(jnp.ones((n, n), BF16))
    return jnp.concatenate([tril, tril, tril], axis=1)


def _fox_placement():
    src = jnp.arange(3 * LANE)[:, None]
    dst = jnp.arange(LANE)[None, :]
    p, h = src // LANE, src % LANE
    valid = h < FOX_HEADS
    pq = jnp.where(valid & (dst == h * FOX_AUG + p), 1.0, 0.0).astype(BF16)
    pk = jnp.where(valid & (dst == h * FOX_AUG + 3 + p), -1.0, 0.0).astype(BF16)
    used = dst < FOX_HEADS * FOX_AUG
    cq = jnp.where(used & (dst % FOX_AUG >= 3), 1.0, 0.0).astype(F32)
    ck = jnp.where(used & (dst % FOX_AUG < 3), 1.0, 0.0).astype(F32)
    return pq, pk, cq, ck


def _fox_gate(small, b_f, batch, seq):
    bf = jnp.zeros((1, LANE), F32).at[0, SMALL_FF:SMALL_FF + FOX_HEADS].set(b_f)
    w = LANE
    aug = jax.ShapeDtypeStruct((batch * seq, w), BF16)

    def const(shape):
        return pl.BlockSpec(shape, lambda b: (0, 0))

    return pl.pallas_call(
        _fox_gate_kernel,
        grid=(batch,),
        in_specs=[
            pl.BlockSpec((seq, LANE), lambda b: (b, 0)),
            const((1, LANE)), const((LANE, 3 * LANE)),
            const((3 * LANE, w)), const((3 * LANE, w)), const((1, w)), const((1, w)),
        ],
        out_specs=[pl.BlockSpec((seq, w), lambda b: (b, 0))] * 2,
        out_shape=[aug, aug],
        compiler_params=_params(("parallel",)),
        name="fox_gate",
    )(small, bf, _tril3(LANE), *_fox_placement())


def _fox_kernel(q_ref, k_ref, v_ref, qa_ref, ka_ref, o_ref, *, t, heads):
    seq = q_ref.shape[0]
    dh = FOX_HEAD_DIM
    lane = lax.broadcasted_iota(jnp.int32, (t, LANE), 1)
    ones_col = jnp.where(lane == 0, 1.0, 0.0).astype(BF16)
    row = lax.broadcasted_iota(jnp.int32, (t, t), 0)
    col = lax.broadcasted_iota(jnp.int32, (t, t), 1)
    causal = row >= col

    for g in range(heads):
        head = pl.program_id(1) * heads + g
        own = (lane >= head * FOX_AUG) & (lane < (head + 1) * FOX_AUG)
        cols = slice(g * dh, (g + 1) * dh)

        def k_block(j):
            rows = slice(j * t, (j + 1) * t)
            ka = jnp.where(own, ka_ref[rows, :], jnp.zeros((), BF16))
            return (jnp.concatenate([k_ref[rows, cols], ka], axis=1),
                    jnp.concatenate([v_ref[rows, cols], ones_col], axis=1))

        for i in range(seq // t):
            rows = slice(i * t, (i + 1) * t)
            qc = jnp.concatenate([q_ref[rows, cols], qa_ref[rows, :]], axis=1)
            kc, vc = k_block(i)
            s = jnp.where(causal, _dot_nt(qc, kc), NEG_BIG)
            m = jnp.max(s, axis=1, keepdims=True)
            acc = _dot(jnp.exp(s - m).astype(BF16), vc)
            for j in range(i):
                kc, vc = k_block(j)
                s = _dot_nt(qc, kc)
                m_new = jnp.maximum(m, jnp.max(s, axis=1, keepdims=True))
                acc = jnp.exp(m - m_new) * acc + _dot(jnp.exp(s - m_new).astype(BF16), vc)
                m = m_new
            o_ref[rows, cols] = (acc[:, :dh] / acc[:, dh:dh + 1]).astype(o_ref.dtype)


def _fox(zb, qa, ka, batch, seq, *, t=FOX_TILE, heads=FOX_HEADS_PER_STEP):
    m = zb.shape[0]
    w = heads * FOX_HEAD_DIM

    def group(seg):
        return pl.BlockSpec((seq, w), lambda b, h: (b, seg * LANE // w + h))

    aug = pl.BlockSpec((seq, LANE), lambda b, h: (b, 0))
    return pl.pallas_call(
        functools.partial(_fox_kernel, t=t, heads=heads),
        grid=(batch, FOX_HEADS // heads),
        in_specs=[group(ZB_FQ), group(ZB_FK), group(ZB_FV), aug, aug],
        out_specs=group(0),
        out_shape=jax.ShapeDtypeStruct((m, FOX_WIDTH), BF16),
        compiler_params=_params(("parallel", "parallel")),
        name="fox",
    )(zb, zb, zb, qa, ka)


GLA_BLOCK = 4 * GLA_CHUNK


def _gla_decay_kernel(a_ref, wg_ref, bg_ref, cm_ref, o_ref, *, blocks):
    t = GLA_BLOCK
    for b in range(blocks):
        rows = slice(b * t, (b + 1) * t)
        gl = _dot(a_ref[rows, :].astype(BF16), wg_ref[...]) + bg_ref[...]
        g = _log_sigmoid(gl) / GLA_GATE_TAU
        o_ref[rows, :] = _dot_01(cm_ref[...], g)


def _gla_decay(small, w_gate_emb, b_gate, *, blocks=GLA_DECAY_BLOCKS_PER_STEP):
    m = small.shape[0]
    t = GLA_BLOCK
    rows = blocks * t
    idx = jnp.arange(t)
    same = (idx[:, None] // GLA_CHUNK) == (idx[None, :] // GLA_CHUNK)
    cm = jnp.where(same & (idx[:, None] >= idx[None, :]), 1.0, 0.0).astype(BF16)
    cm3 = jnp.concatenate([cm, cm, cm], axis=1)
    return pl.pallas_call(
        functools.partial(_gla_decay_kernel, blocks=blocks),
        grid=(m // rows,),
        in_specs=[
            pl.BlockSpec((rows, LANE), lambda i: (i, 0)),
            pl.BlockSpec((LANE, GLA_KW), lambda i: (0, 0)),
            pl.BlockSpec((1, GLA_KW), lambda i: (0, 0)),
            pl.BlockSpec((t, 3 * t), lambda i: (0, 0)),
        ],
        out_specs=pl.BlockSpec((rows, GLA_KW), lambda i: (i, 0)),
        out_shape=jax.ShapeDtypeStruct((m, GLA_KW), F32),
        compiler_params=_params(("parallel",)),
        name="gla_decay",
    )(small, w_gate_emb, b_gate.reshape(1, GLA_KW), cm3)


def _gla_kernel(q_ref, k_ref, v_ref, r_ref, bc_ref, on_ref, o_ref, *, heads):
    for g in range(heads):
        _gla_head(q_ref, k_ref, v_ref, r_ref, bc_ref, on_ref, o_ref,
                  slice(g * GLA_DK, (g + 1) * GLA_DK), slice(g * GLA_DV, (g + 1) * GLA_DV))


def _gla_head(q_ref, k_ref, v_ref, r_ref, bc_ref, on_ref, o_ref, kcols, vcols):
    c = GLA_CHUNK
    t = GLA_BLOCK
    dk = GLA_DK
    seq = q_ref.shape[0]
    st = jnp.zeros((GLA_DV, dk), F32)
    row = lax.broadcasted_iota(jnp.int32, (t, t), 0)
    col = lax.broadcasted_iota(jnp.int32, (t, t), 1)
    intra = (row >= col) & ((row // c) == (col // c))
    o_gain = on_ref[:, vcols]

    for n in range(seq // t):
        rows = slice(n * t, (n + 1) * t)
        bc = bc_ref[rows, kcols]
        qb = q_ref[rows, kcols] * (dk ** -0.5)
        kb = k_ref[rows, kcols]
        vb = v_ref[rows, vcols]
        chunks = [slice(i * c, (i + 1) * c) for i in range(t // c)]
        b_mid = jnp.concatenate(
            [jnp.broadcast_to(bc[s.start + c // 2:s.start + c // 2 + 1, :], (c, dk)) for s in chunks], axis=0)
        b_last = jnp.concatenate(
            [jnp.broadcast_to(bc[s.stop - 1:s.stop, :], (c, dk)) for s in chunks], axis=0)
        qd = (qb * jnp.exp(bc - b_mid)).astype(BF16)
        kd = (kb * jnp.exp(b_mid - bc)).astype(BF16)
        ku = (kb * jnp.exp(b_last - bc)).astype(BF16)
        qi = (qb * jnp.exp(bc)).astype(BF16)
        att = jnp.where(intra, _dot_nt(qd, kd), 0.0)
        o = _dot(att.astype(BF16), vb)
        inter = []
        for s in chunks:
            inter.append(_dot_nt(qi[s, :], st.astype(BF16)))
            st = st * jnp.exp(bc[s.stop - 1:s.stop, :]) + _dot_tn(vb[s, :], ku[s, :])
        o = _rms(o + jnp.concatenate(inter, axis=0), o_gain)
        o_ref[rows, vcols] = (o * _silu(r_ref[rows, vcols])).astype(o_ref.dtype)


def _gla(zf, zb, bc, o_norm, batch, seq, *, heads=GLA_HEADS_PER_STEP):
    m = zf.shape[0]
    dk, dv = heads * GLA_DK, heads * GLA_DV
    return pl.pallas_call(
        functools.partial(_gla_kernel, heads=heads),
        grid=(batch, GLA_HEADS // heads),
        in_specs=[
            pl.BlockSpec((seq, dk), lambda b, h: (b, ZF_GQ * LANE // dk + h)),
            pl.BlockSpec((seq, dk), lambda b, h: (b, ZF_GK * LANE // dk + h)),
            pl.BlockSpec((seq, dv), lambda b, h: (b, ZB_GV * LANE // dv + h)),
            pl.BlockSpec((seq, dv), lambda b, h: (b, ZF_GR * LANE // dv + h)),
            pl.BlockSpec((seq, dk), lambda b, h: (b, h)),
            pl.BlockSpec((1, dv), lambda b, h: (0, h)),
        ],
        out_specs=pl.BlockSpec((seq, dv), lambda b, h: (b, h)),
        out_shape=jax.ShapeDtypeStruct((m, GLA_VW), BF16),
        compiler_params=_params(("parallel", "parallel")),
        name="gla",
    )(zf, zf, zb, zf, bc, o_norm.reshape(1, GLA_VW))


def _merge_kernel(u0_ref, v0_ref, u_ref, v_ref, lng_ref, lnb_ref, ws_ref, bs_ref, b_ref, c_ref,
                  ga_ref, gb_ref, gc_ref, wa_ref, wb_ref, wc_ref, wo_ref, h_ref, o_ref, ya_ref):
    i = pl.program_id(0)
    chunks = ya_ref.shape[1] // SGU_CHUNK

    @pl.when(i == 0)
    def _():
        _sgu_kernel(u0_ref, v0_ref, lng_ref, lnb_ref, ws_ref, bs_ref, ya_ref.at[0], chunks=chunks)

    slot = i % 2
    mix = (jax.nn.sigmoid(ga_ref[...]) * _dot(ya_ref[slot], wa_ref[...])
           + jax.nn.sigmoid(gb_ref[...]) * _dot(b_ref[...], wb_ref[...])
           + jax.nn.sigmoid(gc_ref[...]) * _dot(c_ref[...], wc_ref[...]))
    o_ref[...] = h_ref[...] + _dot(mix.astype(BF16), wo_ref[...])
    _sgu_kernel(u_ref, v_ref, lng_ref, lnb_ref, ws_ref, bs_ref, ya_ref.at[1 - slot], chunks=chunks)


def _merge(sgu_params, yb, yc, zf, wa, wb, wc, wo, h, layer, *, tm=MERGE_ROWS):
    m, d = h.shape
    kw = yb.shape[1]
    g0 = ZF_GATES * LANE // d
    last = m // tm - 1
    ln_g, ln_b, w_s, b_s = sgu_params
    branch = pl.BlockSpec((tm, kw), lambda i: (i, 0))
    resident = dict(pipeline_mode=pl.Buffered(1))
    wspec = pl.BlockSpec((None, kw, d), lambda i: (layer, 0, 0), **resident)
    return pl.pallas_call(
        _merge_kernel,
        grid=(m // tm,),
        in_specs=[
            pl.BlockSpec((tm, kw), lambda i: (0, ZF_SU * LANE // kw)),
            pl.BlockSpec((tm, kw), lambda i: (0, ZF_SV * LANE // kw)),
            pl.BlockSpec((tm, kw), lambda i: (jnp.minimum(i + 1, last), ZF_SU * LANE // kw)),
            pl.BlockSpec((tm, kw), lambda i: (jnp.minimum(i + 1, last), ZF_SV * LANE // kw)),
            pl.BlockSpec((1, kw), lambda i: (0, 0)),
            pl.BlockSpec((1, kw), lambda i: (0, 0)),
            pl.BlockSpec((SGU_GROUPS, SGU_CHUNK, SGU_CHUNK), lambda i: (0, 0, 0)),
            pl.BlockSpec((SGU_CHUNK, SGU_GROUPS), lambda i: (0, 0)),
            branch, branch,
            pl.BlockSpec((tm, d), lambda i: (i, g0)),
            pl.BlockSpec((tm, d), lambda i: (i, g0 + 1)),
            pl.BlockSpec((tm, d), lambda i: (i, g0 + 2)),
            wspec, wspec, wspec,
            pl.BlockSpec((None, d, d), lambda i: (layer, 0, 0), **resident),
            pl.BlockSpec((tm, d), lambda i: (i, 0)),
        ],
        out_specs=pl.BlockSpec((tm, d), lambda i: (i, 0)),
        out_shape=jax.ShapeDtypeStruct((m, d), F32),
        scratch_shapes=[pltpu.VMEM((2, tm, kw), BF16)],
        compiler_params=_params(("arbitrary",)),
        name="merge",
    )(zf, zf, zf, zf, ln_g.reshape(1, kw), ln_b.reshape(1, kw), w_s, jnp.swapaxes(b_s, 0, 1),
      yb, yc, zf, zf, zf, wa, wb, wc, wo, h)


def _xattn_kernel(h_ref, g_ref, wq_ref, kv_ref, wo_ref, o_ref):
    dh = XA_HEAD_DIM
    x = h_ref[...]
    n = _rms(x, g_ref[...]).astype(BF16)
    q = _dot(n, wq_ref[...].astype(BF16)) * (dh ** -0.5)
    outs = []
    for hd in range(XA_HEADS):
        qh = q[:, hd * dh:(hd + 1) * dh].astype(BF16)
        kh = kv_ref[:, hd * dh:(hd + 1) * dh].astype(BF16)
        vh = kv_ref[:, XA_WIDTH + hd * dh:XA_WIDTH + (hd + 1) * dh].astype(BF16)
        s = _dot_nt(qh, kh)
        e = jnp.exp(s - jnp.max(s, axis=-1, keepdims=True))
        p = e / jnp.sum(e, axis=-1, keepdims=True)
        outs.append(_dot(p.astype(BF16), vh))
    o = jnp.concatenate(outs, axis=1).astype(BF16)
    o_ref[...] = x + _dot(o, wo_ref[...].astype(BF16))


def _xattn(h, norm_g, wq, kv, wo, layer, seq, mem_len, *, tm=XATTN_ROWS):
    m, d = h.shape
    per_batch = seq // tm
    return pl.pallas_call(
        _xattn_kernel,
        grid=(m // tm,),
        in_specs=[
            pl.BlockSpec((tm, d), lambda i: (i, 0)),
            pl.BlockSpec((1, d), lambda i: (0, 0)),
            pl.BlockSpec((None, d, XA_WIDTH), lambda i: (layer, 0, 0)),
            pl.BlockSpec((mem_len, 2 * XA_WIDTH), lambda i: (i // per_batch, 0)),
            pl.BlockSpec((None, XA_WIDTH, d), lambda i: (layer, 0, 0)),
        ],
        out_specs=pl.BlockSpec((tm, d), lambda i: (i, 0)),
        out_shape=jax.ShapeDtypeStruct((m, d), F32),
        compiler_params=_params(("parallel",)),
        name="xattn",
    )(h, norm_g.reshape(1, d), wq, kv, wo)


CAST_BLOCK_BYTES = 6 * 1024 * 1024


def _cast_kernel(x_ref, o_ref):
    o_ref[...] = x_ref[...].astype(o_ref.dtype)


def _to_bf16(w, cols=None):
    lead = w.shape[:-1]
    c = w.shape[-1] if cols is None else cols
    w2 = w.reshape(-1, w.shape[-1])
    r = w2.shape[0]
    rows = 16
    while rows * 2 * c * 4 <= CAST_BLOCK_BYTES and r % (rows * 2) == 0:
        rows *= 2
    spec = pl.BlockSpec((rows, c), lambda i: (i, 0))
    out = pl.pallas_call(
        _cast_kernel,
        grid=(r // rows,),
        in_specs=[spec],
        out_specs=spec,
        out_shape=jax.ShapeDtypeStruct((r, c), BF16),
        compiler_params=_params(("parallel",)),
        name="cast",
    )(w2)
    return out.reshape(*lead, c)

W_FQ = 2 * SGU_WIDTH
W_FF = W_FQ + 3 * FOX_WIDTH
W_GQ = W_FF + FOX_HEADS
W_GV = W_GQ + 2 * GLA_KW
W_GA = W_GV + GLA_VW
W_GR = W_GA + GLA_GATE_RANK
W_END = W_GR + GLA_VW + 3 * D_MODEL


REPACK_CHUNK = 1024


def _repack_kernel(wt_ref, wf_ref, wb_ref, ws_ref):
    def move(dst_ref, dst, src, width):
        for off in range(0, width, REPACK_CHUNK):
            n = min(REPACK_CHUNK, width - off)
            dst_ref[:, dst + off:dst + off + n] = wt_ref[src + off:src + off + n, :].T.astype(BF16)

    move(wf_ref, 0, 0, W_FQ)
    move(wf_ref, W_FQ, W_GQ, 2 * GLA_KW)
    move(wf_ref, W_FQ + 2 * GLA_KW, W_GR, W_END - W_GR)
    move(wb_ref, 0, W_FQ, 3 * FOX_WIDTH)
    move(wb_ref, 3 * FOX_WIDTH, W_GV, GLA_VW)
    lane = lax.broadcasted_iota(jnp.int32, (wt_ref.shape[1], LANE), 1)
    ff = wt_ref[W_FF - SMALL_FF:W_FF - SMALL_FF + LANE, :].T
    ga = wt_ref[W_GA - SMALL_GA:W_GA - SMALL_GA + LANE, :].T
    side = jnp.where(lane < SMALL_FF + FOX_HEADS, ff,
                     jnp.where(lane < SMALL_GA + GLA_GATE_RANK, ga, 0.0))
    ws_ref[...] = side.astype(BF16)


def _repack_w_in(w, *, kb=REPACK_K_ROWS):
    nl, d, n = w.shape
    wt = jnp.swapaxes(w, 1, 2)

    def spec(width):
        return pl.BlockSpec((None, kb, width), lambda l, i: (l, i, 0))

    return pl.pallas_call(
        _repack_kernel,
        grid=(nl, d // kb),
        in_specs=[pl.BlockSpec((None, n, kb), lambda l, i: (l, 0, i))],
        out_specs=[spec(ZF_WIDTH), spec(ZB_WIDTH), spec(LANE)],
        out_shape=[jax.ShapeDtypeStruct((nl, d, width), BF16) for width in (ZF_WIDTH, ZB_WIDTH, LANE)],
        compiler_params=_params(("parallel", "parallel")),
        name="repack",
    )(wt)


def _embed_w_gate(w_gate):
    emb = jnp.zeros((LANE, GLA_KW), BF16)
    return emb.at[SMALL_GA:SMALL_GA + GLA_GATE_RANK, :].set(w_gate.astype(BF16))


def kernel(x, mem, ffn1_norm, ffn1_w_in, ffn1_w_out, mix_norm, w_in, sgu_ln_g, sgu_ln_b, sgu_w_s, sgu_b_s, fox_b_f, gla_w_gate, gla_b_gate, gla_o_norm, w_branch_a, w_branch_b, w_branch_c, w_out, xa_norm, mem_norm, xa_w_q, xa_w_kv, xa_w_o, ffn2_norm, ffn2_w_in, ffn2_w_out, final_norm):
    batch, seq, d = x.shape
    mem_len = mem.shape[1]
    h = x.reshape(batch * seq, d)
    mem2 = mem.reshape(batch * mem_len, d)
    ones_f = jnp.ones((1, ZF_WIDTH), F32)
    scale_b = jnp.ones((1, ZB_WIDTH), F32).at[:, ZB_FQ * LANE:ZB_FQ * LANE + FOX_WIDTH].set(
        FOX_HEAD_DIM ** -0.5)
    f1_gate, f2_gate = _to_bf16(ffn1_w_in, D_FF), _to_bf16(ffn2_w_in, D_FF)
    wa, wb, wc, wo = (_to_bf16(w) for w in (w_branch_a, w_branch_b, w_branch_c, w_out))
    w_f, w_b, w_small = _repack_w_in(w_in)
    for l in range(DEPTH):
        h = _ffn(h, ffn1_norm[l], f1_gate, ffn1_w_in, ffn1_w_out, l)

        zf, small = _proj(h, mix_norm[l], w_f, l, ones_f, F32, w_small)
        zb = _proj(h, mix_norm[l], w_b, l, scale_b, BF16)
        qa, ka = _fox_gate(small, fox_b_f[l], batch, seq)
        yb = _fox(zb, qa, ka, batch, seq)
        bc = _gla_decay(small, _embed_w_gate(gla_w_gate[l]), gla_b_gate[l])
        yc = _gla(zf, zb, bc, gla_o_norm[l], batch, seq)
        sgu_params = (sgu_ln_g[l], sgu_ln_b[l], sgu_w_s[l], sgu_b_s[l])
        h = _merge(sgu_params, yb, yc, zf, wa, wb, wc, wo, h, l)

        kv = _norm_proj(mem2, mem_norm[l], xa_w_kv, l)
        h = _xattn(h, xa_norm[l], xa_w_q, kv, xa_w_o, l, seq, mem_len)

        h = _ffn(h, ffn2_norm[l], f2_gate, ffn2_w_in, ffn2_w_out, l,
                 final_norm if l == DEPTH - 1 else None)
    return h.reshape(batch, seq, d)
```

```python
import functools

import jax
import jax.numpy as jnp
from jax import lax
from jax.experimental import pallas as pl
from jax.experimental.pallas import tpu as pltpu

F32 = jnp.float32
BF16 = jnp.bfloat16

D_MODEL = 2048
DEPTH = 2
D_FF = 5632
EPS = 1e-6

SGU_GROUPS = 4
SGU_GROUP_DIM = 256
SGU_WIDTH = 1024
SGU_CHUNK = 128

FOX_HEADS = 8
FOX_HEAD_DIM = 128
FOX_WIDTH = 1024

GLA_HEADS = 4
GLA_DK = 128
GLA_DV = 256
GLA_KW = 512
GLA_VW = 1024
GLA_GATE_RANK = 16
GLA_GATE_TAU = 16.0
GLA_CHUNK = 64

XA_HEADS = 4
XA_HEAD_DIM = 128
XA_WIDTH = 512

LANE = 128

ZF_SU, ZF_SV, ZF_GQ, ZF_GK, ZF_GR, ZF_GATES = 0, 8, 16, 20, 24, 32
ZF_WIDTH = 80 * LANE
ZB_FQ, ZB_FK, ZB_FV, ZB_GV = 0, 8, 16, 24
ZB_WIDTH = 32 * LANE
SMALL_FF = 0
SMALL_GA = 8

NEG_BIG = -1e30
VMEM_LIMIT = 63 * 1024 * 1024

FFN_ROWS, FFN_HIDDEN_TILE = 1024, 512
PROJ_ROWS, PROJ_COLS = 1024, 2048
MEM_KV_ROWS = 1024
FOX_TILE, FOX_HEADS_PER_STEP = 512, 2
GLA_DECAY_BLOCKS_PER_STEP = 4
GLA_HEADS_PER_STEP = 2
MERGE_ROWS = 256
XATTN_ROWS = 1024
REPACK_K_ROWS = 256


def _params(sem):
    return pltpu.CompilerParams(dimension_semantics=sem, vmem_limit_bytes=VMEM_LIMIT)


def _rms(x, g):
    return x * lax.rsqrt(jnp.mean(x * x, axis=-1, keepdims=True) + EPS) * g


def _log_sigmoid(x):
    return jnp.minimum(x, 0.0) - jnp.log1p(jnp.exp(-jnp.abs(x)))


def _gelu_tanh(x):
    c = 0.7978845608028654
    return x * (0.5 * (1.0 + jnp.tanh(c * (x + 0.044715 * (x * x * x)))))


def _silu(x):
    return x * jax.nn.sigmoid(x)


def _dot(a, b):
    return jnp.dot(a, b, preferred_element_type=F32)


def _dot_nt(a, b):
    return lax.dot_general(a, b, (((1,), (1,)), ((), ())), preferred_element_type=F32)


def _dot_tn(a, b):
    return lax.dot_general(a, b, (((0,), (0,)), ((), ())), preferred_element_type=F32)


def _split3(x):
    hi = x.astype(BF16)
    r = x - hi.astype(F32)
    lo = r.astype(BF16)
    lo2 = (r - lo.astype(F32)).astype(BF16)
    return hi, lo, lo2


def _dot_01(mat3, x):
    return _dot(mat3, jnp.concatenate(_split3(x), axis=0))


def _ffn_kernel(*refs, final):
    if final:
        x_ref, g_ref, wg_ref, wu_ref, wo_ref, fg_ref, o_ref, xn_ref = refs
    else:
        x_ref, g_ref, wg_ref, wu_ref, wo_ref, o_ref, xn_ref = refs
    j = pl.program_id(1)

    def tile(xn):
        gate = _dot(xn, wg_ref[...])
        up = _dot(xn, wu_ref[...].astype(BF16))
        act = (_silu(gate) * up * 0.5).astype(BF16)
        return _dot(act, wo_ref[...].astype(BF16))

    @pl.when(j == 0)
    def _():
        x = x_ref[...]
        xn = _rms(x, g_ref[...]).astype(BF16)
        xn_ref[...] = xn
        o_ref[...] = x + tile(xn)

    @pl.when(j > 0)
    def _():
        o_ref[...] += tile(xn_ref[...])

    if final:
        @pl.when(j == pl.num_programs(1) - 1)
        def _():
            o_ref[...] = _rms(o_ref[...], fg_ref[...])


def _ffn(h, norm_g, w_gate, w_in, w_out, layer, final_g=None, *, tm=FFN_ROWS, tf=FFN_HIDDEN_TILE):
    m, d = h.shape
    f = w_out.shape[1]
    nt = f // tf
    final = final_g is not None
    vec = pl.BlockSpec((1, d), lambda i, j: (0, 0))
    rows = pl.BlockSpec((tm, d), lambda i, j: (i, 0))
    in_specs = [
        rows, vec,
        pl.BlockSpec((None, d, tf), lambda i, j: (layer, 0, j)),
        pl.BlockSpec((None, d, tf), lambda i, j: (layer, 0, j + nt)),
        pl.BlockSpec((None, tf, d), lambda i, j: (layer, j, 0)),
    ]
    args = [h, norm_g.reshape(1, d), w_gate, w_in, w_out]
    if final:
        in_specs.append(vec)
        args.append(final_g.reshape(1, d))
    def outer(*refs):
        *io_refs, xn_ref = refs
        pltpu.emit_pipeline(
            functools.partial(_ffn_kernel, final=final),
            grid=(m // tm, nt),
            in_specs=in_specs,
            out_specs=[rows],
        )(*io_refs, scratches=(xn_ref,))

    hbm = pl.BlockSpec(memory_space=pl.ANY)
    return pl.pallas_call(
        outer,
        in_specs=[hbm] * len(args),
        out_specs=hbm,
        out_shape=jax.ShapeDtypeStruct((m, d), F32),
        scratch_shapes=[pltpu.VMEM((tm, d), BF16)],
        compiler_params=pltpu.CompilerParams(vmem_limit_bytes=VMEM_LIMIT),
        name="ffn",
    )(*args)


def _proj_kernel(*refs, small):
    if small:
        x_ref, g_ref, w_ref, cs_ref, ws_ref, o_ref, os_ref, xn_ref = refs
    else:
        x_ref, g_ref, w_ref, cs_ref, o_ref, xn_ref = refs
    j = pl.program_id(1)

    def project(xn):
        o_ref[...] = (_dot(xn, w_ref[...]) * cs_ref[...]).astype(o_ref.dtype)

    @pl.when(j == 0)
    def _():
        xn = _rms(x_ref[...], g_ref[...]).astype(BF16)
        xn_ref[...] = xn
        if small:
            os_ref[...] = _dot(xn, ws_ref[...])
        project(xn)

    @pl.when(j > 0)
    def _():
        project(xn_ref[...])


def _proj(x, norm_g, w, layer, col_scale, out_dtype, w_small=None, *, tm=PROJ_ROWS, tn=PROJ_COLS):
    m, d = x.shape
    n = w.shape[2]
    small = w_small is not None
    in_specs = [
        pl.BlockSpec((tm, d), lambda i, j: (i, 0)),
        pl.BlockSpec((1, d), lambda i, j: (0, 0)),
        pl.BlockSpec((None, d, tn), lambda i, j: (layer, 0, j)),
        pl.BlockSpec((1, tn), lambda i, j: (0, j)),
    ]
    args = [x, norm_g.reshape(1, d), w, col_scale]
    out_specs = [pl.BlockSpec((tm, tn), lambda i, j: (i, j))]
    out_shape = [jax.ShapeDtypeStruct((m, n), out_dtype)]
    if small:
        in_specs.append(pl.BlockSpec((None, d, LANE), lambda i, j: (layer, 0, 0)))
        args.append(w_small)
        out_specs.append(pl.BlockSpec((tm, LANE), lambda i, j: (i, 0)))
        out_shape.append(jax.ShapeDtypeStruct((m, LANE), F32))
    out = pl.pallas_call(
        functools.partial(_proj_kernel, small=small),
        grid=(m // tm, n // tn),
        in_specs=in_specs,
        out_specs=out_specs,
        out_shape=out_shape,
        scratch_shapes=[pltpu.VMEM((tm, d), BF16)],
        compiler_params=_params(("parallel", "arbitrary")),
        name="proj",
    )(*args)
    return out if small else out[0]


def _norm_proj_kernel(x_ref, g_ref, w_ref, o_ref):
    o_ref[...] = _dot(_rms(x_ref[...], g_ref[...]).astype(BF16), w_ref[...].astype(BF16))


def _norm_proj(x, norm_g, w, layer, *, tm=MEM_KV_ROWS):
    m, d = x.shape
    n = w.shape[2]
    return pl.pallas_call(
        _norm_proj_kernel,
        grid=(m // tm,),
        in_specs=[
            pl.BlockSpec((tm, d), lambda i: (i, 0)),
            pl.BlockSpec((1, d), lambda i: (0, 0)),
            pl.BlockSpec((None, d, n), lambda i: (layer, 0, 0)),
        ],
        out_specs=pl.BlockSpec((tm, n), lambda i: (i, 0)),
        out_shape=jax.ShapeDtypeStruct((m, n), F32),
        compiler_params=_params(("parallel",)),
        name="mem_kv",
    )(x, norm_g.reshape(1, d), w)


def _sgu_kernel(u_ref, v_ref, lng_ref, lnb_ref, ws_ref, bs_ref, o_ref, *, chunks):
    t = SGU_CHUNK
    gd = SGU_GROUP_DIM
    row = lax.broadcasted_iota(jnp.int32, (t, t), 0)
    col = lax.broadcasted_iota(jnp.int32, (t, t), 1)
    causal = row >= col
    for g in range(SGU_GROUPS):
        w = jnp.where(causal, ws_ref[g], 0.0).astype(BF16)
        bias = bs_ref[:, g:g + 1]
        ln_g = lng_ref[:, g * gd:(g + 1) * gd]
        ln_b = lnb_ref[:, g * gd:(g + 1) * gd]
        for c in range(chunks):
            rows = slice(c * t, (c + 1) * t)
            cols = slice(g * gd, (g + 1) * gd)
            v = _gelu_tanh(v_ref[rows, cols])
            mu = jnp.mean(v, axis=-1, keepdims=True)
            vc = v - mu
            var = jnp.mean(vc * vc, axis=-1, keepdims=True)
            vn = vc * lax.rsqrt(var + EPS) * ln_g + ln_b
            mixed = _dot(w, vn.astype(BF16)) + bias
            o_ref[rows, cols] = (_gelu_tanh(u_ref[rows, cols]) * mixed).astype(o_ref.dtype)


FOX_AUG = 6


def _fox_gate_kernel(f_ref, bf_ref, tril_ref, pq_ref, pk_ref, cq_ref, ck_ref, qa_ref, ka_ref):
    t = LANE
    seq = f_ref.shape[0]
    carry = jnp.zeros((1, t), F32)
    for b in range(seq // t):
        rows = slice(b * t, (b + 1) * t)
        lf = _log_sigmoid(f_ref[rows, :] + bf_ref[...])
        cum = _dot_01(tril_ref[...], lf) + carry
        carry = cum[t - 1:t, :]
        parts = jnp.concatenate(_split3(cum), axis=1)
        qa_ref[rows, :] = (_dot(parts, pq_ref[...]) + cq_ref[...]).astype(BF16)
        ka_ref[rows, :] = (_dot(parts, pk_ref[...]) + ck_ref[...]).astype(BF16)


def _tril3(n):
    tril = jnp.tril(jnp.ones((n, n), BF16))
    return jnp.concatenate([tril, tril, tril], axis=1)


def _fox_placement():
    src = jnp.arange(3 * LANE)[:, None]
    dst = jnp.arange(LANE)[None, :]
    p, h = src // LANE, src % LANE
    valid = h < FOX_HEADS
    pq = jnp.where(valid & (dst == h * FOX_AUG + p), 1.0, 0.0).astype(BF16)
    pk = jnp.where(valid & (dst == h * FOX_AUG + 3 + p), -1.0, 0.0).astype(BF16)
    used = dst < FOX_HEADS * FOX_AUG
    cq = jnp.where(used & (dst % FOX_AUG >= 3), 1.0, 0.0).astype(F32)
    ck = jnp.where(used & (dst % FOX_AUG < 3), 1.0, 0.0).astype(F32)
    return pq, pk, cq, ck


def _fox_gate(small, b_f, batch, seq):
    bf = jnp.zeros((1, LANE), F32).at[0, SMALL_FF:SMALL_FF + FOX_HEADS].set(b_f)
    w = LANE
    aug = jax.ShapeDtypeStruct((batch * seq, w), BF16)

    def const(shape):
        return pl.BlockSpec(shape, lambda b: (0, 0))

    return pl.pallas_call(
        _fox_gate_kernel,
        grid=(batch,),
        in_specs=[
            pl.BlockSpec((seq, LANE), lambda b: (b, 0)),
            const((1, LANE)), const((LANE, 3 * LANE)),
            const((3 * LANE, w)), const((3 * LANE, w)), const((1, w)), const((1, w)),
        ],
        out_specs=[pl.BlockSpec((seq, w), lambda b: (b, 0))] * 2,
        out_shape=[aug, aug],
        compiler_params=_params(("parallel",)),
        name="fox_gate",
    )(small, bf, _tril3(LANE), *_fox_placement())


def _fox_kernel(q_ref, k_ref, v_ref, qa_ref, ka_ref, o_ref, *, t, heads):
    seq = q_ref.shape[0]
    dh = FOX_HEAD_DIM
    lane = lax.broadcasted_iota(jnp.int32, (t, LANE), 1)
    ones_col = jnp.where(lane == 0, 1.0, 0.0).astype(BF16)
    row = lax.broadcasted_iota(jnp.int32, (t, t), 0)
    col = lax.broadcasted_iota(jnp.int32, (t, t), 1)
    causal = row >= col

    for g in range(heads):
        head = pl.program_id(1) * heads + g
        own = (lane >= head * FOX_AUG) & (lane < (head + 1) * FOX_AUG)
        cols = slice(g * dh, (g + 1) * dh)

        def k_block(j):
            rows = slice(j * t, (j + 1) * t)
            ka = jnp.where(own, ka_ref[rows, :], jnp.zeros((), BF16))
            return (jnp.concatenate([k_ref[rows, cols], ka], axis=1),
                    jnp.concatenate([v_ref[rows, cols], ones_col], axis=1))

        for i in range(seq // t):
            rows = slice(i * t, (i + 1) * t)
            qc = jnp.concatenate([q_ref[rows, cols], qa_ref[rows, :]], axis=1)
            kc, vc = k_block(i)
            s = jnp.where(causal, _dot_nt(qc, kc), NEG_BIG)
            m = jnp.max(s, axis=1, keepdims=True)
            acc = _dot(jnp.exp(s - m).astype(BF16), vc)
            for j in range(i):
                kc, vc = k_block(j)
                s = _dot_nt(qc, kc)
                m_new = jnp.maximum(m, jnp.max(s, axis=1, keepdims=True))
                acc = jnp.exp(m - m_new) * acc + _dot(jnp.exp(s - m_new).astype(BF16), vc)
                m = m_new
            o_ref[rows, cols] = (acc[:, :dh] / acc[:, dh:dh + 1]).astype(o_ref.dtype)


def _fox(zb, qa, ka, batch, seq, *, t=FOX_TILE, heads=FOX_HEADS_PER_STEP):
    m = zb.shape[0]
    w = heads * FOX_HEAD_DIM

    def group(seg):
        return pl.BlockSpec((seq, w), lambda b, h: (b, seg * LANE // w + h))

    aug = pl.BlockSpec((seq, LANE), lambda b, h: (b, 0))
    return pl.pallas_call(
        functools.partial(_fox_kernel, t=t, heads=heads),
        grid=(batch, FOX_HEADS // heads),
        in_specs=[group(ZB_FQ), group(ZB_FK), group(ZB_FV), aug, aug],
        out_specs=group(0),
        out_shape=jax.ShapeDtypeStruct((m, FOX_WIDTH), BF16),
        compiler_params=_params(("parallel", "parallel")),
        name="fox",
    )(zb, zb, zb, qa, ka)


GLA_BLOCK = 4 * GLA_CHUNK


def _gla_decay_kernel(a_ref, wg_ref, bg_ref, cm_ref, o_ref, *, blocks):
    t = GLA_BLOCK
    for b in range(blocks):
        rows = slice(b * t, (b + 1) * t)
        gl = _dot(a_ref[rows, :].astype(BF16), wg_ref[...]) + bg_ref[...]
        g = _log_sigmoid(gl) / GLA_GATE_TAU
        o_ref[rows, :] = _dot_01(cm_ref[...], g)


def _gla_decay(small, w_gate_emb, b_gate, *, blocks=GLA_DECAY_BLOCKS_PER_STEP):
    m = small.shape[0]
    t = GLA_BLOCK
    rows = blocks * t
    idx = jnp.arange(t)
    same = (idx[:, None] // GLA_CHUNK) == (idx[None, :] // GLA_CHUNK)
    cm = jnp.where(same & (idx[:, None] >= idx[None, :]), 1.0, 0.0).astype(BF16)
    cm3 = jnp.concatenate([cm, cm, cm], axis=1)
    return pl.pallas_call(
        functools.partial(_gla_decay_kernel, blocks=blocks),
        grid=(m // rows,),
        in_specs=[
            pl.BlockSpec((rows, LANE), lambda i: (i, 0)),
            pl.BlockSpec((LANE, GLA_KW), lambda i: (0, 0)),
            pl.BlockSpec((1, GLA_KW), lambda i: (0, 0)),
            pl.BlockSpec((t, 3 * t), lambda i: (0, 0)),
        ],
        out_specs=pl.BlockSpec((rows, GLA_KW), lambda i: (i, 0)),
        out_shape=jax.ShapeDtypeStruct((m, GLA_KW), F32),
        compiler_params=_params(("parallel",)),
        name="gla_decay",
    )(small, w_gate_emb, b_gate.reshape(1, GLA_KW), cm3)


def _gla_kernel(q_ref, k_ref, v_ref, r_ref, bc_ref, on_ref, o_ref, *, heads):
    for g in range(heads):
        _gla_head(q_ref, k_ref, v_ref, r_ref, bc_ref, on_ref, o_ref,
                  slice(g * GLA_DK, (g + 1) * GLA_DK), slice(g * GLA_DV, (g + 1) * GLA_DV))


def _gla_head(q_ref, k_ref, v_ref, r_ref, bc_ref, on_ref, o_ref, kcols, vcols):
    c = GLA_CHUNK
    t = GLA_BLOCK
    dk = GLA_DK
    seq = q_ref.shape[0]
    st = jnp.zeros((GLA_DV, dk), F32)
    row = lax.broadcasted_iota(jnp.int32, (t, t), 0)
    col = lax.broadcasted_iota(jnp.int32, (t, t), 1)
    intra = (row >= col) & ((row // c) == (col // c))
    o_gain = on_ref[:, vcols]

    for n in range(seq // t):
        rows = slice(n * t, (n + 1) * t)
        bc = bc_ref[rows, kcols]
        qb = q_ref[rows, kcols] * (dk ** -0.5)
        kb = k_ref[rows, kcols]
        vb = v_ref[rows, vcols]
        chunks = [slice(i * c, (i + 1) * c) for i in range(t // c)]
        b_mid = jnp.concatenate(
            [jnp.broadcast_to(bc[s.start + c // 2:s.start + c // 2 + 1, :], (c, dk)) for s in chunks], axis=0)
        b_last = jnp.concatenate(
            [jnp.broadcast_to(bc[s.stop - 1:s.stop, :], (c, dk)) for s in chunks], axis=0)
        qd = (qb * jnp.exp(bc - b_mid)).astype(BF16)
        kd = (kb * jnp.exp(b_mid - bc)).astype(BF16)
        ku = (kb * jnp.exp(b_last - bc)).astype(BF16)
        qi = (qb * jnp.exp(bc)).astype(BF16)
        att = jnp.where(intra, _dot_nt(qd, kd), 0.0)
        o = _dot(att.astype(BF16), vb)
        inter = []
        for s in chunks:
            inter.append(_dot_nt(qi[s, :], st.astype(BF16)))
            st = st * jnp.exp(bc[s.stop - 1:s.stop, :]) + _dot_tn(vb[s, :], ku[s, :])
        o = _rms(o + jnp.concatenate(inter, axis=0), o_gain)
        o_ref[rows, vcols] = (o * _silu(r_ref[rows, vcols])).astype(o_ref.dtype)


def _gla(zf, zb, bc, o_norm, batch, seq, *, heads=GLA_HEADS_PER_STEP):
    m = zf.shape[0]
    dk, dv = heads * GLA_DK, heads * GLA_DV
    return pl.pallas_call(
        functools.partial(_gla_kernel, heads=heads),
        grid=(batch, GLA_HEADS // heads),
        in_specs=[
            pl.BlockSpec((seq, dk), lambda b, h: (b, ZF_GQ * LANE // dk + h)),
            pl.BlockSpec((seq, dk), lambda b, h: (b, ZF_GK * LANE // dk + h)),
            pl.BlockSpec((seq, dv), lambda b, h: (b, ZB_GV * LANE // dv + h)),
            pl.BlockSpec((seq, dv), lambda b, h: (b, ZF_GR * LANE // dv + h)),
            pl.BlockSpec((seq, dk), lambda b, h: (b, h)),
            pl.BlockSpec((1, dv), lambda b, h: (0, h)),
        ],
        out_specs=pl.BlockSpec((seq, dv), lambda b, h: (b, h)),
        out_shape=jax.ShapeDtypeStruct((m, GLA_VW), BF16),
        compiler_params=_params(("parallel", "parallel")),
        name="gla",
    )(zf, zf, zb, zf, bc, o_norm.reshape(1, GLA_VW))


def _merge_kernel(u0_ref, v0_ref, u_ref, v_ref, lng_ref, lnb_ref, ws_ref, bs_ref, b_ref, c_ref,
                  ga_ref, gb_ref, gc_ref, wa_ref, wb_ref, wc_ref, wo_ref, h_ref, o_ref, ya_ref):
    i = pl.program_id(0)
    chunks = ya_ref.shape[1] // SGU_CHUNK

    @pl.when(i == 0)
    def _():
        _sgu_kernel(u0_ref, v0_ref, lng_ref, lnb_ref, ws_ref, bs_ref, ya_ref.at[0], chunks=chunks)

    slot = i % 2
    mix = (jax.nn.sigmoid(ga_ref[...]) * _dot(ya_ref[slot], wa_ref[...])
           + jax.nn.sigmoid(gb_ref[...]) * _dot(b_ref[...], wb_ref[...])
           + jax.nn.sigmoid(gc_ref[...]) * _dot(c_ref[...], wc_ref[...]))
    o_ref[...] = h_ref[...] + _dot(mix.astype(BF16), wo_ref[...])
    _sgu_kernel(u_ref, v_ref, lng_ref, lnb_ref, ws_ref, bs_ref, ya_ref.at[1 - slot], chunks=chunks)


def _merge(sgu_params, yb, yc, zf, wa, wb, wc, wo, h, layer, *, tm=MERGE_ROWS):
    m, d = h.shape
    kw = yb.shape[1]
    g0 = ZF_GATES * LANE // d
    last = m // tm - 1
    ln_g, ln_b, w_s, b_s = sgu_params
    branch = pl.BlockSpec((tm, kw), lambda i: (i, 0))
    resident = dict(pipeline_mode=pl.Buffered(1))
    wspec = pl.BlockSpec((None, kw, d), lambda i: (layer, 0, 0), **resident)
    return pl.pallas_call(
        _merge_kernel,
        grid=(m // tm,),
        in_specs=[
            pl.BlockSpec((tm, kw), lambda i: (0, ZF_SU * LANE // kw)),
            pl.BlockSpec((tm, kw), lambda i: (0, ZF_SV * LANE // kw)),
            pl.BlockSpec((tm, kw), lambda i: (jnp.minimum(i + 1, last), ZF_SU * LANE // kw)),
            pl.BlockSpec((tm, kw), lambda i: (jnp.minimum(i + 1, last), ZF_SV * LANE // kw)),
            pl.BlockSpec((1, kw), lambda i: (0, 0)),
            pl.BlockSpec((1, kw), lambda i: (0, 0)),
            pl.BlockSpec((SGU_GROUPS, SGU_CHUNK, SGU_CHUNK), lambda i: (0, 0, 0)),
            pl.BlockSpec((SGU_CHUNK, SGU_GROUPS), lambda i: (0, 0)),
            branch, branch,
            pl.BlockSpec((tm, d), lambda i: (i, g0)),
            pl.BlockSpec((tm, d), lambda i: (i, g0 + 1)),
            pl.BlockSpec((tm, d), lambda i: (i, g0 + 2)),
            wspec, wspec, wspec,
            pl.BlockSpec((None, d, d), lambda i: (layer, 0, 0), **resident),
            pl.BlockSpec((tm, d), lambda i: (i, 0)),
        ],
        out_specs=pl.BlockSpec((tm, d), lambda i: (i, 0)),
        out_shape=jax.ShapeDtypeStruct((m, d), F32),
        scratch_shapes=[pltpu.VMEM((2, tm, kw), BF16)],
        compiler_params=_params(("arbitrary",)),
        name="merge",
    )(zf, zf, zf, zf, ln_g.reshape(1, kw), ln_b.reshape(1, kw), w_s, jnp.swapaxes(b_s, 0, 1),
      yb, yc, zf, zf, zf, wa, wb, wc, wo, h)


def _xattn_kernel(h_ref, g_ref, wq_ref, kv_ref, wo_ref, o_ref):
    dh = XA_HEAD_DIM
    x = h_ref[...]
    n = _rms(x, g_ref[...]).astype(BF16)
    q = _dot(n, wq_ref[...].astype(BF16)) * (dh ** -0.5)
    outs = []
    for hd in range(XA_HEADS):
        qh = q[:, hd * dh:(hd + 1) * dh].astype(BF16)
        kh = kv_ref[:, hd * dh:(hd + 1) * dh].astype(BF16)
        vh = kv_ref[:, XA_WIDTH + hd * dh:XA_WIDTH + (hd + 1) * dh].astype(BF16)
        s = _dot_nt(qh, kh)
        e = jnp.exp(s - jnp.max(s, axis=-1, keepdims=True))
        p = e / jnp.sum(e, axis=-1, keepdims=True)
        outs.append(_dot(p.astype(BF16), vh))
    o = jnp.concatenate(outs, axis=1).astype(BF16)
    o_ref[...] = x + _dot(o, wo_ref[...].astype(BF16))


def _xattn(h, norm_g, wq, kv, wo, layer, seq, mem_len, *, tm=XATTN_ROWS):
    m, d = h.shape
    per_batch = seq // tm
    return pl.pallas_call(
        _xattn_kernel,
        grid=(m // tm,),
        in_specs=[
            pl.BlockSpec((tm, d), lambda i: (i, 0)),
            pl.BlockSpec((1, d), lambda i: (0, 0)),
            pl.BlockSpec((None, d, XA_WIDTH), lambda i: (layer, 0, 0)),
            pl.BlockSpec((mem_len, 2 * XA_WIDTH), lambda i: (i // per_batch, 0)),
            pl.BlockSpec((None, XA_WIDTH, d), lambda i: (layer, 0, 0)),
        ],
        out_specs=pl.BlockSpec((tm, d), lambda i: (i, 0)),
        out_shape=jax.ShapeDtypeStruct((m, d), F32),
        compiler_params=_params(("parallel",)),
        name="xattn",
    )(h, norm_g.reshape(1, d), wq, kv, wo)


CAST_BLOCK_BYTES = 6 * 1024 * 1024


def _cast_kernel(x_ref, o_ref):
    o_ref[...] = x_ref[...].astype(o_ref.dtype)


def _to_bf16(w, cols=None):
    lead = w.shape[:-1]
    c = w.shape[-1] if cols is None else cols
    w2 = w.reshape(-1, w.shape[-1])
    r = w2.shape[0]
    rows = 16
    while rows * 2 * c * 4 <= CAST_BLOCK_BYTES and r % (rows * 2) == 0:
        rows *= 2
    spec = pl.BlockSpec((rows, c), lambda i: (i, 0))
    out = pl.pallas_call(
        _cast_kernel,
        grid=(r // rows,),
        in_specs=[spec],
        out_specs=spec,
        out_shape=jax.ShapeDtypeStruct((r, c), BF16),
        compiler_params=_params(("parallel",)),
        name="cast",
    )(w2)
    return out.reshape(*lead, c)

W_FQ = 2 * SGU_WIDTH
W_FF = W_FQ + 3 * FOX_WIDTH
W_GQ = W_FF + FOX_HEADS
W_GV = W_GQ + 2 * GLA_KW
W_GA = W_GV + GLA_VW
W_GR = W_GA + GLA_GATE_RANK
W_END = W_GR + GLA_VW + 3 * D_MODEL


REPACK_CHUNK = 1024


def _repack_kernel(wt_ref, wf_ref, wb_ref, ws_ref):
    def move(dst_ref, dst, src, width):
        for off in range(0, width, REPACK_CHUNK):
            n = min(REPACK_CHUNK, width - off)
            dst_ref[:, dst + off:dst + off + n] = wt_ref[src + off:src + off + n, :].T.astype(BF16)

    move(wf_ref, 0, 0, W_FQ)
    move(wf_ref, W_FQ, W_GQ, 2 * GLA_KW)
    move(wf_ref, W_FQ + 2 * GLA_KW, W_GR, W_END - W_GR)
    move(wb_ref, 0, W_FQ, 3 * FOX_WIDTH)
    move(wb_ref, 3 * FOX_WIDTH, W_GV, GLA_VW)
    lane = lax.broadcasted_iota(jnp.int32, (wt_ref.shape[1], LANE), 1)
    ff = wt_ref[W_FF - SMALL_FF:W_FF - SMALL_FF + LANE, :].T
    ga = wt_ref[W_GA - SMALL_GA:W_GA - SMALL_GA + LANE, :].T
    side = jnp.where(lane < SMALL_FF + FOX_HEADS, ff,
                     jnp.where(lane < SMALL_GA + GLA_GATE_RANK, ga, 0.0))
    ws_ref[...] = side.astype(BF16)


def _repack_w_in(w, *, kb=REPACK_K_ROWS):
    nl, d, n = w.shape
    wt = jnp.swapaxes(w, 1, 2)

    def spec(width):
        return pl.BlockSpec((None, kb, width), lambda l, i: (l, i, 0))

    return pl.pallas_call(
        _repack_kernel,
        grid=(nl, d // kb),
        in_specs=[pl.BlockSpec((None, n, kb), lambda l, i: (l, 0, i))],
        out_specs=[spec(ZF_WIDTH), spec(ZB_WIDTH), spec(LANE)],
        out_shape=[jax.ShapeDtypeStruct((nl, d, width), BF16) for width in (ZF_WIDTH, ZB_WIDTH, LANE)],
        compiler_params=_params(("parallel", "parallel")),
        name="repack",
    )(wt)


def _embed_w_gate(w_gate):
    emb = jnp.zeros((LANE, GLA_KW), BF16)
    return emb.at[SMALL_GA:SMALL_GA + GLA_GATE_RANK, :].set(w_gate.astype(BF16))


def kernel(x, mem, ffn1_norm, ffn1_w_in, ffn1_w_out, mix_norm, w_in, sgu_ln_g, sgu_ln_b, sgu_w_s, sgu_b_s, fox_b_f, gla_w_gate, gla_b_gate, gla_o_norm, w_branch_a, w_branch_b, w_branch_c, w_out, xa_norm, mem_norm, xa_w_q, xa_w_kv, xa_w_o, ffn2_norm, ffn2_w_in, ffn2_w_out, final_norm):
    batch, seq, d = x.shape
    mem_len = mem.shape[1]
    h = x.reshape(batch * seq, d)
    mem2 = mem.reshape(batch * mem_len, d)
    ones_f = jnp.ones((1, ZF_WIDTH), F32)
    scale_b = jnp.ones((1, ZB_WIDTH), F32).at[:, ZB_FQ * LANE:ZB_FQ * LANE + FOX_WIDTH].set(
        FOX_HEAD_DIM ** -0.5)
    f1_gate, f2_gate = _to_bf16(ffn1_w_in, D_FF), _to_bf16(ffn2_w_in, D_FF)
    wa, wb, wc, wo = (_to_bf16(w) for w in (w_branch_a, w_branch_b, w_branch_c, w_out))
    w_f, w_b, w_small = _repack_w_in(w_in)
    for l in range(DEPTH):
        h = _ffn(h, ffn1_norm[l], f1_gate, ffn1_w_in, ffn1_w_out, l)

        zf, small = _proj(h, mix_norm[l], w_f, l, ones_f, F32, w_small)
        zb = _proj(h, mix_norm[l], w_b, l, scale_b, BF16)
        qa, ka = _fox_gate(small, fox_b_f[l], batch, seq)
        yb = _fox(zb, qa, ka, batch, seq)
        bc = _gla_decay(small, _embed_w_gate(gla_w_gate[l]), gla_b_gate[l])
        yc = _gla(zf, zb, bc, gla_o_norm[l], batch, seq)
        sgu_params = (sgu_ln_g[l], sgu_ln_b[l], sgu_w_s[l], sgu_b_s[l])
        h = _merge(sgu_params, yb, yc, zf, wa, wb, wc, wo, h, l)

        kv = _norm_proj(mem2, mem_norm[l], xa_w_kv, l)
        h = _xattn(h, xa_norm[l], xa_w_q, kv, xa_w_o, l, seq, mem_len)

        h = _ffn(h, ffn2_norm[l], f2_gate, ffn2_w_in, ffn2_w_out, l,
                 final_norm if l == DEPTH - 1 else None)
    return h.reshape(batch, seq, d)
```

```python
import functools

import jax
import jax.numpy as jnp
from jax import lax
from jax.experimental import pallas as pl
from jax.experimental.pallas import tpu as pltpu

F32 = jnp.float32
BF16 = jnp.bfloat16

D_MODEL = 2048
DEPTH = 2
D_FF = 5632
EPS = 1e-6

SGU_GROUPS = 4
SGU_GROUP_DIM = 256
SGU_WIDTH = 1024
SGU_CHUNK = 128

FOX_HEADS = 8
FOX_HEAD_DIM = 128
FOX_WIDTH = 1024

GLA_HEADS = 4
GLA_DK = 128
GLA_DV = 256
GLA_KW = 512
GLA_VW = 1024
GLA_GATE_RANK = 16
GLA_GATE_TAU = 16.0
GLA_CHUNK = 64

XA_HEADS = 4
XA_HEAD_DIM = 128
XA_WIDTH = 512

LANE = 128

ZF_SU, ZF_SV, ZF_GQ, ZF_GK, ZF_GR, ZF_GATES = 0, 8, 16, 20, 24, 32
ZF_WIDTH = 80 * LANE
ZB_FQ, ZB_FK, ZB_FV, ZB_GV = 0, 8, 16, 24
ZB_WIDTH = 32 * LANE
SMALL_FF = 0
SMALL_GA = 8

NEG_BIG = -1e30
VMEM_LIMIT = 63 * 1024 * 1024

FFN_ROWS, FFN_HIDDEN_TILE = 1024, 512
PROJ_ROWS, PROJ_COLS = 1024, 2048
MEM_KV_ROWS = 1024
FOX_TILE, FOX_HEADS_PER_STEP = 512, 2
GLA_DECAY_BLOCKS_PER_STEP = 4
GLA_HEADS_PER_STEP = 2
MERGE_ROWS = 256
XATTN_ROWS = 1024
REPACK_K_ROWS = 256


def _params(sem):
    return pltpu.CompilerParams(dimension_semantics=sem, vmem_limit_bytes=VMEM_LIMIT)


RING = dict(pipeline_mode=pl.Buffered(3))


def _ring_call(body, *, grid, in_specs, out_specs, out_shape, args, name):
    def outer(*refs):
        pltpu.emit_pipeline(body, grid=grid, in_specs=in_specs, out_specs=out_specs)(*refs)

    hbm = pl.BlockSpec(memory_space=pl.ANY)
    return pl.pallas_call(
        outer,
        in_specs=[hbm] * len(args),
        out_specs=[hbm] * len(out_shape),
        out_shape=out_shape,
        compiler_params=pltpu.CompilerParams(vmem_limit_bytes=VMEM_LIMIT),
        name=name,
    )(*args)


def _rms(x, g):
    return x * lax.rsqrt(jnp.mean(x * x, axis=-1, keepdims=True) + EPS) * g


def _log_sigmoid(x):
    return jnp.minimum(x, 0.0) - jnp.log1p(jnp.exp(-jnp.abs(x)))


def _gelu_tanh(x):
    c = 0.7978845608028654
    return x * (0.5 * (1.0 + jnp.tanh(c * (x + 0.044715 * (x * x * x)))))


def _silu(x):
    return x * jax.nn.sigmoid(x)


def _dot(a, b):
    return jnp.dot(a, b, preferred_element_type=F32)


def _dot_nt(a, b):
    return lax.dot_general(a, b, (((1,), (1,)), ((), ())), preferred_element_type=F32)


def _dot_tn(a, b):
    return lax.dot_general(a, b, (((0,), (0,)), ((), ())), preferred_element_type=F32)


def _split3(x):
    hi = x.astype(BF16)
    r = x - hi.astype(F32)
    lo = r.astype(BF16)
    lo2 = (r - lo.astype(F32)).astype(BF16)
    return hi, lo, lo2


def _dot_01(mat3, x):
    return _dot(mat3, jnp.concatenate(_split3(x), axis=0))


def _ffn_kernel(*refs, final):
    if final:
        x_ref, g_ref, wg_ref, wu_ref, wo_ref, fg_ref, o_ref, xn_ref = refs
    else:
        x_ref, g_ref, wg_ref, wu_ref, wo_ref, o_ref, xn_ref = refs
    j = pl.program_id(1)

    def tile(xn):
        gate = _dot(xn, wg_ref[...])
        up = _dot(xn, wu_ref[...].astype(BF16))
        act = (_silu(gate) * up * 0.5).astype(BF16)
        return _dot(act, wo_ref[...].astype(BF16))

    @pl.when(j == 0)
    def _():
        x = x_ref[...]
        xn = _rms(x, g_ref[...]).astype(BF16)
        xn_ref[...] = xn
        o_ref[...] = x + tile(xn)

    @pl.when(j > 0)
    def _():
        o_ref[...] += tile(xn_ref[...])

    if final:
        @pl.when(j == pl.num_programs(1) - 1)
        def _():
            o_ref[...] = _rms(o_ref[...], fg_ref[...])


def _ffn(h, norm_g, w_gate, w_in, w_out, layer, final_g=None, *, tm=FFN_ROWS, tf=FFN_HIDDEN_TILE):
    m, d = h.shape
    f = w_out.shape[1]
    nt = f // tf
    final = final_g is not None
    vec = pl.BlockSpec((1, d), lambda i, j: (0, 0))
    rows = pl.BlockSpec((tm, d), lambda i, j: (i, 0))
    in_specs = [
        rows, vec,
        pl.BlockSpec((None, d, tf), lambda i, j: (layer, 0, j)),
        pl.BlockSpec((None, d, tf), lambda i, j: (layer, 0, j + nt)),
        pl.BlockSpec((None, tf, d), lambda i, j: (layer, j, 0)),
    ]
    args = [h, norm_g.reshape(1, d), w_gate, w_in, w_out]
    if final:
        in_specs.append(vec)
        args.append(final_g.reshape(1, d))
    return pl.pallas_call(
        functools.partial(_ffn_kernel, final=final),
        grid=(m // tm, nt),
        in_specs=in_specs,
        out_specs=rows,
        out_shape=jax.ShapeDtypeStruct((m, d), F32),
        scratch_shapes=[pltpu.VMEM((tm, d), BF16)],
        compiler_params=_params(("parallel", "arbitrary")),
        name="ffn",
    )(*args)


def _proj_kernel(*refs, small):
    if small:
        x_ref, g_ref, w_ref, cs_ref, ws_ref, o_ref, os_ref, xn_ref = refs
    else:
        x_ref, g_ref, w_ref, cs_ref, o_ref, xn_ref = refs
    j = pl.program_id(1)

    def project(xn):
        o_ref[...] = (_dot(xn, w_ref[...]) * cs_ref[...]).astype(o_ref.dtype)

    @pl.when(j == 0)
    def _():
        xn = _rms(x_ref[...], g_ref[...]).astype(BF16)
        xn_ref[...] = xn
        if small:
            os_ref[...] = _dot(xn, ws_ref[...])
        project(xn)

    @pl.when(j > 0)
    def _():
        project(xn_ref[...])


def _proj(x, norm_g, w, layer, col_scale, out_dtype, w_small=None, *, tm=PROJ_ROWS, tn=PROJ_COLS):
    m, d = x.shape
    n = w.shape[2]
    small = w_small is not None
    in_specs = [
        pl.BlockSpec((tm, d), lambda i, j: (i, 0)),
        pl.BlockSpec((1, d), lambda i, j: (0, 0)),
        pl.BlockSpec((None, d, tn), lambda i, j: (layer, 0, j)),
        pl.BlockSpec((1, tn), lambda i, j: (0, j)),
    ]
    args = [x, norm_g.reshape(1, d), w, col_scale]
    out_specs = [pl.BlockSpec((tm, tn), lambda i, j: (i, j))]
    out_shape = [jax.ShapeDtypeStruct((m, n), out_dtype)]
    if small:
        in_specs.append(pl.BlockSpec((None, d, LANE), lambda i, j: (layer, 0, 0)))
        args.append(w_small)
        out_specs.append(pl.BlockSpec((tm, LANE), lambda i, j: (i, 0)))
        out_shape.append(jax.ShapeDtypeStruct((m, LANE), F32))
    out = pl.pallas_call(
        functools.partial(_proj_kernel, small=small),
        grid=(m // tm, n // tn),
        in_specs=in_specs,
        out_specs=out_specs,
        out_shape=out_shape,
        scratch_shapes=[pltpu.VMEM((tm, d), BF16)],
        compiler_params=_params(("parallel", "arbitrary")),
        name="proj",
    )(*args)
    return out if small else out[0]


def _norm_proj_kernel(x_ref, g_ref, w_ref, o_ref):
    o_ref[...] = _dot(_rms(x_ref[...], g_ref[...]).astype(BF16), w_ref[...].astype(BF16))


def _norm_proj(x, norm_g, w, layer, *, tm=MEM_KV_ROWS):
    m, d = x.shape
    n = w.shape[2]
    return pl.pallas_call(
        _norm_proj_kernel,
        grid=(m // tm,),
        in_specs=[
            pl.BlockSpec((tm, d), lambda i: (i, 0)),
            pl.BlockSpec((1, d), lambda i: (0, 0)),
            pl.BlockSpec((None, d, n), lambda i: (layer, 0, 0)),
        ],
        out_specs=pl.BlockSpec((tm, n), lambda i: (i, 0)),
        out_shape=jax.ShapeDtypeStruct((m, n), F32),
        compiler_params=_params(("parallel",)),
        name="mem_kv",
    )(x, norm_g.reshape(1, d), w)


def _sgu_kernel(u_ref, v_ref, lng_ref, lnb_ref, ws_ref, bs_ref, o_ref, *, chunks):
    t = SGU_CHUNK
    gd = SGU_GROUP_DIM
    row = lax.broadcasted_iota(jnp.int32, (t, t), 0)
    col = lax.broadcasted_iota(jnp.int32, (t, t), 1)
    causal = row >= col
    for g in range(SGU_GROUPS):
        w = jnp.where(causal, ws_ref[g], 0.0).astype(BF16)
        bias = bs_ref[:, g:g + 1]
        ln_g = lng_ref[:, g * gd:(g + 1) * gd]
        ln_b = lnb_ref[:, g * gd:(g + 1) * gd]
        for c in range(chunks):
            rows = slice(c * t, (c + 1) * t)
            cols = slice(g * gd, (g + 1) * gd)
            v = _gelu_tanh(v_ref[rows, cols])
            mu = jnp.mean(v, axis=-1, keepdims=True)
            vc = v - mu
            var = jnp.mean(vc * vc, axis=-1, keepdims=True)
            vn = vc * lax.rsqrt(var + EPS) * ln_g + ln_b
            mixed = _dot(w, vn.astype(BF16)) + bias
            o_ref[rows, cols] = (_gelu_tanh(u_ref[rows, cols]) * mixed).astype(o_ref.dtype)


FOX_AUG = 6


def _fox_gate_kernel(f_ref, bf_ref, tril_ref, pq_ref, pk_ref, cq_ref, ck_ref, qa_ref, ka_ref):
    t = LANE
    seq = f_ref.shape[0]
    carry = jnp.zeros((1, t), F32)
    for b in range(seq // t):
        rows = slice(b * t, (b + 1) * t)
        lf = _log_sigmoid(f_ref[rows, :] + bf_ref[...])
        cum = _dot_01(tril_ref[...], lf) + carry
        carry = cum[t - 1:t, :]
        parts = jnp.concatenate(_split3(cum), axis=1)
        qa_ref[rows, :] = (_dot(parts, pq_ref[...]) + cq_ref[...]).astype(BF16)
        ka_ref[rows, :] = (_dot(parts, pk_ref[...]) + ck_ref[...]).astype(BF16)


def _tril3(n):
    tril = jnp.tril(jnp.ones((n, n), BF16))
    return jnp.concatenate([tril, tril, tril], axis=1)


def _fox_placement():
    src = jnp.arange(3 * LANE)[:, None]
    dst = jnp.arange(LANE)[None, :]
    p, h = src // LANE, src % LANE
    valid = h < FOX_HEADS
    pq = jnp.where(valid & (dst == h * FOX_AUG + p), 1.0, 0.0).astype(BF16)
    pk = jnp.where(valid & (dst == h * FOX_AUG + 3 + p), -1.0, 0.0).astype(BF16)
    used = dst < FOX_HEADS * FOX_AUG
    cq = jnp.where(used & (dst % FOX_AUG >= 3), 1.0, 0.0).astype(F32)
    ck = jnp.where(used & (dst % FOX_AUG < 3), 1.0, 0.0).astype(F32)
    return pq, pk, cq, ck


def _fox_gate(small, b_f, batch, seq):
    bf = jnp.zeros((1, LANE), F32).at[0, SMALL_FF:SMALL_FF + FOX_HEADS].set(b_f)
    w = LANE
    aug = jax.ShapeDtypeStruct((batch * seq, w), BF16)

    def const(shape):
        return pl.BlockSpec(shape, lambda b: (0, 0))

    return pl.pallas_call(
        _fox_gate_kernel,
        grid=(batch,),
        in_specs=[
            pl.BlockSpec((seq, LANE), lambda b: (b, 0)),
            const((1, LANE)), const((LANE, 3 * LANE)),
            const((3 * LANE, w)), const((3 * LANE, w)), const((1, w)), const((1, w)),
        ],
        out_specs=[pl.BlockSpec((seq, w), lambda b: (b, 0))] * 2,
        out_shape=[aug, aug],
        compiler_params=_params(("parallel",)),
        name="fox_gate",
    )(small, bf, _tril3(LANE), *_fox_placement())


def _fox_kernel(q_ref, k_ref, v_ref, qa_ref, ka_ref, o_ref, *, t, heads):
    seq = q_ref.shape[0]
    dh = FOX_HEAD_DIM
    lane = lax.broadcasted_iota(jnp.int32, (t, LANE), 1)
    ones_col = jnp.where(lane == 0, 1.0, 0.0).astype(BF16)
    row = lax.broadcasted_iota(jnp.int32, (t, t), 0)
    col = lax.broadcasted_iota(jnp.int32, (t, t), 1)
    causal = row >= col

    for g in range(heads):
        head = pl.program_id(1) * heads + g
        own = (lane >= head * FOX_AUG) & (lane < (head + 1) * FOX_AUG)
        cols = slice(g * dh, (g + 1) * dh)

        def k_block(j):
            rows = slice(j * t, (j + 1) * t)
            ka = jnp.where(own, ka_ref[rows, :], jnp.zeros((), BF16))
            return (jnp.concatenate([k_ref[rows, cols], ka], axis=1),
                    jnp.concatenate([v_ref[rows, cols], ones_col], axis=1))

        for i in range(seq // t):
            rows = slice(i * t, (i + 1) * t)
            qc = jnp.concatenate([q_ref[rows, cols], qa_ref[rows, :]], axis=1)
            kc, vc = k_block(i)
            s = jnp.where(causal, _dot_nt(qc, kc), NEG_BIG)
            m = jnp.max(s, axis=1, keepdims=True)
            acc = _dot(jnp.exp(s - m).astype(BF16), vc)
            for j in range(i):
                kc, vc = k_block(j)
                s = _dot_nt(qc, kc)
                m_new = jnp.maximum(m, jnp.max(s, axis=1, keepdims=True))
                acc = jnp.exp(m - m_new) * acc + _dot(jnp.exp(s - m_new).astype(BF16), vc)
                m = m_new
            o_ref[rows, cols] = (acc[:, :dh] / acc[:, dh:dh + 1]).astype(o_ref.dtype)


def _fox(zb, qa, ka, batch, seq, *, t=FOX_TILE, heads=FOX_HEADS_PER_STEP):
    m = zb.shape[0]
    w = heads * FOX_HEAD_DIM

    def group(seg):
        return pl.BlockSpec((seq, w), lambda b, h: (b, seg * LANE // w + h))

    aug = pl.BlockSpec((seq, LANE), lambda b, h: (b, 0))
    return pl.pallas_call(
        functools.partial(_fox_kernel, t=t, heads=heads),
        grid=(batch, FOX_HEADS // heads),
        in_specs=[group(ZB_FQ), group(ZB_FK), group(ZB_FV), aug, aug],
        out_specs=group(0),
        out_shape=jax.ShapeDtypeStruct((m, FOX_WIDTH), BF16),
        compiler_params=_params(("parallel", "parallel")),
        name="fox",
    )(zb, zb, zb, qa, ka)


GLA_BLOCK = 4 * GLA_CHUNK


def _gla_decay_kernel(a_ref, wg_ref, bg_ref, cm_ref, o_ref, *, blocks):
    t = GLA_BLOCK
    for b in range(blocks):
        rows = slice(b * t, (b + 1) * t)
        gl = _dot(a_ref[rows, :].astype(BF16), wg_ref[...]) + bg_ref[...]
        g = _log_sigmoid(gl) / GLA_GATE_TAU
        o_ref[rows, :] = _dot_01(cm_ref[...], g)


def _gla_decay(small, w_gate_emb, b_gate, *, blocks=GLA_DECAY_BLOCKS_PER_STEP):
    m = small.shape[0]
    t = GLA_BLOCK
    rows = blocks * t
    idx = jnp.arange(t)
    same = (idx[:, None] // GLA_CHUNK) == (idx[None, :] // GLA_CHUNK)
    cm = jnp.where(same & (idx[:, None] >= idx[None, :]), 1.0, 0.0).astype(BF16)
    cm3 = jnp.concatenate([cm, cm, cm], axis=1)
    return pl.pallas_call(
        functools.partial(_gla_decay_kernel, blocks=blocks),
        grid=(m // rows,),
        in_specs=[
            pl.BlockSpec((rows, LANE), lambda i: (i, 0)),
            pl.BlockSpec((LANE, GLA_KW), lambda i: (0, 0)),
            pl.BlockSpec((1, GLA_KW), lambda i: (0, 0)),
            pl.BlockSpec((t, 3 * t), lambda i: (0, 0)),
        ],
        out_specs=pl.BlockSpec((rows, GLA_KW), lambda i: (i, 0)),
        out_shape=jax.ShapeDtypeStruct((m, GLA_KW), F32),
        compiler_params=_params(("parallel",)),
        name="gla_decay",
    )(small, w_gate_emb, b_gate.reshape(1, GLA_KW), cm3)


def _gla_kernel(q_ref, k_ref, v_ref, r_ref, bc_ref, on_ref, o_ref, *, heads):
    for g in range(heads):
        _gla_head(q_ref, k_ref, v_ref, r_ref, bc_ref, on_ref, o_ref,
                  slice(g * GLA_DK, (g + 1) * GLA_DK), slice(g * GLA_DV, (g + 1) * GLA_DV))


def _gla_head(q_ref, k_ref, v_ref, r_ref, bc_ref, on_ref, o_ref, kcols, vcols):
    c = GLA_CHUNK
    t = GLA_BLOCK
    dk = GLA_DK
    seq = q_ref.shape[0]
    st = jnp.zeros((GLA_DV, dk), F32)
    row = lax.broadcasted_iota(jnp.int32, (t, t), 0)
    col = lax.broadcasted_iota(jnp.int32, (t, t), 1)
    intra = (row >= col) & ((row // c) == (col // c))
    o_gain = on_ref[:, vcols]

    for n in range(seq // t):
        rows = slice(n * t, (n + 1) * t)
        bc = bc_ref[rows, kcols]
        qb = q_ref[rows, kcols] * (dk ** -0.5)
        kb = k_ref[rows, kcols]
        vb = v_ref[rows, vcols]
        chunks = [slice(i * c, (i + 1) * c) for i in range(t // c)]
        b_mid = jnp.concatenate(
            [jnp.broadcast_to(bc[s.start + c // 2:s.start + c // 2 + 1, :], (c, dk)) for s in chunks], axis=0)
        b_last = jnp.concatenate(
            [jnp.broadcast_to(bc[s.stop - 1:s.stop, :], (c, dk)) for s in chunks], axis=0)
        qd = (qb * jnp.exp(bc - b_mid)).astype(BF16)
        kd = (kb * jnp.exp(b_mid - bc)).astype(BF16)
        ku = (kb * jnp.exp(b_last - bc)).astype(BF16)
        qi = (qb * jnp.exp(bc)).astype(BF16)
        att = jnp.where(intra, _dot_nt(qd, kd), 0.0)
        o = _dot(att.astype(BF16), vb)
        inter = []
        for s in chunks:
            inter.append(_dot_nt(qi[s, :], st.astype(BF16)))
            st = st * jnp.exp(bc[s.stop - 1:s.stop, :]) + _dot_tn(vb[s, :], ku[s, :])
        o = _rms(o + jnp.concatenate(inter, axis=0), o_gain)
        o_ref[rows, vcols] = (o * _silu(r_ref[rows, vcols])).astype(o_ref.dtype)


def _gla(zf, zb, bc, o_norm, batch, seq, *, heads=GLA_HEADS_PER_STEP):
    m = zf.shape[0]
    dk, dv = heads * GLA_DK, heads * GLA_DV
    return _ring_call(
        functools.partial(_gla_kernel, heads=heads),
        grid=(batch, GLA_HEADS // heads),
        in_specs=[
            pl.BlockSpec((seq, dk), lambda b, h: (b, ZF_GQ * LANE // dk + h), **RING),
            pl.BlockSpec((seq, dk), lambda b, h: (b, ZF_GK * LANE // dk + h), **RING),
            pl.BlockSpec((seq, dv), lambda b, h: (b, ZB_GV * LANE // dv + h), **RING),
            pl.BlockSpec((seq, dv), lambda b, h: (b, ZF_GR * LANE // dv + h), **RING),
            pl.BlockSpec((seq, dk), lambda b, h: (b, h), **RING),
            pl.BlockSpec((1, dv), lambda b, h: (0, h)),
        ],
        out_specs=[pl.BlockSpec((seq, dv), lambda b, h: (b, h))],
        out_shape=[jax.ShapeDtypeStruct((m, GLA_VW), BF16)],
        args=(zf, zf, zb, zf, bc, o_norm.reshape(1, GLA_VW)),
        name="gla",
    )[0]


def _merge_kernel(u0_ref, v0_ref, u_ref, v_ref, lng_ref, lnb_ref, ws_ref, bs_ref, b_ref, c_ref,
                  ga_ref, gb_ref, gc_ref, wa_ref, wb_ref, wc_ref, wo_ref, h_ref, o_ref, ya_ref):
    i = pl.program_id(0)
    chunks = ya_ref.shape[1] // SGU_CHUNK

    @pl.when(i == 0)
    def _():
        _sgu_kernel(u0_ref, v0_ref, lng_ref, lnb_ref, ws_ref, bs_ref, ya_ref.at[0], chunks=chunks)

    slot = i % 2
    mix = (jax.nn.sigmoid(ga_ref[...]) * _dot(ya_ref[slot], wa_ref[...])
           + jax.nn.sigmoid(gb_ref[...]) * _dot(b_ref[...], wb_ref[...])
           + jax.nn.sigmoid(gc_ref[...]) * _dot(c_ref[...], wc_ref[...]))
    o_ref[...] = h_ref[...] + _dot(mix.astype(BF16), wo_ref[...])
    _sgu_kernel(u_ref, v_ref, lng_ref, lnb_ref, ws_ref, bs_ref, ya_ref.at[1 - slot], chunks=chunks)


def _merge(sgu_params, yb, yc, zf, wa, wb, wc, wo, h, layer, *, tm=MERGE_ROWS):
    m, d = h.shape
    kw = yb.shape[1]
    g0 = ZF_GATES * LANE // d
    last = m // tm - 1
    ln_g, ln_b, w_s, b_s = sgu_params
    branch = pl.BlockSpec((tm, kw), lambda i: (i, 0))
    resident = dict(pipeline_mode=pl.Buffered(1))
    wspec = pl.BlockSpec((None, kw, d), lambda i: (layer, 0, 0), **resident)
    return pl.pallas_call(
        _merge_kernel,
        grid=(m // tm,),
        in_specs=[
            pl.BlockSpec((tm, kw), lambda i: (0, ZF_SU * LANE // kw)),
            pl.BlockSpec((tm, kw), lambda i: (0, ZF_SV * LANE // kw)),
            pl.BlockSpec((tm, kw), lambda i: (jnp.minimum(i + 1, last), ZF_SU * LANE // kw)),
            pl.BlockSpec((tm, kw), lambda i: (jnp.minimum(i + 1, last), ZF_SV * LANE // kw)),
            pl.BlockSpec((1, kw), lambda i: (0, 0)),
            pl.BlockSpec((1, kw), lambda i: (0, 0)),
            pl.BlockSpec((SGU_GROUPS, SGU_CHUNK, SGU_CHUNK), lambda i: (0, 0, 0)),
            pl.BlockSpec((SGU_CHUNK, SGU_GROUPS), lambda i: (0, 0)),
            branch, branch,
            pl.BlockSpec((tm, d), lambda i: (i, g0)),
            pl.BlockSpec((tm, d), lambda i: (i, g0 + 1)),
            pl.BlockSpec((tm, d), lambda i: (i, g0 + 2)),
            wspec, wspec, wspec,
            pl.BlockSpec((None, d, d), lambda i: (layer, 0, 0), **resident),
            pl.BlockSpec((tm, d), lambda i: (i, 0)),
        ],
        out_specs=pl.BlockSpec((tm, d), lambda i: (i, 0)),
        out_shape=jax.ShapeDtypeStruct((m, d), F32),
        scratch_shapes=[pltpu.VMEM((2, tm, kw), BF16)],
        compiler_params=_params(("arbitrary",)),
        name="merge",
    )(zf, zf, zf, zf, ln_g.reshape(1, kw), ln_b.reshape(1, kw), w_s, jnp.swapaxes(b_s, 0, 1),
      yb, yc, zf, zf, zf, wa, wb, wc, wo, h)


def _xattn_kernel(h_ref, g_ref, wq_ref, kv_ref, wo_ref, o_ref):
    dh = XA_HEAD_DIM
    x = h_ref[...]
    n = _rms(x, g_ref[...]).astype(BF16)
    q = _dot(n, wq_ref[...].astype(BF16)) * (dh ** -0.5)
    outs = []
    for hd in range(XA_HEADS):
        qh = q[:, hd * dh:(hd + 1) * dh].astype(BF16)
        kh = kv_ref[:, hd * dh:(hd + 1) * dh].astype(BF16)
        vh = kv_ref[:, XA_WIDTH + hd * dh:XA_WIDTH + (hd + 1) * dh].astype(BF16)
        s = _dot_nt(qh, kh)
        e = jnp.exp(s - jnp.max(s, axis=-1, keepdims=True))
        p = e / jnp.sum(e, axis=-1, keepdims=True)
        outs.append(_dot(p.astype(BF16), vh))
    o = jnp.concatenate(outs, axis=1).astype(BF16)
    o_ref[...] = x + _dot(o, wo_ref[...].astype(BF16))


def _xattn(h, norm_g, wq, kv, wo, layer, seq, mem_len, *, tm=XATTN_ROWS):
    m, d = h.shape
    per_batch = seq // tm
    return _ring_call(
        _xattn_kernel,
        grid=(m // tm,),
        in_specs=[
            pl.BlockSpec((tm, d), lambda i: (i, 0), **RING),
            pl.BlockSpec((1, d), lambda i: (0, 0)),
            pl.BlockSpec((None, d, XA_WIDTH), lambda i: (layer, 0, 0), pipeline_mode=pl.Buffered(1)),
            pl.BlockSpec((mem_len, 2 * XA_WIDTH), lambda i: (i // per_batch, 0)),
            pl.BlockSpec((None, XA_WIDTH, d), lambda i: (layer, 0, 0), pipeline_mode=pl.Buffered(1)),
        ],
        out_specs=[pl.BlockSpec((tm, d), lambda i: (i, 0))],
        out_shape=[jax.ShapeDtypeStruct((m, d), F32)],
        args=(h, norm_g.reshape(1, d), wq, kv, wo),
        name="xattn",
    )[0]


CAST_BLOCK_BYTES = 6 * 1024 * 1024


def _cast_kernel(x_ref, o_ref):
    o_ref[...] = x_ref[...].astype(o_ref.dtype)


def _to_bf16(w, cols=None):
    lead = w.shape[:-1]
    c = w.shape[-1] if cols is None else cols
    w2 = w.reshape(-1, w.shape[-1])
    r = w2.shape[0]
    rows = 16
    while rows * 2 * c * 4 <= CAST_BLOCK_BYTES and r % (rows * 2) == 0:
        rows *= 2
    spec = pl.BlockSpec((rows, c), lambda i: (i, 0))
    out = pl.pallas_call(
        _cast_kernel,
        grid=(r // rows,),
        in_specs=[spec],
        out_specs=spec,
        out_shape=jax.ShapeDtypeStruct((r, c), BF16),
        compiler_params=_params(("parallel",)),
        name="cast",
    )(w2)
    return out.reshape(*lead, c)

W_FQ = 2 * SGU_WIDTH
W_FF = W_FQ + 3 * FOX_WIDTH
W_GQ = W_FF + FOX_HEADS
W_GV = W_GQ + 2 * GLA_KW
W_GA = W_GV + GLA_VW
W_GR = W_GA + GLA_GATE_RANK
W_END = W_GR + GLA_VW + 3 * D_MODEL


REPACK_CHUNK = 1024


def _repack_kernel(wt_ref, wf_ref, wb_ref, ws_ref):
    def move(dst_ref, dst, src, width):
        for off in range(0, width, REPACK_CHUNK):
            n = min(REPACK_CHUNK, width - off)
            dst_ref[:, dst + off:dst + off + n] = wt_ref[src + off:src + off + n, :].T.astype(BF16)

    move(wf_ref, 0, 0, W_FQ)
    move(wf_ref, W_FQ, W_GQ, 2 * GLA_KW)
    move(wf_ref, W_FQ + 2 * GLA_KW, W_GR, W_END - W_GR)
    move(wb_ref, 0, W_FQ, 3 * FOX_WIDTH)
    move(wb_ref, 3 * FOX_WIDTH, W_GV, GLA_VW)
    lane = lax.broadcasted_iota(jnp.int32, (wt_ref.shape[1], LANE), 1)
    ff = wt_ref[W_FF - SMALL_FF:W_FF - SMALL_FF + LANE, :].T
    ga = wt_ref[W_GA - SMALL_GA:W_GA - SMALL_GA + LANE, :].T
    side = jnp.where(lane < SMALL_FF + FOX_HEADS, ff,
                     jnp.where(lane < SMALL_GA + GLA_GATE_RANK, ga, 0.0))
    ws_ref[...] = side.astype(BF16)


def _repack_w_in(w, *, kb=REPACK_K_ROWS):
    nl, d, n = w.shape
    wt = jnp.swapaxes(w, 1, 2)

    def spec(width):
        return pl.BlockSpec((None, kb, width), lambda l, i: (l, i, 0))

    return pl.pallas_call(
        _repack_kernel,
        grid=(nl, d // kb),
        in_specs=[pl.BlockSpec((None, n, kb), lambda l, i: (l, 0, i))],
        out_specs=[spec(ZF_WIDTH), spec(ZB_WIDTH), spec(LANE)],
        out_shape=[jax.ShapeDtypeStruct((nl, d, width), BF16) for width in (ZF_WIDTH, ZB_WIDTH, LANE)],
        compiler_params=_params(("parallel", "parallel")),
        name="repack",
    )(wt)


def _embed_w_gate(w_gate):
    emb = jnp.zeros((LANE, GLA_KW), BF16)
    return emb.at[SMALL_GA:SMALL_GA + GLA_GATE_RANK, :].set(w_gate.astype(BF16))


def kernel(x, mem, ffn1_norm, ffn1_w_in, ffn1_w_out, mix_norm, w_in, sgu_ln_g, sgu_ln_b, sgu_w_s, sgu_b_s, fox_b_f, gla_w_gate, gla_b_gate, gla_o_norm, w_branch_a, w_branch_b, w_branch_c, w_out, xa_norm, mem_norm, xa_w_q, xa_w_kv, xa_w_o, ffn2_norm, ffn2_w_in, ffn2_w_out, final_norm):
    batch, seq, d = x.shape
    mem_len = mem.shape[1]
    h = x.reshape(batch * seq, d)
    mem2 = mem.reshape(batch * mem_len, d)
    ones_f = jnp.ones((1, ZF_WIDTH), F32)
    scale_b = jnp.ones((1, ZB_WIDTH), F32).at[:, ZB_FQ * LANE:ZB_FQ * LANE + FOX_WIDTH].set(
        FOX_HEAD_DIM ** -0.5)
    f1_gate, f2_gate = _to_bf16(ffn1_w_in, D_FF), _to_bf16(ffn2_w_in, D_FF)
    wa, wb, wc, wo = (_to_bf16(w) for w in (w_branch_a, w_branch_b, w_branch_c, w_out))
    w_f, w_b, w_small = _repack_w_in(w_in)
    for l in range(DEPTH):
        h = _ffn(h, ffn1_norm[l], f1_gate, ffn1_w_in, ffn1_w_out, l)

        zf, small = _proj(h, mix_norm[l], w_f, l, ones_f, F32, w_small)
        zb = _proj(h, mix_norm[l], w_b, l, scale_b, BF16)
        qa, ka = _fox_gate(small, fox_b_f[l], batch, seq)
        yb = _fox(zb, qa, ka, batch, seq)
        bc = _gla_decay(small, _embed_w_gate(gla_w_gate[l]), gla_b_gate[l])
        yc = _gla(zf, zb, bc, gla_o_norm[l], batch, seq)
        sgu_params = (sgu_ln_g[l], sgu_ln_b[l], sgu_w_s[l], sgu_b_s[l])
        h = _merge(sgu_params, yb, yc, zf, wa, wb, wc, wo, h, l)

        kv = _norm_proj(mem2, mem_norm[l], xa_w_kv, l)
        h = _xattn(h, xa_norm[l], xa_w_q, kv, xa_w_o, l, seq, mem_len)

        h = _ffn(h, ffn2_norm[l], f2_gate, ffn2_w_in, ffn2_w_out, l,
                 final_norm if l == DEPTH - 1 else None)
    return h.reshape(batch, seq, d)
```

```python
import functools

import jax
import jax.numpy as jnp
from jax import lax
from jax.experimental import pallas as pl
from jax.experimental.pallas import tpu as pltpu

F32 = jnp.float32
BF16 = jnp.bfloat16

D_MODEL = 2048
DEPTH = 2
D_FF = 5632
EPS = 1e-6

SGU_GROUPS = 4
SGU_GROUP_DIM = 256
SGU_WIDTH = 1024
SGU_CHUNK = 128

FOX_HEADS = 8
FOX_HEAD_DIM = 128
FOX_WIDTH = 1024

GLA_HEADS = 4
GLA_DK = 128
GLA_DV = 256
GLA_KW = 512
GLA_VW = 1024
GLA_GATE_RANK = 16
GLA_GATE_TAU = 16.0
GLA_CHUNK = 64

XA_HEADS = 4
XA_HEAD_DIM = 128
XA_WIDTH = 512

LANE = 128

ZF_SU, ZF_SV, ZF_GQ, ZF_GK, ZF_GR, ZF_GATES = 0, 8, 16, 20, 24, 32
ZF_WIDTH = 80 * LANE
ZB_FQ, ZB_FK, ZB_FV, ZB_GV = 0, 8, 16, 24
ZB_WIDTH = 32 * LANE
SMALL_FF = 0
SMALL_GA = 8

NEG_BIG = -1e30
VMEM_LIMIT = 63 * 1024 * 1024

FFN_ROWS, FFN_HIDDEN_TILE = 1024, 512
PROJ_ROWS, PROJ_COLS = 1024, 2048
MEM_KV_ROWS = 1024
FOX_TILE, FOX_HEADS_PER_STEP = 512, 2
GLA_DECAY_BLOCKS_PER_STEP = 4
GLA_HEADS_PER_STEP = 2
MERGE_ROWS = 256
XATTN_ROWS = 1024
REPACK_K_ROWS = 256


def _params(sem):
    return pltpu.CompilerParams(dimension_semantics=sem, vmem_limit_bytes=VMEM_LIMIT)


def _rms(x, g):
    return x * lax.rsqrt(jnp.mean(x * x, axis=-1, keepdims=True) + EPS) * g


def _log_sigmoid(x):
    return jnp.minimum(x, 0.0) - jnp.log1p(jnp.exp(-jnp.abs(x)))


def _gelu_tanh(x):
    c = 0.7978845608028654
    return x * (0.5 * (1.0 + jnp.tanh(c * (x + 0.044715 * (x * x * x)))))


def _silu(x):
    return x * jax.nn.sigmoid(x)


def _dot(a, b):
    return jnp.dot(a, b, preferred_element_type=F32)


def _dot_nt(a, b):
    return lax.dot_general(a, b, (((1,), (1,)), ((), ())), preferred_element_type=F32)


def _dot_tn(a, b):
    return lax.dot_general(a, b, (((0,), (0,)), ((), ())), preferred_element_type=F32)


def _split3(x):
    hi = x.astype(BF16)
    r = x - hi.astype(F32)
    lo = r.astype(BF16)
    lo2 = (r - lo.astype(F32)).astype(BF16)
    return hi, lo, lo2


def _dot_01(mat3, x):
    return _dot(mat3, jnp.concatenate(_split3(x), axis=0))


def _ffn_kernel(*refs, final):
    if final:
        x_ref, g_ref, wg_ref, wu_ref, wo_ref, fg_ref, o_ref, xn_ref = refs
    else:
        x_ref, g_ref, wg_ref, wu_ref, wo_ref, o_ref, xn_ref = refs
    j = pl.program_id(1)

    def tile(xn):
        gate = _dot(xn, wg_ref[...])
        up = _dot(xn, wu_ref[...].astype(BF16))
        act = (_silu(gate) * up * 0.5).astype(BF16)
        return _dot(act, wo_ref[...].astype(BF16))

    @pl.when(j == 0)
    def _():
        x = x_ref[...]
        xn = _rms(x, g_ref[...]).astype(BF16)
        xn_ref[...] = xn
        o_ref[...] = x + tile(xn)

    @pl.when(j > 0)
    def _():
        o_ref[...] += tile(xn_ref[...])

    if final:
        @pl.when(j == pl.num_programs(1) - 1)
        def _():
            o_ref[...] = _rms(o_ref[...], fg_ref[...])


def _ffn(h, norm_g, w_gate, w_in, w_out, layer, final_g=None, *, tm=FFN_ROWS, tf=FFN_HIDDEN_TILE):
    m, d = h.shape
    f = w_out.shape[1]
    nt = f // tf
    final = final_g is not None
    vec = pl.BlockSpec((1, d), lambda i, j: (0, 0))
    rows = pl.BlockSpec((tm, d), lambda i, j: (i, 0))
    in_specs = [
        rows, vec,
        pl.BlockSpec((d, tf), lambda i, j: (0, j)),
        pl.BlockSpec((None, d, tf), lambda i, j: (layer, 0, j + nt)),
        pl.BlockSpec((None, tf, d), lambda i, j: (layer, j, 0)),
    ]
    args = [h, norm_g.reshape(1, d), w_gate, w_in, w_out]
    if final:
        in_specs.append(vec)
        args.append(final_g.reshape(1, d))
    return pl.pallas_call(
        functools.partial(_ffn_kernel, final=final),
        grid=(m // tm, nt),
        in_specs=in_specs,
        out_specs=rows,
        out_shape=jax.ShapeDtypeStruct((m, d), F32),
        scratch_shapes=[pltpu.VMEM((tm, d), BF16)],
        compiler_params=_params(("parallel", "arbitrary")),
        name="ffn",
    )(*args)


def _cast_rider(src, layer, cols, steps, step_of):
    rows = src.shape[1] // steps
    in_spec = pl.BlockSpec((None, rows, cols), lambda *g: (layer, step_of(*g), 0))
    out_spec = pl.BlockSpec((rows, cols), lambda *g: (step_of(*g), 0))
    return in_spec, out_spec, jax.ShapeDtypeStruct((src.shape[1], cols), BF16)


def _proj_kernel(*refs, small, rider):
    if rider:
        x_ref, g_ref, w_ref, cs_ref, src_ref, o_ref, dst_ref, xn_ref = refs
        dst_ref[...] = src_ref[...].astype(BF16)
    elif small:
        x_ref, g_ref, w_ref, cs_ref, ws_ref, o_ref, os_ref, xn_ref = refs
    else:
        x_ref, g_ref, w_ref, cs_ref, o_ref, xn_ref = refs
    j = pl.program_id(1)

    def project(xn):
        o_ref[...] = (_dot(xn, w_ref[...]) * cs_ref[...]).astype(o_ref.dtype)

    @pl.when(j == 0)
    def _():
        xn = _rms(x_ref[...], g_ref[...]).astype(BF16)
        xn_ref[...] = xn
        if small:
            os_ref[...] = _dot(xn, ws_ref[...])
        project(xn)

    @pl.when(j > 0)
    def _():
        project(xn_ref[...])


def _proj(x, norm_g, w, layer, col_scale, out_dtype, w_small=None, rider=None, *, tm=PROJ_ROWS, tn=PROJ_COLS):
    m, d = x.shape
    n = w.shape[2]
    small = w_small is not None
    assert not (small and rider)
    in_specs = [
        pl.BlockSpec((tm, d), lambda i, j: (i, 0)),
        pl.BlockSpec((1, d), lambda i, j: (0, 0)),
        pl.BlockSpec((None, d, tn), lambda i, j: (layer, 0, j)),
        pl.BlockSpec((1, tn), lambda i, j: (0, j)),
    ]
    args = [x, norm_g.reshape(1, d), w, col_scale]
    out_specs = [pl.BlockSpec((tm, tn), lambda i, j: (i, j))]
    out_shape = [jax.ShapeDtypeStruct((m, n), out_dtype)]
    if small:
        in_specs.append(pl.BlockSpec((None, d, LANE), lambda i, j: (layer, 0, 0)))
        args.append(w_small)
        out_specs.append(pl.BlockSpec((tm, LANE), lambda i, j: (i, 0)))
        out_shape.append(jax.ShapeDtypeStruct((m, LANE), F32))
    if rider:
        nj = n // tn
        r_in, r_out, r_shape = _cast_rider(*rider, steps=(m // tm) * nj, step_of=lambda i, j: i * nj + j)
        in_specs.append(r_in)
        args.append(rider[0])
        out_specs.append(r_out)
        out_shape.append(r_shape)
    out = pl.pallas_call(
        functools.partial(_proj_kernel, small=small, rider=rider is not None),
        grid=(m // tm, n // tn),
        in_specs=in_specs,
        out_specs=out_specs,
        out_shape=out_shape,
        scratch_shapes=[pltpu.VMEM((tm, d), BF16)],
        compiler_params=_params(("parallel", "arbitrary")),
        name="proj",
    )(*args)
    return out if small or rider else out[0]


def _norm_proj_kernel(x_ref, g_ref, w_ref, o_ref):
    o_ref[...] = _dot(_rms(x_ref[...], g_ref[...]).astype(BF16), w_ref[...].astype(BF16))


def _norm_proj(x, norm_g, w, layer, *, tm=MEM_KV_ROWS):
    m, d = x.shape
    n = w.shape[2]
    return pl.pallas_call(
        _norm_proj_kernel,
        grid=(m // tm,),
        in_specs=[
            pl.BlockSpec((tm, d), lambda i: (i, 0)),
            pl.BlockSpec((1, d), lambda i: (0, 0)),
            pl.BlockSpec((None, d, n), lambda i: (layer, 0, 0)),
        ],
        out_specs=pl.BlockSpec((tm, n), lambda i: (i, 0)),
        out_shape=jax.ShapeDtypeStruct((m, n), F32),
        compiler_params=_params(("parallel",)),
        name="mem_kv",
    )(x, norm_g.reshape(1, d), w)


def _sgu_kernel(u_ref, v_ref, lng_ref, lnb_ref, ws_ref, bs_ref, o_ref, *, chunks):
    t = SGU_CHUNK
    gd = SGU_GROUP_DIM
    row = lax.broadcasted_iota(jnp.int32, (t, t), 0)
    col = lax.broadcasted_iota(jnp.int32, (t, t), 1)
    causal = row >= col
    for g in range(SGU_GROUPS):
        w = jnp.where(causal, ws_ref[g], 0.0).astype(BF16)
        bias = bs_ref[:, g:g + 1]
        ln_g = lng_ref[:, g * gd:(g + 1) * gd]
        ln_b = lnb_ref[:, g * gd:(g + 1) * gd]
        for c in range(chunks):
            rows = slice(c * t, (c + 1) * t)
            cols = slice(g * gd, (g + 1) * gd)
            v = _gelu_tanh(v_ref[rows, cols])
            mu = jnp.mean(v, axis=-1, keepdims=True)
            vc = v - mu
            var = jnp.mean(vc * vc, axis=-1, keepdims=True)
            vn = vc * lax.rsqrt(var + EPS) * ln_g + ln_b
            mixed = _dot(w, vn.astype(BF16)) + bias
            o_ref[rows, cols] = (_gelu_tanh(u_ref[rows, cols]) * mixed).astype(o_ref.dtype)


FOX_AUG = 6


def _fox_gate_kernel(f_ref, bf_ref, tril_ref, pq_ref, pk_ref, cq_ref, ck_ref, qa_ref, ka_ref):
    t = LANE
    seq = f_ref.shape[0]
    carry = jnp.zeros((1, t), F32)
    for b in range(seq // t):
        rows = slice(b * t, (b + 1) * t)
        lf = _log_sigmoid(f_ref[rows, :] + bf_ref[...])
        cum = _dot_01(tril_ref[...], lf) + carry
        carry = cum[t - 1:t, :]
        parts = jnp.concatenate(_split3(cum), axis=1)
        qa_ref[rows, :] = (_dot(parts, pq_ref[...]) + cq_ref[...]).astype(BF16)
        ka_ref[rows, :] = (_dot(parts, pk_ref[...]) + ck_ref[...]).astype(BF16)


def _tril3(n):
    tril = jnp.tril(jnp.ones((n, n), BF16))
    return jnp.concatenate([tril, tril, tril], axis=1)


def _fox_placement():
    src = jnp.arange(3 * LANE)[:, None]
    dst = jnp.arange(LANE)[None, :]
    p, h = src // LANE, src % LANE
    valid = h < FOX_HEADS
    pq = jnp.where(valid & (dst == h * FOX_AUG + p), 1.0, 0.0).astype(BF16)
    pk = jnp.where(valid & (dst == h * FOX_AUG + 3 + p), -1.0, 0.0).astype(BF16)
    used = dst < FOX_HEADS * FOX_AUG
    cq = jnp.where(used & (dst % FOX_AUG >= 3), 1.0, 0.0).astype(F32)
    ck = jnp.where(used & (dst % FOX_AUG < 3), 1.0, 0.0).astype(F32)
    return pq, pk, cq, ck


def _fox_gate(small, b_f, batch, seq):
    bf = jnp.zeros((1, LANE), F32).at[0, SMALL_FF:SMALL_FF + FOX_HEADS].set(b_f)
    w = LANE
    aug = jax.ShapeDtypeStruct((batch * seq, w), BF16)

    def const(shape):
        return pl.BlockSpec(shape, lambda b: (0, 0))

    return pl.pallas_call(
        _fox_gate_kernel,
        grid=(batch,),
        in_specs=[
            pl.BlockSpec((seq, LANE), lambda b: (b, 0)),
            const((1, LANE)), const((LANE, 3 * LANE)),
            const((3 * LANE, w)), const((3 * LANE, w)), const((1, w)), const((1, w)),
        ],
        out_specs=[pl.BlockSpec((seq, w), lambda b: (b, 0))] * 2,
        out_shape=[aug, aug],
        compiler_params=_params(("parallel",)),
        name="fox_gate",
    )(small, bf, _tril3(LANE), *_fox_placement())


def _fox_kernel(q_ref, k_ref, v_ref, qa_ref, ka_ref, o_ref, *, t, heads):
    seq = q_ref.shape[0]
    dh = FOX_HEAD_DIM
    lane = lax.broadcasted_iota(jnp.int32, (t, LANE), 1)
    ones_col = jnp.where(lane == 0, 1.0, 0.0).astype(BF16)
    row = lax.broadcasted_iota(jnp.int32, (t, t), 0)
    col = lax.broadcasted_iota(jnp.int32, (t, t), 1)
    causal = row >= col

    for g in range(heads):
        head = pl.program_id(1) * heads + g
        own = (lane >= head * FOX_AUG) & (lane < (head + 1) * FOX_AUG)
        cols = slice(g * dh, (g + 1) * dh)

        def k_block(j):
            rows = slice(j * t, (j + 1) * t)
            ka = jnp.where(own, ka_ref[rows, :], jnp.zeros((), BF16))
            return (jnp.concatenate([k_ref[rows, cols], ka], axis=1),
                    jnp.concatenate([v_ref[rows, cols], ones_col], axis=1))

        for i in range(seq // t):
            rows = slice(i * t, (i + 1) * t)
            qc = jnp.concatenate([q_ref[rows, cols], qa_ref[rows, :]], axis=1)
            kc, vc = k_block(i)
            s = jnp.where(causal, _dot_nt(qc, kc), NEG_BIG)
            m = jnp.max(s, axis=1, keepdims=True)
            acc = _dot(jnp.exp(s - m).astype(BF16), vc)
            for j in range(i):
                kc, vc = k_block(j)
                s = _dot_nt(qc, kc)
                m_new = jnp.maximum(m, jnp.max(s, axis=1, keepdims=True))
                acc = jnp.exp(m - m_new) * acc + _dot(jnp.exp(s - m_new).astype(BF16), vc)
                m = m_new
            o_ref[rows, cols] = (acc[:, :dh] / acc[:, dh:dh + 1]).astype(o_ref.dtype)


def _fox(zb, qa, ka, batch, seq, *, t=FOX_TILE, heads=FOX_HEADS_PER_STEP):
    m = zb.shape[0]
    w = heads * FOX_HEAD_DIM

    def group(seg):
        return pl.BlockSpec((seq, w), lambda b, h: (b, seg * LANE // w + h))

    aug = pl.BlockSpec((seq, LANE), lambda b, h: (b, 0))
    return pl.pallas_call(
        functools.partial(_fox_kernel, t=t, heads=heads),
        grid=(batch, FOX_HEADS // heads),
        in_specs=[group(ZB_FQ), group(ZB_FK), group(ZB_FV), aug, aug],
        out_specs=group(0),
        out_shape=jax.ShapeDtypeStruct((m, FOX_WIDTH), BF16),
        compiler_params=_params(("parallel", "parallel")),
        name="fox",
    )(zb, zb, zb, qa, ka)


GLA_BLOCK = 4 * GLA_CHUNK


def _gla_decay_kernel(a_ref, wg_ref, bg_ref, cm_ref, o_ref, *, blocks):
    t = GLA_BLOCK
    for b in range(blocks):
        rows = slice(b * t, (b + 1) * t)
        gl = _dot(a_ref[rows, :].astype(BF16), wg_ref[...]) + bg_ref[...]
        g = _log_sigmoid(gl) / GLA_GATE_TAU
        o_ref[rows, :] = _dot_01(cm_ref[...], g)


def _gla_decay(small, w_gate_emb, b_gate, *, blocks=GLA_DECAY_BLOCKS_PER_STEP):
    m = small.shape[0]
    t = GLA_BLOCK
    rows = blocks * t
    idx = jnp.arange(t)
    same = (idx[:, None] // GLA_CHUNK) == (idx[None, :] // GLA_CHUNK)
    cm = jnp.where(same & (idx[:, None] >= idx[None, :]), 1.0, 0.0).astype(BF16)
    cm3 = jnp.concatenate([cm, cm, cm], axis=1)
    return pl.pallas_call(
        functools.partial(_gla_decay_kernel, blocks=blocks),
        grid=(m // rows,),
        in_specs=[
            pl.BlockSpec((rows, LANE), lambda i: (i, 0)),
            pl.BlockSpec((LANE, GLA_KW), lambda i: (0, 0)),
            pl.BlockSpec((1, GLA_KW), lambda i: (0, 0)),
            pl.BlockSpec((t, 3 * t), lambda i: (0, 0)),
        ],
        out_specs=pl.BlockSpec((rows, GLA_KW), lambda i: (i, 0)),
        out_shape=jax.ShapeDtypeStruct((m, GLA_KW), F32),
        compiler_params=_params(("parallel",)),
        name="gla_decay",
    )(small, w_gate_emb, b_gate.reshape(1, GLA_KW), cm3)


def _gla_kernel(q_ref, k_ref, v_ref, r_ref, bc_ref, on_ref, o_ref, *, heads):
    for g in range(heads):
        _gla_head(q_ref, k_ref, v_ref, r_ref, bc_ref, on_ref, o_ref,
                  slice(g * GLA_DK, (g + 1) * GLA_DK), slice(g * GLA_DV, (g + 1) * GLA_DV))


def _gla_head(q_ref, k_ref, v_ref, r_ref, bc_ref, on_ref, o_ref, kcols, vcols):
    c = GLA_CHUNK
    t = GLA_BLOCK
    dk = GLA_DK
    seq = q_ref.shape[0]
    st = jnp.zeros((GLA_DV, dk), F32)
    row = lax.broadcasted_iota(jnp.int32, (t, t), 0)
    col = lax.broadcasted_iota(jnp.int32, (t, t), 1)
    intra = (row >= col) & ((row // c) == (col // c))
    o_gain = on_ref[:, vcols]

    for n in range(seq // t):
        rows = slice(n * t, (n + 1) * t)
        bc = bc_ref[rows, kcols]
        qb = q_ref[rows, kcols] * (dk ** -0.5)
        kb = k_ref[rows, kcols]
        vb = v_ref[rows, vcols]
        chunks = [slice(i * c, (i + 1) * c) for i in range(t // c)]
        b_mid = jnp.concatenate(
            [jnp.broadcast_to(bc[s.start + c // 2:s.start + c // 2 + 1, :], (c, dk)) for s in chunks], axis=0)
        b_last = jnp.concatenate(
            [jnp.broadcast_to(bc[s.stop - 1:s.stop, :], (c, dk)) for s in chunks], axis=0)
        qd = (qb * jnp.exp(bc - b_mid)).astype(BF16)
        kd = (kb * jnp.exp(b_mid - bc)).astype(BF16)
        ku = (kb * jnp.exp(b_last - bc)).astype(BF16)
        qi = (qb * jnp.exp(bc)).astype(BF16)
        att = jnp.where(intra, _dot_nt(qd, kd), 0.0)
        o = _dot(att.astype(BF16), vb)
        inter = []
        for s in chunks:
            inter.append(_dot_nt(qi[s, :], st.astype(BF16)))
            st = st * jnp.exp(bc[s.stop - 1:s.stop, :]) + _dot_tn(vb[s, :], ku[s, :])
        o = _rms(o + jnp.concatenate(inter, axis=0), o_gain)
        o_ref[rows, vcols] = (o * _silu(r_ref[rows, vcols])).astype(o_ref.dtype)


def _gla(zf, zb, bc, o_norm, batch, seq, *, heads=GLA_HEADS_PER_STEP):
    m = zf.shape[0]
    dk, dv = heads * GLA_DK, heads * GLA_DV
    return pl.pallas_call(
        functools.partial(_gla_kernel, heads=heads),
        grid=(batch, GLA_HEADS // heads),
        in_specs=[
            pl.BlockSpec((seq, dk), lambda b, h: (b, ZF_GQ * LANE // dk + h)),
            pl.BlockSpec((seq, dk), lambda b, h: (b, ZF_GK * LANE // dk + h)),
            pl.BlockSpec((seq, dv), lambda b, h: (b, ZB_GV * LANE // dv + h)),
            pl.BlockSpec((seq, dv), lambda b, h: (b, ZF_GR * LANE // dv + h)),
            pl.BlockSpec((seq, dk), lambda b, h: (b, h)),
            pl.BlockSpec((1, dv), lambda b, h: (0, h)),
        ],
        out_specs=pl.BlockSpec((seq, dv), lambda b, h: (b, h)),
        out_shape=jax.ShapeDtypeStruct((m, GLA_VW), BF16),
        compiler_params=_params(("parallel", "parallel")),
        name="gla",
    )(zf, zf, zb, zf, bc, o_norm.reshape(1, GLA_VW))


def _merge_kernel(u0_ref, v0_ref, u_ref, v_ref, lng_ref, lnb_ref, ws_ref, bs_ref, b_ref, c_ref,
                  ga_ref, gb_ref, gc_ref, wa_ref, wb_ref, wc_ref, wo_ref, h_ref, o_ref, ya_ref):
    i = pl.program_id(0)
    chunks = ya_ref.shape[1] // SGU_CHUNK

    @pl.when(i == 0)
    def _():
        _sgu_kernel(u0_ref, v0_ref, lng_ref, lnb_ref, ws_ref, bs_ref, ya_ref.at[0], chunks=chunks)

    slot = i % 2
    mix = (jax.nn.sigmoid(ga_ref[...]) * _dot(ya_ref[slot], wa_ref[...])
           + jax.nn.sigmoid(gb_ref[...]) * _dot(b_ref[...], wb_ref[...])
           + jax.nn.sigmoid(gc_ref[...]) * _dot(c_ref[...], wc_ref[...]))
    o_ref[...] = h_ref[...] + _dot(mix.astype(BF16), wo_ref[...])
    _sgu_kernel(u_ref, v_ref, lng_ref, lnb_ref, ws_ref, bs_ref, ya_ref.at[1 - slot], chunks=chunks)


def _merge(sgu_params, yb, yc, zf, wa, wb, wc, wo, h, layer, *, tm=MERGE_ROWS):
    m, d = h.shape
    kw = yb.shape[1]
    g0 = ZF_GATES * LANE // d
    last = m // tm - 1
    ln_g, ln_b, w_s, b_s = sgu_params
    branch = pl.BlockSpec((tm, kw), lambda i: (i, 0))
    resident = dict(pipeline_mode=pl.Buffered(1))
    wspec = pl.BlockSpec((None, kw, d), lambda i: (layer, 0, 0), **resident)
    return pl.pallas_call(
        _merge_kernel,
        grid=(m // tm,),
        in_specs=[
            pl.BlockSpec((tm, kw), lambda i: (0, ZF_SU * LANE // kw)),
            pl.BlockSpec((tm, kw), lambda i: (0, ZF_SV * LANE // kw)),
            pl.BlockSpec((tm, kw), lambda i: (jnp.minimum(i + 1, last), ZF_SU * LANE // kw)),
            pl.BlockSpec((tm, kw), lambda i: (jnp.minimum(i + 1, last), ZF_SV * LANE // kw)),
            pl.BlockSpec((1, kw), lambda i: (0, 0)),
            pl.BlockSpec((1, kw), lambda i: (0, 0)),
            pl.BlockSpec((SGU_GROUPS, SGU_CHUNK, SGU_CHUNK), lambda i: (0, 0, 0)),
            pl.BlockSpec((SGU_CHUNK, SGU_GROUPS), lambda i: (0, 0)),
            branch, branch,
            pl.BlockSpec((tm, d), lambda i: (i, g0)),
            pl.BlockSpec((tm, d), lambda i: (i, g0 + 1)),
            pl.BlockSpec((tm, d), lambda i: (i, g0 + 2)),
            wspec, wspec, wspec,
            pl.BlockSpec((None, d, d), lambda i: (layer, 0, 0), **resident),
            pl.BlockSpec((tm, d), lambda i: (i, 0)),
        ],
        out_specs=pl.BlockSpec((tm, d), lambda i: (i, 0)),
        out_shape=jax.ShapeDtypeStruct((m, d), F32),
        scratch_shapes=[pltpu.VMEM((2, tm, kw), BF16)],
        compiler_params=_params(("arbitrary",)),
        name="merge",
    )(zf, zf, zf, zf, ln_g.reshape(1, kw), ln_b.reshape(1, kw), w_s, jnp.swapaxes(b_s, 0, 1),
      yb, yc, zf, zf, zf, wa, wb, wc, wo, h)


def _xattn_kernel(*refs, rider):
    if rider:
        h_ref, g_ref, wq_ref, kv_ref, wo_ref, src_ref, o_ref, dst_ref = refs
        dst_ref[...] = src_ref[...].astype(BF16)
    else:
        h_ref, g_ref, wq_ref, kv_ref, wo_ref, o_ref = refs
    dh = XA_HEAD_DIM
    x = h_ref[...]
    n = _rms(x, g_ref[...]).astype(BF16)
    q = _dot(n, wq_ref[...].astype(BF16)) * (dh ** -0.5)
    outs = []
    for hd in range(XA_HEADS):
        qh = q[:, hd * dh:(hd + 1) * dh].astype(BF16)
        kh = kv_ref[:, hd * dh:(hd + 1) * dh].astype(BF16)
        vh = kv_ref[:, XA_WIDTH + hd * dh:XA_WIDTH + (hd + 1) * dh].astype(BF16)
        s = _dot_nt(qh, kh)
        e = jnp.exp(s - jnp.max(s, axis=-1, keepdims=True))
        p = e / jnp.sum(e, axis=-1, keepdims=True)
        outs.append(_dot(p.astype(BF16), vh))
    o = jnp.concatenate(outs, axis=1).astype(BF16)
    o_ref[...] = x + _dot(o, wo_ref[...].astype(BF16))


def _xattn(h, norm_g, wq, kv, wo, layer, seq, mem_len, rider=None, *, tm=XATTN_ROWS):
    m, d = h.shape
    per_batch = seq // tm
    in_specs = [
        pl.BlockSpec((tm, d), lambda i: (i, 0)),
        pl.BlockSpec((1, d), lambda i: (0, 0)),
        pl.BlockSpec((None, d, XA_WIDTH), lambda i: (layer, 0, 0)),
        pl.BlockSpec((mem_len, 2 * XA_WIDTH), lambda i: (i // per_batch, 0)),
        pl.BlockSpec((None, XA_WIDTH, d), lambda i: (layer, 0, 0)),
    ]
    args = [h, norm_g.reshape(1, d), wq, kv, wo]
    out_specs = [pl.BlockSpec((tm, d), lambda i: (i, 0))]
    out_shape = [jax.ShapeDtypeStruct((m, d), F32)]
    if rider:
        r_in, r_out, r_shape = _cast_rider(*rider, steps=m // tm, step_of=lambda i: i)
        in_specs.append(r_in)
        args.append(rider[0])
        out_specs.append(r_out)
        out_shape.append(r_shape)
    out = pl.pallas_call(
        functools.partial(_xattn_kernel, rider=rider is not None),
        grid=(m // tm,),
        in_specs=in_specs,
        out_specs=out_specs,
        out_shape=out_shape,
        compiler_params=_params(("parallel",)),
        name="xattn",
    )(*args)
    return out if rider else out[0]


CAST_BLOCK_BYTES = 6 * 1024 * 1024


def _cast_kernel(x_ref, o_ref):
    o_ref[...] = x_ref[...].astype(o_ref.dtype)


def _to_bf16(w, cols=None, layers=None):
    lead = w.shape[:-1] if layers is None else (layers,) + w.shape[1:-1]
    c = w.shape[-1] if cols is None else cols
    w2 = w.reshape(-1, w.shape[-1])
    r = w2.shape[0] if layers is None else layers * (w2.shape[0] // w.shape[0])
    rows = 16
    while rows * 2 * c * 4 <= CAST_BLOCK_BYTES and r % (rows * 2) == 0:
        rows *= 2
    spec = pl.BlockSpec((rows, c), lambda i: (i, 0))
    out = pl.pallas_call(
        _cast_kernel,
        grid=(r // rows,),
        in_specs=[spec],
        out_specs=spec,
        out_shape=jax.ShapeDtypeStruct((r, c), BF16),
        compiler_params=_params(("parallel",)),
        name="cast",
    )(w2)
    return out.reshape(*lead, c)

W_FQ = 2 * SGU_WIDTH
W_FF = W_FQ + 3 * FOX_WIDTH
W_GQ = W_FF + FOX_HEADS
W_GV = W_GQ + 2 * GLA_KW
W_GA = W_GV + GLA_VW
W_GR = W_GA + GLA_GATE_RANK
W_END = W_GR + GLA_VW + 3 * D_MODEL


REPACK_CHUNK = 1024


def _repack_kernel(wt_ref, wf_ref, wb_ref, ws_ref):
    def move(dst_ref, dst, src, width):
        for off in range(0, width, REPACK_CHUNK):
            n = min(REPACK_CHUNK, width - off)
            dst_ref[:, dst + off:dst + off + n] = wt_ref[src + off:src + off + n, :].T.astype(BF16)

    move(wf_ref, 0, 0, W_FQ)
    move(wf_ref, W_FQ, W_GQ, 2 * GLA_KW)
    move(wf_ref, W_FQ + 2 * GLA_KW, W_GR, W_END - W_GR)
    move(wb_ref, 0, W_FQ, 3 * FOX_WIDTH)
    move(wb_ref, 3 * FOX_WIDTH, W_GV, GLA_VW)
    lane = lax.broadcasted_iota(jnp.int32, (wt_ref.shape[1], LANE), 1)
    ff = wt_ref[W_FF - SMALL_FF:W_FF - SMALL_FF + LANE, :].T
    ga = wt_ref[W_GA - SMALL_GA:W_GA - SMALL_GA + LANE, :].T
    side = jnp.where(lane < SMALL_FF + FOX_HEADS, ff,
                     jnp.where(lane < SMALL_GA + GLA_GATE_RANK, ga, 0.0))
    ws_ref[...] = side.astype(BF16)


def _repack_w_in(w, *, kb=REPACK_K_ROWS):
    nl, d, n = w.shape
    wt = jnp.swapaxes(w, 1, 2)

    def spec(width):
        return pl.BlockSpec((None, kb, width), lambda l, i: (l, i, 0))

    return pl.pallas_call(
        _repack_kernel,
        grid=(nl, d // kb),
        in_specs=[pl.BlockSpec((None, n, kb), lambda l, i: (l, 0, i))],
        out_specs=[spec(ZF_WIDTH), spec(ZB_WIDTH), spec(LANE)],
        out_shape=[jax.ShapeDtypeStruct((nl, d, width), BF16) for width in (ZF_WIDTH, ZB_WIDTH, LANE)],
        compiler_params=_params(("parallel", "parallel")),
        name="repack",
    )(wt)


def _embed_w_gate(w_gate):
    emb = jnp.zeros((LANE, GLA_KW), BF16)
    return emb.at[SMALL_GA:SMALL_GA + GLA_GATE_RANK, :].set(w_gate.astype(BF16))


def kernel(x, mem, ffn1_norm, ffn1_w_in, ffn1_w_out, mix_norm, w_in, sgu_ln_g, sgu_ln_b, sgu_w_s, sgu_b_s, fox_b_f, gla_w_gate, gla_b_gate, gla_o_norm, w_branch_a, w_branch_b, w_branch_c, w_out, xa_norm, mem_norm, xa_w_q, xa_w_kv, xa_w_o, ffn2_norm, ffn2_w_in, ffn2_w_out, final_norm):
    batch, seq, d = x.shape
    mem_len = mem.shape[1]
    h = x.reshape(batch * seq, d)
    mem2 = mem.reshape(batch * mem_len, d)
    ones_f = jnp.ones((1, ZF_WIDTH), F32)
    scale_b = jnp.ones((1, ZB_WIDTH), F32).at[:, ZB_FQ * LANE:ZB_FQ * LANE + FOX_WIDTH].set(
        FOX_HEAD_DIM ** -0.5)
    f1_gate = _to_bf16(ffn1_w_in, D_FF, layers=1)[0]
    wa, wb, wc, wo = (_to_bf16(w) for w in (w_branch_a, w_branch_b, w_branch_c, w_out))
    w_f, w_b, w_small = _repack_w_in(w_in)
    for l in range(DEPTH):
        h = _ffn(h, ffn1_norm[l], f1_gate, ffn1_w_in, ffn1_w_out, l)

        zf, small = _proj(h, mix_norm[l], w_f, l, ones_f, F32, w_small)
        zb, f2_gate = _proj(h, mix_norm[l], w_b, l, scale_b, BF16, rider=(ffn2_w_in, l, D_FF))
        qa, ka = _fox_gate(small, fox_b_f[l], batch, seq)
        yb = _fox(zb, qa, ka, batch, seq)
        bc = _gla_decay(small, _embed_w_gate(gla_w_gate[l]), gla_b_gate[l])
        yc = _gla(zf, zb, bc, gla_o_norm[l], batch, seq)
        sgu_params = (sgu_ln_g[l], sgu_ln_b[l], sgu_w_s[l], sgu_b_s[l])
        h = _merge(sgu_params, yb, yc, zf, wa, wb, wc, wo, h, l)

        kv = _norm_proj(mem2, mem_norm[l], xa_w_kv, l)
        if l + 1 < DEPTH:
            h, f1_gate = _xattn(h, xa_norm[l], xa_w_q, kv, xa_w_o, l, seq, mem_len,
                                rider=(ffn1_w_in, l + 1, D_FF))
        else:
            h = _xattn(h, xa_norm[l], xa_w_q, kv, xa_w_o, l, seq, mem_len)

        h = _ffn(h, ffn2_norm[l], f2_gate, ffn2_w_in, ffn2_w_out, l,
                 final_norm if l == DEPTH - 1 else None)
    return h.reshape(batch, seq, d)
```

```python
import functools

import jax
import jax.numpy as jnp
from jax import lax
from jax.experimental import pallas as pl
from jax.experimental.pallas import tpu as pltpu

F32 = jnp.float32
BF16 = jnp.bfloat16

D_MODEL = 2048
DEPTH = 2
D_FF = 5632
EPS = 1e-6

SGU_GROUPS = 4
SGU_GROUP_DIM = 256
SGU_WIDTH = 1024
SGU_CHUNK = 128

FOX_HEADS = 8
FOX_HEAD_DIM = 128
FOX_WIDTH = 1024

GLA_HEADS = 4
GLA_DK = 128
GLA_DV = 256
GLA_KW = 512
GLA_VW = 1024
GLA_GATE_RANK = 16
GLA_GATE_TAU = 16.0
GLA_CHUNK = 64

XA_HEADS = 4
XA_HEAD_DIM = 128
XA_WIDTH = 512

LANE = 128

ZF_SU, ZF_SV, ZF_GQ, ZF_GK, ZF_GR, ZF_GATES = 0, 8, 16, 20, 24, 32
ZF_WIDTH = 80 * LANE
ZB_FQ, ZB_FK, ZB_FV, ZB_GV = 0, 8, 16, 24
ZB_WIDTH = 32 * LANE
SMALL_FF = 0
SMALL_GA = 8

NEG_BIG = -1e30
VMEM_LIMIT = 63 * 1024 * 1024

FFN_ROWS, FFN_HIDDEN_TILE = 1024, 512
PROJ_ROWS, PROJ_COLS = 1024, 2048
MEM_KV_ROWS = 1024
FOX_TILE, FOX_HEADS_PER_STEP = 512, 2
GLA_DECAY_BLOCKS_PER_STEP = 4
GLA_HEADS_PER_STEP = 2
MERGE_ROWS = 256
XATTN_ROWS = 1024
REPACK_K_ROWS = 256


def _params(sem):
    return pltpu.CompilerParams(dimension_semantics=sem, vmem_limit_bytes=VMEM_LIMIT)


def _rms(x, g):
    return x * lax.rsqrt(jnp.mean(x * x, axis=-1, keepdims=True) + EPS) * g


def _log_sigmoid(x):
    return jnp.minimum(x, 0.0) - jnp.log1p(jnp.exp(-jnp.abs(x)))


def _gelu_tanh(x):
    c = 0.7978845608028654
    return x * (0.5 * (1.0 + jnp.tanh(c * (x + 0.044715 * (x * x * x)))))


def _silu(x):
    return x * jax.nn.sigmoid(x)


def _dot(a, b):
    return jnp.dot(a, b, preferred_element_type=F32)


def _dot_nt(a, b):
    return lax.dot_general(a, b, (((1,), (1,)), ((), ())), preferred_element_type=F32)


def _dot_tn(a, b):
    return lax.dot_general(a, b, (((0,), (0,)), ((), ())), preferred_element_type=F32)


def _split3(x):
    hi = x.astype(BF16)
    r = x - hi.astype(F32)
    lo = r.astype(BF16)
    lo2 = (r - lo.astype(F32)).astype(BF16)
    return hi, lo, lo2


def _dot_01(mat3, x):
    return _dot(mat3, jnp.concatenate(_split3(x), axis=0))


def _ffn_kernel(*refs, final):
    if final:
        x_ref, g_ref, wg_ref, wu_ref, wo_ref, fg_ref, o_ref, xn_ref = refs
    else:
        x_ref, g_ref, wg_ref, wu_ref, wo_ref, o_ref, xn_ref = refs
    j = pl.program_id(1)

    def tile(xn):
        gate = _dot(xn, wg_ref[...])
        up = _dot(xn, wu_ref[...].astype(BF16))
        act = (_silu(gate) * up * 0.5).astype(BF16)
        return _dot(act, wo_ref[...].astype(BF16))

    @pl.when(j == 0)
    def _():
        x = x_ref[...]
        xn = _rms(x, g_ref[...]).astype(BF16)
        xn_ref[...] = xn
        o_ref[...] = x + tile(xn)

    @pl.when(j > 0)
    def _():
        o_ref[...] += tile(xn_ref[...])

    if final:
        @pl.when(j == pl.num_programs(1) - 1)
        def _():
            o_ref[...] = _rms(o_ref[...], fg_ref[...])


def _ffn(h, norm_g, w_gate, w_in, w_out, layer, final_g=None, *, tm=FFN_ROWS, tf=FFN_HIDDEN_TILE):
    m, d = h.shape
    f = w_out.shape[1]
    nt = f // tf
    final = final_g is not None
    vec = pl.BlockSpec((1, d), lambda i, j: (0, 0))
    rows = pl.BlockSpec((tm, d), lambda i, j: (i, 0))
    in_specs = [
        rows, vec,
        pl.BlockSpec((d, tf), lambda i, j: (0, j)),
        pl.BlockSpec((None, d, tf), lambda i, j: (layer, 0, j + nt)),
        pl.BlockSpec((None, tf, d), lambda i, j: (layer, j, 0)),
    ]
    args = [h, norm_g.reshape(1, d), w_gate, w_in, w_out]
    if final:
        in_specs.append(vec)
        args.append(final_g.reshape(1, d))
    return pl.pallas_call(
        functools.partial(_ffn_kernel, final=final),
        grid=(m // tm, nt),
        in_specs=in_specs,
        out_specs=rows,
        out_shape=jax.ShapeDtypeStruct((m, d), F32),
        scratch_shapes=[pltpu.VMEM((tm, d), BF16)],
        compiler_params=_params(("parallel", "arbitrary")),
        name="ffn",
    )(*args)


def _cast_rider(src, layer, cols, steps, step_of):
    rows = src.shape[1] // steps
    in_spec = pl.BlockSpec((None, rows, cols), lambda *g: (layer, step_of(*g), 0))
    out_spec = pl.BlockSpec((rows, cols), lambda *g: (step_of(*g), 0))
    return in_spec, out_spec, jax.ShapeDtypeStruct((src.shape[1], cols), BF16)


def _proj_kernel(*refs, small, rider):
    if rider:
        x_ref, g_ref, w_ref, cs_ref, src_ref, o_ref, dst_ref, xn_ref = refs
        dst_ref[...] = src_ref[...].astype(BF16)
    elif small:
        x_ref, g_ref, w_ref, cs_ref, ws_ref, o_ref, os_ref, xn_ref = refs
    else:
        x_ref, g_ref, w_ref, cs_ref, o_ref, xn_ref = refs
    j = pl.program_id(1)

    def project(xn):
        o_ref[...] = (_dot(xn, w_ref[...]) * cs_ref[...]).astype(o_ref.dtype)

    @pl.when(j == 0)
    def _():
        xn = _rms(x_ref[...], g_ref[...]).astype(BF16)
        xn_ref[...] = xn
        if small:
            os_ref[...] = _dot(xn, ws_ref[...])
        project(xn)

    @pl.when(j > 0)
    def _():
        project(xn_ref[...])


def _proj(x, norm_g, w, layer, col_scale, out_dtype, w_small=None, rider=None, *, tm=PROJ_ROWS, tn=PROJ_COLS):
    m, d = x.shape
    n = w.shape[2]
    small = w_small is not None
    assert not (small and rider)
    in_specs = [
        pl.BlockSpec((tm, d), lambda i, j: (i, 0)),
        pl.BlockSpec((1, d), lambda i, j: (0, 0)),
        pl.BlockSpec((None, d, tn), lambda i, j: (layer, 0, j)),
        pl.BlockSpec((1, tn), lambda i, j: (0, j)),
    ]
    args = [x, norm_g.reshape(1, d), w, col_scale]
    out_specs = [pl.BlockSpec((tm, tn), lambda i, j: (i, j))]
    out_shape = [jax.ShapeDtypeStruct((m, n), out_dtype)]
    if small:
        in_specs.append(pl.BlockSpec((None, d, LANE), lambda i, j: (layer, 0, 0)))
        args.append(w_small)
        out_specs.append(pl.BlockSpec((tm, LANE), lambda i, j: (i, 0)))
        out_shape.append(jax.ShapeDtypeStruct((m, LANE), F32))
    if rider:
        nj = n // tn
        r_in, r_out, r_shape = _cast_rider(*rider, steps=(m // tm) * nj, step_of=lambda i, j: i * nj + j)
        in_specs.append(r_in)
        args.append(rider[0])
        out_specs.append(r_out)
        out_shape.append(r_shape)
    out = pl.pallas_call(
        functools.partial(_proj_kernel, small=small, rider=rider is not None),
        grid=(m // tm, n // tn),
        in_specs=in_specs,
        out_specs=out_specs,
        out_shape=out_shape,
        scratch_shapes=[pltpu.VMEM((tm, d), BF16)],
        compiler_params=_params(("parallel", "arbitrary")),
        name="proj",
    )(*args)
    return out if small or rider else out[0]


def _norm_proj_kernel(x_ref, g_ref, w_ref, o_ref):
    o_ref[...] = _dot(_rms(x_ref[...], g_ref[...]).astype(BF16), w_ref[...].astype(BF16))


def _norm_proj(x, norm_g, w, layer, *, tm=MEM_KV_ROWS):
    m, d = x.shape
    n = w.shape[2]
    return pl.pallas_call(
        _norm_proj_kernel,
        grid=(m // tm,),
        in_specs=[
            pl.BlockSpec((tm, d), lambda i: (i, 0)),
            pl.BlockSpec((1, d), lambda i: (0, 0)),
            pl.BlockSpec((None, d, n), lambda i: (layer, 0, 0)),
        ],
        out_specs=pl.BlockSpec((tm, n), lambda i: (i, 0)),
        out_shape=jax.ShapeDtypeStruct((m, n), F32),
        compiler_params=_params(("parallel",)),
        name="mem_kv",
    )(x, norm_g.reshape(1, d), w)


def _sgu_kernel(u_ref, v_ref, lng_ref, lnb_ref, ws_ref, bs_ref, o_ref, *, chunks):
    t = SGU_CHUNK
    gd = SGU_GROUP_DIM
    row = lax.broadcasted_iota(jnp.int32, (t, t), 0)
    col = lax.broadcasted_iota(jnp.int32, (t, t), 1)
    causal = row >= col
    for g in range(SGU_GROUPS):
        w = jnp.where(causal, ws_ref[g], 0.0).astype(BF16)
        bias = bs_ref[:, g:g + 1]
        ln_g = lng_ref[:, g * gd:(g + 1) * gd]
        ln_b = lnb_ref[:, g * gd:(g + 1) * gd]
        for c in range(chunks):
            rows = slice(c * t, (c + 1) * t)
            cols = slice(g * gd, (g + 1) * gd)
            v = _gelu_tanh(v_ref[rows, cols])
            mu = jnp.mean(v, axis=-1, keepdims=True)
            vc = v - mu
            var = jnp.mean(vc * vc, axis=-1, keepdims=True)
            vn = vc * lax.rsqrt(var + EPS) * ln_g + ln_b
            mixed = _dot(w, vn.astype(BF16)) + bias
            o_ref[rows, cols] = (_gelu_tanh(u_ref[rows, cols]) * mixed).astype(o_ref.dtype)


FOX_AUG = 6


def _fox_gate_kernel(f_ref, bf_ref, tril_ref, pq_ref, pk_ref, cq_ref, ck_ref, qa_ref, ka_ref):
    t = LANE
    seq = f_ref.shape[0]
    carry = jnp.zeros((1, t), F32)
    for b in range(seq // t):
        rows = slice(b * t, (b + 1) * t)
        lf = _log_sigmoid(f_ref[rows, :] + bf_ref[...])
        cum = _dot_01(tril_ref[...], lf) + carry
        carry = cum[t - 1:t, :]
        parts = jnp.concatenate(_split3(cum), axis=1)
        qa_ref[rows, :] = (_dot(parts, pq_ref[...]) + cq_ref[...]).astype(BF16)
        ka_ref[rows, :] = (_dot(parts, pk_ref[...]) + ck_ref[...]).astype(BF16)


def _tril3(n):
    tril = jnp.tril(jnp.ones((n, n), BF16))
    return jnp.concatenate([tril, tril, tril], axis=1)


def _fox_placement():
    src = jnp.arange(3 * LANE)[:, None]
    dst = jnp.arange(LANE)[None, :]
    p, h = src // LANE, src % LANE
    valid = h < FOX_HEADS
    pq = jnp.where(valid & (dst == h * FOX_AUG + p), 1.0, 0.0).astype(BF16)
    pk = jnp.where(valid & (dst == h * FOX_AUG + 3 + p), -1.0, 0.0).astype(BF16)
    used = dst < FOX_HEADS * FOX_AUG
    cq = jnp.where(used & (dst % FOX_AUG >= 3), 1.0, 0.0).astype(F32)
    ck = jnp.where(used & (dst % FOX_AUG < 3), 1.0, 0.0).astype(F32)
    return pq, pk, cq, ck


def _fox_gate(small, b_f, batch, seq):
    bf = jnp.zeros((1, LANE), F32).at[0, SMALL_FF:SMALL_FF + FOX_HEADS].set(b_f)
    w = LANE
    aug = jax.ShapeDtypeStruct((batch * seq, w), BF16)

    def const(shape):
        return pl.BlockSpec(shape, lambda b: (0, 0))

    return pl.pallas_call(
        _fox_gate_kernel,
        grid=(batch,),
        in_specs=[
            pl.BlockSpec((seq, LANE), lambda b: (b, 0)),
            const((1, LANE)), const((LANE, 3 * LANE)),
            const((3 * LANE, w)), const((3 * LANE, w)), const((1, w)), const((1, w)),
        ],
        out_specs=[pl.BlockSpec((seq, w), lambda b: (b, 0))] * 2,
        out_shape=[aug, aug],
        compiler_params=_params(("parallel",)),
        name="fox_gate",
    )(small, bf, _tril3(LANE), *_fox_placement())


def _fox_kernel(*refs, t, heads, riders):
    q_ref, k_ref, v_ref, qa_ref, ka_ref = refs[:5]
    o_ref = refs[5 + riders]
    for src_ref, dst_ref in zip(refs[5:5 + riders], refs[6 + riders:]):
        dst_ref[...] = src_ref[...].astype(BF16)
    seq = q_ref.shape[0]
    dh = FOX_HEAD_DIM
    lane = lax.broadcasted_iota(jnp.int32, (t, LANE), 1)
    ones_col = jnp.where(lane == 0, 1.0, 0.0).astype(BF16)
    row = lax.broadcasted_iota(jnp.int32, (t, t), 0)
    col = lax.broadcasted_iota(jnp.int32, (t, t), 1)
    causal = row >= col

    for g in range(heads):
        head = pl.program_id(1) * heads + g
        own = (lane >= head * FOX_AUG) & (lane < (head + 1) * FOX_AUG)
        cols = slice(g * dh, (g + 1) * dh)

        def k_block(j):
            rows = slice(j * t, (j + 1) * t)
            ka = jnp.where(own, ka_ref[rows, :], jnp.zeros((), BF16))
            return (jnp.concatenate([k_ref[rows, cols], ka], axis=1),
                    jnp.concatenate([v_ref[rows, cols], ones_col], axis=1))

        for i in range(seq // t):
            rows = slice(i * t, (i + 1) * t)
            qc = jnp.concatenate([q_ref[rows, cols], qa_ref[rows, :]], axis=1)
            kc, vc = k_block(i)
            s = jnp.where(causal, _dot_nt(qc, kc), NEG_BIG)
            m = jnp.max(s, axis=1, keepdims=True)
            acc = _dot(jnp.exp(s - m).astype(BF16), vc)
            for j in range(i):
                kc, vc = k_block(j)
                s = _dot_nt(qc, kc)
                m_new = jnp.maximum(m, jnp.max(s, axis=1, keepdims=True))
                acc = jnp.exp(m - m_new) * acc + _dot(jnp.exp(s - m_new).astype(BF16), vc)
                m = m_new
            o_ref[rows, cols] = (acc[:, :dh] / acc[:, dh:dh + 1]).astype(o_ref.dtype)


def _fox(zb, qa, ka, batch, seq, riders=(), *, t=FOX_TILE, heads=FOX_HEADS_PER_STEP):
    m = zb.shape[0]
    w = heads * FOX_HEAD_DIM
    groups = FOX_HEADS // heads

    def group(seg):
        return pl.BlockSpec((seq, w), lambda b, h: (b, seg * LANE // w + h))

    aug = pl.BlockSpec((seq, LANE), lambda b, h: (b, 0))
    rider_specs = [_cast_rider(*r, steps=batch * groups, step_of=lambda b, h: b * groups + h) for r in riders]
    out = pl.pallas_call(
        functools.partial(_fox_kernel, t=t, heads=heads, riders=len(riders)),
        grid=(batch, groups),
        in_specs=[group(ZB_FQ), group(ZB_FK), group(ZB_FV), aug, aug] + [r[0] for r in rider_specs],
        out_specs=[group(0)] + [r[1] for r in rider_specs],
        out_shape=[jax.ShapeDtypeStruct((m, FOX_WIDTH), BF16)] + [r[2] for r in rider_specs],
        compiler_params=_params(("parallel", "parallel")),
        name="fox",
    )(zb, zb, zb, qa, ka, *[r[0] for r in riders])
    return out if riders else out[0]


GLA_BLOCK = 4 * GLA_CHUNK


def _gla_decay_kernel(a_ref, wg_ref, bg_ref, cm_ref, o_ref, *, blocks):
    t = GLA_BLOCK
    for b in range(blocks):
        rows = slice(b * t, (b + 1) * t)
        gl = _dot(a_ref[rows, :].astype(BF16), wg_ref[...]) + bg_ref[...]
        g = _log_sigmoid(gl) / GLA_GATE_TAU
        o_ref[rows, :] = _dot_01(cm_ref[...], g)


def _gla_decay(small, w_gate_emb, b_gate, *, blocks=GLA_DECAY_BLOCKS_PER_STEP):
    m = small.shape[0]
    t = GLA_BLOCK
    rows = blocks * t
    idx = jnp.arange(t)
    same = (idx[:, None] // GLA_CHUNK) == (idx[None, :] // GLA_CHUNK)
    cm = jnp.where(same & (idx[:, None] >= idx[None, :]), 1.0, 0.0).astype(BF16)
    cm3 = jnp.concatenate([cm, cm, cm], axis=1)
    return pl.pallas_call(
        functools.partial(_gla_decay_kernel, blocks=blocks),
        grid=(m // rows,),
        in_specs=[
            pl.BlockSpec((rows, LANE), lambda i: (i, 0)),
            pl.BlockSpec((LANE, GLA_KW), lambda i: (0, 0)),
            pl.BlockSpec((1, GLA_KW), lambda i: (0, 0)),
            pl.BlockSpec((t, 3 * t), lambda i: (0, 0)),
        ],
        out_specs=pl.BlockSpec((rows, GLA_KW), lambda i: (i, 0)),
        out_shape=jax.ShapeDtypeStruct((m, GLA_KW), F32),
        compiler_params=_params(("parallel",)),
        name="gla_decay",
    )(small, w_gate_emb, b_gate.reshape(1, GLA_KW), cm3)


def _gla_kernel(q_ref, k_ref, v_ref, r_ref, bc_ref, on_ref, o_ref, *, heads):
    for g in range(heads):
        _gla_head(q_ref, k_ref, v_ref, r_ref, bc_ref, on_ref, o_ref,
                  slice(g * GLA_DK, (g + 1) * GLA_DK), slice(g * GLA_DV, (g + 1) * GLA_DV))


def _gla_head(q_ref, k_ref, v_ref, r_ref, bc_ref, on_ref, o_ref, kcols, vcols):
    c = GLA_CHUNK
    t = GLA_BLOCK
    dk = GLA_DK
    seq = q_ref.shape[0]
    st = jnp.zeros((GLA_DV, dk), F32)
    row = lax.broadcasted_iota(jnp.int32, (t, t), 0)
    col = lax.broadcasted_iota(jnp.int32, (t, t), 1)
    intra = (row >= col) & ((row // c) == (col // c))
    o_gain = on_ref[:, vcols]

    for n in range(seq // t):
        rows = slice(n * t, (n + 1) * t)
        bc = bc_ref[rows, kcols]
        qb = q_ref[rows, kcols] * (dk ** -0.5)
        kb = k_ref[rows, kcols]
        vb = v_ref[rows, vcols]
        chunks = [slice(i * c, (i + 1) * c) for i in range(t // c)]
        b_mid = jnp.concatenate(
            [jnp.broadcast_to(bc[s.start + c // 2:s.start + c // 2 + 1, :], (c, dk)) for s in chunks], axis=0)
        b_last = jnp.concatenate(
            [jnp.broadcast_to(bc[s.stop - 1:s.stop, :], (c, dk)) for s in chunks], axis=0)
        qd = (qb * jnp.exp(bc - b_mid)).astype(BF16)
        kd = (kb * jnp.exp(b_mid - bc)).astype(BF16)
        ku = (kb * jnp.exp(b_last - bc)).astype(BF16)
        qi = (qb * jnp.exp(bc)).astype(BF16)
        att = jnp.where(intra, _dot_nt(qd, kd), 0.0)
        o = _dot(att.astype(BF16), vb)
        inter = []
        for s in chunks:
            inter.append(_dot_nt(qi[s, :], st.astype(BF16)))
            st = st * jnp.exp(bc[s.stop - 1:s.stop, :]) + _dot_tn(vb[s, :], ku[s, :])
        o = _rms(o + jnp.concatenate(inter, axis=0), o_gain)
        o_ref[rows, vcols] = (o * _silu(r_ref[rows, vcols])).astype(o_ref.dtype)


def _gla(zf, zb, bc, o_norm, batch, seq, *, heads=GLA_HEADS_PER_STEP):
    m = zf.shape[0]
    dk, dv = heads * GLA_DK, heads * GLA_DV
    return pl.pallas_call(
        functools.partial(_gla_kernel, heads=heads),
        grid=(batch, GLA_HEADS // heads),
        in_specs=[
            pl.BlockSpec((seq, dk), lambda b, h: (b, ZF_GQ * LANE // dk + h)),
            pl.BlockSpec((seq, dk), lambda b, h: (b, ZF_GK * LANE // dk + h)),
            pl.BlockSpec((seq, dv), lambda b, h: (b, ZB_GV * LANE // dv + h)),
            pl.BlockSpec((seq, dv), lambda b, h: (b, ZF_GR * LANE // dv + h)),
            pl.BlockSpec((seq, dk), lambda b, h: (b, h)),
            pl.BlockSpec((1, dv), lambda b, h: (0, h)),
        ],
        out_specs=pl.BlockSpec((seq, dv), lambda b, h: (b, h)),
        out_shape=jax.ShapeDtypeStruct((m, GLA_VW), BF16),
        compiler_params=_params(("parallel", "parallel")),
        name="gla",
    )(zf, zf, zb, zf, bc, o_norm.reshape(1, GLA_VW))


def _merge_kernel(u0_ref, v0_ref, u_ref, v_ref, lng_ref, lnb_ref, ws_ref, bs_ref, b_ref, c_ref,
                  ga_ref, gb_ref, gc_ref, wa_ref, wb_ref, wc_ref, wo_ref, h_ref, o_ref, ya_ref):
    i = pl.program_id(0)
    chunks = ya_ref.shape[1] // SGU_CHUNK

    @pl.when(i == 0)
    def _():
        _sgu_kernel(u0_ref, v0_ref, lng_ref, lnb_ref, ws_ref, bs_ref, ya_ref.at[0], chunks=chunks)

    slot = i % 2
    mix = (jax.nn.sigmoid(ga_ref[...]) * _dot(ya_ref[slot], wa_ref[...])
           + jax.nn.sigmoid(gb_ref[...]) * _dot(b_ref[...], wb_ref[...])
           + jax.nn.sigmoid(gc_ref[...]) * _dot(c_ref[...], wc_ref[...]))
    o_ref[...] = h_ref[...] + _dot(mix.astype(BF16), wo_ref[...])
    _sgu_kernel(u_ref, v_ref, lng_ref, lnb_ref, ws_ref, bs_ref, ya_ref.at[1 - slot], chunks=chunks)


def _merge(sgu_params, yb, yc, zf, wa, wb, wc, wo, h, *, tm=MERGE_ROWS):
    m, d = h.shape
    kw = yb.shape[1]
    g0 = ZF_GATES * LANE // d
    last = m // tm - 1
    ln_g, ln_b, w_s, b_s = sgu_params
    branch = pl.BlockSpec((tm, kw), lambda i: (i, 0))
    resident = dict(pipeline_mode=pl.Buffered(1))
    wspec = pl.BlockSpec((kw, d), lambda i: (0, 0), **resident)
    return pl.pallas_call(
        _merge_kernel,
        grid=(m // tm,),
        in_specs=[
            pl.BlockSpec((tm, kw), lambda i: (0, ZF_SU * LANE // kw)),
            pl.BlockSpec((tm, kw), lambda i: (0, ZF_SV * LANE // kw)),
            pl.BlockSpec((tm, kw), lambda i: (jnp.minimum(i + 1, last), ZF_SU * LANE // kw)),
            pl.BlockSpec((tm, kw), lambda i: (jnp.minimum(i + 1, last), ZF_SV * LANE // kw)),
            pl.BlockSpec((1, kw), lambda i: (0, 0)),
            pl.BlockSpec((1, kw), lambda i: (0, 0)),
            pl.BlockSpec((SGU_GROUPS, SGU_CHUNK, SGU_CHUNK), lambda i: (0, 0, 0)),
            pl.BlockSpec((SGU_CHUNK, SGU_GROUPS), lambda i: (0, 0)),
            branch, branch,
            pl.BlockSpec((tm, d), lambda i: (i, g0)),
            pl.BlockSpec((tm, d), lambda i: (i, g0 + 1)),
            pl.BlockSpec((tm, d), lambda i: (i, g0 + 2)),
            wspec, wspec, wspec,
            pl.BlockSpec((d, d), lambda i: (0, 0), **resident),
            pl.BlockSpec((tm, d), lambda i: (i, 0)),
        ],
        out_specs=pl.BlockSpec((tm, d), lambda i: (i, 0)),
        out_shape=jax.ShapeDtypeStruct((m, d), F32),
        scratch_shapes=[pltpu.VMEM((2, tm, kw), BF16)],
        compiler_params=_params(("arbitrary",)),
        name="merge",
    )(zf, zf, zf, zf, ln_g.reshape(1, kw), ln_b.reshape(1, kw), w_s, jnp.swapaxes(b_s, 0, 1),
      yb, yc, zf, zf, zf, wa, wb, wc, wo, h)


def _xattn_kernel(*refs, rider):
    if rider:
        h_ref, g_ref, wq_ref, kv_ref, wo_ref, src_ref, o_ref, dst_ref = refs
        dst_ref[...] = src_ref[...].astype(BF16)
    else:
        h_ref, g_ref, wq_ref, kv_ref, wo_ref, o_ref = refs
    dh = XA_HEAD_DIM
    x = h_ref[...]
    n = _rms(x, g_ref[...]).astype(BF16)
    q = _dot(n, wq_ref[...].astype(BF16)) * (dh ** -0.5)
    outs = []
    for hd in range(XA_HEADS):
        qh = q[:, hd * dh:(hd + 1) * dh].astype(BF16)
        kh = kv_ref[:, hd * dh:(hd + 1) * dh].astype(BF16)
        vh = kv_ref[:, XA_WIDTH + hd * dh:XA_WIDTH + (hd + 1) * dh].astype(BF16)
        s = _dot_nt(qh, kh)
        e = jnp.exp(s - jnp.max(s, axis=-1, keepdims=True))
        p = e / jnp.sum(e, axis=-1, keepdims=True)
        outs.append(_dot(p.astype(BF16), vh))
    o = jnp.concatenate(outs, axis=1).astype(BF16)
    o_ref[...] = x + _dot(o, wo_ref[...].astype(BF16))


def _xattn(h, norm_g, wq, kv, wo, layer, seq, mem_len, rider=None, *, tm=XATTN_ROWS):
    m, d = h.shape
    per_batch = seq // tm
    in_specs = [
        pl.BlockSpec((tm, d), lambda i: (i, 0)),
        pl.BlockSpec((1, d), lambda i: (0, 0)),
        pl.BlockSpec((None, d, XA_WIDTH), lambda i: (layer, 0, 0)),
        pl.BlockSpec((mem_len, 2 * XA_WIDTH), lambda i: (i // per_batch, 0)),
        pl.BlockSpec((None, XA_WIDTH, d), lambda i: (layer, 0, 0)),
    ]
    args = [h, norm_g.reshape(1, d), wq, kv, wo]
    out_specs = [pl.BlockSpec((tm, d), lambda i: (i, 0))]
    out_shape = [jax.ShapeDtypeStruct((m, d), F32)]
    if rider:
        r_in, r_out, r_shape = _cast_rider(*rider, steps=m // tm, step_of=lambda i: i)
        in_specs.append(r_in)
        args.append(rider[0])
        out_specs.append(r_out)
        out_shape.append(r_shape)
    out = pl.pallas_call(
        functools.partial(_xattn_kernel, rider=rider is not None),
        grid=(m // tm,),
        in_specs=in_specs,
        out_specs=out_specs,
        out_shape=out_shape,
        compiler_params=_params(("parallel",)),
        name="xattn",
    )(*args)
    return out if rider else out[0]


CAST_BLOCK_BYTES = 6 * 1024 * 1024


def _cast_kernel(x_ref, o_ref):
    o_ref[...] = x_ref[...].astype(o_ref.dtype)


def _to_bf16(w, cols=None, layers=None):
    lead = w.shape[:-1] if layers is None else (layers,) + w.shape[1:-1]
    c = w.shape[-1] if cols is None else cols
    w2 = w.reshape(-1, w.shape[-1])
    r = w2.shape[0] if layers is None else layers * (w2.shape[0] // w.shape[0])
    rows = 16
    while rows * 2 * c * 4 <= CAST_BLOCK_BYTES and r % (rows * 2) == 0:
        rows *= 2
    spec = pl.BlockSpec((rows, c), lambda i: (i, 0))
    out = pl.pallas_call(
        _cast_kernel,
        grid=(r // rows,),
        in_specs=[spec],
        out_specs=spec,
        out_shape=jax.ShapeDtypeStruct((r, c), BF16),
        compiler_params=_params(("parallel",)),
        name="cast",
    )(w2)
    return out.reshape(*lead, c)

W_FQ = 2 * SGU_WIDTH
W_FF = W_FQ + 3 * FOX_WIDTH
W_GQ = W_FF + FOX_HEADS
W_GV = W_GQ + 2 * GLA_KW
W_GA = W_GV + GLA_VW
W_GR = W_GA + GLA_GATE_RANK
W_END = W_GR + GLA_VW + 3 * D_MODEL


REPACK_CHUNK = 1024


def _repack_kernel(wt_ref, wf_ref, wb_ref, ws_ref):
    def move(dst_ref, dst, src, width):
        for off in range(0, width, REPACK_CHUNK):
            n = min(REPACK_CHUNK, width - off)
            dst_ref[:, dst + off:dst + off + n] = wt_ref[src + off:src + off + n, :].T.astype(BF16)

    move(wf_ref, 0, 0, W_FQ)
    move(wf_ref, W_FQ, W_GQ, 2 * GLA_KW)
    move(wf_ref, W_FQ + 2 * GLA_KW, W_GR, W_END - W_GR)
    move(wb_ref, 0, W_FQ, 3 * FOX_WIDTH)
    move(wb_ref, 3 * FOX_WIDTH, W_GV, GLA_VW)
    lane = lax.broadcasted_iota(jnp.int32, (wt_ref.shape[1], LANE), 1)
    ff = wt_ref[W_FF - SMALL_FF:W_FF - SMALL_FF + LANE, :].T
    ga = wt_ref[W_GA - SMALL_GA:W_GA - SMALL_GA + LANE, :].T
    side = jnp.where(lane < SMALL_FF + FOX_HEADS, ff,
                     jnp.where(lane < SMALL_GA + GLA_GATE_RANK, ga, 0.0))
    ws_ref[...] = side.astype(BF16)


def _repack_w_in(w, *, kb=REPACK_K_ROWS):
    nl, d, n = w.shape
    wt = jnp.swapaxes(w, 1, 2)

    def spec(width):
        return pl.BlockSpec((None, kb, width), lambda l, i: (l, i, 0))

    return pl.pallas_call(
        _repack_kernel,
        grid=(nl, d // kb),
        in_specs=[pl.BlockSpec((None, n, kb), lambda l, i: (l, 0, i))],
        out_specs=[spec(ZF_WIDTH), spec(ZB_WIDTH), spec(LANE)],
        out_shape=[jax.ShapeDtypeStruct((nl, d, width), BF16) for width in (ZF_WIDTH, ZB_WIDTH, LANE)],
        compiler_params=_params(("parallel", "parallel")),
        name="repack",
    )(wt)


def _embed_w_gate(w_gate):
    emb = jnp.zeros((LANE, GLA_KW), BF16)
    return emb.at[SMALL_GA:SMALL_GA + GLA_GATE_RANK, :].set(w_gate.astype(BF16))


def kernel(x, mem, ffn1_norm, ffn1_w_in, ffn1_w_out, mix_norm, w_in, sgu_ln_g, sgu_ln_b, sgu_w_s, sgu_b_s, fox_b_f, gla_w_gate, gla_b_gate, gla_o_norm, w_branch_a, w_branch_b, w_branch_c, w_out, xa_norm, mem_norm, xa_w_q, xa_w_kv, xa_w_o, ffn2_norm, ffn2_w_in, ffn2_w_out, final_norm):
    batch, seq, d = x.shape
    mem_len = mem.shape[1]
    h = x.reshape(batch * seq, d)
    mem2 = mem.reshape(batch * mem_len, d)
    ones_f = jnp.ones((1, ZF_WIDTH), F32)
    scale_b = jnp.ones((1, ZB_WIDTH), F32).at[:, ZB_FQ * LANE:ZB_FQ * LANE + FOX_WIDTH].set(
        FOX_HEAD_DIM ** -0.5)
    f1_gate = _to_bf16(ffn1_w_in, D_FF, layers=1)[0]
    w_f, w_b, w_small = _repack_w_in(w_in)
    for l in range(DEPTH):
        h = _ffn(h, ffn1_norm[l], f1_gate, ffn1_w_in, ffn1_w_out, l)

        zf, small = _proj(h, mix_norm[l], w_f, l, ones_f, F32, w_small)
        zb, f2_gate = _proj(h, mix_norm[l], w_b, l, scale_b, BF16, rider=(ffn2_w_in, l, D_FF))
        qa, ka = _fox_gate(small, fox_b_f[l], batch, seq)
        merge_w = [(w, l, w.shape[2]) for w in (w_branch_a, w_branch_b, w_branch_c, w_out)]
        yb, wa, wb, wc, wo = _fox(zb, qa, ka, batch, seq, merge_w)
        bc = _gla_decay(small, _embed_w_gate(gla_w_gate[l]), gla_b_gate[l])
        yc = _gla(zf, zb, bc, gla_o_norm[l], batch, seq)
        sgu_params = (sgu_ln_g[l], sgu_ln_b[l], sgu_w_s[l], sgu_b_s[l])
        h = _merge(sgu_params, yb, yc, zf, wa, wb, wc, wo, h)

        kv = _norm_proj(mem2, mem_norm[l], xa_w_kv, l)
        if l + 1 < DEPTH:
            h, f1_gate = _xattn(h, xa_norm[l], xa_w_q, kv, xa_w_o, l, seq, mem_len,
                                rider=(ffn1_w_in, l + 1, D_FF))
        else:
            h = _xattn(h, xa_norm[l], xa_w_q, kv, xa_w_o, l, seq, mem_len)

        h = _ffn(h, ffn2_norm[l], f2_gate, ffn2_w_in, ffn2_w_out, l,
                 final_norm if l == DEPTH - 1 else None)
    return h.reshape(batch, seq, d)
```

```python
import functools

import jax
import jax.numpy as jnp
from jax import lax
from jax.experimental import pallas as pl
from jax.experimental.pallas import tpu as pltpu

F32 = jnp.float32
BF16 = jnp.bfloat16

D_MODEL = 2048
DEPTH = 2
D_FF = 5632
EPS = 1e-6

SGU_GROUPS = 4
SGU_GROUP_DIM = 256
SGU_WIDTH = 1024
SGU_CHUNK = 128

FOX_HEADS = 8
FOX_HEAD_DIM = 128
FOX_WIDTH = 1024

GLA_HEADS = 4
GLA_DK = 128
GLA_DV = 256
GLA_KW = 512
GLA_VW = 1024
GLA_GATE_RANK = 16
GLA_GATE_TAU = 16.0
GLA_CHUNK = 64

XA_HEADS = 4
XA_HEAD_DIM = 128
XA_WIDTH = 512

LANE = 128

ZF_SU, ZF_SV, ZF_GQ, ZF_GK, ZF_GR, ZF_GATES = 0, 8, 16, 20, 24, 32
ZF_WIDTH = 80 * LANE
ZB_FQ, ZB_FK, ZB_FV, ZB_GV = 0, 8, 16, 24
ZB_WIDTH = 32 * LANE
SMALL_FF = 0
SMALL_GA = 8

NEG_BIG = -1e30
VMEM_LIMIT = 63 * 1024 * 1024

FFN_ROWS, FFN_HIDDEN_TILE = 1024, 512
PROJ_ROWS, PROJ_COLS = 1024, 2048
MEM_KV_ROWS = 1024
FOX_TILE, FOX_HEADS_PER_STEP = 512, 2
GLA_DECAY_BLOCKS_PER_STEP = 4
GLA_HEADS_PER_STEP = 2
MERGE_ROWS = 256
XATTN_ROWS = 1024
REPACK_K_ROWS = 256


def _params(sem):
    return pltpu.CompilerParams(dimension_semantics=sem, vmem_limit_bytes=VMEM_LIMIT)


def _rms(x, g):
    return x * lax.rsqrt(jnp.mean(x * x, axis=-1, keepdims=True) + EPS) * g


def _log_sigmoid(x):
    return jnp.minimum(x, 0.0) - jnp.log1p(jnp.exp(-jnp.abs(x)))


def _gelu_tanh(x):
    c = 0.7978845608028654
    return x * (0.5 * (1.0 + jnp.tanh(c * (x + 0.044715 * (x * x * x)))))


def _silu(x):
    return x * jax.nn.sigmoid(x)


def _dot(a, b):
    return jnp.dot(a, b, preferred_element_type=F32)


def _dot_nt(a, b):
    return lax.dot_general(a, b, (((1,), (1,)), ((), ())), preferred_element_type=F32)


def _dot_tn(a, b):
    return lax.dot_general(a, b, (((0,), (0,)), ((), ())), preferred_element_type=F32)


def _split3(x):
    hi = x.astype(BF16)
    r = x - hi.astype(F32)
    lo = r.astype(BF16)
    lo2 = (r - lo.astype(F32)).astype(BF16)
    return hi, lo, lo2


def _dot_01(mat3, x):
    return _dot(mat3, jnp.concatenate(_split3(x), axis=0))


def _ffn_kernel(*refs, final):
    if final:
        x_ref, g_ref, wg_ref, wu_ref, wo_ref, fg_ref, o_ref, xn_ref = refs
    else:
        x_ref, g_ref, wg_ref, wu_ref, wo_ref, o_ref, xn_ref = refs
    j = pl.program_id(1)

    def tile(xn):
        gate = _dot(xn, wg_ref[...])
        up = _dot(xn, wu_ref[...].astype(BF16))
        act = (_silu(gate) * up * 0.5).astype(BF16)
        return _dot(act, wo_ref[...].astype(BF16))

    @pl.when(j == 0)
    def _():
        x = x_ref[...]
        xn = _rms(x, g_ref[...]).astype(BF16)
        xn_ref[...] = xn
        o_ref[...] = x + tile(xn)

    @pl.when(j > 0)
    def _():
        o_ref[...] += tile(xn_ref[...])

    if final:
        @pl.when(j == pl.num_programs(1) - 1)
        def _():
            o_ref[...] = _rms(o_ref[...], fg_ref[...])


def _ffn(h, norm_g, w_gate, w_in, w_out, layer, final_g=None, *, tm=FFN_ROWS, tf=FFN_HIDDEN_TILE):
    m, d = h.shape
    f = w_out.shape[1]
    nt = f // tf
    final = final_g is not None
    vec = pl.BlockSpec((1, d), lambda i, j: (0, 0))
    rows = pl.BlockSpec((tm, d), lambda i, j: (i, 0))
    in_specs = [
        rows, vec,
        pl.BlockSpec((d, tf), lambda i, j: (0, j)),
        pl.BlockSpec((None, d, tf), lambda i, j: (layer, 0, j + nt)),
        pl.BlockSpec((None, tf, d), lambda i, j: (layer, j, 0)),
    ]
    args = [h, norm_g.reshape(1, d), w_gate, w_in, w_out]
    if final:
        in_specs.append(vec)
        args.append(final_g.reshape(1, d))
    return pl.pallas_call(
        functools.partial(_ffn_kernel, final=final),
        grid=(m // tm, nt),
        in_specs=in_specs,
        out_specs=rows,
        out_shape=jax.ShapeDtypeStruct((m, d), F32),
        scratch_shapes=[pltpu.VMEM((tm, d), BF16)],
        compiler_params=_params(("parallel", "arbitrary")),
        name="ffn",
    )(*args)


def _cast_rider(src, layer, cols, steps, step_of):
    rows = src.shape[1] // steps
    in_spec = pl.BlockSpec((None, rows, cols), lambda *g: (layer, step_of(*g), 0))
    out_spec = pl.BlockSpec((rows, cols), lambda *g: (step_of(*g), 0))
    return in_spec, out_spec, jax.ShapeDtypeStruct((src.shape[1], cols), BF16)


def _proj_kernel(*refs, small, rider):
    if rider:
        x_ref, g_ref, w_ref, cs_ref, src_ref, o_ref, dst_ref, xn_ref = refs
        dst_ref[...] = src_ref[...].astype(BF16)
    elif small:
        x_ref, g_ref, w_ref, cs_ref, ws_ref, o_ref, os_ref, xn_ref = refs
    else:
        x_ref, g_ref, w_ref, cs_ref, o_ref, xn_ref = refs
    j = pl.program_id(1)

    def project(xn):
        o_ref[...] = (_dot(xn, w_ref[...]) * cs_ref[...]).astype(o_ref.dtype)

    @pl.when(j == 0)
    def _():
        xn = _rms(x_ref[...], g_ref[...]).astype(BF16)
        xn_ref[...] = xn
        if small:
            os_ref[...] = _dot(xn, ws_ref[...])
        project(xn)

    @pl.when(j > 0)
    def _():
        project(xn_ref[...])


def _proj(x, norm_g, w, col_scale, out_dtype, w_small=None, rider=None, *, tm=PROJ_ROWS, tn=PROJ_COLS):
    m, d = x.shape
    n = w.shape[1]
    small = w_small is not None
    assert not (small and rider)
    in_specs = [
        pl.BlockSpec((tm, d), lambda i, j: (i, 0)),
        pl.BlockSpec((1, d), lambda i, j: (0, 0)),
        pl.BlockSpec((d, tn), lambda i, j: (0, j)),
        pl.BlockSpec((1, tn), lambda i, j: (0, j)),
    ]
    args = [x, norm_g.reshape(1, d), w, col_scale]
    out_specs = [pl.BlockSpec((tm, tn), lambda i, j: (i, j))]
    out_shape = [jax.ShapeDtypeStruct((m, n), out_dtype)]
    if small:
        in_specs.append(pl.BlockSpec((d, LANE), lambda i, j: (0, 0)))
        args.append(w_small)
        out_specs.append(pl.BlockSpec((tm, LANE), lambda i, j: (i, 0)))
        out_shape.append(jax.ShapeDtypeStruct((m, LANE), F32))
    if rider:
        nj = n // tn
        r_in, r_out, r_shape = _cast_rider(*rider, steps=(m // tm) * nj, step_of=lambda i, j: i * nj + j)
        in_specs.append(r_in)
        args.append(rider[0])
        out_specs.append(r_out)
        out_shape.append(r_shape)
    out = pl.pallas_call(
        functools.partial(_proj_kernel, small=small, rider=rider is not None),
        grid=(m // tm, n // tn),
        in_specs=in_specs,
        out_specs=out_specs,
        out_shape=out_shape,
        scratch_shapes=[pltpu.VMEM((tm, d), BF16)],
        compiler_params=_params(("parallel", "arbitrary")),
        name="proj",
    )(*args)
    return out if small or rider else out[0]


def _norm_proj_kernel(x_ref, g_ref, w_ref, o_ref):
    o_ref[...] = _dot(_rms(x_ref[...], g_ref[...]).astype(BF16), w_ref[...].astype(BF16))


def _norm_proj(x, norm_g, w, layer, *, tm=MEM_KV_ROWS):
    m, d = x.shape
    n = w.shape[2]
    return pl.pallas_call(
        _norm_proj_kernel,
        grid=(m // tm,),
        in_specs=[
            pl.BlockSpec((tm, d), lambda i: (i, 0)),
            pl.BlockSpec((1, d), lambda i: (0, 0)),
            pl.BlockSpec((None, d, n), lambda i: (layer, 0, 0)),
        ],
        out_specs=pl.BlockSpec((tm, n), lambda i: (i, 0)),
        out_shape=jax.ShapeDtypeStruct((m, n), F32),
        compiler_params=_params(("parallel",)),
        name="mem_kv",
    )(x, norm_g.reshape(1, d), w)


def _sgu_kernel(u_ref, v_ref, lng_ref, lnb_ref, ws_ref, bs_ref, o_ref, *, chunks):
    t = SGU_CHUNK
    gd = SGU_GROUP_DIM
    row = lax.broadcasted_iota(jnp.int32, (t, t), 0)
    col = lax.broadcasted_iota(jnp.int32, (t, t), 1)
    causal = row >= col
    for g in range(SGU_GROUPS):
        w = jnp.where(causal, ws_ref[g], 0.0).astype(BF16)
        bias = bs_ref[:, g:g + 1]
        ln_g = lng_ref[:, g * gd:(g + 1) * gd]
        ln_b = lnb_ref[:, g * gd:(g + 1) * gd]
        for c in range(chunks):
            rows = slice(c * t, (c + 1) * t)
            cols = slice(g * gd, (g + 1) * gd)
            v = _gelu_tanh(v_ref[rows, cols])
            mu = jnp.mean(v, axis=-1, keepdims=True)
            vc = v - mu
            var = jnp.mean(vc * vc, axis=-1, keepdims=True)
            vn = vc * lax.rsqrt(var + EPS) * ln_g + ln_b
            mixed = _dot(w, vn.astype(BF16)) + bias
            o_ref[rows, cols] = (_gelu_tanh(u_ref[rows, cols]) * mixed).astype(o_ref.dtype)


FOX_AUG = 6


def _fox_gate_kernel(f_ref, bf_ref, tril_ref, pq_ref, pk_ref, cq_ref, ck_ref, qa_ref, ka_ref):
    t = LANE
    seq = f_ref.shape[0]
    carry = jnp.zeros((1, t), F32)
    for b in range(seq // t):
        rows = slice(b * t, (b + 1) * t)
        lf = _log_sigmoid(f_ref[rows, :] + bf_ref[...])
        cum = _dot_01(tril_ref[...], lf) + carry
        carry = cum[t - 1:t, :]
        parts = jnp.concatenate(_split3(cum), axis=1)
        qa_ref[rows, :] = (_dot(parts, pq_ref[...]) + cq_ref[...]).astype(BF16)
        ka_ref[rows, :] = (_dot(parts, pk_ref[...]) + ck_ref[...]).astype(BF16)


def _tril3(n):
    tril = jnp.tril(jnp.ones((n, n), BF16))
    return jnp.concatenate([tril, tril, tril], axis=1)


def _fox_placement():
    src = jnp.arange(3 * LANE)[:, None]
    dst = jnp.arange(LANE)[None, :]
    p, h = src // LANE, src % LANE
    valid = h < FOX_HEADS
    pq = jnp.where(valid & (dst == h * FOX_AUG + p), 1.0, 0.0).astype(BF16)
    pk = jnp.where(valid & (dst == h * FOX_AUG + 3 + p), -1.0, 0.0).astype(BF16)
    used = dst < FOX_HEADS * FOX_AUG
    cq = jnp.where(used & (dst % FOX_AUG >= 3), 1.0, 0.0).astype(F32)
    ck = jnp.where(used & (dst % FOX_AUG < 3), 1.0, 0.0).astype(F32)
    return pq, pk, cq, ck


def _fox_gate(small, b_f, batch, seq):
    bf = jnp.zeros((1, LANE), F32).at[0, SMALL_FF:SMALL_FF + FOX_HEADS].set(b_f)
    w = LANE
    aug = jax.ShapeDtypeStruct((batch * seq, w), BF16)

    def const(shape):
        return pl.BlockSpec(shape, lambda b: (0, 0))

    return pl.pallas_call(
        _fox_gate_kernel,
        grid=(batch,),
        in_specs=[
            pl.BlockSpec((seq, LANE), lambda b: (b, 0)),
            const((1, LANE)), const((LANE, 3 * LANE)),
            const((3 * LANE, w)), const((3 * LANE, w)), const((1, w)), const((1, w)),
        ],
        out_specs=[pl.BlockSpec((seq, w), lambda b: (b, 0))] * 2,
        out_shape=[aug, aug],
        compiler_params=_params(("parallel",)),
        name="fox_gate",
    )(small, bf, _tril3(LANE), *_fox_placement())


def _fox_kernel(*refs, t, heads, riders):
    q_ref, k_ref, v_ref, qa_ref, ka_ref = refs[:5]
    o_ref = refs[5 + riders]
    for src_ref, dst_ref in zip(refs[5:5 + riders], refs[6 + riders:]):
        dst_ref[...] = src_ref[...].astype(BF16)
    seq = q_ref.shape[0]
    dh = FOX_HEAD_DIM
    lane = lax.broadcasted_iota(jnp.int32, (t, LANE), 1)
    ones_col = jnp.where(lane == 0, 1.0, 0.0).astype(BF16)
    row = lax.broadcasted_iota(jnp.int32, (t, t), 0)
    col = lax.broadcasted_iota(jnp.int32, (t, t), 1)
    causal = row >= col

    for g in range(heads):
        head = pl.program_id(1) * heads + g
        own = (lane >= head * FOX_AUG) & (lane < (head + 1) * FOX_AUG)
        cols = slice(g * dh, (g + 1) * dh)

        def k_block(j):
            rows = slice(j * t, (j + 1) * t)
            ka = jnp.where(own, ka_ref[rows, :], jnp.zeros((), BF16))
            return (jnp.concatenate([k_ref[rows, cols], ka], axis=1),
                    jnp.concatenate([v_ref[rows, cols], ones_col], axis=1))

        for i in range(seq // t):
            rows = slice(i * t, (i + 1) * t)
            qc = jnp.concatenate([q_ref[rows, cols], qa_ref[rows, :]], axis=1)
            kc, vc = k_block(i)
            s = jnp.where(causal, _dot_nt(qc, kc), NEG_BIG)
            m = jnp.max(s, axis=1, keepdims=True)
            acc = _dot(jnp.exp(s - m).astype(BF16), vc)
            for j in range(i):
                kc, vc = k_block(j)
                s = _dot_nt(qc, kc)
                m_new = jnp.maximum(m, jnp.max(s, axis=1, keepdims=True))
                acc = jnp.exp(m - m_new) * acc + _dot(jnp.exp(s - m_new).astype(BF16), vc)
                m = m_new
            o_ref[rows, cols] = (acc[:, :dh] / acc[:, dh:dh + 1]).astype(o_ref.dtype)


def _fox(zb, qa, ka, batch, seq, riders=(), *, t=FOX_TILE, heads=FOX_HEADS_PER_STEP):
    m = zb.shape[0]
    w = heads * FOX_HEAD_DIM
    groups = FOX_HEADS // heads

    def group(seg):
        return pl.BlockSpec((seq, w), lambda b, h: (b, seg * LANE // w + h))

    aug = pl.BlockSpec((seq, LANE), lambda b, h: (b, 0))
    rider_specs = [_cast_rider(*r, steps=batch * groups, step_of=lambda b, h: b * groups + h) for r in riders]
    out = pl.pallas_call(
        functools.partial(_fox_kernel, t=t, heads=heads, riders=len(riders)),
        grid=(batch, groups),
        in_specs=[group(ZB_FQ), group(ZB_FK), group(ZB_FV), aug, aug] + [r[0] for r in rider_specs],
        out_specs=[group(0)] + [r[1] for r in rider_specs],
        out_shape=[jax.ShapeDtypeStruct((m, FOX_WIDTH), BF16)] + [r[2] for r in rider_specs],
        compiler_params=_params(("parallel", "parallel")),
        name="fox",
    )(zb, zb, zb, qa, ka, *[r[0] for r in riders])
    return out if riders else out[0]


GLA_BLOCK = 4 * GLA_CHUNK


def _gla_decay_kernel(*refs, blocks, rider):
    if rider:
        a_ref, wg_ref, bg_ref, cm_ref, wt_ref, o_ref, wf_ref, wb_ref, ws_ref = refs
        _repack_kernel(wt_ref, wf_ref, wb_ref, ws_ref)
    else:
        a_ref, wg_ref, bg_ref, cm_ref, o_ref = refs
    t = GLA_BLOCK
    for b in range(blocks):
        rows = slice(b * t, (b + 1) * t)
        gl = _dot(a_ref[rows, :].astype(BF16), wg_ref[...]) + bg_ref[...]
        g = _log_sigmoid(gl) / GLA_GATE_TAU
        o_ref[rows, :] = _dot_01(cm_ref[...], g)


def _gla_decay(small, w_gate_emb, b_gate, repack=None, *, blocks=GLA_DECAY_BLOCKS_PER_STEP):
    m = small.shape[0]
    t = GLA_BLOCK
    rows = blocks * t
    steps = m // rows
    idx = jnp.arange(t)
    same = (idx[:, None] // GLA_CHUNK) == (idx[None, :] // GLA_CHUNK)
    cm = jnp.where(same & (idx[:, None] >= idx[None, :]), 1.0, 0.0).astype(BF16)
    cm3 = jnp.concatenate([cm, cm, cm], axis=1)
    in_specs = [
        pl.BlockSpec((rows, LANE), lambda i: (i, 0)),
        pl.BlockSpec((LANE, GLA_KW), lambda i: (0, 0)),
        pl.BlockSpec((1, GLA_KW), lambda i: (0, 0)),
        pl.BlockSpec((t, 3 * t), lambda i: (0, 0)),
    ]
    args = [small, w_gate_emb, b_gate.reshape(1, GLA_KW), cm3]
    out_specs = [pl.BlockSpec((rows, GLA_KW), lambda i: (i, 0))]
    out_shape = [jax.ShapeDtypeStruct((m, GLA_KW), F32)]
    if repack:
        wt, layer = repack
        r_in, r_outs, r_shapes = _repack_specs(wt, layer, wt.shape[2] // steps)
        in_specs.append(r_in)
        args.append(wt)
        out_specs += r_outs
        out_shape += r_shapes
    out = pl.pallas_call(
        functools.partial(_gla_decay_kernel, blocks=blocks, rider=repack is not None),
        grid=(steps,),
        in_specs=in_specs,
        out_specs=out_specs,
        out_shape=out_shape,
        compiler_params=_params(("parallel",)),
        name="gla_decay",
    )(*args)
    return out if repack else out[0]


def _gla_kernel(q_ref, k_ref, v_ref, r_ref, bc_ref, on_ref, o_ref, *, heads):
    for g in range(heads):
        _gla_head(q_ref, k_ref, v_ref, r_ref, bc_ref, on_ref, o_ref,
                  slice(g * GLA_DK, (g + 1) * GLA_DK), slice(g * GLA_DV, (g + 1) * GLA_DV))


def _gla_head(q_ref, k_ref, v_ref, r_ref, bc_ref, on_ref, o_ref, kcols, vcols):
    c = GLA_CHUNK
    t = GLA_BLOCK
    dk = GLA_DK
    seq = q_ref.shape[0]
    st = jnp.zeros((GLA_DV, dk), F32)
    row = lax.broadcasted_iota(jnp.int32, (t, t), 0)
    col = lax.broadcasted_iota(jnp.int32, (t, t), 1)
    intra = (row >= col) & ((row // c) == (col // c))
    o_gain = on_ref[:, vcols]

    for n in range(seq // t):
        rows = slice(n * t, (n + 1) * t)
        bc = bc_ref[rows, kcols]
        qb = q_ref[rows, kcols] * (dk ** -0.5)
        kb = k_ref[rows, kcols]
        vb = v_ref[rows, vcols]
        chunks = [slice(i * c, (i + 1) * c) for i in range(t // c)]
        b_mid = jnp.concatenate(
            [jnp.broadcast_to(bc[s.start + c // 2:s.start + c // 2 + 1, :], (c, dk)) for s in chunks], axis=0)
        b_last = jnp.concatenate(
            [jnp.broadcast_to(bc[s.stop - 1:s.stop, :], (c, dk)) for s in chunks], axis=0)
        qd = (qb * jnp.exp(bc - b_mid)).astype(BF16)
        kd = (kb * jnp.exp(b_mid - bc)).astype(BF16)
        ku = (kb * jnp.exp(b_last - bc)).astype(BF16)
        qi = (qb * jnp.exp(bc)).astype(BF16)
        att = jnp.where(intra, _dot_nt(qd, kd), 0.0)
        o = _dot(att.astype(BF16), vb)
        inter = []
        for s in chunks:
            inter.append(_dot_nt(qi[s, :], st.astype(BF16)))
            st = st * jnp.exp(bc[s.stop - 1:s.stop, :]) + _dot_tn(vb[s, :], ku[s, :])
        o = _rms(o + jnp.concatenate(inter, axis=0), o_gain)
        o_ref[rows, vcols] = (o * _silu(r_ref[rows, vcols])).astype(o_ref.dtype)


def _gla(zf, zb, bc, o_norm, batch, seq, *, heads=GLA_HEADS_PER_STEP):
    m = zf.shape[0]
    dk, dv = heads * GLA_DK, heads * GLA_DV
    return pl.pallas_call(
        functools.partial(_gla_kernel, heads=heads),
        grid=(batch, GLA_HEADS // heads),
        in_specs=[
            pl.BlockSpec((seq, dk), lambda b, h: (b, ZF_GQ * LANE // dk + h)),
            pl.BlockSpec((seq, dk), lambda b, h: (b, ZF_GK * LANE // dk + h)),
            pl.BlockSpec((seq, dv), lambda b, h: (b, ZB_GV * LANE // dv + h)),
            pl.BlockSpec((seq, dv), lambda b, h: (b, ZF_GR * LANE // dv + h)),
            pl.BlockSpec((seq, dk), lambda b, h: (b, h)),
            pl.BlockSpec((1, dv), lambda b, h: (0, h)),
        ],
        out_specs=pl.BlockSpec((seq, dv), lambda b, h: (b, h)),
        out_shape=jax.ShapeDtypeStruct((m, GLA_VW), BF16),
        compiler_params=_params(("parallel", "parallel")),
        name="gla",
    )(zf, zf, zb, zf, bc, o_norm.reshape(1, GLA_VW))


def _merge_kernel(u0_ref, v0_ref, u_ref, v_ref, lng_ref, lnb_ref, ws_ref, bs_ref, b_ref, c_ref,
                  ga_ref, gb_ref, gc_ref, wa_ref, wb_ref, wc_ref, wo_ref, h_ref, o_ref, ya_ref):
    i = pl.program_id(0)
    chunks = ya_ref.shape[1] // SGU_CHUNK

    @pl.when(i == 0)
    def _():
        _sgu_kernel(u0_ref, v0_ref, lng_ref, lnb_ref, ws_ref, bs_ref, ya_ref.at[0], chunks=chunks)

    slot = i % 2
    mix = (jax.nn.sigmoid(ga_ref[...]) * _dot(ya_ref[slot], wa_ref[...])
           + jax.nn.sigmoid(gb_ref[...]) * _dot(b_ref[...], wb_ref[...])
           + jax.nn.sigmoid(gc_ref[...]) * _dot(c_ref[...], wc_ref[...]))
    o_ref[...] = h_ref[...] + _dot(mix.astype(BF16), wo_ref[...])
    _sgu_kernel(u_ref, v_ref, lng_ref, lnb_ref, ws_ref, bs_ref, ya_ref.at[1 - slot], chunks=chunks)


def _merge(sgu_params, yb, yc, zf, wa, wb, wc, wo, h, *, tm=MERGE_ROWS):
    m, d = h.shape
    kw = yb.shape[1]
    g0 = ZF_GATES * LANE // d
    last = m // tm - 1
    ln_g, ln_b, w_s, b_s = sgu_params
    branch = pl.BlockSpec((tm, kw), lambda i: (i, 0))
    resident = dict(pipeline_mode=pl.Buffered(1))
    wspec = pl.BlockSpec((kw, d), lambda i: (0, 0), **resident)
    return pl.pallas_call(
        _merge_kernel,
        grid=(m // tm,),
        in_specs=[
            pl.BlockSpec((tm, kw), lambda i: (0, ZF_SU * LANE // kw)),
            pl.BlockSpec((tm, kw), lambda i: (0, ZF_SV * LANE // kw)),
            pl.BlockSpec((tm, kw), lambda i: (jnp.minimum(i + 1, last), ZF_SU * LANE // kw)),
            pl.BlockSpec((tm, kw), lambda i: (jnp.minimum(i + 1, last), ZF_SV * LANE // kw)),
            pl.BlockSpec((1, kw), lambda i: (0, 0)),
            pl.BlockSpec((1, kw), lambda i: (0, 0)),
            pl.BlockSpec((SGU_GROUPS, SGU_CHUNK, SGU_CHUNK), lambda i: (0, 0, 0)),
            pl.BlockSpec((SGU_CHUNK, SGU_GROUPS), lambda i: (0, 0)),
            branch, branch,
            pl.BlockSpec((tm, d), lambda i: (i, g0)),
            pl.BlockSpec((tm, d), lambda i: (i, g0 + 1)),
            pl.BlockSpec((tm, d), lambda i: (i, g0 + 2)),
            wspec, wspec, wspec,
            pl.BlockSpec((d, d), lambda i: (0, 0), **resident),
            pl.BlockSpec((tm, d), lambda i: (i, 0)),
        ],
        out_specs=pl.BlockSpec((tm, d), lambda i: (i, 0)),
        out_shape=jax.ShapeDtypeStruct((m, d), F32),
        scratch_shapes=[pltpu.VMEM((2, tm, kw), BF16)],
        compiler_params=_params(("arbitrary",)),
        name="merge",
    )(zf, zf, zf, zf, ln_g.reshape(1, kw), ln_b.reshape(1, kw), w_s, jnp.swapaxes(b_s, 0, 1),
      yb, yc, zf, zf, zf, wa, wb, wc, wo, h)


def _xattn_kernel(*refs, rider):
    if rider:
        h_ref, g_ref, wq_ref, kv_ref, wo_ref, src_ref, o_ref, dst_ref = refs
        dst_ref[...] = src_ref[...].astype(BF16)
    else:
        h_ref, g_ref, wq_ref, kv_ref, wo_ref, o_ref = refs
    dh = XA_HEAD_DIM
    x = h_ref[...]
    n = _rms(x, g_ref[...]).astype(BF16)
    q = _dot(n, wq_ref[...].astype(BF16)) * (dh ** -0.5)
    outs = []
    for hd in range(XA_HEADS):
        qh = q[:, hd * dh:(hd + 1) * dh].astype(BF16)
        kh = kv_ref[:, hd * dh:(hd + 1) * dh].astype(BF16)
        vh = kv_ref[:, XA_WIDTH + hd * dh:XA_WIDTH + (hd + 1) * dh].astype(BF16)
        s = _dot_nt(qh, kh)
        e = jnp.exp(s - jnp.max(s, axis=-1, keepdims=True))
        p = e / jnp.sum(e, axis=-1, keepdims=True)
        outs.append(_dot(p.astype(BF16), vh))
    o = jnp.concatenate(outs, axis=1).astype(BF16)
    o_ref[...] = x + _dot(o, wo_ref[...].astype(BF16))


def _xattn(h, norm_g, wq, kv, wo, layer, seq, mem_len, rider=None, *, tm=XATTN_ROWS):
    m, d = h.shape
    per_batch = seq // tm
    in_specs = [
        pl.BlockSpec((tm, d), lambda i: (i, 0)),
        pl.BlockSpec((1, d), lambda i: (0, 0)),
        pl.BlockSpec((None, d, XA_WIDTH), lambda i: (layer, 0, 0)),
        pl.BlockSpec((mem_len, 2 * XA_WIDTH), lambda i: (i // per_batch, 0)),
        pl.BlockSpec((None, XA_WIDTH, d), lambda i: (layer, 0, 0)),
    ]
    args = [h, norm_g.reshape(1, d), wq, kv, wo]
    out_specs = [pl.BlockSpec((tm, d), lambda i: (i, 0))]
    out_shape = [jax.ShapeDtypeStruct((m, d), F32)]
    if rider:
        r_in, r_out, r_shape = _cast_rider(*rider, steps=m // tm, step_of=lambda i: i)
        in_specs.append(r_in)
        args.append(rider[0])
        out_specs.append(r_out)
        out_shape.append(r_shape)
    out = pl.pallas_call(
        functools.partial(_xattn_kernel, rider=rider is not None),
        grid=(m // tm,),
        in_specs=in_specs,
        out_specs=out_specs,
        out_shape=out_shape,
        compiler_params=_params(("parallel",)),
        name="xattn",
    )(*args)
    return out if rider else out[0]


CAST_BLOCK_BYTES = 6 * 1024 * 1024


def _cast_kernel(x_ref, o_ref):
    o_ref[...] = x_ref[...].astype(o_ref.dtype)


def _to_bf16(w, cols=None, layers=None):
    lead = w.shape[:-1] if layers is None else (layers,) + w.shape[1:-1]
    c = w.shape[-1] if cols is None else cols
    w2 = w.reshape(-1, w.shape[-1])
    r = w2.shape[0] if layers is None else layers * (w2.shape[0] // w.shape[0])
    rows = 16
    while rows * 2 * c * 4 <= CAST_BLOCK_BYTES and r % (rows * 2) == 0:
        rows *= 2
    spec = pl.BlockSpec((rows, c), lambda i: (i, 0))
    out = pl.pallas_call(
        _cast_kernel,
        grid=(r // rows,),
        in_specs=[spec],
        out_specs=spec,
        out_shape=jax.ShapeDtypeStruct((r, c), BF16),
        compiler_params=_params(("parallel",)),
        name="cast",
    )(w2)
    return out.reshape(*lead, c)

W_FQ = 2 * SGU_WIDTH
W_FF = W_FQ + 3 * FOX_WIDTH
W_GQ = W_FF + FOX_HEADS
W_GV = W_GQ + 2 * GLA_KW
W_GA = W_GV + GLA_VW
W_GR = W_GA + GLA_GATE_RANK
W_END = W_GR + GLA_VW + 3 * D_MODEL


REPACK_CHUNK = 1024


def _repack_kernel(wt_ref, wf_ref, wb_ref, ws_ref):
    def move(dst_ref, dst, src, width):
        for off in range(0, width, REPACK_CHUNK):
            n = min(REPACK_CHUNK, width - off)
            dst_ref[:, dst + off:dst + off + n] = wt_ref[src + off:src + off + n, :].T.astype(BF16)

    move(wf_ref, 0, 0, W_FQ)
    move(wf_ref, W_FQ, W_GQ, 2 * GLA_KW)
    move(wf_ref, W_FQ + 2 * GLA_KW, W_GR, W_END - W_GR)
    move(wb_ref, 0, W_FQ, 3 * FOX_WIDTH)
    move(wb_ref, 3 * FOX_WIDTH, W_GV, GLA_VW)
    lane = lax.broadcasted_iota(jnp.int32, (wt_ref.shape[1], LANE), 1)
    ff = wt_ref[W_FF - SMALL_FF:W_FF - SMALL_FF + LANE, :].T
    ga = wt_ref[W_GA - SMALL_GA:W_GA - SMALL_GA + LANE, :].T
    side = jnp.where(lane < SMALL_FF + FOX_HEADS, ff,
                     jnp.where(lane < SMALL_GA + GLA_GATE_RANK, ga, 0.0))
    ws_ref[...] = side.astype(BF16)


def _repack_specs(wt, layer, kb):
    widths = (ZF_WIDTH, ZB_WIDTH, LANE)
    in_spec = pl.BlockSpec((None, wt.shape[1], kb), lambda i: (layer, 0, i))
    out_specs = [pl.BlockSpec((kb, width), lambda i: (i, 0)) for width in widths]
    out_shape = [jax.ShapeDtypeStruct((wt.shape[2], width), BF16) for width in widths]
    return in_spec, out_specs, out_shape


def _repack_w_in(wt, layer, *, kb=REPACK_K_ROWS):
    in_spec, out_specs, out_shape = _repack_specs(wt, layer, kb)
    return pl.pallas_call(
        _repack_kernel,
        grid=(wt.shape[2] // kb,),
        in_specs=[in_spec],
        out_specs=out_specs,
        out_shape=out_shape,
        compiler_params=_params(("parallel",)),
        name="repack",
    )(wt)


def _embed_w_gate(w_gate):
    emb = jnp.zeros((LANE, GLA_KW), BF16)
    return emb.at[SMALL_GA:SMALL_GA + GLA_GATE_RANK, :].set(w_gate.astype(BF16))


def kernel(x, mem, ffn1_norm, ffn1_w_in, ffn1_w_out, mix_norm, w_in, sgu_ln_g, sgu_ln_b, sgu_w_s, sgu_b_s, fox_b_f, gla_w_gate, gla_b_gate, gla_o_norm, w_branch_a, w_branch_b, w_branch_c, w_out, xa_norm, mem_norm, xa_w_q, xa_w_kv, xa_w_o, ffn2_norm, ffn2_w_in, ffn2_w_out, final_norm):
    batch, seq, d = x.shape
    mem_len = mem.shape[1]
    h = x.reshape(batch * seq, d)
    mem2 = mem.reshape(batch * mem_len, d)
    ones_f = jnp.ones((1, ZF_WIDTH), F32)
    scale_b = jnp.ones((1, ZB_WIDTH), F32).at[:, ZB_FQ * LANE:ZB_FQ * LANE + FOX_WIDTH].set(
        FOX_HEAD_DIM ** -0.5)
    f1_gate = _to_bf16(ffn1_w_in, D_FF, layers=1)[0]
    w_in_t = jnp.swapaxes(w_in, 1, 2)
    w_f, w_b, w_small = _repack_w_in(w_in_t, 0)
    for l in range(DEPTH):
        h = _ffn(h, ffn1_norm[l], f1_gate, ffn1_w_in, ffn1_w_out, l)

        zf, small = _proj(h, mix_norm[l], w_f, ones_f, F32, w_small)
        zb, f2_gate = _proj(h, mix_norm[l], w_b, scale_b, BF16, rider=(ffn2_w_in, l, D_FF))
        qa, ka = _fox_gate(small, fox_b_f[l], batch, seq)
        merge_w = [(w, l, w.shape[2]) for w in (w_branch_a, w_branch_b, w_branch_c, w_out)]
        yb, wa, wb, wc, wo = _fox(zb, qa, ka, batch, seq, merge_w)
        if l + 1 < DEPTH:
            bc, w_f, w_b, w_small = _gla_decay(small, _embed_w_gate(gla_w_gate[l]), gla_b_gate[l],
                                               repack=(w_in_t, l + 1))
        else:
            bc = _gla_decay(small, _embed_w_gate(gla_w_gate[l]), gla_b_gate[l])
        yc = _gla(zf, zb, bc, gla_o_norm[l], batch, seq)
        sgu_params = (sgu_ln_g[l], sgu_ln_b[l], sgu_w_s[l], sgu_b_s[l])
        h = _merge(sgu_params, yb, yc, zf, wa, wb, wc, wo, h)

        kv = _norm_proj(mem2, mem_norm[l], xa_w_kv, l)
        if l + 1 < DEPTH:
            h, f1_gate = _xattn(h, xa_norm[l], xa_w_q, kv, xa_w_o, l, seq, mem_len,
                                rider=(ffn1_w_in, l + 1, D_FF))
        else:
            h = _xattn(h, xa_norm[l], xa_w_q, kv, xa_w_o, l, seq, mem_len)

        h = _ffn(h, ffn2_norm[l], f2_gate, ffn2_w_in, ffn2_w_out, l,
                 final_norm if l == DEPTH - 1 else None)
    return h.reshape(batch, seq, d)
```

```python
import functools

import jax
import jax.numpy as jnp
from jax import lax
from jax.experimental import pallas as pl
from jax.experimental.pallas import tpu as pltpu

F32 = jnp.float32
BF16 = jnp.bfloat16

D_MODEL = 2048
DEPTH = 2
D_FF = 5632
EPS = 1e-6

SGU_GROUPS = 4
SGU_GROUP_DIM = 256
SGU_WIDTH = 1024
SGU_CHUNK = 128

FOX_HEADS = 8
FOX_HEAD_DIM = 128
FOX_WIDTH = 1024

GLA_HEADS = 4
GLA_DK = 128
GLA_DV = 256
GLA_KW = 512
GLA_VW = 1024
GLA_GATE_RANK = 16
GLA_GATE_TAU = 16.0
GLA_CHUNK = 64

XA_HEADS = 4
XA_HEAD_DIM = 128
XA_WIDTH = 512

LANE = 128

ZF_SU, ZF_SV, ZF_GQ, ZF_GK, ZF_GR, ZF_GATES = 0, 8, 16, 20, 24, 32
ZF_WIDTH = 80 * LANE
ZB_FQ, ZB_FK, ZB_FV, ZB_GV = 0, 8, 16, 24
ZB_WIDTH = 32 * LANE
SMALL_FF = 0
SMALL_GA = 8

NEG_BIG = -1e30
LOG2E = 1.4426950408889634
VMEM_LIMIT = 63 * 1024 * 1024

FFN_ROWS, FFN_HIDDEN_TILE = 1024, 512
PROJ_ROWS, PROJ_COLS = 1024, 2048
MEM_KV_ROWS = 1024
FOX_TILE, FOX_HEADS_PER_STEP = 512, 2
GLA_DECAY_BLOCKS_PER_STEP = 4
GLA_HEADS_PER_STEP = 2
MERGE_ROWS = 256
XATTN_ROWS = 1024
REPACK_K_ROWS = 256


def _params(sem):
    return pltpu.CompilerParams(dimension_semantics=sem, vmem_limit_bytes=VMEM_LIMIT)


def _rms(x, g):
    return x * lax.rsqrt(jnp.mean(x * x, axis=-1, keepdims=True) + EPS) * g


def _log_sigmoid(x):
    return jnp.minimum(x, 0.0) - jnp.log1p(jnp.exp(-jnp.abs(x)))


def _gelu_tanh(x):
    c = 0.7978845608028654
    return x * (0.5 * (1.0 + jnp.tanh(c * (x + 0.044715 * (x * x * x)))))


def _silu(x):
    return x * jax.nn.sigmoid(x)


def _dot(a, b):
    return jnp.dot(a, b, preferred_element_type=F32)


def _dot_nt(a, b):
    return lax.dot_general(a, b, (((1,), (1,)), ((), ())), preferred_element_type=F32)


def _dot_tn(a, b):
    return lax.dot_general(a, b, (((0,), (0,)), ((), ())), preferred_element_type=F32)


def _split3(x):
    hi = x.astype(BF16)
    r = x - hi.astype(F32)
    lo = r.astype(BF16)
    lo2 = (r - lo.astype(F32)).astype(BF16)
    return hi, lo, lo2


def _dot_01(mat3, x):
    return _dot(mat3, jnp.concatenate(_split3(x), axis=0))


def _ffn_kernel(*refs, final):
    if final:
        x_ref, g_ref, wg_ref, wu_ref, wo_ref, fg_ref, o_ref, xn_ref = refs
    else:
        x_ref, g_ref, wg_ref, wu_ref, wo_ref, o_ref, xn_ref = refs
    j = pl.program_id(1)

    def tile(xn):
        gate = _dot(xn, wg_ref[...])
        up = _dot(xn, wu_ref[...].astype(BF16))
        act = (_silu(gate) * up * 0.5).astype(BF16)
        return _dot(act, wo_ref[...].astype(BF16))

    @pl.when(j == 0)
    def _():
        x = x_ref[...]
        xn = _rms(x, g_ref[...]).astype(BF16)
        xn_ref[...] = xn
        o_ref[...] = x + tile(xn)

    @pl.when(j > 0)
    def _():
        o_ref[...] += tile(xn_ref[...])

    if final:
        @pl.when(j == pl.num_programs(1) - 1)
        def _():
            o_ref[...] = _rms(o_ref[...], fg_ref[...])


def _ffn(h, norm_g, w_gate, w_in, w_out, layer, final_g=None, *, tm=FFN_ROWS, tf=FFN_HIDDEN_TILE):
    m, d = h.shape
    f = w_out.shape[1]
    nt = f // tf
    final = final_g is not None
    vec = pl.BlockSpec((1, d), lambda i, j: (0, 0))
    rows = pl.BlockSpec((tm, d), lambda i, j: (i, 0))
    in_specs = [
        rows, vec,
        pl.BlockSpec((d, tf), lambda i, j: (0, j)),
        pl.BlockSpec((None, d, tf), lambda i, j: (layer, 0, j + nt)),
        pl.BlockSpec((None, tf, d), lambda i, j: (layer, j, 0)),
    ]
    args = [h, norm_g.reshape(1, d), w_gate, w_in, w_out]
    if final:
        in_specs.append(vec)
        args.append(final_g.reshape(1, d))
    return pl.pallas_call(
        functools.partial(_ffn_kernel, final=final),
        grid=(m // tm, nt),
        in_specs=in_specs,
        out_specs=rows,
        out_shape=jax.ShapeDtypeStruct((m, d), F32),
        scratch_shapes=[pltpu.VMEM((tm, d), BF16)],
        compiler_params=_params(("parallel", "arbitrary")),
        name="ffn",
    )(*args)


def _cast_rider(src, layer, cols, steps, step_of):
    rows = src.shape[1] // steps
    in_spec = pl.BlockSpec((None, rows, cols), lambda *g: (layer, step_of(*g), 0))
    out_spec = pl.BlockSpec((rows, cols), lambda *g: (step_of(*g), 0))
    return in_spec, out_spec, jax.ShapeDtypeStruct((src.shape[1], cols), BF16)


def _proj_kernel(*refs, small, rider):
    if rider:
        x_ref, g_ref, w_ref, cs_ref, src_ref, o_ref, dst_ref, xn_ref = refs
        dst_ref[...] = src_ref[...].astype(BF16)
    elif small:
        x_ref, g_ref, w_ref, cs_ref, ws_ref, o_ref, os_ref, xn_ref = refs
    else:
        x_ref, g_ref, w_ref, cs_ref, o_ref, xn_ref = refs
    j = pl.program_id(1)

    def project(xn):
        o_ref[...] = (_dot(xn, w_ref[...]) * cs_ref[...]).astype(o_ref.dtype)

    @pl.when(j == 0)
    def _():
        xn = _rms(x_ref[...], g_ref[...]).astype(BF16)
        xn_ref[...] = xn
        if small:
            os_ref[...] = _dot(xn, ws_ref[...])
        project(xn)

    @pl.when(j > 0)
    def _():
        project(xn_ref[...])


def _proj(x, norm_g, w, layer, col_scale, out_dtype, w_small=None, rider=None, *, tm=PROJ_ROWS, tn=PROJ_COLS):
    m, d = x.shape
    n = w.shape[2]
    small = w_small is not None
    assert not (small and rider)
    in_specs = [
        pl.BlockSpec((tm, d), lambda i, j: (i, 0)),
        pl.BlockSpec((1, d), lambda i, j: (0, 0)),
        pl.BlockSpec((None, d, tn), lambda i, j: (layer, 0, j)),
        pl.BlockSpec((1, tn), lambda i, j: (0, j)),
    ]
    args = [x, norm_g.reshape(1, d), w, col_scale]
    out_specs = [pl.BlockSpec((tm, tn), lambda i, j: (i, j))]
    out_shape = [jax.ShapeDtypeStruct((m, n), out_dtype)]
    if small:
        in_specs.append(pl.BlockSpec((None, d, LANE), lambda i, j: (layer, 0, 0)))
        args.append(w_small)
        out_specs.append(pl.BlockSpec((tm, LANE), lambda i, j: (i, 0)))
        out_shape.append(jax.ShapeDtypeStruct((m, LANE), F32))
    if rider:
        nj = n // tn
        r_in, r_out, r_shape = _cast_rider(*rider, steps=(m // tm) * nj, step_of=lambda i, j: i * nj + j)
        in_specs.append(r_in)
        args.append(rider[0])
        out_specs.append(r_out)
        out_shape.append(r_shape)
    out = pl.pallas_call(
        functools.partial(_proj_kernel, small=small, rider=rider is not None),
        grid=(m // tm, n // tn),
        in_specs=in_specs,
        out_specs=out_specs,
        out_shape=out_shape,
        scratch_shapes=[pltpu.VMEM((tm, d), BF16)],
        compiler_params=_params(("parallel", "arbitrary")),
        name="proj",
    )(*args)
    return out if small or rider else out[0]


def _norm_proj_kernel(x_ref, g_ref, w_ref, o_ref):
    o_ref[...] = _dot(_rms(x_ref[...], g_ref[...]).astype(BF16), w_ref[...].astype(BF16))


def _norm_proj(x, norm_g, w, layer, *, tm=MEM_KV_ROWS):
    m, d = x.shape
    n = w.shape[2]
    return pl.pallas_call(
        _norm_proj_kernel,
        grid=(m // tm,),
        in_specs=[
            pl.BlockSpec((tm, d), lambda i: (i, 0)),
            pl.BlockSpec((1, d), lambda i: (0, 0)),
            pl.BlockSpec((None, d, n), lambda i: (layer, 0, 0)),
        ],
        out_specs=pl.BlockSpec((tm, n), lambda i: (i, 0)),
        out_shape=jax.ShapeDtypeStruct((m, n), F32),
        compiler_params=_params(("parallel",)),
        name="mem_kv",
    )(x, norm_g.reshape(1, d), w)


def _sgu_kernel(u_ref, v_ref, lng_ref, lnb_ref, ws_ref, bs_ref, o_ref, *, chunks):
    t = SGU_CHUNK
    gd = SGU_GROUP_DIM
    row = lax.broadcasted_iota(jnp.int32, (t, t), 0)
    col = lax.broadcasted_iota(jnp.int32, (t, t), 1)
    causal = row >= col
    for g in range(SGU_GROUPS):
        w = jnp.where(causal, ws_ref[g], 0.0).astype(BF16)
        bias = bs_ref[:, g:g + 1]
        ln_g = lng_ref[:, g * gd:(g + 1) * gd]
        ln_b = lnb_ref[:, g * gd:(g + 1) * gd]
        for c in range(chunks):
            rows = slice(c * t, (c + 1) * t)
            cols = slice(g * gd, (g + 1) * gd)
            v = _gelu_tanh(v_ref[rows, cols])
            mu = jnp.mean(v, axis=-1, keepdims=True)
            vc = v - mu
            var = jnp.mean(vc * vc, axis=-1, keepdims=True)
            vn = vc * lax.rsqrt(var + EPS) * ln_g + ln_b
            mixed = _dot(w, vn.astype(BF16)) + bias
            o_ref[rows, cols] = (_gelu_tanh(u_ref[rows, cols]) * mixed).astype(o_ref.dtype)


FOX_AUG = 6


def _fox_gate_kernel(f_ref, bf_ref, tril_ref, pq_ref, pk_ref, cq_ref, ck_ref, qa_ref, ka_ref):
    t = LANE
    seq = f_ref.shape[0]
    carry = jnp.zeros((1, t), F32)
    for b in range(seq // t):
        rows = slice(b * t, (b + 1) * t)
        lf = _log_sigmoid(f_ref[rows, :] + bf_ref[...]) * LOG2E
        cum = _dot_01(tril_ref[...], lf) + carry
        carry = cum[t - 1:t, :]
        parts = jnp.concatenate(_split3(cum), axis=1)
        qa_ref[rows, :] = (_dot(parts, pq_ref[...]) + cq_ref[...]).astype(BF16)
        ka_ref[rows, :] = (_dot(parts, pk_ref[...]) + ck_ref[...]).astype(BF16)


def _tril3(n):
    tril = jnp.tril(jnp.ones((n, n), BF16))
    return jnp.concatenate([tril, tril, tril], axis=1)


def _fox_placement():
    src = jnp.arange(3 * LANE)[:, None]
    dst = jnp.arange(LANE)[None, :]
    p, h = src // LANE, src % LANE
    valid = h < FOX_HEADS
    pq = jnp.where(valid & (dst == h * FOX_AUG + p), 1.0, 0.0).astype(BF16)
    pk = jnp.where(valid & (dst == h * FOX_AUG + 3 + p), -1.0, 0.0).astype(BF16)
    used = dst < FOX_HEADS * FOX_AUG
    cq = jnp.where(used & (dst % FOX_AUG >= 3), 1.0, 0.0).astype(F32)
    ck = jnp.where(used & (dst % FOX_AUG < 3), 1.0, 0.0).astype(F32)
    return pq, pk, cq, ck


def _fox_gate(small, b_f, batch, seq):
    bf = jnp.zeros((1, LANE), F32).at[0, SMALL_FF:SMALL_FF + FOX_HEADS].set(b_f)
    w = LANE
    aug = jax.ShapeDtypeStruct((batch * seq, w), BF16)

    def const(shape):
        return pl.BlockSpec(shape, lambda b: (0, 0))

    return pl.pallas_call(
        _fox_gate_kernel,
        grid=(batch,),
        in_specs=[
            pl.BlockSpec((seq, LANE), lambda b: (b, 0)),
            const((1, LANE)), const((LANE, 3 * LANE)),
            const((3 * LANE, w)), const((3 * LANE, w)), const((1, w)), const((1, w)),
        ],
        out_specs=[pl.BlockSpec((seq, w), lambda b: (b, 0))] * 2,
        out_shape=[aug, aug],
        compiler_params=_params(("parallel",)),
        name="fox_gate",
    )(small, bf, _tril3(LANE), *_fox_placement())


def _fox_kernel(*refs, t, heads, riders):
    q_ref, k_ref, v_ref, qa_ref, ka_ref = refs[:5]
    o_ref = refs[5 + riders]
    for src_ref, dst_ref in zip(refs[5:5 + riders], refs[6 + riders:]):
        dst_ref[...] = src_ref[...].astype(BF16)
    seq = q_ref.shape[0]
    dh = FOX_HEAD_DIM
    lane = lax.broadcasted_iota(jnp.int32, (t, LANE), 1)
    ones_col = jnp.where(lane == 0, 1.0, 0.0).astype(BF16)
    row = lax.broadcasted_iota(jnp.int32, (t, t), 0)
    col = lax.broadcasted_iota(jnp.int32, (t, t), 1)
    causal = row >= col

    for g in range(heads):
        head = pl.program_id(1) * heads + g
        own = (lane >= head * FOX_AUG) & (lane < (head + 1) * FOX_AUG)
        cols = slice(g * dh, (g + 1) * dh)

        def k_block(j):
            rows = slice(j * t, (j + 1) * t)
            ka = jnp.where(own, ka_ref[rows, :], jnp.zeros((), BF16))
            return (jnp.concatenate([k_ref[rows, cols], ka], axis=1),
                    jnp.concatenate([v_ref[rows, cols], ones_col], axis=1))

        for i in range(seq // t):
            rows = slice(i * t, (i + 1) * t)
            qc = jnp.concatenate([q_ref[rows, cols], qa_ref[rows, :]], axis=1)
            kc, vc = k_block(i)
            s = jnp.where(causal, _dot_nt(qc, kc), NEG_BIG)
            m = jnp.max(s, axis=1, keepdims=True)
            acc = _dot(jnp.exp2(s - m).astype(BF16), vc)
            for j in range(i):
                kc, vc = k_block(j)
                s = _dot_nt(qc, kc)
                m_new = jnp.maximum(m, jnp.max(s, axis=1, keepdims=True))
                acc = jnp.exp2(m - m_new) * acc + _dot(jnp.exp2(s - m_new).astype(BF16), vc)
                m = m_new
            o_ref[rows, cols] = (acc[:, :dh] / acc[:, dh:dh + 1]).astype(o_ref.dtype)


def _fox(zb, qa, ka, batch, seq, riders=(), *, t=FOX_TILE, heads=FOX_HEADS_PER_STEP):
    m = zb.shape[0]
    w = heads * FOX_HEAD_DIM
    groups = FOX_HEADS // heads

    def group(seg):
        return pl.BlockSpec((seq, w), lambda b, h: (b, seg * LANE // w + h))

    aug = pl.BlockSpec((seq, LANE), lambda b, h: (b, 0))
    rider_specs = [_cast_rider(*r, steps=batch * groups, step_of=lambda b, h: b * groups + h) for r in riders]
    out = pl.pallas_call(
        functools.partial(_fox_kernel, t=t, heads=heads, riders=len(riders)),
        grid=(batch, groups),
        in_specs=[group(ZB_FQ), group(ZB_FK), group(ZB_FV), aug, aug] + [r[0] for r in rider_specs],
        out_specs=[group(0)] + [r[1] for r in rider_specs],
        out_shape=[jax.ShapeDtypeStruct((m, FOX_WIDTH), BF16)] + [r[2] for r in rider_specs],
        compiler_params=_params(("parallel", "parallel")),
        name="fox",
    )(zb, zb, zb, qa, ka, *[r[0] for r in riders])
    return out if riders else out[0]


GLA_BLOCK = 4 * GLA_CHUNK


def _gla_decay_kernel(a_ref, wg_ref, bg_ref, cm_ref, o_ref, *, blocks):
    t = GLA_BLOCK
    for b in range(blocks):
        rows = slice(b * t, (b + 1) * t)
        gl = _dot(a_ref[rows, :].astype(BF16), wg_ref[...]) + bg_ref[...]
        g = _log_sigmoid(gl) * (LOG2E / GLA_GATE_TAU)
        o_ref[rows, :] = _dot_01(cm_ref[...], g)


def _gla_decay(small, w_gate_emb, b_gate, *, blocks=GLA_DECAY_BLOCKS_PER_STEP):
    m = small.shape[0]
    t = GLA_BLOCK
    rows = blocks * t
    idx = jnp.arange(t)
    same = (idx[:, None] // GLA_CHUNK) == (idx[None, :] // GLA_CHUNK)
    cm = jnp.where(same & (idx[:, None] >= idx[None, :]), 1.0, 0.0).astype(BF16)
    cm3 = jnp.concatenate([cm, cm, cm], axis=1)
    return pl.pallas_call(
        functools.partial(_gla_decay_kernel, blocks=blocks),
        grid=(m // rows,),
        in_specs=[
            pl.BlockSpec((rows, LANE), lambda i: (i, 0)),
            pl.BlockSpec((LANE, GLA_KW), lambda i: (0, 0)),
            pl.BlockSpec((1, GLA_KW), lambda i: (0, 0)),
            pl.BlockSpec((t, 3 * t), lambda i: (0, 0)),
        ],
        out_specs=pl.BlockSpec((rows, GLA_KW), lambda i: (i, 0)),
        out_shape=jax.ShapeDtypeStruct((m, GLA_KW), F32),
        compiler_params=_params(("parallel",)),
        name="gla_decay",
    )(small, w_gate_emb, b_gate.reshape(1, GLA_KW), cm3)


def _gla_kernel(q_ref, k_ref, v_ref, r_ref, bc_ref, on_ref, o_ref, *, heads):
    for g in range(heads):
        _gla_head(q_ref, k_ref, v_ref, r_ref, bc_ref, on_ref, o_ref,
                  slice(g * GLA_DK, (g + 1) * GLA_DK), slice(g * GLA_DV, (g + 1) * GLA_DV))


def _gla_head(q_ref, k_ref, v_ref, r_ref, bc_ref, on_ref, o_ref, kcols, vcols):
    c = GLA_CHUNK
    t = GLA_BLOCK
    dk = GLA_DK
    seq = q_ref.shape[0]
    st = jnp.zeros((GLA_DV, dk), F32)
    row = lax.broadcasted_iota(jnp.int32, (t, t), 0)
    col = lax.broadcasted_iota(jnp.int32, (t, t), 1)
    intra = (row >= col) & ((row // c) == (col // c))
    o_gain = on_ref[:, vcols]

    for n in range(seq // t):
        rows = slice(n * t, (n + 1) * t)
        bc = bc_ref[rows, kcols]
        qb = q_ref[rows, kcols] * (dk ** -0.5)
        kb = k_ref[rows, kcols]
        vb = v_ref[rows, vcols]
        chunks = [slice(i * c, (i + 1) * c) for i in range(t // c)]
        b_mid = jnp.concatenate(
            [jnp.broadcast_to(bc[s.start + c // 2:s.start + c // 2 + 1, :], (c, dk)) for s in chunks], axis=0)
        b_last = jnp.concatenate(
            [jnp.broadcast_to(bc[s.stop - 1:s.stop, :], (c, dk)) for s in chunks], axis=0)
        qd = (qb * jnp.exp2(bc - b_mid)).astype(BF16)
        kd = (kb * jnp.exp2(b_mid - bc)).astype(BF16)
        ku = (kb * jnp.exp2(b_last - bc)).astype(BF16)
        qi = (qb * jnp.exp2(bc)).astype(BF16)
        att = jnp.where(intra, _dot_nt(qd, kd), 0.0)
        o = _dot(att.astype(BF16), vb)
        inter = []
        for s in chunks:
            inter.append(_dot_nt(qi[s, :], st.astype(BF16)))
            st = st * jnp.exp2(bc[s.stop - 1:s.stop, :]) + _dot_tn(vb[s, :], ku[s, :])
        o = _rms(o + jnp.concatenate(inter, axis=0), o_gain)
        o_ref[rows, vcols] = (o * _silu(r_ref[rows, vcols])).astype(o_ref.dtype)


def _gla(zf, zb, bc, o_norm, batch, seq, *, heads=GLA_HEADS_PER_STEP):
    m = zf.shape[0]
    dk, dv = heads * GLA_DK, heads * GLA_DV
    return pl.pallas_call(
        functools.partial(_gla_kernel, heads=heads),
        grid=(batch, GLA_HEADS // heads),
        in_specs=[
            pl.BlockSpec((seq, dk), lambda b, h: (b, ZF_GQ * LANE // dk + h)),
            pl.BlockSpec((seq, dk), lambda b, h: (b, ZF_GK * LANE // dk + h)),
            pl.BlockSpec((seq, dv), lambda b, h: (b, ZB_GV * LANE // dv + h)),
            pl.BlockSpec((seq, dv), lambda b, h: (b, ZF_GR * LANE // dv + h)),
            pl.BlockSpec((seq, dk), lambda b, h: (b, h)),
            pl.BlockSpec((1, dv), lambda b, h: (0, h)),
        ],
        out_specs=pl.BlockSpec((seq, dv), lambda b, h: (b, h)),
        out_shape=jax.ShapeDtypeStruct((m, GLA_VW), BF16),
        compiler_params=_params(("parallel", "parallel")),
        name="gla",
    )(zf, zf, zb, zf, bc, o_norm.reshape(1, GLA_VW))


def _merge_kernel(u0_ref, v0_ref, u_ref, v_ref, lng_ref, lnb_ref, ws_ref, bs_ref, b_ref, c_ref,
                  ga_ref, gb_ref, gc_ref, wa_ref, wb_ref, wc_ref, wo_ref, h_ref, o_ref, ya_ref):
    i = pl.program_id(0)
    chunks = ya_ref.shape[1] // SGU_CHUNK

    @pl.when(i == 0)
    def _():
        _sgu_kernel(u0_ref, v0_ref, lng_ref, lnb_ref, ws_ref, bs_ref, ya_ref.at[0], chunks=chunks)

    slot = i % 2
    mix = (jax.nn.sigmoid(ga_ref[...]) * _dot(ya_ref[slot], wa_ref[...])
           + jax.nn.sigmoid(gb_ref[...]) * _dot(b_ref[...], wb_ref[...])
           + jax.nn.sigmoid(gc_ref[...]) * _dot(c_ref[...], wc_ref[...]))
    o_ref[...] = h_ref[...] + _dot(mix.astype(BF16), wo_ref[...])
    _sgu_kernel(u_ref, v_ref, lng_ref, lnb_ref, ws_ref, bs_ref, ya_ref.at[1 - slot], chunks=chunks)


def _merge(sgu_params, yb, yc, zf, wa, wb, wc, wo, h, *, tm=MERGE_ROWS):
    m, d = h.shape
    kw = yb.shape[1]
    g0 = ZF_GATES * LANE // d
    last = m // tm - 1
    ln_g, ln_b, w_s, b_s = sgu_params
    branch = pl.BlockSpec((tm, kw), lambda i: (i, 0))
    resident = dict(pipeline_mode=pl.Buffered(1))
    wspec = pl.BlockSpec((kw, d), lambda i: (0, 0), **resident)
    return pl.pallas_call(
        _merge_kernel,
        grid=(m // tm,),
        in_specs=[
            pl.BlockSpec((tm, kw), lambda i: (0, ZF_SU * LANE // kw)),
            pl.BlockSpec((tm, kw), lambda i: (0, ZF_SV * LANE // kw)),
            pl.BlockSpec((tm, kw), lambda i: (jnp.minimum(i + 1, last), ZF_SU * LANE // kw)),
            pl.BlockSpec((tm, kw), lambda i: (jnp.minimum(i + 1, last), ZF_SV * LANE // kw)),
            pl.BlockSpec((1, kw), lambda i: (0, 0)),
            pl.BlockSpec((1, kw), lambda i: (0, 0)),
            pl.BlockSpec((SGU_GROUPS, SGU_CHUNK, SGU_CHUNK), lambda i: (0, 0, 0)),
            pl.BlockSpec((SGU_CHUNK, SGU_GROUPS), lambda i: (0, 0)),
            branch, branch,
            pl.BlockSpec((tm, d), lambda i: (i, g0)),
            pl.BlockSpec((tm, d), lambda i: (i, g0 + 1)),
            pl.BlockSpec((tm, d), lambda i: (i, g0 + 2)),
            wspec, wspec, wspec,
            pl.BlockSpec((d, d), lambda i: (0, 0), **resident),
            pl.BlockSpec((tm, d), lambda i: (i, 0)),
        ],
        out_specs=pl.BlockSpec((tm, d), lambda i: (i, 0)),
        out_shape=jax.ShapeDtypeStruct((m, d), F32),
        scratch_shapes=[pltpu.VMEM((2, tm, kw), BF16)],
        compiler_params=_params(("arbitrary",)),
        name="merge",
    )(zf, zf, zf, zf, ln_g.reshape(1, kw), ln_b.reshape(1, kw), w_s, jnp.swapaxes(b_s, 0, 1),
      yb, yc, zf, zf, zf, wa, wb, wc, wo, h)


def _xattn_kernel(*refs, rider):
    if rider:
        h_ref, g_ref, wq_ref, kv_ref, wo_ref, src_ref, o_ref, dst_ref = refs
        dst_ref[...] = src_ref[...].astype(BF16)
    else:
        h_ref, g_ref, wq_ref, kv_ref, wo_ref, o_ref = refs
    dh = XA_HEAD_DIM
    x = h_ref[...]
    n = _rms(x, g_ref[...]).astype(BF16)
    q = _dot(n, wq_ref[...].astype(BF16)) * (dh ** -0.5)
    outs = []
    for hd in range(XA_HEADS):
        qh = q[:, hd * dh:(hd + 1) * dh].astype(BF16)
        kh = kv_ref[:, hd * dh:(hd + 1) * dh].astype(BF16)
        vh = kv_ref[:, XA_WIDTH + hd * dh:XA_WIDTH + (hd + 1) * dh].astype(BF16)
        s = _dot_nt(qh, kh)
        e = jnp.exp(s - jnp.max(s, axis=-1, keepdims=True))
        p = e / jnp.sum(e, axis=-1, keepdims=True)
        outs.append(_dot(p.astype(BF16), vh))
    o = jnp.concatenate(outs, axis=1).astype(BF16)
    o_ref[...] = x + _dot(o, wo_ref[...].astype(BF16))


def _xattn(h, norm_g, wq, kv, wo, layer, seq, mem_len, rider=None, *, tm=XATTN_ROWS):
    m, d = h.shape
    per_batch = seq // tm
    in_specs = [
        pl.BlockSpec((tm, d), lambda i: (i, 0)),
        pl.BlockSpec((1, d), lambda i: (0, 0)),
        pl.BlockSpec((None, d, XA_WIDTH), lambda i: (layer, 0, 0)),
        pl.BlockSpec((mem_len, 2 * XA_WIDTH), lambda i: (i // per_batch, 0)),
        pl.BlockSpec((None, XA_WIDTH, d), lambda i: (layer, 0, 0)),
    ]
    args = [h, norm_g.reshape(1, d), wq, kv, wo]
    out_specs = [pl.BlockSpec((tm, d), lambda i: (i, 0))]
    out_shape = [jax.ShapeDtypeStruct((m, d), F32)]
    if rider:
        r_in, r_out, r_shape = _cast_rider(*rider, steps=m // tm, step_of=lambda i: i)
        in_specs.append(r_in)
        args.append(rider[0])
        out_specs.append(r_out)
        out_shape.append(r_shape)
    out = pl.pallas_call(
        functools.partial(_xattn_kernel, rider=rider is not None),
        grid=(m // tm,),
        in_specs=in_specs,
        out_specs=out_specs,
        out_shape=out_shape,
        compiler_params=_params(("parallel",)),
        name="xattn",
    )(*args)
    return out if rider else out[0]


CAST_BLOCK_BYTES = 6 * 1024 * 1024


def _cast_kernel(x_ref, o_ref):
    o_ref[...] = x_ref[...].astype(o_ref.dtype)


def _to_bf16(w, cols=None, layers=None):
    lead = w.shape[:-1] if layers is None else (layers,) + w.shape[1:-1]
    c = w.shape[-1] if cols is None else cols
    w2 = w.reshape(-1, w.shape[-1])
    r = w2.shape[0] if layers is None else layers * (w2.shape[0] // w.shape[0])
    rows = 16
    while rows * 2 * c * 4 <= CAST_BLOCK_BYTES and r % (rows * 2) == 0:
        rows *= 2
    spec = pl.BlockSpec((rows, c), lambda i: (i, 0))
    out = pl.pallas_call(
        _cast_kernel,
        grid=(r // rows,),
        in_specs=[spec],
        out_specs=spec,
        out_shape=jax.ShapeDtypeStruct((r, c), BF16),
        compiler_params=_params(("parallel",)),
        name="cast",
    )(w2)
    return out.reshape(*lead, c)

W_FQ = 2 * SGU_WIDTH
W_FF = W_FQ + 3 * FOX_WIDTH
W_GQ = W_FF + FOX_HEADS
W_GV = W_GQ + 2 * GLA_KW
W_GA = W_GV + GLA_VW
W_GR = W_GA + GLA_GATE_RANK
W_END = W_GR + GLA_VW + 3 * D_MODEL


REPACK_CHUNK = 1024


def _repack_kernel(wt_ref, wf_ref, wb_ref, ws_ref):
    def move(dst_ref, dst, src, width):
        for off in range(0, width, REPACK_CHUNK):
            n = min(REPACK_CHUNK, width - off)
            dst_ref[:, dst + off:dst + off + n] = wt_ref[src + off:src + off + n, :].T.astype(BF16)

    move(wf_ref, 0, 0, W_FQ)
    move(wf_ref, W_FQ, W_GQ, 2 * GLA_KW)
    move(wf_ref, W_FQ + 2 * GLA_KW, W_GR, W_END - W_GR)
    move(wb_ref, 0, W_FQ, 3 * FOX_WIDTH)
    move(wb_ref, 3 * FOX_WIDTH, W_GV, GLA_VW)
    lane = lax.broadcasted_iota(jnp.int32, (wt_ref.shape[1], LANE), 1)
    ff = wt_ref[W_FF - SMALL_FF:W_FF - SMALL_FF + LANE, :].T
    ga = wt_ref[W_GA - SMALL_GA:W_GA - SMALL_GA + LANE, :].T
    side = jnp.where(lane < SMALL_FF + FOX_HEADS, ff,
                     jnp.where(lane < SMALL_GA + GLA_GATE_RANK, ga, 0.0))
    ws_ref[...] = side.astype(BF16)


def _repack_w_in(w, *, kb=REPACK_K_ROWS):
    nl, d, n = w.shape
    wt = jnp.swapaxes(w, 1, 2)

    def spec(width):
        return pl.BlockSpec((None, kb, width), lambda l, i: (l, i, 0))

    return pl.pallas_call(
        _repack_kernel,
        grid=(nl, d // kb),
        in_specs=[pl.BlockSpec((None, n, kb), lambda l, i: (l, 0, i))],
        out_specs=[spec(ZF_WIDTH), spec(ZB_WIDTH), spec(LANE)],
        out_shape=[jax.ShapeDtypeStruct((nl, d, width), BF16) for width in (ZF_WIDTH, ZB_WIDTH, LANE)],
        compiler_params=_params(("parallel", "parallel")),
        name="repack",
    )(wt)


def _embed_w_gate(w_gate):
    emb = jnp.zeros((LANE, GLA_KW), BF16)
    return emb.at[SMALL_GA:SMALL_GA + GLA_GATE_RANK, :].set(w_gate.astype(BF16))


def kernel(x, mem, ffn1_norm, ffn1_w_in, ffn1_w_out, mix_norm, w_in, sgu_ln_g, sgu_ln_b, sgu_w_s, sgu_b_s, fox_b_f, gla_w_gate, gla_b_gate, gla_o_norm, w_branch_a, w_branch_b, w_branch_c, w_out, xa_norm, mem_norm, xa_w_q, xa_w_kv, xa_w_o, ffn2_norm, ffn2_w_in, ffn2_w_out, final_norm):
    batch, seq, d = x.shape
    mem_len = mem.shape[1]
    h = x.reshape(batch * seq, d)
    mem2 = mem.reshape(batch * mem_len, d)
    ones_f = jnp.ones((1, ZF_WIDTH), F32)
    scale_b = jnp.ones((1, ZB_WIDTH), F32).at[:, ZB_FQ * LANE:ZB_FQ * LANE + FOX_WIDTH].set(
        FOX_HEAD_DIM ** -0.5 * LOG2E)
    f1_gate = _to_bf16(ffn1_w_in, D_FF, layers=1)[0]
    w_f, w_b, w_small = _repack_w_in(w_in)
    for l in range(DEPTH):
        h = _ffn(h, ffn1_norm[l], f1_gate, ffn1_w_in, ffn1_w_out, l)

        zf, small = _proj(h, mix_norm[l], w_f, l, ones_f, F32, w_small)
        zb, f2_gate = _proj(h, mix_norm[l], w_b, l, scale_b, BF16, rider=(ffn2_w_in, l, D_FF))
        qa, ka = _fox_gate(small, fox_b_f[l], batch, seq)
        merge_w = [(w, l, w.shape[2]) for w in (w_branch_a, w_branch_b, w_branch_c, w_out)]
        yb, wa, wb, wc, wo = _fox(zb, qa, ka, batch, seq, merge_w)
        bc = _gla_decay(small, _embed_w_gate(gla_w_gate[l]), gla_b_gate[l])
        yc = _gla(zf, zb, bc, gla_o_norm[l], batch, seq)
        sgu_params = (sgu_ln_g[l], sgu_ln_b[l], sgu_w_s[l], sgu_b_s[l])
        h = _merge(sgu_params, yb, yc, zf, wa, wb, wc, wo, h)

        kv = _norm_proj(mem2, mem_norm[l], xa_w_kv, l)
        if l + 1 < DEPTH:
            h, f1_gate = _xattn(h, xa_norm[l], xa_w_q, kv, xa_w_o, l, seq, mem_len,
                                rider=(ffn1_w_in, l + 1, D_FF))
        else:
            h = _xattn(h, xa_norm[l], xa_w_q, kv, xa_w_o, l, seq, mem_len)

        h = _ffn(h, ffn2_norm[l], f2_gate, ffn2_w_in, ffn2_w_out, l,
                 final_norm if l == DEPTH - 1 else None)
    return h.reshape(batch, seq, d)
```

```python
import functools

import jax
import jax.numpy as jnp
from jax import lax
from jax.experimental import pallas as pl
from jax.experimental.pallas import tpu as pltpu

F32 = jnp.float32
BF16 = jnp.bfloat16

D_MODEL = 2048
DEPTH = 2
D_FF = 5632
EPS = 1e-6

SGU_GROUPS = 4
SGU_GROUP_DIM = 256
SGU_WIDTH = 1024
SGU_CHUNK = 128

FOX_HEADS = 8
FOX_HEAD_DIM = 128
FOX_WIDTH = 1024

GLA_HEADS = 4
GLA_DK = 128
GLA_DV = 256
GLA_KW = 512
GLA_VW = 1024
GLA_GATE_RANK = 16
GLA_GATE_TAU = 16.0
GLA_CHUNK = 64

XA_HEADS = 4
XA_HEAD_DIM = 128
XA_WIDTH = 512

LANE = 128

ZF_SU, ZF_SV, ZF_GQ, ZF_GK, ZF_GR, ZF_GATES = 0, 8, 16, 20, 24, 32
ZF_WIDTH = 80 * LANE
ZB_FQ, ZB_FK, ZB_FV, ZB_GV = 0, 8, 16, 24
ZB_WIDTH = 32 * LANE
SMALL_FF = 0
SMALL_GA = 8

NEG_BIG = -1e30
LOG2E = 1.4426950408889634
VMEM_LIMIT = 63 * 1024 * 1024

FFN_ROWS, FFN_HIDDEN_TILE = 1024, 512
PROJ_ROWS, PROJ_COLS = 1024, 2048
MEM_KV_ROWS = 1024
FOX_TILE, FOX_HEADS_PER_STEP = 512, 4
GLA_DECAY_BLOCKS_PER_STEP = 4
GLA_HEADS_PER_STEP = 2
MERGE_ROWS = 256
XATTN_ROWS = 1024
REPACK_K_ROWS = 256


def _params(sem):
    return pltpu.CompilerParams(dimension_semantics=sem, vmem_limit_bytes=VMEM_LIMIT)


def _rms(x, g):
    return x * lax.rsqrt(jnp.mean(x * x, axis=-1, keepdims=True) + EPS) * g


def _log_sigmoid(x):
    return jnp.minimum(x, 0.0) - jnp.log1p(jnp.exp(-jnp.abs(x)))


def _gelu_tanh(x):
    c = 0.7978845608028654
    return x * (0.5 * (1.0 + jnp.tanh(c * (x + 0.044715 * (x * x * x)))))


def _silu(x):
    return x * jax.nn.sigmoid(x)


def _dot(a, b):
    return jnp.dot(a, b, preferred_element_type=F32)


def _dot_nt(a, b):
    return lax.dot_general(a, b, (((1,), (1,)), ((), ())), preferred_element_type=F32)


def _dot_tn(a, b):
    return lax.dot_general(a, b, (((0,), (0,)), ((), ())), preferred_element_type=F32)


def _split3(x):
    hi = x.astype(BF16)
    r = x - hi.astype(F32)
    lo = r.astype(BF16)
    lo2 = (r - lo.astype(F32)).astype(BF16)
    return hi, lo, lo2


def _dot_01(mat3, x):
    return _dot(mat3, jnp.concatenate(_split3(x), axis=0))


def _ffn_kernel(*refs, final):
    if final:
        x_ref, g_ref, wg_ref, wu_ref, wo_ref, fg_ref, o_ref, xn_ref = refs
    else:
        x_ref, g_ref, wg_ref, wu_ref, wo_ref, o_ref, xn_ref = refs
    j = pl.program_id(1)

    def tile(xn):
        gate = _dot(xn, wg_ref[...])
        up = _dot(xn, wu_ref[...].astype(BF16))
        act = (_silu(gate) * up * 0.5).astype(BF16)
        return _dot(act, wo_ref[...].astype(BF16))

    @pl.when(j == 0)
    def _():
        x = x_ref[...]
        xn = _rms(x, g_ref[...]).astype(BF16)
        xn_ref[...] = xn
        o_ref[...] = x + tile(xn)

    @pl.when(j > 0)
    def _():
        o_ref[...] += tile(xn_ref[...])

    if final:
        @pl.when(j == pl.num_programs(1) - 1)
        def _():
            o_ref[...] = _rms(o_ref[...], fg_ref[...])


def _ffn(h, norm_g, w_gate, w_in, w_out, layer, final_g=None, *, tm=FFN_ROWS, tf=FFN_HIDDEN_TILE):
    m, d = h.shape
    f = w_out.shape[1]
    nt = f // tf
    final = final_g is not None
    vec = pl.BlockSpec((1, d), lambda i, j: (0, 0))
    rows = pl.BlockSpec((tm, d), lambda i, j: (i, 0))
    in_specs = [
        rows, vec,
        pl.BlockSpec((d, tf), lambda i, j: (0, j)),
        pl.BlockSpec((None, d, tf), lambda i, j: (layer, 0, j + nt)),
        pl.BlockSpec((None, tf, d), lambda i, j: (layer, j, 0)),
    ]
    args = [h, norm_g.reshape(1, d), w_gate, w_in, w_out]
    if final:
        in_specs.append(vec)
        args.append(final_g.reshape(1, d))
    return pl.pallas_call(
        functools.partial(_ffn_kernel, final=final),
        grid=(m // tm, nt),
        in_specs=in_specs,
        out_specs=rows,
        out_shape=jax.ShapeDtypeStruct((m, d), F32),
        scratch_shapes=[pltpu.VMEM((tm, d), BF16)],
        compiler_params=_params(("parallel", "arbitrary")),
        name="ffn",
    )(*args)


def _cast_rider(src, layer, cols, steps, step_of):
    rows = src.shape[1] // steps
    in_spec = pl.BlockSpec((None, rows, cols), lambda *g: (layer, step_of(*g), 0))
    out_spec = pl.BlockSpec((rows, cols), lambda *g: (step_of(*g), 0))
    return in_spec, out_spec, jax.ShapeDtypeStruct((src.shape[1], cols), BF16)


def _proj_kernel(*refs, small, rider):
    if rider:
        x_ref, g_ref, w_ref, cs_ref, src_ref, o_ref, dst_ref, xn_ref = refs
        dst_ref[...] = src_ref[...].astype(BF16)
    elif small:
        x_ref, g_ref, w_ref, cs_ref, ws_ref, o_ref, os_ref, xn_ref = refs
    else:
        x_ref, g_ref, w_ref, cs_ref, o_ref, xn_ref = refs
    j = pl.program_id(1)

    def project(xn):
        o_ref[...] = (_dot(xn, w_ref[...]) * cs_ref[...]).astype(o_ref.dtype)

    @pl.when(j == 0)
    def _():
        xn = _rms(x_ref[...], g_ref[...]).astype(BF16)
        xn_ref[...] = xn
        if small:
            os_ref[...] = _dot(xn, ws_ref[...])
        project(xn)

    @pl.when(j > 0)
    def _():
        project(xn_ref[...])


def _proj(x, norm_g, w, layer, col_scale, out_dtype, w_small=None, rider=None, *, tm=PROJ_ROWS, tn=PROJ_COLS):
    m, d = x.shape
    n = w.shape[2]
    small = w_small is not None
    assert not (small and rider)
    in_specs = [
        pl.BlockSpec((tm, d), lambda i, j: (i, 0)),
        pl.BlockSpec((1, d), lambda i, j: (0, 0)),
        pl.BlockSpec((None, d, tn), lambda i, j: (layer, 0, j)),
        pl.BlockSpec((1, tn), lambda i, j: (0, j)),
    ]
    args = [x, norm_g.reshape(1, d), w, col_scale]
    out_specs = [pl.BlockSpec((tm, tn), lambda i, j: (i, j))]
    out_shape = [jax.ShapeDtypeStruct((m, n), out_dtype)]
    if small:
        in_specs.append(pl.BlockSpec((None, d, LANE), lambda i, j: (layer, 0, 0)))
        args.append(w_small)
        out_specs.append(pl.BlockSpec((tm, LANE), lambda i, j: (i, 0)))
        out_shape.append(jax.ShapeDtypeStruct((m, LANE), F32))
    if rider:
        nj = n // tn
        r_in, r_out, r_shape = _cast_rider(*rider, steps=(m // tm) * nj, step_of=lambda i, j: i * nj + j)
        in_specs.append(r_in)
        args.append(rider[0])
        out_specs.append(r_out)
        out_shape.append(r_shape)
    out = pl.pallas_call(
        functools.partial(_proj_kernel, small=small, rider=rider is not None),
        grid=(m // tm, n // tn),
        in_specs=in_specs,
        out_specs=out_specs,
        out_shape=out_shape,
        scratch_shapes=[pltpu.VMEM((tm, d), BF16)],
        compiler_params=_params(("parallel", "arbitrary")),
        name="proj",
    )(*args)
    return out if small or rider else out[0]


def _norm_proj_kernel(x_ref, g_ref, w_ref, o_ref):
    o_ref[...] = _dot(_rms(x_ref[...], g_ref[...]).astype(BF16), w_ref[...].astype(BF16))


def _norm_proj(x, norm_g, w, layer, *, tm=MEM_KV_ROWS):
    m, d = x.shape
    n = w.shape[2]
    return pl.pallas_call(
        _norm_proj_kernel,
        grid=(m // tm,),
        in_specs=[
            pl.BlockSpec((tm, d), lambda i: (i, 0)),
            pl.BlockSpec((1, d), lambda i: (0, 0)),
            pl.BlockSpec((None, d, n), lambda i: (layer, 0, 0)),
        ],
        out_specs=pl.BlockSpec((tm, n), lambda i: (i, 0)),
        out_shape=jax.ShapeDtypeStruct((m, n), F32),
        compiler_params=_params(("parallel",)),
        name="mem_kv",
    )(x, norm_g.reshape(1, d), w)


def _sgu_kernel(u_ref, v_ref, lng_ref, lnb_ref, ws_ref, bs_ref, o_ref, *, chunks):
    t = SGU_CHUNK
    gd = SGU_GROUP_DIM
    row = lax.broadcasted_iota(jnp.int32, (t, t), 0)
    col = lax.broadcasted_iota(jnp.int32, (t, t), 1)
    causal = row >= col
    for g in range(SGU_GROUPS):
        w = jnp.where(causal, ws_ref[g], 0.0).astype(BF16)
        bias = bs_ref[:, g:g + 1]
        ln_g = lng_ref[:, g * gd:(g + 1) * gd]
        ln_b = lnb_ref[:, g * gd:(g + 1) * gd]
        for c in range(chunks):
            rows = slice(c * t, (c + 1) * t)
            cols = slice(g * gd, (g + 1) * gd)
            v = _gelu_tanh(v_ref[rows, cols])
            mu = jnp.mean(v, axis=-1, keepdims=True)
            vc = v - mu
            var = jnp.mean(vc * vc, axis=-1, keepdims=True)
            vn = vc * lax.rsqrt(var + EPS) * ln_g + ln_b
            mixed = _dot(w, vn.astype(BF16)) + bias
            o_ref[rows, cols] = (_gelu_tanh(u_ref[rows, cols]) * mixed).astype(o_ref.dtype)


FOX_AUG = 6


def _fox_gate_kernel(f_ref, bf_ref, tril_ref, pq_ref, pk_ref, cq_ref, ck_ref, qa_ref, ka_ref):
    t = LANE
    seq = f_ref.shape[0]
    carry = jnp.zeros((1, t), F32)
    for b in range(seq // t):
        rows = slice(b * t, (b + 1) * t)
        lf = _log_sigmoid(f_ref[rows, :] + bf_ref[...]) * LOG2E
        cum = _dot_01(tril_ref[...], lf) + carry
        carry = cum[t - 1:t, :]
        parts = jnp.concatenate(_split3(cum), axis=1)
        qa_ref[rows, :] = (_dot(parts, pq_ref[...]) + cq_ref[...]).astype(BF16)
        ka_ref[rows, :] = (_dot(parts, pk_ref[...]) + ck_ref[...]).astype(BF16)


def _tril3(n):
    tril = jnp.tril(jnp.ones((n, n), BF16))
    return jnp.concatenate([tril, tril, tril], axis=1)


def _fox_placement():
    src = jnp.arange(3 * LANE)[:, None]
    dst = jnp.arange(LANE)[None, :]
    p, h = src // LANE, src % LANE
    valid = h < FOX_HEADS
    pq = jnp.where(valid & (dst == h * FOX_AUG + p), 1.0, 0.0).astype(BF16)
    pk = jnp.where(valid & (dst == h * FOX_AUG + 3 + p), -1.0, 0.0).astype(BF16)
    used = dst < FOX_HEADS * FOX_AUG
    cq = jnp.where(used & (dst % FOX_AUG >= 3), 1.0, 0.0).astype(F32)
    ck = jnp.where(used & (dst % FOX_AUG < 3), 1.0, 0.0).astype(F32)
    return pq, pk, cq, ck


def _fox_gate(small, b_f, batch, seq):
    bf = jnp.zeros((1, LANE), F32).at[0, SMALL_FF:SMALL_FF + FOX_HEADS].set(b_f)
    w = LANE
    aug = jax.ShapeDtypeStruct((batch * seq, w), BF16)

    def const(shape):
        return pl.BlockSpec(shape, lambda b: (0, 0))

    return pl.pallas_call(
        _fox_gate_kernel,
        grid=(batch,),
        in_specs=[
            pl.BlockSpec((seq, LANE), lambda b: (b, 0)),
            const((1, LANE)), const((LANE, 3 * LANE)),
            const((3 * LANE, w)), const((3 * LANE, w)), const((1, w)), const((1, w)),
        ],
        out_specs=[pl.BlockSpec((seq, w), lambda b: (b, 0))] * 2,
        out_shape=[aug, aug],
        compiler_params=_params(("parallel",)),
        name="fox_gate",
    )(small, bf, _tril3(LANE), *_fox_placement())


def _fox_kernel(*refs, t, heads, riders):
    q_ref, k_ref, v_ref, qa_ref, ka_ref = refs[:5]
    o_ref = refs[5 + riders]
    for src_ref, dst_ref in zip(refs[5:5 + riders], refs[6 + riders:]):
        dst_ref[...] = src_ref[...].astype(BF16)
    seq = q_ref.shape[0]
    dh = FOX_HEAD_DIM
    lane = lax.broadcasted_iota(jnp.int32, (t, LANE), 1)
    ones_col = jnp.where(lane == 0, 1.0, 0.0).astype(BF16)
    row = lax.broadcasted_iota(jnp.int32, (t, t), 0)
    col = lax.broadcasted_iota(jnp.int32, (t, t), 1)
    causal = row >= col

    for g in range(heads):
        head = pl.program_id(1) * heads + g
        own = (lane >= head * FOX_AUG) & (lane < (head + 1) * FOX_AUG)
        cols = slice(g * dh, (g + 1) * dh)

        def k_block(j):
            rows = slice(j * t, (j + 1) * t)
            ka = jnp.where(own, ka_ref[rows, :], jnp.zeros((), BF16))
            return (jnp.concatenate([k_ref[rows, cols], ka], axis=1),
                    jnp.concatenate([v_ref[rows, cols], ones_col], axis=1))

        for i in range(seq // t):
            rows = slice(i * t, (i + 1) * t)
            qc = jnp.concatenate([q_ref[rows, cols], qa_ref[rows, :]], axis=1)
            kc, vc = k_block(i)
            s = jnp.where(causal, _dot_nt(qc, kc), NEG_BIG)
            m = jnp.max(s, axis=1, keepdims=True)
            acc = _dot(jnp.exp2(s - m).astype(BF16), vc)
            for j in range(i):
                kc, vc = k_block(j)
                s = _dot_nt(qc, kc)
                m_new = jnp.maximum(m, jnp.max(s, axis=1, keepdims=True))
                acc = jnp.exp2(m - m_new) * acc + _dot(jnp.exp2(s - m_new).astype(BF16), vc)
                m = m_new
            o_ref[rows, cols] = (acc[:, :dh] / acc[:, dh:dh + 1]).astype(o_ref.dtype)


def _fox(zb, qa, ka, batch, seq, riders=(), *, t=FOX_TILE, heads=FOX_HEADS_PER_STEP):
    m = zb.shape[0]
    w = heads * FOX_HEAD_DIM
    groups = FOX_HEADS // heads

    def group(seg):
        return pl.BlockSpec((seq, w), lambda b, h: (b, seg * LANE // w + h))

    aug = pl.BlockSpec((seq, LANE), lambda b, h: (b, 0))
    rider_specs = [_cast_rider(*r, steps=batch * groups, step_of=lambda b, h: b * groups + h) for r in riders]
    out = pl.pallas_call(
        functools.partial(_fox_kernel, t=t, heads=heads, riders=len(riders)),
        grid=(batch, groups),
        in_specs=[group(ZB_FQ), group(ZB_FK), group(ZB_FV), aug, aug] + [r[0] for r in rider_specs],
        out_specs=[group(0)] + [r[1] for r in rider_specs],
        out_shape=[jax.ShapeDtypeStruct((m, FOX_WIDTH), BF16)] + [r[2] for r in rider_specs],
        compiler_params=_params(("parallel", "parallel")),
        name="fox",
    )(zb, zb, zb, qa, ka, *[r[0] for r in riders])
    return out if riders else out[0]


GLA_BLOCK = 4 * GLA_CHUNK


def _gla_decay_kernel(a_ref, wg_ref, bg_ref, cm_ref, o_ref, *, blocks):
    t = GLA_BLOCK
    for b in range(blocks):
        rows = slice(b * t, (b + 1) * t)
        gl = _dot(a_ref[rows, :].astype(BF16), wg_ref[...]) + bg_ref[...]
        g = _log_sigmoid(gl) * (LOG2E / GLA_GATE_TAU)
        o_ref[rows, :] = _dot_01(cm_ref[...], g)


def _gla_decay(small, w_gate_emb, b_gate, *, blocks=GLA_DECAY_BLOCKS_PER_STEP):
    m = small.shape[0]
    t = GLA_BLOCK
    rows = blocks * t
    idx = jnp.arange(t)
    same = (idx[:, None] // GLA_CHUNK) == (idx[None, :] // GLA_CHUNK)
    cm = jnp.where(same & (idx[:, None] >= idx[None, :]), 1.0, 0.0).astype(BF16)
    cm3 = jnp.concatenate([cm, cm, cm], axis=1)
    return pl.pallas_call(
        functools.partial(_gla_decay_kernel, blocks=blocks),
        grid=(m // rows,),
        in_specs=[
            pl.BlockSpec((rows, LANE), lambda i: (i, 0)),
            pl.BlockSpec((LANE, GLA_KW), lambda i: (0, 0)),
            pl.BlockSpec((1, GLA_KW), lambda i: (0, 0)),
            pl.BlockSpec((t, 3 * t), lambda i: (0, 0)),
        ],
        out_specs=pl.BlockSpec((rows, GLA_KW), lambda i: (i, 0)),
        out_shape=jax.ShapeDtypeStruct((m, GLA_KW), F32),
        compiler_params=_params(("parallel",)),
        name="gla_decay",
    )(small, w_gate_emb, b_gate.reshape(1, GLA_KW), cm3)


def _gla_kernel(q_ref, k_ref, v_ref, r_ref, bc_ref, on_ref, o_ref, *, heads):
    for g in range(heads):
        _gla_head(q_ref, k_ref, v_ref, r_ref, bc_ref, on_ref, o_ref,
                  slice(g * GLA_DK, (g + 1) * GLA_DK), slice(g * GLA_DV, (g + 1) * GLA_DV))


def _gla_head(q_ref, k_ref, v_ref, r_ref, bc_ref, on_ref, o_ref, kcols, vcols):
    c = GLA_CHUNK
    t = GLA_BLOCK
    dk = GLA_DK
    seq = q_ref.shape[0]
    st = jnp.zeros((GLA_DV, dk), F32)
    row = lax.broadcasted_iota(jnp.int32, (t, t), 0)
    col = lax.broadcasted_iota(jnp.int32, (t, t), 1)
    intra = (row >= col) & ((row // c) == (col // c))
    o_gain = on_ref[:, vcols]

    for n in range(seq // t):
        rows = slice(n * t, (n + 1) * t)
        bc = bc_ref[rows, kcols]
        qb = q_ref[rows, kcols] * (dk ** -0.5)
        kb = k_ref[rows, kcols]
        vb = v_ref[rows, vcols]
        chunks = [slice(i * c, (i + 1) * c) for i in range(t // c)]
        b_mid = jnp.concatenate(
            [jnp.broadcast_to(bc[s.start + c // 2:s.start + c // 2 + 1, :], (c, dk)) for s in chunks], axis=0)
        b_last = jnp.concatenate(
            [jnp.broadcast_to(bc[s.stop - 1:s.stop, :], (c, dk)) for s in chunks], axis=0)
        qd = (qb * jnp.exp2(bc - b_mid)).astype(BF16)
        kd = (kb * jnp.exp2(b_mid - bc)).astype(BF16)
        ku = (kb * jnp.exp2(b_last - bc)).astype(BF16)
        qi = (qb * jnp.exp2(bc)).astype(BF16)
        att = jnp.where(intra, _dot_nt(qd, kd), 0.0)
        o = _dot(att.astype(BF16), vb)
        inter = []
        for s in chunks:
            inter.append(_dot_nt(qi[s, :], st.astype(BF16)))
            st = st * jnp.exp2(bc[s.stop - 1:s.stop, :]) + _dot_tn(vb[s, :], ku[s, :])
        o = _rms(o + jnp.concatenate(inter, axis=0), o_gain)
        o_ref[rows, vcols] = (o * _silu(r_ref[rows, vcols])).astype(o_ref.dtype)


def _gla(zf, zb, bc, o_norm, batch, seq, *, heads=GLA_HEADS_PER_STEP):
    m = zf.shape[0]
    dk, dv = heads * GLA_DK, heads * GLA_DV
    return pl.pallas_call(
        functools.partial(_gla_kernel, heads=heads),
        grid=(batch, GLA_HEADS // heads),
        in_specs=[
            pl.BlockSpec((seq, dk), lambda b, h: (b, ZF_GQ * LANE // dk + h)),
            pl.BlockSpec((seq, dk), lambda b, h: (b, ZF_GK * LANE // dk + h)),
            pl.BlockSpec((seq, dv), lambda b, h: (b, ZB_GV * LANE // dv + h)),
            pl.BlockSpec((seq, dv), lambda b, h: (b, ZF_GR * LANE // dv + h)),
            pl.BlockSpec((seq, dk), lambda b, h: (b, h)),
            pl.BlockSpec((1, dv), lambda b, h: (0, h)),
        ],
        out_specs=pl.BlockSpec((seq, dv), lambda b, h: (b, h)),
        out_shape=jax.ShapeDtypeStruct((m, GLA_VW), BF16),
        compiler_params=_params(("parallel", "parallel")),
        name="gla",
    )(zf, zf, zb, zf, bc, o_norm.reshape(1, GLA_VW))


def _merge_kernel(u0_ref, v0_ref, u_ref, v_ref, lng_ref, lnb_ref, ws_ref, bs_ref, b_ref, c_ref,
                  ga_ref, gb_ref, gc_ref, wa_ref, wb_ref, wc_ref, wo_ref, h_ref, o_ref, ya_ref):
    i = pl.program_id(0)
    chunks = ya_ref.shape[1] // SGU_CHUNK

    @pl.when(i == 0)
    def _():
        _sgu_kernel(u0_ref, v0_ref, lng_ref, lnb_ref, ws_ref, bs_ref, ya_ref.at[0], chunks=chunks)

    slot = i % 2
    mix = (jax.nn.sigmoid(ga_ref[...]) * _dot(ya_ref[slot], wa_ref[...])
           + jax.nn.sigmoid(gb_ref[...]) * _dot(b_ref[...], wb_ref[...])
           + jax.nn.sigmoid(gc_ref[...]) * _dot(c_ref[...], wc_ref[...]))
    o_ref[...] = h_ref[...] + _dot(mix.astype(BF16), wo_ref[...])
    _sgu_kernel(u_ref, v_ref, lng_ref, lnb_ref, ws_ref, bs_ref, ya_ref.at[1 - slot], chunks=chunks)


def _merge(sgu_params, yb, yc, zf, wa, wb, wc, wo, h, *, tm=MERGE_ROWS):
    m, d = h.shape
    kw = yb.shape[1]
    g0 = ZF_GATES * LANE // d
    last = m // tm - 1
    ln_g, ln_b, w_s, b_s = sgu_params
    branch = pl.BlockSpec((tm, kw), lambda i: (i, 0))
    resident = dict(pipeline_mode=pl.Buffered(1))
    wspec = pl.BlockSpec((kw, d), lambda i: (0, 0), **resident)
    return pl.pallas_call(
        _merge_kernel,
        grid=(m // tm,),
        in_specs=[
            pl.BlockSpec((tm, kw), lambda i: (0, ZF_SU * LANE // kw)),
            pl.BlockSpec((tm, kw), lambda i: (0, ZF_SV * LANE // kw)),
            pl.BlockSpec((tm, kw), lambda i: (jnp.minimum(i + 1, last), ZF_SU * LANE // kw)),
            pl.BlockSpec((tm, kw), lambda i: (jnp.minimum(i + 1, last), ZF_SV * LANE // kw)),
            pl.BlockSpec((1, kw), lambda i: (0, 0)),
            pl.BlockSpec((1, kw), lambda i: (0, 0)),
            pl.BlockSpec((SGU_GROUPS, SGU_CHUNK, SGU_CHUNK), lambda i: (0, 0, 0)),
            pl.BlockSpec((SGU_CHUNK, SGU_GROUPS), lambda i: (0, 0)),
            branch, branch,
            pl.BlockSpec((tm, d), lambda i: (i, g0)),
            pl.BlockSpec((tm, d), lambda i: (i, g0 + 1)),
            pl.BlockSpec((tm, d), lambda i: (i, g0 + 2)),
            wspec, wspec, wspec,
            pl.BlockSpec((d, d), lambda i: (0, 0), **resident),
            pl.BlockSpec((tm, d), lambda i: (i, 0)),
        ],
        out_specs=pl.BlockSpec((tm, d), lambda i: (i, 0)),
        out_shape=jax.ShapeDtypeStruct((m, d), F32),
        scratch_shapes=[pltpu.VMEM((2, tm, kw), BF16)],
        compiler_params=_params(("arbitrary",)),
        name="merge",
    )(zf, zf, zf, zf, ln_g.reshape(1, kw), ln_b.reshape(1, kw), w_s, jnp.swapaxes(b_s, 0, 1),
      yb, yc, zf, zf, zf, wa, wb, wc, wo, h)


def _xattn_kernel(*refs, rider):
    if rider:
        h_ref, g_ref, wq_ref, kv_ref, wo_ref, src_ref, o_ref, dst_ref = refs
        dst_ref[...] = src_ref[...].astype(BF16)
    else:
        h_ref, g_ref, wq_ref, kv_ref, wo_ref, o_ref = refs
    dh = XA_HEAD_DIM
    x = h_ref[...]
    n = _rms(x, g_ref[...]).astype(BF16)
    q = _dot(n, wq_ref[...].astype(BF16)) * (dh ** -0.5)
    outs = []
    for hd in range(XA_HEADS):
        qh = q[:, hd * dh:(hd + 1) * dh].astype(BF16)
        kh = kv_ref[:, hd * dh:(hd + 1) * dh].astype(BF16)
        vh = kv_ref[:, XA_WIDTH + hd * dh:XA_WIDTH + (hd + 1) * dh].astype(BF16)
        s = _dot_nt(qh, kh)
        e = jnp.exp(s - jnp.max(s, axis=-1, keepdims=True))
        p = e / jnp.sum(e, axis=-1, keepdims=True)
        outs.append(_dot(p.astype(BF16), vh))
    o = jnp.concatenate(outs, axis=1).astype(BF16)
    o_ref[...] = x + _dot(o, wo_ref[...].astype(BF16))


def _xattn(h, norm_g, wq, kv, wo, layer, seq, mem_len, rider=None, *, tm=XATTN_ROWS):
    m, d = h.shape
    per_batch = seq // tm
    in_specs = [
        pl.BlockSpec((tm, d), lambda i: (i, 0)),
        pl.BlockSpec((1, d), lambda i: (0, 0)),
        pl.BlockSpec((None, d, XA_WIDTH), lambda i: (layer, 0, 0)),
        pl.BlockSpec((mem_len, 2 * XA_WIDTH), lambda i: (i // per_batch, 0)),
        pl.BlockSpec((None, XA_WIDTH, d), lambda i: (layer, 0, 0)),
    ]
    args = [h, norm_g.reshape(1, d), wq, kv, wo]
    out_specs = [pl.BlockSpec((tm, d), lambda i: (i, 0))]
    out_shape = [jax.ShapeDtypeStruct((m, d), F32)]
    if rider:
        r_in, r_out, r_shape = _cast_rider(*rider, steps=m // tm, step_of=lambda i: i)
        in_specs.append(r_in)
        args.append(rider[0])
        out_specs.append(r_out)
        out_shape.append(r_shape)
    out = pl.pallas_call(
        functools.partial(_xattn_kernel, rider=rider is not None),
        grid=(m // tm,),
        in_specs=in_specs,
        out_specs=out_specs,
        out_shape=out_shape,
        compiler_params=_params(("parallel",)),
        name="xattn",
    )(*args)
    return out if rider else out[0]


CAST_BLOCK_BYTES = 6 * 1024 * 1024


def _cast_kernel(x_ref, o_ref):
    o_ref[...] = x_ref[...].astype(o_ref.dtype)


def _to_bf16(w, cols=None, layers=None):
    lead = w.shape[:-1] if layers is None else (layers,) + w.shape[1:-1]
    c = w.shape[-1] if cols is None else cols
    w2 = w.reshape(-1, w.shape[-1])
    r = w2.shape[0] if layers is None else layers * (w2.shape[0] // w.shape[0])
    rows = 16
    while rows * 2 * c * 4 <= CAST_BLOCK_BYTES and r % (rows * 2) == 0:
        rows *= 2
    spec = pl.BlockSpec((rows, c), lambda i: (i, 0))
    out = pl.pallas_call(
        _cast_kernel,
        grid=(r // rows,),
        in_specs=[spec],
        out_specs=spec,
        out_shape=jax.ShapeDtypeStruct((r, c), BF16),
        compiler_params=_params(("parallel",)),
        name="cast",
    )(w2)
    return out.reshape(*lead, c)

W_FQ = 2 * SGU_WIDTH
W_FF = W_FQ + 3 * FOX_WIDTH
W_GQ = W_FF + FOX_HEADS
W_GV = W_GQ + 2 * GLA_KW
W_GA = W_GV + GLA_VW
W_GR = W_GA + GLA_GATE_RANK
W_END = W_GR + GLA_VW + 3 * D_MODEL


REPACK_CHUNK = 1024


def _repack_kernel(wt_ref, wf_ref, wb_ref, ws_ref):
    def move(dst_ref, dst, src, width):
        for off in range(0, width, REPACK_CHUNK):
            n = min(REPACK_CHUNK, width - off)
            dst_ref[:, dst + off:dst + off + n] = wt_ref[src + off:src + off + n, :].T.astype(BF16)

    move(wf_ref, 0, 0, W_FQ)
    move(wf_ref, W_FQ, W_GQ, 2 * GLA_KW)
    move(wf_ref, W_FQ + 2 * GLA_KW, W_GR, W_END - W_GR)
    move(wb_ref, 0, W_FQ, 3 * FOX_WIDTH)
    move(wb_ref, 3 * FOX_WIDTH, W_GV, GLA_VW)
    lane = lax.broadcasted_iota(jnp.int32, (wt_ref.shape[1], LANE), 1)
    ff = wt_ref[W_FF - SMALL_FF:W_FF - SMALL_FF + LANE, :].T
    ga = wt_ref[W_GA - SMALL_GA:W_GA - SMALL_GA + LANE, :].T
    side = jnp.where(lane < SMALL_FF + FOX_HEADS, ff,
                     jnp.where(lane < SMALL_GA + GLA_GATE_RANK, ga, 0.0))
    ws_ref[...] = side.astype(BF16)


def _repack_w_in(w, *, kb=REPACK_K_ROWS):
    nl, d, n = w.shape
    wt = jnp.swapaxes(w, 1, 2)

    def spec(width):
        return pl.BlockSpec((None, kb, width), lambda l, i: (l, i, 0))

    return pl.pallas_call(
        _repack_kernel,
        grid=(nl, d // kb),
        in_specs=[pl.BlockSpec((None, n, kb), lambda l, i: (l, 0, i))],
        out_specs=[spec(ZF_WIDTH), spec(ZB_WIDTH), spec(LANE)],
        out_shape=[jax.ShapeDtypeStruct((nl, d, width), BF16) for width in (ZF_WIDTH, ZB_WIDTH, LANE)],
        compiler_params=_params(("parallel", "parallel")),
        name="repack",
    )(wt)


def _embed_w_gate(w_gate):
    emb = jnp.zeros((LANE, GLA_KW), BF16)
    return emb.at[SMALL_GA:SMALL_GA + GLA_GATE_RANK, :].set(w_gate.astype(BF16))


def kernel(x, mem, ffn1_norm, ffn1_w_in, ffn1_w_out, mix_norm, w_in, sgu_ln_g, sgu_ln_b, sgu_w_s, sgu_b_s, fox_b_f, gla_w_gate, gla_b_gate, gla_o_norm, w_branch_a, w_branch_b, w_branch_c, w_out, xa_norm, mem_norm, xa_w_q, xa_w_kv, xa_w_o, ffn2_norm, ffn2_w_in, ffn2_w_out, final_norm):
    batch, seq, d = x.shape
    mem_len = mem.shape[1]
    h = x.reshape(batch * seq, d)
    mem2 = mem.reshape(batch * mem_len, d)
    ones_f = jnp.ones((1, ZF_WIDTH), F32)
    scale_b = jnp.ones((1, ZB_WIDTH), F32).at[:, ZB_FQ * LANE:ZB_FQ * LANE + FOX_WIDTH].set(
        FOX_HEAD_DIM ** -0.5 * LOG2E)
    f1_gate = _to_bf16(ffn1_w_in, D_FF, layers=1)[0]
    w_f, w_b, w_small = _repack_w_in(w_in)
    for l in range(DEPTH):
        h = _ffn(h, ffn1_norm[l], f1_gate, ffn1_w_in, ffn1_w_out, l)

        zf, small = _proj(h, mix_norm[l], w_f, l, ones_f, F32, w_small)
        zb, f2_gate = _proj(h, mix_norm[l], w_b, l, scale_b, BF16, rider=(ffn2_w_in, l, D_FF))
        qa, ka = _fox_gate(small, fox_b_f[l], batch, seq)
        merge_w = [(w, l, w.shape[2]) for w in (w_branch_a, w_branch_b, w_branch_c, w_out)]
        yb, wa, wb, wc, wo = _fox(zb, qa, ka, batch, seq, merge_w)
        bc = _gla_decay(small, _embed_w_gate(gla_w_gate[l]), gla_b_gate[l])
        yc = _gla(zf, zb, bc, gla_o_norm[l], batch, seq)
        sgu_params = (sgu_ln_g[l], sgu_ln_b[l], sgu_w_s[l], sgu_b_s[l])
        h = _merge(sgu_params, yb, yc, zf, wa, wb, wc, wo, h)

        kv = _norm_proj(mem2, mem_norm[l], xa_w_kv, l)
        if l + 1 < DEPTH:
            h, f1_gate = _xattn(h, xa_norm[l], xa_w_q, kv, xa_w_o, l, seq, mem_len,
                                rider=(ffn1_w_in, l + 1, D_FF))
        else:
            h = _xattn(h, xa_norm[l], xa_w_q, kv, xa_w_o, l, seq, mem_len)

        h = _ffn(h, ffn2_norm[l], f2_gate, ffn2_w_in, ffn2_w_out, l,
                 final_norm if l == DEPTH - 1 else None)
    return h.reshape(batch, seq, d)
```

```python
import functools

import jax
import jax.numpy as jnp
from jax import lax
from jax.experimental import pallas as pl
from jax.experimental.pallas import tpu as pltpu

F32 = jnp.float32
BF16 = jnp.bfloat16

D_MODEL = 2048
DEPTH = 2
D_FF = 5632
EPS = 1e-6

SGU_GROUPS = 4
SGU_GROUP_DIM = 256
SGU_WIDTH = 1024
SGU_CHUNK = 128

FOX_HEADS = 8
FOX_HEAD_DIM = 128
FOX_WIDTH = 1024

GLA_HEADS = 4
GLA_DK = 128
GLA_DV = 256
GLA_KW = 512
GLA_VW = 1024
GLA_GATE_RANK = 16
GLA_GATE_TAU = 16.0
GLA_CHUNK = 64

XA_HEADS = 4
XA_HEAD_DIM = 128
XA_WIDTH = 512

LANE = 128

ZF_GATES, ZF_SU, ZF_SV, ZF_GQ, ZF_GK, ZF_GR = 0, 48, 56, 64, 68, 72
ZF_WIDTH = 80 * LANE
ZB_FQ, ZB_FK, ZB_FV, ZB_GV = 0, 8, 16, 24
ZB_WIDTH = 32 * LANE
SMALL_FF = 0
SMALL_GA = 8

NEG_BIG = -1e30
LOG2E = 1.4426950408889634
VMEM_LIMIT = 63 * 1024 * 1024

FFN_ROWS, FFN_HIDDEN_TILE = 1024, 512
PROJ_ROWS, PROJ_COLS = 1024, 2048
MEM_KV_ROWS = 1024
FOX_TILE, FOX_HEADS_PER_STEP = 512, 2
GLA_DECAY_BLOCKS_PER_STEP = 4
GLA_HEADS_PER_STEP = 2
MERGE_ROWS = 256
XATTN_ROWS = 1024
REPACK_K_ROWS = 256


def _params(sem):
    return pltpu.CompilerParams(dimension_semantics=sem, vmem_limit_bytes=VMEM_LIMIT)


def _rms(x, g):
    return x * lax.rsqrt(jnp.mean(x * x, axis=-1, keepdims=True) + EPS) * g


def _log_sigmoid(x):
    return jnp.minimum(x, 0.0) - jnp.log1p(jnp.exp(-jnp.abs(x)))


def _gelu_tanh(x):
    c = 0.7978845608028654
    return x * (0.5 * (1.0 + jnp.tanh(c * (x + 0.044715 * (x * x * x)))))


def _silu(x):
    return x * jax.nn.sigmoid(x)


def _dot(a, b):
    return jnp.dot(a, b, preferred_element_type=F32)


def _dot_nt(a, b):
    return lax.dot_general(a, b, (((1,), (1,)), ((), ())), preferred_element_type=F32)


def _dot_tn(a, b):
    return lax.dot_general(a, b, (((0,), (0,)), ((), ())), preferred_element_type=F32)


def _split3(x):
    hi = x.astype(BF16)
    r = x - hi.astype(F32)
    lo = r.astype(BF16)
    lo2 = (r - lo.astype(F32)).astype(BF16)
    return hi, lo, lo2


def _dot_01(mat3, x):
    return _dot(mat3, jnp.concatenate(_split3(x), axis=0))


def _ffn_kernel(*refs, final):
    if final:
        x_ref, g_ref, wg_ref, wu_ref, wo_ref, fg_ref, o_ref, xn_ref = refs
    else:
        x_ref, g_ref, wg_ref, wu_ref, wo_ref, o_ref, xn_ref = refs
    j = pl.program_id(1)

    def tile(xn):
        gate = _dot(xn, wg_ref[...])
        up = _dot(xn, wu_ref[...].astype(BF16))
        act = (_silu(gate) * up * 0.5).astype(BF16)
        return _dot(act, wo_ref[...].astype(BF16))

    @pl.when(j == 0)
    def _():
        x = x_ref[...]
        xn = _rms(x, g_ref[...]).astype(BF16)
        xn_ref[...] = xn
        o_ref[...] = x + tile(xn)

    @pl.when(j > 0)
    def _():
        o_ref[...] += tile(xn_ref[...])

    if final:
        @pl.when(j == pl.num_programs(1) - 1)
        def _():
            o_ref[...] = _rms(o_ref[...], fg_ref[...])


def _ffn(h, norm_g, w_gate, w_in, w_out, layer, final_g=None, *, tm=FFN_ROWS, tf=FFN_HIDDEN_TILE):
    m, d = h.shape
    f = w_out.shape[1]
    nt = f // tf
    final = final_g is not None
    vec = pl.BlockSpec((1, d), lambda i, j: (0, 0))
    rows = pl.BlockSpec((tm, d), lambda i, j: (i, 0))
    in_specs = [
        rows, vec,
        pl.BlockSpec((d, tf), lambda i, j: (0, j)),
        pl.BlockSpec((None, d, tf), lambda i, j: (layer, 0, j + nt)),
        pl.BlockSpec((None, tf, d), lambda i, j: (layer, j, 0)),
    ]
    args = [h, norm_g.reshape(1, d), w_gate, w_in, w_out]
    if final:
        in_specs.append(vec)
        args.append(final_g.reshape(1, d))
    return pl.pallas_call(
        functools.partial(_ffn_kernel, final=final),
        grid=(m // tm, nt),
        in_specs=in_specs,
        out_specs=rows,
        out_shape=jax.ShapeDtypeStruct((m, d), F32),
        scratch_shapes=[pltpu.VMEM((tm, d), BF16)],
        compiler_params=_params(("parallel", "arbitrary")),
        name="ffn",
    )(*args)


def _cast_rider(src, layer, cols, steps, step_of):
    rows = src.shape[1] // steps
    in_spec = pl.BlockSpec((None, rows, cols), lambda *g: (layer, step_of(*g), 0))
    out_spec = pl.BlockSpec((rows, cols), lambda *g: (step_of(*g), 0))
    return in_spec, out_spec, jax.ShapeDtypeStruct((src.shape[1], cols), BF16)


def _proj_kernel(*refs, small, rider):
    if rider:
        x_ref, g_ref, w_ref, cs_ref, src_ref, o_ref, dst_ref, xn_ref = refs
        dst_ref[...] = src_ref[...].astype(BF16)
    elif small:
        x_ref, g_ref, w_ref, cs_ref, ws_ref, o_ref, os_ref, xn_ref = refs
    else:
        x_ref, g_ref, w_ref, cs_ref, o_ref, xn_ref = refs
    j = pl.program_id(1)

    def project(xn):
        o_ref[...] = (_dot(xn, w_ref[...]) * cs_ref[...]).astype(o_ref.dtype)

    @pl.when(j == 0)
    def _():
        xn = _rms(x_ref[...], g_ref[...]).astype(BF16)
        xn_ref[...] = xn
        if small:
            os_ref[...] = _dot(xn, ws_ref[...])
        project(xn)

    @pl.when(j > 0)
    def _():
        project(xn_ref[...])


def _proj(x, norm_g, w, layer, col_scale, out_dtype, w_small=None, rider=None, *, tm=PROJ_ROWS, tn=PROJ_COLS):
    m, d = x.shape
    n = w.shape[2]
    small = w_small is not None
    assert not (small and rider)
    in_specs = [
        pl.BlockSpec((tm, d), lambda i, j: (i, 0)),
        pl.BlockSpec((1, d), lambda i, j: (0, 0)),
        pl.BlockSpec((None, d, tn), lambda i, j: (layer, 0, j)),
        pl.BlockSpec((1, tn), lambda i, j: (0, j)),
    ]
    args = [x, norm_g.reshape(1, d), w, col_scale]
    out_specs = [pl.BlockSpec((tm, tn), lambda i, j: (i, j))]
    out_shape = [jax.ShapeDtypeStruct((m, n), out_dtype)]
    if small:
        in_specs.append(pl.BlockSpec((None, d, LANE), lambda i, j: (layer, 0, 0)))
        args.append(w_small)
        out_specs.append(pl.BlockSpec((tm, LANE), lambda i, j: (i, 0)))
        out_shape.append(jax.ShapeDtypeStruct((m, LANE), F32))
    if rider:
        nj = n // tn
        r_in, r_out, r_shape = _cast_rider(*rider, steps=(m // tm) * nj, step_of=lambda i, j: i * nj + j)
        in_specs.append(r_in)
        args.append(rider[0])
        out_specs.append(r_out)
        out_shape.append(r_shape)
    out = pl.pallas_call(
        functools.partial(_proj_kernel, small=small, rider=rider is not None),
        grid=(m // tm, n // tn),
        in_specs=in_specs,
        out_specs=out_specs,
        out_shape=out_shape,
        scratch_shapes=[pltpu.VMEM((tm, d), BF16)],
        compiler_params=_params(("parallel", "arbitrary")),
        name="proj",
    )(*args)
    return out if small or rider else out[0]


def _norm_proj_kernel(x_ref, g_ref, w_ref, o_ref):
    o_ref[...] = _dot(_rms(x_ref[...], g_ref[...]).astype(BF16), w_ref[...].astype(BF16))


def _norm_proj(x, norm_g, w, layer, *, tm=MEM_KV_ROWS):
    m, d = x.shape
    n = w.shape[2]
    return pl.pallas_call(
        _norm_proj_kernel,
        grid=(m // tm,),
        in_specs=[
            pl.BlockSpec((tm, d), lambda i: (i, 0)),
            pl.BlockSpec((1, d), lambda i: (0, 0)),
            pl.BlockSpec((None, d, n), lambda i: (layer, 0, 0)),
        ],
        out_specs=pl.BlockSpec((tm, n), lambda i: (i, 0)),
        out_shape=jax.ShapeDtypeStruct((m, n), F32),
        compiler_params=_params(("parallel",)),
        name="mem_kv",
    )(x, norm_g.reshape(1, d), w)


def _sgu_kernel(uv_ref, lng_ref, lnb_ref, ws_ref, bs_ref, o_ref, *, chunks):
    t = SGU_CHUNK
    gd = SGU_GROUP_DIM
    row = lax.broadcasted_iota(jnp.int32, (t, t), 0)
    col = lax.broadcasted_iota(jnp.int32, (t, t), 1)
    causal = row >= col
    for g in range(SGU_GROUPS):
        w = jnp.where(causal, ws_ref[g], 0.0).astype(BF16)
        bias = bs_ref[:, g:g + 1]
        ln_g = lng_ref[:, g * gd:(g + 1) * gd]
        ln_b = lnb_ref[:, g * gd:(g + 1) * gd]
        for c in range(chunks):
            rows = slice(c * t, (c + 1) * t)
            cols = slice(g * gd, (g + 1) * gd)
            v = _gelu_tanh(uv_ref[rows, SGU_WIDTH + g * gd:SGU_WIDTH + (g + 1) * gd])
            mu = jnp.mean(v, axis=-1, keepdims=True)
            vc = v - mu
            var = jnp.mean(vc * vc, axis=-1, keepdims=True)
            vn = vc * lax.rsqrt(var + EPS) * ln_g + ln_b
            mixed = _dot(w, vn.astype(BF16)) + bias
            o_ref[rows, cols] = (_gelu_tanh(uv_ref[rows, cols]) * mixed).astype(o_ref.dtype)


FOX_AUG = 6


def _fox_gate_kernel(f_ref, bf_ref, tril_ref, pq_ref, pk_ref, cq_ref, ck_ref, qa_ref, ka_ref):
    t = LANE
    seq = f_ref.shape[0]
    carry = jnp.zeros((1, t), F32)
    for b in range(seq // t):
        rows = slice(b * t, (b + 1) * t)
        lf = _log_sigmoid(f_ref[rows, :] + bf_ref[...]) * LOG2E
        cum = _dot_01(tril_ref[...], lf) + carry
        carry = cum[t - 1:t, :]
        parts = jnp.concatenate(_split3(cum), axis=1)
        qa_ref[rows, :] = (_dot(parts, pq_ref[...]) + cq_ref[...]).astype(BF16)
        ka_ref[rows, :] = (_dot(parts, pk_ref[...]) + ck_ref[...]).astype(BF16)


def _tril3(n):
    tril = jnp.tril(jnp.ones((n, n), BF16))
    return jnp.concatenate([tril, tril, tril], axis=1)


def _fox_placement():
    src = jnp.arange(3 * LANE)[:, None]
    dst = jnp.arange(LANE)[None, :]
    p, h = src // LANE, src % LANE
    valid = h < FOX_HEADS
    pq = jnp.where(valid & (dst == h * FOX_AUG + p), 1.0, 0.0).astype(BF16)
    pk = jnp.where(valid & (dst == h * FOX_AUG + 3 + p), -1.0, 0.0).astype(BF16)
    used = dst < FOX_HEADS * FOX_AUG
    cq = jnp.where(used & (dst % FOX_AUG >= 3), 1.0, 0.0).astype(F32)
    ck = jnp.where(used & (dst % FOX_AUG < 3), 1.0, 0.0).astype(F32)
    return pq, pk, cq, ck


def _fox_gate(small, b_f, batch, seq):
    bf = jnp.zeros((1, LANE), F32).at[0, SMALL_FF:SMALL_FF + FOX_HEADS].set(b_f)
    w = LANE
    aug = jax.ShapeDtypeStruct((batch * seq, w), BF16)

    def const(shape):
        return pl.BlockSpec(shape, lambda b: (0, 0))

    return pl.pallas_call(
        _fox_gate_kernel,
        grid=(batch,),
        in_specs=[
            pl.BlockSpec((seq, LANE), lambda b: (b, 0)),
            const((1, LANE)), const((LANE, 3 * LANE)),
            const((3 * LANE, w)), const((3 * LANE, w)), const((1, w)), const((1, w)),
        ],
        out_specs=[pl.BlockSpec((seq, w), lambda b: (b, 0))] * 2,
        out_shape=[aug, aug],
        compiler_params=_params(("parallel",)),
        name="fox_gate",
    )(small, bf, _tril3(LANE), *_fox_placement())


def _fox_kernel(*refs, t, heads, riders):
    q_ref, k_ref, v_ref, qa_ref, ka_ref = refs[:5]
    o_ref = refs[5 + riders]
    for src_ref, dst_ref in zip(refs[5:5 + riders], refs[6 + riders:]):
        dst_ref[...] = src_ref[...].astype(BF16)
    seq = q_ref.shape[0]
    dh = FOX_HEAD_DIM
    lane = lax.broadcasted_iota(jnp.int32, (t, LANE), 1)
    ones_col = jnp.where(lane == 0, 1.0, 0.0).astype(BF16)
    row = lax.broadcasted_iota(jnp.int32, (t, t), 0)
    col = lax.broadcasted_iota(jnp.int32, (t, t), 1)
    causal = row >= col

    for g in range(heads):
        head = pl.program_id(1) * heads + g
        own = (lane >= head * FOX_AUG) & (lane < (head + 1) * FOX_AUG)
        cols = slice(g * dh, (g + 1) * dh)

        def k_block(j):
            rows = slice(j * t, (j + 1) * t)
            ka = jnp.where(own, ka_ref[rows, :], jnp.zeros((), BF16))
            return (jnp.concatenate([k_ref[rows, cols], ka], axis=1),
                    jnp.concatenate([v_ref[rows, cols], ones_col], axis=1))

        for i in range(seq // t):
            rows = slice(i * t, (i + 1) * t)
            qc = jnp.concatenate([q_ref[rows, cols], qa_ref[rows, :]], axis=1)
            kc, vc = k_block(i)
            s = jnp.where(causal, _dot_nt(qc, kc), NEG_BIG)
            m = jnp.max(s, axis=1, keepdims=True)
            acc = _dot(jnp.exp2(s - m).astype(BF16), vc)
            for j in range(i):
                kc, vc = k_block(j)
                s = _dot_nt(qc, kc)
                m_new = jnp.maximum(m, jnp.max(s, axis=1, keepdims=True))
                acc = jnp.exp2(m - m_new) * acc + _dot(jnp.exp2(s - m_new).astype(BF16), vc)
                m = m_new
            o_ref[rows, cols] = (acc[:, :dh] / acc[:, dh:dh + 1]).astype(o_ref.dtype)


def _fox(zb, qa, ka, batch, seq, riders=(), *, t=FOX_TILE, heads=FOX_HEADS_PER_STEP):
    m = zb.shape[0]
    w = heads * FOX_HEAD_DIM
    groups = FOX_HEADS // heads

    def group(seg):
        return pl.BlockSpec((seq, w), lambda b, h: (b, seg * LANE // w + h))

    aug = pl.BlockSpec((seq, LANE), lambda b, h: (b, 0))
    rider_specs = [_cast_rider(*r, steps=batch * groups, step_of=lambda b, h: b * groups + h) for r in riders]
    out = pl.pallas_call(
        functools.partial(_fox_kernel, t=t, heads=heads, riders=len(riders)),
        grid=(batch, groups),
        in_specs=[group(ZB_FQ), group(ZB_FK), group(ZB_FV), aug, aug] + [r[0] for r in rider_specs],
        out_specs=[group(0)] + [r[1] for r in rider_specs],
        out_shape=[jax.ShapeDtypeStruct((m, FOX_WIDTH), BF16)] + [r[2] for r in rider_specs],
        compiler_params=_params(("parallel", "parallel")),
        name="fox",
    )(zb, zb, zb, qa, ka, *[r[0] for r in riders])
    return out if riders else out[0]


GLA_BLOCK = 4 * GLA_CHUNK


def _gla_decay_kernel(a_ref, wg_ref, bg_ref, cm_ref, o_ref, *, blocks):
    t = GLA_BLOCK
    for b in range(blocks):
        rows = slice(b * t, (b + 1) * t)
        gl = _dot(a_ref[rows, :].astype(BF16), wg_ref[...]) + bg_ref[...]
        g = _log_sigmoid(gl) * (LOG2E / GLA_GATE_TAU)
        o_ref[rows, :] = _dot_01(cm_ref[...], g)


def _gla_decay(small, w_gate_emb, b_gate, *, blocks=GLA_DECAY_BLOCKS_PER_STEP):
    m = small.shape[0]
    t = GLA_BLOCK
    rows = blocks * t
    idx = jnp.arange(t)
    same = (idx[:, None] // GLA_CHUNK) == (idx[None, :] // GLA_CHUNK)
    cm = jnp.where(same & (idx[:, None] >= idx[None, :]), 1.0, 0.0).astype(BF16)
    cm3 = jnp.concatenate([cm, cm, cm], axis=1)
    return pl.pallas_call(
        functools.partial(_gla_decay_kernel, blocks=blocks),
        grid=(m // rows,),
        in_specs=[
            pl.BlockSpec((rows, LANE), lambda i: (i, 0)),
            pl.BlockSpec((LANE, GLA_KW), lambda i: (0, 0)),
            pl.BlockSpec((1, GLA_KW), lambda i: (0, 0)),
            pl.BlockSpec((t, 3 * t), lambda i: (0, 0)),
        ],
        out_specs=pl.BlockSpec((rows, GLA_KW), lambda i: (i, 0)),
        out_shape=jax.ShapeDtypeStruct((m, GLA_KW), F32),
        compiler_params=_params(("parallel",)),
        name="gla_decay",
    )(small, w_gate_emb, b_gate.reshape(1, GLA_KW), cm3)


def _gla_kernel(q_ref, k_ref, v_ref, r_ref, bc_ref, on_ref, o_ref, *, heads):
    for g in range(heads):
        _gla_head(q_ref, k_ref, v_ref, r_ref, bc_ref, on_ref, o_ref,
                  slice(g * GLA_DK, (g + 1) * GLA_DK), slice(g * GLA_DV, (g + 1) * GLA_DV))


def _gla_head(q_ref, k_ref, v_ref, r_ref, bc_ref, on_ref, o_ref, kcols, vcols):
    c = GLA_CHUNK
    t = GLA_BLOCK
    dk = GLA_DK
    seq = q_ref.shape[0]
    st = jnp.zeros((GLA_DV, dk), F32)
    row = lax.broadcasted_iota(jnp.int32, (t, t), 0)
    col = lax.broadcasted_iota(jnp.int32, (t, t), 1)
    intra = (row >= col) & ((row // c) == (col // c))
    o_gain = on_ref[:, vcols]

    for n in range(seq // t):
        rows = slice(n * t, (n + 1) * t)
        bc = bc_ref[rows, kcols]
        qb = q_ref[rows, kcols] * (dk ** -0.5)
        kb = k_ref[rows, kcols]
        vb = v_ref[rows, vcols]
        chunks = [slice(i * c, (i + 1) * c) for i in range(t // c)]
        b_mid = jnp.concatenate(
            [jnp.broadcast_to(bc[s.start + c // 2:s.start + c // 2 + 1, :], (c, dk)) for s in chunks], axis=0)
        b_last = jnp.concatenate(
            [jnp.broadcast_to(bc[s.stop - 1:s.stop, :], (c, dk)) for s in chunks], axis=0)
        qd = (qb * jnp.exp2(bc - b_mid)).astype(BF16)
        kd = (kb * jnp.exp2(b_mid - bc)).astype(BF16)
        ku = (kb * jnp.exp2(b_last - bc)).astype(BF16)
        qi = (qb * jnp.exp2(bc)).astype(BF16)
        att = jnp.where(intra, _dot_nt(qd, kd), 0.0)
        o = _dot(att.astype(BF16), vb)
        inter = []
        for s in chunks:
            inter.append(_dot_nt(qi[s, :], st.astype(BF16)))
            st = st * jnp.exp2(bc[s.stop - 1:s.stop, :]) + _dot_tn(vb[s, :], ku[s, :])
        o = _rms(o + jnp.concatenate(inter, axis=0), o_gain)
        o_ref[rows, vcols] = (o * _silu(r_ref[rows, vcols])).astype(o_ref.dtype)


def _gla(zf, zb, bc, o_norm, batch, seq, *, heads=GLA_HEADS_PER_STEP):
    m = zf.shape[0]
    dk, dv = heads * GLA_DK, heads * GLA_DV
    return pl.pallas_call(
        functools.partial(_gla_kernel, heads=heads),
        grid=(batch, GLA_HEADS // heads),
        in_specs=[
            pl.BlockSpec((seq, dk), lambda b, h: (b, ZF_GQ * LANE // dk + h)),
            pl.BlockSpec((seq, dk), lambda b, h: (b, ZF_GK * LANE // dk + h)),
            pl.BlockSpec((seq, dv), lambda b, h: (b, ZB_GV * LANE // dv + h)),
            pl.BlockSpec((seq, dv), lambda b, h: (b, ZF_GR * LANE // dv + h)),
            pl.BlockSpec((seq, dk), lambda b, h: (b, h)),
            pl.BlockSpec((1, dv), lambda b, h: (0, h)),
        ],
        out_specs=pl.BlockSpec((seq, dv), lambda b, h: (b, h)),
        out_shape=jax.ShapeDtypeStruct((m, GLA_VW), BF16),
        compiler_params=_params(("parallel", "parallel")),
        name="gla",
    )(zf, zf, zb, zf, bc, o_norm.reshape(1, GLA_VW))


def _merge_kernel(uv0_ref, uv_ref, lng_ref, lnb_ref, ws_ref, bs_ref, b_ref, c_ref,
                  g_ref, wa_ref, wb_ref, wc_ref, wo_ref, h_ref, o_ref, ya_ref):
    i = pl.program_id(0)
    chunks = ya_ref.shape[1] // SGU_CHUNK
    d = h_ref.shape[1]

    @pl.when(i == 0)
    def _():
        _sgu_kernel(uv0_ref, lng_ref, lnb_ref, ws_ref, bs_ref, ya_ref.at[0], chunks=chunks)

    slot = i % 2
    mix = (jax.nn.sigmoid(g_ref[:, 0:d]) * _dot(ya_ref[slot], wa_ref[...])
           + jax.nn.sigmoid(g_ref[:, d:2 * d]) * _dot(b_ref[...], wb_ref[...])
           + jax.nn.sigmoid(g_ref[:, 2 * d:3 * d]) * _dot(c_ref[...], wc_ref[...]))
    o_ref[...] = h_ref[...] + _dot(mix.astype(BF16), wo_ref[...])
    _sgu_kernel(uv_ref, lng_ref, lnb_ref, ws_ref, bs_ref, ya_ref.at[1 - slot], chunks=chunks)


def _merge(sgu_params, yb, yc, zf, wa, wb, wc, wo, h, *, tm=MERGE_ROWS):
    m, d = h.shape
    kw = yb.shape[1]
    uv_w = 2 * kw
    uv_col = ZF_SU * LANE // uv_w
    last = m // tm - 1
    ln_g, ln_b, w_s, b_s = sgu_params
    branch = pl.BlockSpec((tm, kw), lambda i: (i, 0))
    resident = dict(pipeline_mode=pl.Buffered(1))
    wspec = pl.BlockSpec((kw, d), lambda i: (0, 0), **resident)
    return pl.pallas_call(
        _merge_kernel,
        grid=(m // tm,),
        in_specs=[
            pl.BlockSpec((tm, uv_w), lambda i: (0, uv_col)),
            pl.BlockSpec((tm, uv_w), lambda i: (jnp.minimum(i + 1, last), uv_col)),
            pl.BlockSpec((1, kw), lambda i: (0, 0)),
            pl.BlockSpec((1, kw), lambda i: (0, 0)),
            pl.BlockSpec((SGU_GROUPS, SGU_CHUNK, SGU_CHUNK), lambda i: (0, 0, 0)),
            pl.BlockSpec((SGU_CHUNK, SGU_GROUPS), lambda i: (0, 0)),
            branch, branch,
            pl.BlockSpec((tm, 3 * d), lambda i: (i, ZF_GATES * LANE // (3 * d))),
            wspec, wspec, wspec,
            pl.BlockSpec((d, d), lambda i: (0, 0), **resident),
            pl.BlockSpec((tm, d), lambda i: (i, 0)),
        ],
        out_specs=pl.BlockSpec((tm, d), lambda i: (i, 0)),
        out_shape=jax.ShapeDtypeStruct((m, d), F32),
        scratch_shapes=[pltpu.VMEM((2, tm, kw), BF16)],
        compiler_params=_params(("arbitrary",)),
        name="merge",
    )(zf, zf, ln_g.reshape(1, kw), ln_b.reshape(1, kw), w_s, jnp.swapaxes(b_s, 0, 1),
      yb, yc, zf, wa, wb, wc, wo, h)


def _xattn_kernel(*refs, rider):
    if rider:
        h_ref, g_ref, wq_ref, kv_ref, wo_ref, src_ref, o_ref, dst_ref = refs
        dst_ref[...] = src_ref[...].astype(BF16)
    else:
        h_ref, g_ref, wq_ref, kv_ref, wo_ref, o_ref = refs
    dh = XA_HEAD_DIM
    x = h_ref[...]
    n = _rms(x, g_ref[...]).astype(BF16)
    q = _dot(n, wq_ref[...].astype(BF16)) * (dh ** -0.5)
    outs = []
    for hd in range(XA_HEADS):
        qh = q[:, hd * dh:(hd + 1) * dh].astype(BF16)
        kh = kv_ref[:, hd * dh:(hd + 1) * dh].astype(BF16)
        vh = kv_ref[:, XA_WIDTH + hd * dh:XA_WIDTH + (hd + 1) * dh].astype(BF16)
        s = _dot_nt(qh, kh)
        e = jnp.exp(s - jnp.max(s, axis=-1, keepdims=True))
        p = e / jnp.sum(e, axis=-1, keepdims=True)
        outs.append(_dot(p.astype(BF16), vh))
    o = jnp.concatenate(outs, axis=1).astype(BF16)
    o_ref[...] = x + _dot(o, wo_ref[...].astype(BF16))


def _xattn(h, norm_g, wq, kv, wo, layer, seq, mem_len, rider=None, *, tm=XATTN_ROWS):
    m, d = h.shape
    per_batch = seq // tm
    in_specs = [
        pl.BlockSpec((tm, d), lambda i: (i, 0)),
        pl.BlockSpec((1, d), lambda i: (0, 0)),
        pl.BlockSpec((None, d, XA_WIDTH), lambda i: (layer, 0, 0)),
        pl.BlockSpec((mem_len, 2 * XA_WIDTH), lambda i: (i // per_batch, 0)),
        pl.BlockSpec((None, XA_WIDTH, d), lambda i: (layer, 0, 0)),
    ]
    args = [h, norm_g.reshape(1, d), wq, kv, wo]
    out_specs = [pl.BlockSpec((tm, d), lambda i: (i, 0))]
    out_shape = [jax.ShapeDtypeStruct((m, d), F32)]
    if rider:
        r_in, r_out, r_shape = _cast_rider(*rider, steps=m // tm, step_of=lambda i: i)
        in_specs.append(r_in)
        args.append(rider[0])
        out_specs.append(r_out)
        out_shape.append(r_shape)
    out = pl.pallas_call(
        functools.partial(_xattn_kernel, rider=rider is not None),
        grid=(m // tm,),
        in_specs=in_specs,
        out_specs=out_specs,
        out_shape=out_shape,
        compiler_params=_params(("parallel",)),
        name="xattn",
    )(*args)
    return out if rider else out[0]


CAST_BLOCK_BYTES = 6 * 1024 * 1024


def _cast_kernel(x_ref, o_ref):
    o_ref[...] = x_ref[...].astype(o_ref.dtype)


def _to_bf16(w, cols=None, layers=None):
    lead = w.shape[:-1] if layers is None else (layers,) + w.shape[1:-1]
    c = w.shape[-1] if cols is None else cols
    w2 = w.reshape(-1, w.shape[-1])
    r = w2.shape[0] if layers is None else layers * (w2.shape[0] // w.shape[0])
    rows = 16
    while rows * 2 * c * 4 <= CAST_BLOCK_BYTES and r % (rows * 2) == 0:
        rows *= 2
    spec = pl.BlockSpec((rows, c), lambda i: (i, 0))
    out = pl.pallas_call(
        _cast_kernel,
        grid=(r // rows,),
        in_specs=[spec],
        out_specs=spec,
        out_shape=jax.ShapeDtypeStruct((r, c), BF16),
        compiler_params=_params(("parallel",)),
        name="cast",
    )(w2)
    return out.reshape(*lead, c)

W_FQ = 2 * SGU_WIDTH
W_FF = W_FQ + 3 * FOX_WIDTH
W_GQ = W_FF + FOX_HEADS
W_GV = W_GQ + 2 * GLA_KW
W_GA = W_GV + GLA_VW
W_GR = W_GA + GLA_GATE_RANK
W_END = W_GR + GLA_VW + 3 * D_MODEL


REPACK_CHUNK = 1024


def _repack_kernel(wt_ref, wf_ref, wb_ref, ws_ref):
    def move(dst_ref, dst, src, width):
        for off in range(0, width, REPACK_CHUNK):
            n = min(REPACK_CHUNK, width - off)
            dst_ref[:, dst + off:dst + off + n] = wt_ref[src + off:src + off + n, :].T.astype(BF16)

    w_gates = W_GR + GLA_VW
    move(wf_ref, ZF_GATES * LANE, w_gates, W_END - w_gates)
    move(wf_ref, ZF_SU * LANE, 0, W_FQ)
    move(wf_ref, ZF_GQ * LANE, W_GQ, 2 * GLA_KW)
    move(wf_ref, ZF_GR * LANE, W_GR, GLA_VW)
    move(wb_ref, 0, W_FQ, 3 * FOX_WIDTH)
    move(wb_ref, 3 * FOX_WIDTH, W_GV, GLA_VW)
    lane = lax.broadcasted_iota(jnp.int32, (wt_ref.shape[1], LANE), 1)
    ff = wt_ref[W_FF - SMALL_FF:W_FF - SMALL_FF + LANE, :].T
    ga = wt_ref[W_GA - SMALL_GA:W_GA - SMALL_GA + LANE, :].T
    side = jnp.where(lane < SMALL_FF + FOX_HEADS, ff,
                     jnp.where(lane < SMALL_GA + GLA_GATE_RANK, ga, 0.0))
    ws_ref[...] = side.astype(BF16)


def _repack_w_in(w, *, kb=REPACK_K_ROWS):
    nl, d, n = w.shape
    wt = jnp.swapaxes(w, 1, 2)

    def spec(width):
        return pl.BlockSpec((None, kb, width), lambda l, i: (l, i, 0))

    return pl.pallas_call(
        _repack_kernel,
        grid=(nl, d // kb),
        in_specs=[pl.BlockSpec((None, n, kb), lambda l, i: (l, 0, i))],
        out_specs=[spec(ZF_WIDTH), spec(ZB_WIDTH), spec(LANE)],
        out_shape=[jax.ShapeDtypeStruct((nl, d, width), BF16) for width in (ZF_WIDTH, ZB_WIDTH, LANE)],
        compiler_params=_params(("parallel", "parallel")),
        name="repack",
    )(wt)


def _embed_w_gate(w_gate):
    emb = jnp.zeros((LANE, GLA_KW), BF16)
    return emb.at[SMALL_GA:SMALL_GA + GLA_GATE_RANK, :].set(w_gate.astype(BF16))


def kernel(x, mem, ffn1_norm, ffn1_w_in, ffn1_w_out, mix_norm, w_in, sgu_ln_g, sgu_ln_b, sgu_w_s, sgu_b_s, fox_b_f, gla_w_gate, gla_b_gate, gla_o_norm, w_branch_a, w_branch_b, w_branch_c, w_out, xa_norm, mem_norm, xa_w_q, xa_w_kv, xa_w_o, ffn2_norm, ffn2_w_in, ffn2_w_out, final_norm):
    batch, seq, d = x.shape
    mem_len = mem.shape[1]
    h = x.reshape(batch * seq, d)
    mem2 = mem.reshape(batch * mem_len, d)
    ones_f = jnp.ones((1, ZF_WIDTH), F32)
    scale_b = jnp.ones((1, ZB_WIDTH), F32).at[:, ZB_FQ * LANE:ZB_FQ * LANE + FOX_WIDTH].set(
        FOX_HEAD_DIM ** -0.5 * LOG2E)
    f1_gate = _to_bf16(ffn1_w_in, D_FF, layers=1)[0]
    w_f, w_b, w_small = _repack_w_in(w_in)
    for l in range(DEPTH):
        h = _ffn(h, ffn1_norm[l], f1_gate, ffn1_w_in, ffn1_w_out, l)

        zf, small = _proj(h, mix_norm[l], w_f, l, ones_f, F32, w_small)
        zb, f2_gate = _proj(h, mix_norm[l], w_b, l, scale_b, BF16, rider=(ffn2_w_in, l, D_FF))
        qa, ka = _fox_gate(small, fox_b_f[l], batch, seq)
        merge_w = [(w, l, w.shape[2]) for w in (w_branch_a, w_branch_b, w_branch_c, w_out)]
        yb, wa, wb, wc, wo = _fox(zb, qa, ka, batch, seq, merge_w)
        bc = _gla_decay(small, _embed_w_gate(gla_w_gate[l]), gla_b_gate[l])
        yc = _gla(zf, zb, bc, gla_o_norm[l], batch, seq)
        sgu_params = (sgu_ln_g[l], sgu_ln_b[l], sgu_w_s[l], sgu_b_s[l])
        h = _merge(sgu_params, yb, yc, zf, wa, wb, wc, wo, h)

        kv = _norm_proj(mem2, mem_norm[l], xa_w_kv, l)
        if l + 1 < DEPTH:
            h, f1_gate = _xattn(h, xa_norm[l], xa_w_q, kv, xa_w_o, l, seq, mem_len,
                                rider=(ffn1_w_in, l + 1, D_FF))
        else:
            h = _xattn(h, xa_norm[l], xa_w_q, kv, xa_w_o, l, seq, mem_len)

        h = _ffn(h, ffn2_norm[l], f2_gate, ffn2_w_in, ffn2_w_out, l,
                 final_norm if l == DEPTH - 1 else None)
    return h.reshape(batch, seq, d)
```

```python
import functools

import jax
import jax.numpy as jnp
from jax import lax
from jax.experimental import pallas as pl
from jax.experimental.pallas import tpu as pltpu

F32 = jnp.float32
BF16 = jnp.bfloat16

D_MODEL = 2048
DEPTH = 2
D_FF = 5632
EPS = 1e-6

SGU_GROUPS = 4
SGU_GROUP_DIM = 256
SGU_WIDTH = 1024
SGU_CHUNK = 128

FOX_HEADS = 8
FOX_HEAD_DIM = 128
FOX_WIDTH = 1024

GLA_HEADS = 4
GLA_DK = 128
GLA_DV = 256
GLA_KW = 512
GLA_VW = 1024
GLA_GATE_RANK = 16
GLA_GATE_TAU = 16.0
GLA_CHUNK = 64

XA_HEADS = 4
XA_HEAD_DIM = 128
XA_WIDTH = 512

LANE = 128

ZF_SU, ZF_SV, ZF_GQ, ZF_GK, ZF_GR, ZF_GATES = 0, 8, 16, 20, 24, 32
ZF_WIDTH = 80 * LANE
ZB_FQ, ZB_FK, ZB_FV, ZB_GV = 0, 8, 16, 24
ZB_WIDTH = 32 * LANE
SMALL_FF = 0
SMALL_GA = 8

NEG_BIG = -1e30
LOG2E = 1.4426950408889634
VMEM_LIMIT = 63 * 1024 * 1024

FFN_ROWS, FFN_HIDDEN_TILE = 1024, 512
PROJ_ROWS, PROJ_COLS = 1024, 2048
MEM_KV_ROWS = 1024
FOX_TILE, FOX_HEADS_PER_STEP = 512, 2
GLA_DECAY_BLOCKS_PER_STEP = 4
GLA_HEADS_PER_STEP = 2
MERGE_ROWS = 256
XATTN_ROWS = 1024
REPACK_K_ROWS = 256


def _params(sem):
    return pltpu.CompilerParams(dimension_semantics=sem, vmem_limit_bytes=VMEM_LIMIT)


def _rms(x, g):
    return x * lax.rsqrt(jnp.mean(x * x, axis=-1, keepdims=True) + EPS) * g


def _log_sigmoid(x):
    return jnp.minimum(x, 0.0) - jnp.log1p(jnp.exp(-jnp.abs(x)))


def _gelu_tanh(x):
    c = 0.7978845608028654
    return x * (0.5 * (1.0 + jnp.tanh(c * (x + 0.044715 * (x * x * x)))))


def _silu(x):
    return x * jax.nn.sigmoid(x)


def _dot(a, b):
    return jnp.dot(a, b, preferred_element_type=F32)


def _dot_nt(a, b):
    return lax.dot_general(a, b, (((1,), (1,)), ((), ())), preferred_element_type=F32)


def _dot_tn(a, b):
    return lax.dot_general(a, b, (((0,), (0,)), ((), ())), preferred_element_type=F32)


def _split3(x):
    hi = x.astype(BF16)
    r = x - hi.astype(F32)
    lo = r.astype(BF16)
    lo2 = (r - lo.astype(F32)).astype(BF16)
    return hi, lo, lo2


def _dot_01(mat3, x):
    return _dot(mat3, jnp.concatenate(_split3(x), axis=0))


def _ffn_kernel(*refs, final):
    if final:
        x_ref, g_ref, wg_ref, wu_ref, wo_ref, fg_ref, o_ref, xn_ref = refs
    else:
        x_ref, g_ref, wg_ref, wu_ref, wo_ref, o_ref, xn_ref = refs
    j = pl.program_id(1)

    def tile(xn):
        gate = _dot(xn, wg_ref[...])
        up = _dot(xn, wu_ref[...].astype(BF16))
        act = (_silu(gate) * up * 0.5).astype(BF16)
        return _dot(act, wo_ref[...].astype(BF16))

    @pl.when(j == 0)
    def _():
        x = x_ref[...]
        xn = _rms(x, g_ref[...]).astype(BF16)
        xn_ref[...] = xn
        o_ref[...] = x + tile(xn)

    @pl.when(j > 0)
    def _():
        o_ref[...] += tile(xn_ref[...])

    if final:
        @pl.when(j == pl.num_programs(1) - 1)
        def _():
            o_ref[...] = _rms(o_ref[...], fg_ref[...])


def _ffn(h, norm_g, w_gate, w_in, w_out, layer, final_g=None, *, tm=FFN_ROWS, tf=FFN_HIDDEN_TILE):
    m, d = h.shape
    f = w_out.shape[1]
    nt = f // tf
    final = final_g is not None
    vec = pl.BlockSpec((1, d), lambda i, j: (0, 0))
    rows = pl.BlockSpec((tm, d), lambda i, j: (i, 0))
    in_specs = [
        rows, vec,
        pl.BlockSpec((d, tf), lambda i, j: (0, j)),
        pl.BlockSpec((None, d, tf), lambda i, j: (layer, 0, j + nt)),
        pl.BlockSpec((None, tf, d), lambda i, j: (layer, j, 0)),
    ]
    args = [h, norm_g.reshape(1, d), w_gate, w_in, w_out]
    if final:
        in_specs.append(vec)
        args.append(final_g.reshape(1, d))
    return pl.pallas_call(
        functools.partial(_ffn_kernel, final=final),
        grid=(m // tm, nt),
        in_specs=in_specs,
        out_specs=rows,
        out_shape=jax.ShapeDtypeStruct((m, d), F32),
        scratch_shapes=[pltpu.VMEM((tm, d), BF16)],
        compiler_params=_params(("parallel", "arbitrary")),
        name="ffn",
    )(*args)


def _cast_rider(src, layer, cols, steps, step_of):
    rows = src.shape[1] // steps
    in_spec = pl.BlockSpec((None, rows, cols), lambda *g: (layer, step_of(*g), 0))
    out_spec = pl.BlockSpec((rows, cols), lambda *g: (step_of(*g), 0))
    return in_spec, out_spec, jax.ShapeDtypeStruct((src.shape[1], cols), BF16)


def _proj_kernel(*refs, small, rider):
    if rider:
        x_ref, g_ref, w_ref, cs_ref, src_ref, o_ref, dst_ref, xn_ref = refs
        dst_ref[...] = src_ref[...].astype(BF16)
    elif small:
        x_ref, g_ref, w_ref, cs_ref, ws_ref, o_ref, os_ref, xn_ref = refs
    else:
        x_ref, g_ref, w_ref, cs_ref, o_ref, xn_ref = refs
    j = pl.program_id(1)

    def project(xn):
        o_ref[...] = (_dot(xn, w_ref[...]) * cs_ref[...]).astype(o_ref.dtype)

    @pl.when(j == 0)
    def _():
        xn = _rms(x_ref[...], g_ref[...]).astype(BF16)
        xn_ref[...] = xn
        if small:
            os_ref[...] = _dot(xn, ws_ref[...])
        project(xn)

    @pl.when(j > 0)
    def _():
        project(xn_ref[...])


def _proj(x, norm_g, w, layer, col_scale, out_dtype, w_small=None, rider=None, *, tm=PROJ_ROWS, tn=PROJ_COLS):
    m, d = x.shape
    n = w.shape[2]
    small = w_small is not None
    assert not (small and rider)
    in_specs = [
        pl.BlockSpec((tm, d), lambda i, j: (i, 0)),
        pl.BlockSpec((1, d), lambda i, j: (0, 0)),
        pl.BlockSpec((None, d, tn), lambda i, j: (layer, 0, j)),
        pl.BlockSpec((1, tn), lambda i, j: (0, j)),
    ]
    args = [x, norm_g.reshape(1, d), w, col_scale]
    out_specs = [pl.BlockSpec((tm, tn), lambda i, j: (i, j))]
    out_shape = [jax.ShapeDtypeStruct((m, n), out_dtype)]
    if small:
        in_specs.append(pl.BlockSpec((None, d, LANE), lambda i, j: (layer, 0, 0)))
        args.append(w_small)
        out_specs.append(pl.BlockSpec((tm, LANE), lambda i, j: (i, 0)))
        out_shape.append(jax.ShapeDtypeStruct((m, LANE), F32))
    if rider:
        nj = n // tn
        r_in, r_out, r_shape = _cast_rider(*rider, steps=(m // tm) * nj, step_of=lambda i, j: i * nj + j)
        in_specs.append(r_in)
        args.append(rider[0])
        out_specs.append(r_out)
        out_shape.append(r_shape)
    out = pl.pallas_call(
        functools.partial(_proj_kernel, small=small, rider=rider is not None),
        grid=(m // tm, n // tn),
        in_specs=in_specs,
        out_specs=out_specs,
        out_shape=out_shape,
        scratch_shapes=[pltpu.VMEM((tm, d), BF16)],
        compiler_params=_params(("parallel", "arbitrary")),
        name="proj",
    )(*args)
    return out if small or rider else out[0]


def _norm_proj_kernel(x_ref, g_ref, w_ref, o_ref):
    o_ref[...] = _dot(_rms(x_ref[...], g_ref[...]).astype(BF16), w_ref[...].astype(BF16))


def _norm_proj(x, norm_g, w, layer, *, tm=MEM_KV_ROWS):
    m, d = x.shape
    n = w.shape[2]
    return pl.pallas_call(
        _norm_proj_kernel,
        grid=(m // tm,),
        in_specs=[
            pl.BlockSpec((tm, d), lambda i: (i, 0)),
            pl.BlockSpec((1, d), lambda i: (0, 0)),
            pl.BlockSpec((None, d, n), lambda i: (layer, 0, 0)),
        ],
        out_specs=pl.BlockSpec((tm, n), lambda i: (i, 0)),
        out_shape=jax.ShapeDtypeStruct((m, n), F32),
        compiler_params=_params(("parallel",)),
        name="mem_kv",
    )(x, norm_g.reshape(1, d), w)


def _sgu_kernel(u_ref, v_ref, lng_ref, lnb_ref, ws_ref, bs_ref, o_ref, *, chunks):
    t = SGU_CHUNK
    gd = SGU_GROUP_DIM
    row = lax.broadcasted_iota(jnp.int32, (t, t), 0)
    col = lax.broadcasted_iota(jnp.int32, (t, t), 1)
    causal = row >= col
    for g in range(SGU_GROUPS):
        w = jnp.where(causal, ws_ref[g], 0.0).astype(BF16)
        bias = bs_ref[:, g:g + 1]
        ln_g = lng_ref[:, g * gd:(g + 1) * gd]
        ln_b = lnb_ref[:, g * gd:(g + 1) * gd]
        for c in range(chunks):
            rows = slice(c * t, (c + 1) * t)
            cols = slice(g * gd, (g + 1) * gd)
            v = _gelu_tanh(v_ref[rows, cols])
            mu = jnp.mean(v, axis=-1, keepdims=True)
            vc = v - mu
            var = jnp.mean(vc * vc, axis=-1, keepdims=True)
            vn = vc * lax.rsqrt(var + EPS) * ln_g + ln_b
            mixed = _dot(w, vn.astype(BF16)) + bias
            o_ref[rows, cols] = (_gelu_tanh(u_ref[rows, cols]) * mixed).astype(o_ref.dtype)


FOX_AUG = 6


def _fox_gate_kernel(f_ref, bf_ref, tril_ref, pq_ref, pk_ref, cq_ref, ck_ref, qa_ref, ka_ref):
    t = LANE
    seq = f_ref.shape[0]
    carry = jnp.zeros((1, t), F32)
    for b in range(seq // t):
        rows = slice(b * t, (b + 1) * t)
        lf = _log_sigmoid(f_ref[rows, :] + bf_ref[...]) * LOG2E
        cum = _dot_01(tril_ref[...], lf) + carry
        carry = cum[t - 1:t, :]
        parts = jnp.concatenate(_split3(cum), axis=1)
        qa_ref[rows, :] = (_dot(parts, pq_ref[...]) + cq_ref[...]).astype(BF16)
        ka_ref[rows, :] = (_dot(parts, pk_ref[...]) + ck_ref[...]).astype(BF16)


def _tril3(n):
    tril = jnp.tril(jnp.ones((n, n), BF16))
    return jnp.concatenate([tril, tril, tril], axis=1)


def _fox_placement():
    src = jnp.arange(3 * LANE)[:, None]
    dst = jnp.arange(LANE)[None, :]
    p, h = src // LANE, src % LANE
    valid = h < FOX_HEADS
    pq = jnp.where(valid & (dst == h * FOX_AUG + p), 1.0, 0.0).astype(BF16)
    pk = jnp.where(valid & (dst == h * FOX_AUG + 3 + p), -1.0, 0.0).astype(BF16)
    used = dst < FOX_HEADS * FOX_AUG
    cq = jnp.where(used & (dst % FOX_AUG >= 3), 1.0, 0.0).astype(F32)
    ck = jnp.where(used & (dst % FOX_AUG < 3), 1.0, 0.0).astype(F32)
    return pq, pk, cq, ck


def _fox_gate(small, b_f, batch, seq):
    bf = jnp.zeros((1, LANE), F32).at[0, SMALL_FF:SMALL_FF + FOX_HEADS].set(b_f)
    w = LANE
    aug = jax.ShapeDtypeStruct((batch * seq, w), BF16)

    def const(shape):
        return pl.BlockSpec(shape, lambda b: (0, 0))

    return pl.pallas_call(
        _fox_gate_kernel,
        grid=(batch,),
        in_specs=[
            pl.BlockSpec((seq, LANE), lambda b: (b, 0)),
            const((1, LANE)), const((LANE, 3 * LANE)),
            const((3 * LANE, w)), const((3 * LANE, w)), const((1, w)), const((1, w)),
        ],
        out_specs=[pl.BlockSpec((seq, w), lambda b: (b, 0))] * 2,
        out_shape=[aug, aug],
        compiler_params=_params(("parallel",)),
        name="fox_gate",
    )(small, bf, _tril3(LANE), *_fox_placement())


def _fox_kernel(*refs, t, heads, riders):
    q_ref, k_ref, v_ref, qa_ref, ka_ref = refs[:5]
    o_ref = refs[5 + riders]
    for src_ref, dst_ref in zip(refs[5:5 + riders], refs[6 + riders:]):
        dst_ref[...] = src_ref[...].astype(BF16)
    seq = q_ref.shape[0]
    dh = FOX_HEAD_DIM
    lane = lax.broadcasted_iota(jnp.int32, (t, LANE), 1)
    ones_col = jnp.where(lane == 0, 1.0, 0.0).astype(BF16)
    row = lax.broadcasted_iota(jnp.int32, (t, t), 0)
    col = lax.broadcasted_iota(jnp.int32, (t, t), 1)
    causal = row >= col

    for g in range(heads):
        head = pl.program_id(1) * heads + g
        own = (lane >= head * FOX_AUG) & (lane < (head + 1) * FOX_AUG)
        cols = slice(g * dh, (g + 1) * dh)

        def k_block(j):
            rows = slice(j * t, (j + 1) * t)
            ka = jnp.where(own, ka_ref[rows, :], jnp.zeros((), BF16))
            return (jnp.concatenate([k_ref[rows, cols], ka], axis=1),
                    jnp.concatenate([v_ref[rows, cols], ones_col], axis=1))

        for i in range(seq // t):
            rows = slice(i * t, (i + 1) * t)
            qc = jnp.concatenate([q_ref[rows, cols], qa_ref[rows, :]], axis=1)
            kc, vc = k_block(i)
            s = jnp.where(causal, _dot_nt(qc, kc), NEG_BIG)
            m = jnp.max(s, axis=1, keepdims=True)
            acc = _dot(jnp.exp2(s - m).astype(BF16), vc)
            for j in range(i):
                kc, vc = k_block(j)
                s = _dot_nt(qc, kc)
                m_new = jnp.maximum(m, jnp.max(s, axis=1, keepdims=True))
                acc = jnp.exp2(m - m_new) * acc + _dot(jnp.exp2(s - m_new).astype(BF16), vc)
                m = m_new
            o_ref[rows, cols] = (acc[:, :dh] / acc[:, dh:dh + 1]).astype(o_ref.dtype)


def _fox(zb, qa, ka, batch, seq, riders=(), *, t=FOX_TILE, heads=FOX_HEADS_PER_STEP):
    m = zb.shape[0]
    w = heads * FOX_HEAD_DIM
    groups = FOX_HEADS // heads

    def group(seg):
        return pl.BlockSpec((seq, w), lambda b, h: (b, seg * LANE // w + h))

    aug = pl.BlockSpec((seq, LANE), lambda b, h: (b, 0))
    rider_specs = [_cast_rider(*r, steps=batch * groups, step_of=lambda b, h: b * groups + h) for r in riders]
    out = pl.pallas_call(
        functools.partial(_fox_kernel, t=t, heads=heads, riders=len(riders)),
        grid=(batch, groups),
        in_specs=[group(ZB_FQ), group(ZB_FK), group(ZB_FV), aug, aug] + [r[0] for r in rider_specs],
        out_specs=[group(0)] + [r[1] for r in rider_specs],
        out_shape=[jax.ShapeDtypeStruct((m, FOX_WIDTH), BF16)] + [r[2] for r in rider_specs],
        compiler_params=_params(("parallel", "parallel")),
        name="fox",
    )(zb, zb, zb, qa, ka, *[r[0] for r in riders])
    return out if riders else out[0]


GLA_BLOCK = 4 * GLA_CHUNK


def _gla_decay_kernel(a_ref, wg_ref, bg_ref, cm_ref, o_ref, *, blocks):
    t = GLA_BLOCK
    for b in range(blocks):
        rows = slice(b * t, (b + 1) * t)
        gl = _dot(a_ref[rows, :].astype(BF16), wg_ref[...]) + bg_ref[...]
        g = _log_sigmoid(gl) * (LOG2E / GLA_GATE_TAU)
        o_ref[rows, :] = _dot_01(cm_ref[...], g)


def _gla_decay(small, w_gate_emb, b_gate, *, blocks=GLA_DECAY_BLOCKS_PER_STEP):
    m = small.shape[0]
    t = GLA_BLOCK
    rows = blocks * t
    idx = jnp.arange(t)
    same = (idx[:, None] // GLA_CHUNK) == (idx[None, :] // GLA_CHUNK)
    cm = jnp.where(same & (idx[:, None] >= idx[None, :]), 1.0, 0.0).astype(BF16)
    cm3 = jnp.concatenate([cm, cm, cm], axis=1)
    return pl.pallas_call(
        functools.partial(_gla_decay_kernel, blocks=blocks),
        grid=(m // rows,),
        in_specs=[
            pl.BlockSpec((rows, LANE), lambda i: (i, 0)),
            pl.BlockSpec((LANE, GLA_KW), lambda i: (0, 0)),
            pl.BlockSpec((1, GLA_KW), lambda i: (0, 0)),
            pl.BlockSpec((t, 3 * t), lambda i: (0, 0)),
        ],
        out_specs=pl.BlockSpec((rows, GLA_KW), lambda i: (i, 0)),
        out_shape=jax.ShapeDtypeStruct((m, GLA_KW), F32),
        compiler_params=_params(("parallel",)),
        name="gla_decay",
    )(small, w_gate_emb, b_gate.reshape(1, GLA_KW), cm3)


def _gla_kernel(q_ref, k_ref, v_ref, r_ref, bc_ref, on_ref, o_ref, *, heads):
    for g in range(heads):
        _gla_head(q_ref, k_ref, v_ref, r_ref, bc_ref, on_ref, o_ref,
                  slice(g * GLA_DK, (g + 1) * GLA_DK), slice(g * GLA_DV, (g + 1) * GLA_DV))


def _gla_head(q_ref, k_ref, v_ref, r_ref, bc_ref, on_ref, o_ref, kcols, vcols):
    c = GLA_CHUNK
    t = GLA_BLOCK
    dk = GLA_DK
    seq = q_ref.shape[0]
    st = jnp.zeros((GLA_DV, dk), F32)
    row = lax.broadcasted_iota(jnp.int32, (t, t), 0)
    col = lax.broadcasted_iota(jnp.int32, (t, t), 1)
    intra = (row >= col) & ((row // c) == (col // c))
    o_gain = on_ref[:, vcols]

    for n in range(seq // t):
        rows = slice(n * t, (n + 1) * t)
        bc = bc_ref[rows, kcols]
        qb = q_ref[rows, kcols] * (dk ** -0.5)
        kb = k_ref[rows, kcols]
        vb = v_ref[rows, vcols]
        chunks = [slice(i * c, (i + 1) * c) for i in range(t // c)]
        b_mid = jnp.concatenate(
            [jnp.broadcast_to(bc[s.start + c // 2:s.start + c // 2 + 1, :], (c, dk)) for s in chunks], axis=0)
        b_last = jnp.concatenate(
            [jnp.broadcast_to(bc[s.stop - 1:s.stop, :], (c, dk)) for s in chunks], axis=0)
        qd = (qb * jnp.exp2(bc - b_mid)).astype(BF16)
        kd = (kb * jnp.exp2(b_mid - bc)).astype(BF16)
        ku = (kb * jnp.exp2(b_last - bc)).astype(BF16)
        qi = (qb * jnp.exp2(bc)).astype(BF16)
        att = jnp.where(intra, _dot_nt(qd, kd), 0.0)
        o = _dot(att.astype(BF16), vb)
        inter = []
        for s in chunks:
            inter.append(_dot_nt(qi[s, :], st.astype(BF16)))
            st = st * jnp.exp2(bc[s.stop - 1:s.stop, :]) + _dot_tn(vb[s, :], ku[s, :])
        o = _rms(o + jnp.concatenate(inter, axis=0), o_gain)
        o_ref[rows, vcols] = (o * _silu(r_ref[rows, vcols])).astype(o_ref.dtype)


def _gla(zf, zb, bc, o_norm, batch, seq, *, heads=GLA_HEADS_PER_STEP):
    m = zf.shape[0]
    dk, dv = heads * GLA_DK, heads * GLA_DV
    return pl.pallas_call(
        functools.partial(_gla_kernel, heads=heads),
        grid=(batch, GLA_HEADS // heads),
        in_specs=[
            pl.BlockSpec((seq, dk), lambda b, h: (b, ZF_GQ * LANE // dk + h)),
            pl.BlockSpec((seq, dk), lambda b, h: (b, ZF_GK * LANE // dk + h)),
            pl.BlockSpec((seq, dv), lambda b, h: (b, ZB_GV * LANE // dv + h)),
            pl.BlockSpec((seq, dv), lambda b, h: (b, ZF_GR * LANE // dv + h)),
            pl.BlockSpec((seq, dk), lambda b, h: (b, h)),
            pl.BlockSpec((1, dv), lambda b, h: (0, h)),
        ],
        out_specs=pl.BlockSpec((seq, dv), lambda b, h: (b, h)),
        out_shape=jax.ShapeDtypeStruct((m, GLA_VW), BF16),
        compiler_params=_params(("parallel", "parallel")),
        name="gla",
    )(zf, zf, zb, zf, bc, o_norm.reshape(1, GLA_VW))


def _merge_kernel(*refs, rider):
    if rider:
        *refs, src_ref, o_ref, dst_ref, ya_ref = refs
        dst_ref[...] = src_ref[...].astype(BF16)
    else:
        *refs, o_ref, ya_ref = refs
    (u0_ref, v0_ref, u_ref, v_ref, lng_ref, lnb_ref, ws_ref, bs_ref, b_ref, c_ref,
     ga_ref, gb_ref, gc_ref, wa_ref, wb_ref, wc_ref, wo_ref, h_ref) = refs
    i = pl.program_id(0)
    chunks = ya_ref.shape[1] // SGU_CHUNK

    @pl.when(i == 0)
    def _():
        _sgu_kernel(u0_ref, v0_ref, lng_ref, lnb_ref, ws_ref, bs_ref, ya_ref.at[0], chunks=chunks)

    slot = i % 2
    mix = (jax.nn.sigmoid(ga_ref[...]) * _dot(ya_ref[slot], wa_ref[...])
           + jax.nn.sigmoid(gb_ref[...]) * _dot(b_ref[...], wb_ref[...])
           + jax.nn.sigmoid(gc_ref[...]) * _dot(c_ref[...], wc_ref[...]))
    o_ref[...] = h_ref[...] + _dot(mix.astype(BF16), wo_ref[...])
    _sgu_kernel(u_ref, v_ref, lng_ref, lnb_ref, ws_ref, bs_ref, ya_ref.at[1 - slot], chunks=chunks)


def _merge(sgu_params, yb, yc, zf, wa, wb, wc, wo, h, rider=None, *, tm=MERGE_ROWS):
    m, d = h.shape
    kw = yb.shape[1]
    g0 = ZF_GATES * LANE // d
    last = m // tm - 1
    ln_g, ln_b, w_s, b_s = sgu_params
    branch = pl.BlockSpec((tm, kw), lambda i: (i, 0))
    resident = dict(pipeline_mode=pl.Buffered(1))
    wspec = pl.BlockSpec((kw, d), lambda i: (0, 0), **resident)
    extra_in, extra_args, extra_out, extra_shape = [], [], [], []
    if rider:
        r_in, r_out, r_shape = _cast_rider(*rider, steps=m // tm, step_of=lambda i: i)
        extra_in, extra_args, extra_out, extra_shape = [r_in], [rider[0]], [r_out], [r_shape]
    out = pl.pallas_call(
        functools.partial(_merge_kernel, rider=rider is not None),
        grid=(m // tm,),
        in_specs=[
            pl.BlockSpec((tm, kw), lambda i: (0, ZF_SU * LANE // kw)),
            pl.BlockSpec((tm, kw), lambda i: (0, ZF_SV * LANE // kw)),
            pl.BlockSpec((tm, kw), lambda i: (jnp.minimum(i + 1, last), ZF_SU * LANE // kw)),
            pl.BlockSpec((tm, kw), lambda i: (jnp.minimum(i + 1, last), ZF_SV * LANE // kw)),
            pl.BlockSpec((1, kw), lambda i: (0, 0)),
            pl.BlockSpec((1, kw), lambda i: (0, 0)),
            pl.BlockSpec((SGU_GROUPS, SGU_CHUNK, SGU_CHUNK), lambda i: (0, 0, 0)),
            pl.BlockSpec((SGU_CHUNK, SGU_GROUPS), lambda i: (0, 0)),
            branch, branch,
            pl.BlockSpec((tm, d), lambda i: (i, g0)),
            pl.BlockSpec((tm, d), lambda i: (i, g0 + 1)),
            pl.BlockSpec((tm, d), lambda i: (i, g0 + 2)),
            wspec, wspec, wspec,
            pl.BlockSpec((d, d), lambda i: (0, 0), **resident),
            pl.BlockSpec((tm, d), lambda i: (i, 0)),
        ] + extra_in,
        out_specs=[pl.BlockSpec((tm, d), lambda i: (i, 0))] + extra_out,
        out_shape=[jax.ShapeDtypeStruct((m, d), F32)] + extra_shape,
        scratch_shapes=[pltpu.VMEM((2, tm, kw), BF16)],
        compiler_params=_params(("arbitrary",)),
        name="merge",
    )(zf, zf, zf, zf, ln_g.reshape(1, kw), ln_b.reshape(1, kw), w_s, jnp.swapaxes(b_s, 0, 1),
      yb, yc, zf, zf, zf, wa, wb, wc, wo, h, *extra_args)
    return out if rider else out[0]


def _xattn_kernel(*refs, rider):
    if rider:
        h_ref, g_ref, wq_ref, kv_ref, wo_ref, src_ref, o_ref, dst_ref = refs
        dst_ref[...] = src_ref[...].astype(BF16)
    else:
        h_ref, g_ref, wq_ref, kv_ref, wo_ref, o_ref = refs
    dh = XA_HEAD_DIM
    x = h_ref[...]
    n = _rms(x, g_ref[...]).astype(BF16)
    q = _dot(n, wq_ref[...].astype(BF16)) * (dh ** -0.5)
    outs = []
    for hd in range(XA_HEADS):
        qh = q[:, hd * dh:(hd + 1) * dh].astype(BF16)
        kh = kv_ref[:, hd * dh:(hd + 1) * dh].astype(BF16)
        vh = kv_ref[:, XA_WIDTH + hd * dh:XA_WIDTH + (hd + 1) * dh].astype(BF16)
        s = _dot_nt(qh, kh)
        e = jnp.exp(s - jnp.max(s, axis=-1, keepdims=True))
        p = e / jnp.sum(e, axis=-1, keepdims=True)
        outs.append(_dot(p.astype(BF16), vh))
    o = jnp.concatenate(outs, axis=1).astype(BF16)
    o_ref[...] = x + _dot(o, wo_ref[...].astype(BF16))


def _xattn(h, norm_g, wq, kv, wo, layer, seq, mem_len, rider=None, *, tm=XATTN_ROWS):
    m, d = h.shape
    per_batch = seq // tm
    in_specs = [
        pl.BlockSpec((tm, d), lambda i: (i, 0)),
        pl.BlockSpec((1, d), lambda i: (0, 0)),
        pl.BlockSpec((None, d, XA_WIDTH), lambda i: (layer, 0, 0)),
        pl.BlockSpec((mem_len, 2 * XA_WIDTH), lambda i: (i // per_batch, 0)),
        pl.BlockSpec((None, XA_WIDTH, d), lambda i: (layer, 0, 0)),
    ]
    args = [h, norm_g.reshape(1, d), wq, kv, wo]
    out_specs = [pl.BlockSpec((tm, d), lambda i: (i, 0))]
    out_shape = [jax.ShapeDtypeStruct((m, d), F32)]
    if rider:
        r_in, r_out, r_shape = _cast_rider(*rider, steps=m // tm, step_of=lambda i: i)
        in_specs.append(r_in)
        args.append(rider[0])
        out_specs.append(r_out)
        out_shape.append(r_shape)
    out = pl.pallas_call(
        functools.partial(_xattn_kernel, rider=rider is not None),
        grid=(m // tm,),
        in_specs=in_specs,
        out_specs=out_specs,
        out_shape=out_shape,
        compiler_params=_params(("parallel",)),
        name="xattn",
    )(*args)
    return out if rider else out[0]


CAST_BLOCK_BYTES = 6 * 1024 * 1024


def _cast_kernel(x_ref, o_ref):
    o_ref[...] = x_ref[...].astype(o_ref.dtype)


def _to_bf16(w, cols=None, layers=None):
    lead = w.shape[:-1] if layers is None else (layers,) + w.shape[1:-1]
    c = w.shape[-1] if cols is None else cols
    w2 = w.reshape(-1, w.shape[-1])
    r = w2.shape[0] if layers is None else layers * (w2.shape[0] // w.shape[0])
    rows = 16
    while rows * 2 * c * 4 <= CAST_BLOCK_BYTES and r % (rows * 2) == 0:
        rows *= 2
    spec = pl.BlockSpec((rows, c), lambda i: (i, 0))
    out = pl.pallas_call(
        _cast_kernel,
        grid=(r // rows,),
        in_specs=[spec],
        out_specs=spec,
        out_shape=jax.ShapeDtypeStruct((r, c), BF16),
        compiler_params=_params(("parallel",)),
        name="cast",
    )(w2)
    return out.reshape(*lead, c)

W_FQ = 2 * SGU_WIDTH
W_FF = W_FQ + 3 * FOX_WIDTH
W_GQ = W_FF + FOX_HEADS
W_GV = W_GQ + 2 * GLA_KW
W_GA = W_GV + GLA_VW
W_GR = W_GA + GLA_GATE_RANK
W_END = W_GR + GLA_VW + 3 * D_MODEL


REPACK_CHUNK = 1024


def _repack_kernel(wt_ref, wf_ref, wb_ref, ws_ref):
    def move(dst_ref, dst, src, width):
        for off in range(0, width, REPACK_CHUNK):
            n = min(REPACK_CHUNK, width - off)
            dst_ref[:, dst + off:dst + off + n] = wt_ref[src + off:src + off + n, :].T.astype(BF16)

    move(wf_ref, 0, 0, W_FQ)
    move(wf_ref, W_FQ, W_GQ, 2 * GLA_KW)
    move(wf_ref, W_FQ + 2 * GLA_KW, W_GR, W_END - W_GR)
    move(wb_ref, 0, W_FQ, 3 * FOX_WIDTH)
    move(wb_ref, 3 * FOX_WIDTH, W_GV, GLA_VW)
    lane = lax.broadcasted_iota(jnp.int32, (wt_ref.shape[1], LANE), 1)
    ff = wt_ref[W_FF - SMALL_FF:W_FF - SMALL_FF + LANE, :].T
    ga = wt_ref[W_GA - SMALL_GA:W_GA - SMALL_GA + LANE, :].T
    side = jnp.where(lane < SMALL_FF + FOX_HEADS, ff,
                     jnp.where(lane < SMALL_GA + GLA_GATE_RANK, ga, 0.0))
    ws_ref[...] = side.astype(BF16)


def _repack_w_in(w, *, kb=REPACK_K_ROWS):
    nl, d, n = w.shape
    wt = jnp.swapaxes(w, 1, 2)

    def spec(width):
        return pl.BlockSpec((None, kb, width), lambda l, i: (l, i, 0))

    return pl.pallas_call(
        _repack_kernel,
        grid=(nl, d // kb),
        in_specs=[pl.BlockSpec((None, n, kb), lambda l, i: (l, 0, i))],
        out_specs=[spec(ZF_WIDTH), spec(ZB_WIDTH), spec(LANE)],
        out_shape=[jax.ShapeDtypeStruct((nl, d, width), BF16) for width in (ZF_WIDTH, ZB_WIDTH, LANE)],
        compiler_params=_params(("parallel", "parallel")),
        name="repack",
    )(wt)


def _embed_w_gate(w_gate):
    emb = jnp.zeros((LANE, GLA_KW), BF16)
    return emb.at[SMALL_GA:SMALL_GA + GLA_GATE_RANK, :].set(w_gate.astype(BF16))


def kernel(x, mem, ffn1_norm, ffn1_w_in, ffn1_w_out, mix_norm, w_in, sgu_ln_g, sgu_ln_b, sgu_w_s, sgu_b_s, fox_b_f, gla_w_gate, gla_b_gate, gla_o_norm, w_branch_a, w_branch_b, w_branch_c, w_out, xa_norm, mem_norm, xa_w_q, xa_w_kv, xa_w_o, ffn2_norm, ffn2_w_in, ffn2_w_out, final_norm):
    batch, seq, d = x.shape
    mem_len = mem.shape[1]
    h = x.reshape(batch * seq, d)
    mem2 = mem.reshape(batch * mem_len, d)
    ones_f = jnp.ones((1, ZF_WIDTH), F32)
    scale_b = jnp.ones((1, ZB_WIDTH), F32).at[:, ZB_FQ * LANE:ZB_FQ * LANE + FOX_WIDTH].set(
        FOX_HEAD_DIM ** -0.5 * LOG2E)
    f1_gate = _to_bf16(ffn1_w_in, D_FF, layers=1)[0]
    w_f, w_b, w_small = _repack_w_in(w_in)
    for l in range(DEPTH):
        h = _ffn(h, ffn1_norm[l], f1_gate, ffn1_w_in, ffn1_w_out, l)

        zf, small = _proj(h, mix_norm[l], w_f, l, ones_f, F32, w_small)
        zb, f2_gate = _proj(h, mix_norm[l], w_b, l, scale_b, BF16, rider=(ffn2_w_in, l, D_FF))
        qa, ka = _fox_gate(small, fox_b_f[l], batch, seq)
        merge_w = [(w, l, w.shape[2]) for w in (w_branch_a, w_branch_b, w_branch_c, w_out)]
        yb, wa, wb, wc, wo = _fox(zb, qa, ka, batch, seq, merge_w)
        bc = _gla_decay(small, _embed_w_gate(gla_w_gate[l]), gla_b_gate[l])
        yc = _gla(zf, zb, bc, gla_o_norm[l], batch, seq)
        sgu_params = (sgu_ln_g[l], sgu_ln_b[l], sgu_w_s[l], sgu_b_s[l])
        if l + 1 < DEPTH:
            h, f1_gate = _merge(sgu_params, yb, yc, zf, wa, wb, wc, wo, h,
                                rider=(ffn1_w_in, l + 1, D_FF))
        else:
            h = _merge(sgu_params, yb, yc, zf, wa, wb, wc, wo, h)

        kv = _norm_proj(mem2, mem_norm[l], xa_w_kv, l)
        h = _xattn(h, xa_norm[l], xa_w_q, kv, xa_w_o, l, seq, mem_len)

        h = _ffn(h, ffn2_norm[l], f2_gate, ffn2_w_in, ffn2_w_out, l,
                 final_norm if l == DEPTH - 1 else None)
    return h.reshape(batch, seq, d)
```
